```python
import math
import jax, jax.numpy as jnp
from jax import lax
import numpy as np

D_MODEL = 1024
BATCH = 8
SEQ = 2048
DEPTH = 2

N_MIXERS = 2
D_FF = 4 * D_MODEL
NORM_EPS = 1e-6
GLA_HEADS = 4
GLA_DK = D_MODEL // 2 // GLA_HEADS
GLA_DV = D_MODEL // GLA_HEADS
GLA_GATE_RANK = 16
GLA_TAU = 16.0
GLA_CHUNK = 64
GLA_IN = 2 * GLA_HEADS * GLA_DK + 2 * D_MODEL + GLA_GATE_RANK
DIFF_HEADS = 8
DIFF_DH = D_MODEL // DIFF_HEADS // 2
DIFF_DV = 2 * DIFF_DH
DIFF_IN = 3 * D_MODEL
ROPE_THETA = 500000.0
ROPE_DIM = DIFF_DH // 4
Q_BLOCK = 128
N_GLA = (DEPTH + 1) // 2
N_DIFF = DEPTH // 2

kernel_name = "hybrid_gla_diffattn_adaln_trunk"


def rmsnorm(x, g):
    xf = x.astype(jnp.float32)
    y = xf * lax.rsqrt(jnp.mean(xf * xf, axis=-1, keepdims=True) + NORM_EPS)
    return (y * g.astype(jnp.float32)).astype(x.dtype)


def rope_partial(x, cos, sin):
    half = ROPE_DIM // 2
    cos = cos.astype(x.dtype)
    sin = sin.astype(x.dtype)
    x1 = x[..., :half]
    x2 = x[..., half:ROPE_DIM]
    rest = x[..., ROPE_DIM:]
    return jnp.concatenate([x1 * cos - x2 * sin, x2 * cos + x1 * sin, rest], axis=-1)


def gla_mixer(h, w_in, w_a2, b_a, b_r, norm_g, w_o):
    B, T, _ = h.shape
    H, dk, dv, C = GLA_HEADS, GLA_DK, GLA_DV, GLA_CHUNK
    n = T // C
    hk = H * dk
    f32 = jnp.float32
    q, k, v, r, a_lr = jnp.split(h @ w_in, [hk, 2 * hk, 2 * hk + D_MODEL, 2 * hk + 2 * D_MODEL], axis=-1)
    log_a = jax.nn.log_sigmoid((a_lr @ w_a2 + b_a).astype(f32)) / GLA_TAU

    def chunks(t, d):
        return t.reshape(B, n, C, H, d)

    q = chunks(q.astype(f32), dk) * (dk ** -0.5)
    k = chunks(k.astype(f32), dk)
    v = chunks(v.astype(f32), dv)
    b = jnp.cumsum(chunks(log_a, dk), axis=2)
    b_last = b[:, :, -1:]
    q_dec = q * jnp.exp(b)
    k_intra = k * jnp.exp(-b)
    k_state = k * jnp.exp(b_last - b)

    causal = jnp.tril(jnp.ones((C, C), dtype=bool))
    scores = jnp.einsum('bnchk,bnshk->bnhcs', q_dec, k_intra)
    scores = jnp.where(causal, scores, 0.0)
    o_intra = jnp.einsum('bnhcs,bnshv->bnchv', scores, v)

    def step(state, inp):
        q_c, k_c, v_c, decay = inp
        o_c = jnp.einsum('bchk,bhkv->bchv', q_c, state)
        state = state * jnp.exp(decay)[..., None] + jnp.einsum('bchk,bchv->bhkv', k_c, v_c)
        return state, o_c

    xs = (jnp.moveaxis(q_dec, 1, 0), jnp.moveaxis(k_state, 1, 0),
          jnp.moveaxis(v, 1, 0), jnp.moveaxis(b_last[:, :, 0], 1, 0))
    s0 = jnp.zeros((B, H, dk, dv), f32)
    _, o_inter = lax.scan(step, s0, xs)
    o = o_intra + jnp.moveaxis(o_inter, 0, 1)
    o = rmsnorm(o, norm_g).reshape(B, T, D_MODEL)
    o = o * jax.nn.silu((r + b_r).astype(f32))
    return o.astype(h.dtype) @ w_o


def diff_mixer(h, cos, sin, w_in, lam_vecs, subln_g, w_o, lambda_init):
    B, T, _ = h.shape
    H, dh, dv = DIFF_HEADS, DIFF_DH, DIFF_DV
    f32 = jnp.float32
    q, k, v = jnp.split(h @ w_in, 3, axis=-1)
    q = rope_partial(q.reshape(B, T, H, 2, dh), cos, sin) * (dh ** -0.5)
    k = rope_partial(k.reshape(B, T, H, 2, dh), cos, sin)
    q = q.transpose(0, 2, 3, 1, 4)
    k = k.transpose(0, 2, 3, 1, 4)
    v = v.reshape(B, T, H, dv).transpose(0, 2, 1, 3)
    lf = lam_vecs.astype(f32)
    lam = jnp.exp(jnp.sum(lf[0] * lf[1])) - jnp.exp(jnp.sum(lf[2] * lf[3])) + lambda_init
    outs = []
    for i in range(T // Q_BLOCK):
        L = (i + 1) * Q_BLOCK
        qb = q[:, :, :, i * Q_BLOCK:L]
        s = jnp.einsum('bhiqd,bhikd->bhiqk', qb, k[:, :, :, :L]).astype(f32)
        q_idx = i * Q_BLOCK + jnp.arange(Q_BLOCK)
        mask = jnp.arange(L)[None, :] <= q_idx[:, None]
        p = jax.nn.softmax(jnp.where(mask, s, -jnp.inf), axis=-1)
        a = p[:, :, 0] - lam * p[:, :, 1]
        outs.append(jnp.einsum('bhqk,bhkv->bhqv', a.astype(v.dtype), v[:, :, :L]))
    o = jnp.concatenate(outs, axis=2)
    o = rmsnorm(o, subln_g) * (1.0 - lambda_init)
    o = o.transpose(0, 2, 1, 3).reshape(B, T, D_MODEL)
    return o @ w_o


def setup_inputs(seed: int = 0) -> dict:
    key = jax.random.key(seed)
    ks = jax.random.split(key, 24)
    D = D_MODEL
    nrm = jax.random.normal
    x = nrm(ks[0], (BATCH, SEQ, D), jnp.float32)
    c = nrm(ks[1], (BATCH, D), jnp.float32)
    offsets = jax.random.randint(ks[2], (BATCH, 1), 0, 4096, dtype=jnp.int32)
    positions = (offsets + jnp.arange(SEQ, dtype=jnp.int32)[None, :]).astype(jnp.int32)
    return {
        "x": x,
        "c": c,
        "positions": positions,
        "ada_w": nrm(ks[3], (DEPTH, D, 6 * D), jnp.float32) * (0.5 * D ** -0.5),
        "ada_b": nrm(ks[4], (DEPTH, 6 * D), jnp.float32) * 0.02,
        "norm_g": 1.0 + 0.05 * nrm(ks[5], (DEPTH, 2, D), jnp.float32),
        "mlp_w1": nrm(ks[6], (DEPTH, D, D_FF), jnp.float32) * D ** -0.5,
        "mlp_w2": nrm(ks[7], (DEPTH, D_FF, D), jnp.float32) * D_FF ** -0.5,
        "gla_w_in": nrm(ks[8], (N_GLA, D, GLA_IN), jnp.float32) * D ** -0.5,
        "gla_w_a2": nrm(ks[9], (N_GLA, GLA_GATE_RANK, GLA_HEADS * GLA_DK), jnp.float32) * GLA_GATE_RANK ** -0.5,
        "gla_b_a": 0.5 + 0.1 * nrm(ks[10], (N_GLA, GLA_HEADS * GLA_DK), jnp.float32),
        "gla_b_r": 0.02 * nrm(ks[11], (N_GLA, D), jnp.float32),
        "gla_norm_g": 1.0 + 0.05 * nrm(ks[12], (N_GLA, GLA_DV), jnp.float32),
        "gla_w_o": nrm(ks[13], (N_GLA, D, D), jnp.float32) * D ** -0.5,
        "diff_w_in": nrm(ks[14], (N_DIFF, D, DIFF_IN), jnp.float32) * D ** -0.5,
        "diff_lambda": 0.1 * nrm(ks[15], (N_DIFF, 4, DIFF_DH), jnp.float32),
        "diff_subln_g": 1.0 + 0.05 * nrm(ks[16], (N_DIFF, DIFF_DV), jnp.float32),
        "diff_w_o": nrm(ks[17], (N_DIFF, D, D), jnp.float32) * D ** -0.5,
        "final_g": 1.0 + 0.05 * nrm(ks[18], (D,), jnp.float32),
    }


def reference(x, c, positions, ada_w, ada_b, norm_g, mlp_w1, mlp_w2,
              gla_w_in, gla_w_a2, gla_b_a, gla_b_r, gla_norm_g, gla_w_o,
              diff_w_in, diff_lambda, diff_subln_g, diff_w_o, final_g):
    f32 = jnp.float32
    inv_freq = ROPE_THETA ** (-jnp.arange(0, ROPE_DIM, 2, dtype=f32) / ROPE_DIM)
    ang = positions.astype(f32)[..., None] * inv_freq
    cos = jnp.cos(ang)[:, :, None, None, :]
    sin = jnp.sin(ang)[:, :, None, None, :]
    c_act = jax.nn.silu(c)
    for i in range(DEPTH):
        mod = c_act @ ada_w[i] + ada_b[i]
        shift_t, scale_t, gate_t, shift_c, scale_c, gate_c = [
            m[:, None, :] for m in jnp.split(mod, 6, axis=-1)]
        h = rmsnorm(x, norm_g[i, 0]) * (1.0 + scale_t) + shift_t
        j = i // N_MIXERS
        if i % N_MIXERS == 0:
            y = gla_mixer(h, gla_w_in[j], gla_w_a2[j], gla_b_a[j], gla_b_r[j],
                          gla_norm_g[j], gla_w_o[j])
        else:
            lambda_init = 0.8 - 0.6 * math.exp(-0.3 * i)
            y = diff_mixer(h, cos, sin, diff_w_in[j], diff_lambda[j], diff_subln_g[j],
                           diff_w_o[j], lambda_init)
        x = x + gate_t * y
        h = rmsnorm(x, norm_g[i, 1]) * (1.0 + scale_c) + shift_c
        y = jnp.square(jax.nn.relu(h @ mlp_w1[i])) @ mlp_w2[i]
        x = x + gate_c * y
    return rmsnorm(x, final_g)
```

```python
import functools
import math

import jax
import jax.numpy as jnp
from jax import lax
from jax.experimental import pallas as pl
from jax.experimental.pallas import tpu as pltpu

D_MODEL = 1024
BATCH = 8
SEQ = 2048
DEPTH = 2
D_FF = 4 * D_MODEL
NORM_EPS = 1e-6
GLA_HEADS = 4
GLA_DK = 128
GLA_DV = 256
GLA_HK = GLA_HEADS * GLA_DK
GLA_GATE_RANK = 16
GLA_TAU = 16.0
GLA_CHUNK = 64
DIFF_HEADS = 8
DIFF_DH = 64
DIFF_DV = 128
ROPE_THETA = 500000.0
ROPE_DIM = DIFF_DH // 4
ROPE_HALF = ROPE_DIM // 2
N_TOK = BATCH * SEQ
N_MOD = 6

TM_INPROJ = 512
TM_MLP = 512
TF_MLP = 1024
TN_MOD = 1536
GLA_TC = 256
ATT_TQ = 256
ATT_TK = 256
VMEM_LIMIT = 56 * 1024 * 1024

F32 = jnp.float32
BF16 = jnp.bfloat16
NT_DIMS = (((1,), (1,)), ((), ()))
TN_DIMS = (((0,), (0,)), ((), ()))


def _dot(a, b):
    return jnp.dot(a, b, preferred_element_type=F32)


def _split_bf16(a):
    hi = a.astype(BF16)
    lo = (a - hi.astype(F32)).astype(BF16)
    return hi, lo


def _rms_scale(x):
    return lax.rsqrt(jnp.mean(x * x, axis=-1, keepdims=True) + NORM_EPS)


def _silu(x):
    return x * (1.0 / (1.0 + jnp.exp(-x)))


def _params(*sem):
    return pltpu.CompilerParams(dimension_semantics=sem, vmem_limit_bytes=VMEM_LIMIT)


def _mod_kernel(c_ref, w_ref, b_ref, o_ref):
    c = c_ref[...]
    c_hi, c_lo = _split_bf16(_silu(c))
    w_hi, w_lo = _split_bf16(w_ref[...])
    lhs = jnp.concatenate([c_hi, c_lo], axis=0)
    r = _dot(lhs, w_hi)
    o_ref[...] = r[:BATCH] + r[BATCH:] + _dot(c_hi, w_lo) + b_ref[...]


def _adaln_mod(c, ada_w, ada_b):
    n_out = N_MOD * D_MODEL
    return pl.pallas_call(
        _mod_kernel,
        grid=(DEPTH, n_out // TN_MOD),
        in_specs=[
            pl.BlockSpec((BATCH, D_MODEL), lambda i, j: (0, 0)),
            pl.BlockSpec((None, D_MODEL, TN_MOD), lambda i, j: (i, 0, j)),
            pl.BlockSpec((None, 1, TN_MOD), lambda i, j: (i, 0, j)),
        ],
        out_specs=pl.BlockSpec((None, BATCH, TN_MOD), lambda i, j: (i, 0, j)),
        out_shape=jax.ShapeDtypeStruct((DEPTH, BATCH, n_out), F32),
        compiler_params=_params("arbitrary", "arbitrary"),
        name="adaln_mod",
    )(c, ada_w, ada_b.reshape(DEPTH, 1, n_out))


def _mod_spec(layer, k, tm):
    tiles_per_seq = SEQ // tm

    def index(m, *_):
        return ((layer * BATCH + m // tiles_per_seq) * N_MOD + k, 0, 0)

    return pl.BlockSpec((None, 1, D_MODEL), index)


def _full_spec(shape):
    zeros = (0,) * len(shape)
    return pl.BlockSpec(shape, lambda *_: zeros)


def _modulated_norm(x, g, scale, shift):
    return (x * _rms_scale(x) * g) * (1.0 + scale) + shift


def _inproj_gla_kernel(x_ref, shift_ref, scale_ref, g_ref, wqk_ref, wv_ref, wr_ref, wa_ref,
                       wa2_ref, ba_ref, br_ref,
                       qd_ref, ki_ref, ks_ref, v_ref, gate_ref, dec_ref):
    hb = _modulated_norm(x_ref[...], g_ref[...], scale_ref[...], shift_ref[...]).astype(BF16)
    v_ref[...] = _dot(hb, wv_ref[...]).astype(BF16)
    r = _dot(hb, wr_ref[...]) + br_ref[...]
    gate_ref[...] = _silu(r).astype(BF16)

    a_hi, a_lo = _split_bf16(_dot(hb, wa_ref[...]))
    w_hi, w_lo = _split_bf16(wa2_ref[...])
    la = _dot(a_hi, w_hi) + (_dot(a_lo, w_hi) + _dot(a_hi, w_lo)) + ba_ref[...]
    log_a = (jnp.minimum(la, 0.0) - jnp.log(1.0 + jnp.exp(-jnp.abs(la)))) * (1.0 / GLA_TAU)
    l_hi, l_lo = _split_bf16(log_a)
    l_cat = jnp.concatenate([l_hi, l_lo], axis=1)

    qk = _dot(hb, wqk_ref[...])
    row = lax.broadcasted_iota(jnp.int32, (GLA_CHUNK, GLA_CHUNK), 0)
    col = lax.broadcasted_iota(jnp.int32, (GLA_CHUNK, GLA_CHUNK), 1)
    tril = jnp.where(row >= col, 1.0, 0.0).astype(BF16)
    q_scale = GLA_DK ** -0.5
    for n in range(x_ref.shape[0] // GLA_CHUNK):
        sl = slice(n * GLA_CHUNK, (n + 1) * GLA_CHUNK)
        bc = _dot(tril, l_cat[sl])
        b = bc[:, :GLA_HK] + bc[:, GLA_HK:]
        b_last = b[GLA_CHUNK - 1:GLA_CHUNK, :]
        q = qk[sl, :GLA_HK] * q_scale
        k = qk[sl, GLA_HK:]
        qd_ref[sl, :] = (q * jnp.exp(b)).astype(BF16)
        ki_ref[sl, :] = (k * jnp.exp(-b)).astype(BF16)
        ks_ref[sl, :] = (k * jnp.exp(b_last - b)).astype(BF16)
        dec_ref[n] = jnp.exp(b_last)


def _inproj_gla(x2, mod3, layer, g, wqk, wv, wr, wa, wa2, ba, br):
    tm = TM_INPROJ
    row = lambda m: (m, 0)
    return pl.pallas_call(
        _inproj_gla_kernel,
        grid=(N_TOK // tm,),
        in_specs=[
            pl.BlockSpec((tm, D_MODEL), row),
            _mod_spec(layer, 0, tm),
            _mod_spec(layer, 1, tm),
            _full_spec((1, D_MODEL)),
            _full_spec((D_MODEL, 2 * GLA_HK)),
            _full_spec((D_MODEL, D_MODEL)),
            _full_spec((D_MODEL, D_MODEL)),
            _full_spec((D_MODEL, GLA_GATE_RANK)),
            _full_spec((GLA_GATE_RANK, GLA_HK)),
            _full_spec((1, GLA_HK)),
            _full_spec((1, D_MODEL)),
        ],
        out_specs=[
            pl.BlockSpec((tm, GLA_HK), row),
            pl.BlockSpec((tm, GLA_HK), row),
            pl.BlockSpec((tm, GLA_HK), row),
            pl.BlockSpec((tm, D_MODEL), row),
            pl.BlockSpec((tm, D_MODEL), row),
            pl.BlockSpec((tm // GLA_CHUNK, 1, GLA_HK), lambda m: (m, 0, 0)),
        ],
        out_shape=[
            jax.ShapeDtypeStruct((N_TOK, GLA_HK), BF16),
            jax.ShapeDtypeStruct((N_TOK, GLA_HK), BF16),
            jax.ShapeDtypeStruct((N_TOK, GLA_HK), BF16),
            jax.ShapeDtypeStruct((N_TOK, D_MODEL), BF16),
            jax.ShapeDtypeStruct((N_TOK, D_MODEL), BF16),
            jax.ShapeDtypeStruct((N_TOK // GLA_CHUNK, 1, GLA_HK), F32),
        ],
        compiler_params=_params("arbitrary"),
        name="inproj_gla",
    )(x2, mod3, mod3, g, wqk, wv, wr, wa, wa2, ba, br)


def _gla_kernel(qd_ref, ki_ref, ks_ref, v_ref, gate_ref, dec_ref, ng_ref, o_ref, st_ref):
    @pl.when(pl.program_id(2) == 0)
    def _():
        st_ref[...] = jnp.zeros_like(st_ref)

    row = lax.broadcasted_iota(jnp.int32, (GLA_CHUNK, GLA_CHUNK), 0)
    col = lax.broadcasted_iota(jnp.int32, (GLA_CHUNK, GLA_CHUNK), 1)
    causal = row >= col
    for n in range(qd_ref.shape[0] // GLA_CHUNK):
        sl = slice(n * GLA_CHUNK, (n + 1) * GLA_CHUNK)
        qd = qd_ref[sl, :]
        v = v_ref[sl, :]
        s = lax.dot_general(qd, ki_ref[sl, :], NT_DIMS, preferred_element_type=F32)
        s = jnp.where(causal, s, 0.0).astype(BF16)
        st = st_ref[...]
        o = _dot(s, v) + lax.dot_general(qd, st.astype(BF16), NT_DIMS,
                                         preferred_element_type=F32)
        upd = lax.dot_general(v, ks_ref[sl, :], TN_DIMS, preferred_element_type=F32)
        st_ref[...] = st * dec_ref[n] + upd
        y = o * _rms_scale(o) * ng_ref[...]
        o_ref[sl, :] = (y * gate_ref[sl, :].astype(F32)).astype(BF16)


def _gla_core(qd, ki, ks, v, gate, dec, ng):
    tc = GLA_TC
    qk_spec = pl.BlockSpec((None, tc, GLA_DK), lambda b, h, t: (b, t, h))
    v_spec = pl.BlockSpec((None, tc, GLA_DV), lambda b, h, t: (b, t, h))
    return pl.pallas_call(
        _gla_kernel,
        grid=(BATCH, GLA_HEADS, SEQ // tc),
        in_specs=[
            qk_spec, qk_spec, qk_spec, v_spec, v_spec,
            pl.BlockSpec((None, tc // GLA_CHUNK, 1, GLA_DK), lambda b, h, t: (b, t, 0, h)),
            _full_spec((1, GLA_DV)),
        ],
        out_specs=v_spec,
        out_shape=jax.ShapeDtypeStruct((BATCH, SEQ, D_MODEL), BF16),
        scratch_shapes=[pltpu.VMEM((GLA_DV, GLA_DK), F32)],
        compiler_params=_params("arbitrary", "arbitrary", "arbitrary"),
        name="gla_core",
    )(qd.reshape(BATCH, SEQ, GLA_HK), ki.reshape(BATCH, SEQ, GLA_HK),
      ks.reshape(BATCH, SEQ, GLA_HK), v.reshape(BATCH, SEQ, D_MODEL),
      gate.reshape(BATCH, SEQ, D_MODEL),
      dec.reshape(BATCH, SEQ // GLA_CHUNK, 1, GLA_HK), ng)


def _rope_kernel(pos_ref, freq_ref, cos_ref, sin_ref):
    ang = pos_ref[...] * freq_ref[...]
    cos_ref[...] = jnp.cos(ang)
    sin_ref[...] = jnp.sin(ang)


def _rope_tables(positions):
    n_dense = N_TOK * ROPE_HALF // 128
    inv_freq = ROPE_THETA ** (-jnp.arange(0, ROPE_DIM, 2, dtype=F32) / ROPE_DIM)
    pos = jnp.broadcast_to(positions.astype(F32)[..., None], (BATCH, SEQ, ROPE_HALF))
    freq = jnp.tile(inv_freq, 128 // ROPE_HALF).reshape(1, 128)
    cos, sin = pl.pallas_call(
        _rope_kernel,
        grid=(1,),
        in_specs=[_full_spec((n_dense, 128)), _full_spec((1, 128))],
        out_specs=[_full_spec((n_dense, 128)), _full_spec((n_dense, 128))],
        out_shape=[jax.ShapeDtypeStruct((n_dense, 128), F32)] * 2,
        compiler_params=_params("arbitrary"),
        name="rope_tables",
    )(pos.reshape(n_dense, 128), freq)
    cos = cos.reshape(N_TOK, ROPE_HALF)
    sin = sin.reshape(N_TOK, ROPE_HALF)
    rest = DIFF_DH - ROPE_DIM
    ones = jnp.ones((N_TOK, rest), F32)
    zeros8 = jnp.zeros((N_TOK, ROPE_HALF), F32)
    zeros = jnp.zeros((N_TOK, rest), F32)
    ct = jnp.concatenate([cos, cos, ones] * 2, axis=1)
    s1 = jnp.concatenate([-sin, zeros8, zeros] * 2, axis=1)
    s2 = jnp.concatenate([zeros8, sin, zeros] * 2, axis=1)
    return ct, s1, s2


def _inproj_diff_kernel(x_ref, shift_ref, scale_ref, g_ref, wq_ref, wk_ref, wv_ref,
                        ct_ref, s1_ref, s2_ref, q_ref, k_ref, v_ref):
    hb = _modulated_norm(x_ref[...], g_ref[...], scale_ref[...], shift_ref[...]).astype(BF16)
    v_ref[...] = _dot(hb, wv_ref[...]).astype(BF16)
    ct = ct_ref[...]
    s1 = s1_ref[...]
    s2 = s2_ref[...]
    for w_ref, out_ref, scl in ((wq_ref, q_ref, DIFF_DH ** -0.5), (wk_ref, k_ref, 1.0)):
        z = _dot(hb, w_ref[...])
        for h in range(DIFF_HEADS):
            sl = slice(h * DIFF_DV, (h + 1) * DIFF_DV)
            xh = z[:, sl]
            rot = (xh * ct + pltpu.roll(xh, DIFF_DV - ROPE_HALF, 1) * s1
                   + pltpu.roll(xh, ROPE_HALF, 1) * s2)
            out_ref[:, sl] = (rot * scl).astype(BF16)


def _inproj_diff(x2, mod3, layer, g, wq, wk, wv, ct, s1, s2):
    tm = TM_INPROJ
    row = lambda m: (m, 0)
    w_spec = _full_spec((D_MODEL, D_MODEL))
    t_spec = pl.BlockSpec((tm, 128), row)
    o_spec = pl.BlockSpec((tm, D_MODEL), row)
    return pl.pallas_call(
        _inproj_diff_kernel,
        grid=(N_TOK // tm,),
        in_specs=[
            pl.BlockSpec((tm, D_MODEL), row),
            _mod_spec(layer, 0, tm),
            _mod_spec(layer, 1, tm),
            _full_spec((1, D_MODEL)),
            w_spec, w_spec, w_spec, t_spec, t_spec, t_spec,
        ],
        out_specs=[o_spec, o_spec, o_spec],
        out_shape=[jax.ShapeDtypeStruct((N_TOK, D_MODEL), BF16)] * 3,
        compiler_params=_params("arbitrary"),
        name="inproj_diff",
    )(x2, mod3, mod3, g, wq, wk, wv, ct, s1, s2)


def _attn_kernel(q_ref, k_ref, v_ref, lam_ref, g_ref, o_ref, qs_ref, m_ref, l_ref, acc_ref,
                 *, lambda_init):
    tq, tk = ATT_TQ, ATT_TK
    lf = lam_ref[...]
    lam = (jnp.exp(jnp.sum(lf[0:1] * lf[1:2], axis=-1, keepdims=True))
           - jnp.exp(jnp.sum(lf[2:3] * lf[3:4], axis=-1, keepdims=True)) + lambda_init)
    lane = lax.broadcasted_iota(jnp.int32, (tq, DIFF_DV), 1)
    q_pos = lax.broadcasted_iota(jnp.int32, (2 * tq, tk), 0) % tq
    k_pos = lax.broadcasted_iota(jnp.int32, (2 * tq, tk), 1)

    def kv_step(j, diag_offset):
        start = pl.multiple_of(j * tk, tk)
        kb = k_ref[pl.ds(start, tk), :]
        vb = v_ref[pl.ds(start, tk), :]
        s = lax.dot_general(qs_ref[...], kb, NT_DIMS, preferred_element_type=F32)
        if diag_offset is not None:
            s = jnp.where(k_pos + diag_offset <= q_pos, s, -1e30)
        m_prev = m_ref[...]
        m_new = jnp.maximum(m_prev, jnp.max(s, axis=-1, keepdims=True))
        p = jnp.exp(s - m_new)
        alpha = jnp.exp(m_prev - m_new)
        l_ref[...] = alpha * l_ref[...] + jnp.sum(p, axis=-1, keepdims=True)
        acc_ref[...] = alpha * acc_ref[...] + _dot(p.astype(BF16), vb)
        m_ref[...] = m_new

    def q_block(qi, carry):
        q0 = pl.multiple_of(qi * tq, tq)
        q = q_ref[pl.ds(q0, tq), :]
        qs_ref[0:tq, :] = jnp.where(lane < DIFF_DH, q, jnp.zeros_like(q))
        qs_ref[tq:2 * tq, :] = jnp.where(lane >= DIFF_DH, q, jnp.zeros_like(q))
        m_ref[...] = jnp.full_like(m_ref, -1e30)
        l_ref[...] = jnp.zeros_like(l_ref)
        acc_ref[...] = jnp.zeros_like(acc_ref)
        n_full = qi * (tq // tk)

        def body(j, c):
            kv_step(j, None)
            return c

        lax.fori_loop(0, n_full, body, 0)
        for jj in range(tq // tk):
            kv_step(n_full + jj, jj * tk)
        acc = acc_ref[...]
        l = l_ref[...]
        o = acc[:tq] / l[:tq] - lam * (acc[tq:] / l[tq:])
        y = o * _rms_scale(o) * g_ref[...] * (1.0 - lambda_init)
        o_ref[pl.ds(q0, tq), :] = y.astype(BF16)
        return carry

    lax.fori_loop(0, SEQ // tq, q_block, 0)


def _diff_attention(q, k, v, lam_vecs, g, lambda_init):
    spec = pl.BlockSpec((None, SEQ, DIFF_DV), lambda b, h: (b, 0, h))
    return pl.pallas_call(
        functools.partial(_attn_kernel, lambda_init=lambda_init),
        grid=(BATCH, DIFF_HEADS),
        in_specs=[spec, spec, spec, _full_spec((4, DIFF_DH)), _full_spec((1, DIFF_DV))],
        out_specs=spec,
        out_shape=jax.ShapeDtypeStruct((BATCH, SEQ, D_MODEL), BF16),
        scratch_shapes=[
            pltpu.VMEM((2 * ATT_TQ, DIFF_DV), BF16),
            pltpu.VMEM((2 * ATT_TQ, 1), F32),
            pltpu.VMEM((2 * ATT_TQ, 1), F32),
            pltpu.VMEM((2 * ATT_TQ, DIFF_DV), F32),
        ],
        compiler_params=_params("arbitrary", "arbitrary"),
        name="diff_attention",
    )(q.reshape(BATCH, SEQ, D_MODEL), k.reshape(BATCH, SEQ, D_MODEL),
      v.reshape(BATCH, SEQ, D_MODEL), lam_vecs, g)


def _mlp_kernel(x_ref, o_ref, wo_ref, gt_ref, sh_ref, sc_ref, gc_ref, g_ref, w1_ref, w2_ref,
                fg_ref, out_ref, x1_ref, h_ref, acc_ref, *, final):
    j = pl.program_id(1)

    @pl.when(j == 0)
    def _():
        x1 = x_ref[...] + gt_ref[...] * _dot(o_ref[...], wo_ref[...])
        x1_ref[...] = x1
        h_ref[...] = _modulated_norm(x1, g_ref[...], sc_ref[...], sh_ref[...]).astype(BF16)
        acc_ref[...] = jnp.zeros_like(acc_ref)

    a = jnp.square(jnp.maximum(_dot(h_ref[...], w1_ref[...]), 0.0)).astype(BF16)
    acc_ref[...] += _dot(a, w2_ref[...])

    @pl.when(j == pl.num_programs(1) - 1)
    def _():
        out = x1_ref[...] + gc_ref[...] * acc_ref[...]
        if final:
            out = out * _rms_scale(out) * fg_ref[...]
        out_ref[...] = out


def _outproj_mlp(x2, o2, mod3, layer, wo, g, w1, w2, fg, final):
    tm, tf = TM_MLP, TF_MLP
    row = lambda m, j: (m, 0)
    return pl.pallas_call(
        functools.partial(_mlp_kernel, final=final),
        grid=(N_TOK // tm, D_FF // tf),
        in_specs=[
            pl.BlockSpec((tm, D_MODEL), row),
            pl.BlockSpec((tm, D_MODEL), row),
            _full_spec((D_MODEL, D_MODEL)),
            _mod_spec(layer, 2, tm),
            _mod_spec(layer, 3, tm),
            _mod_spec(layer, 4, tm),
            _mod_spec(layer, 5, tm),
            _full_spec((1, D_MODEL)),
            pl.BlockSpec((D_MODEL, tf), lambda m, j: (0, j)),
            pl.BlockSpec((tf, D_MODEL), lambda m, j: (j, 0)),
            _full_spec((1, D_MODEL)),
        ],
        out_specs=pl.BlockSpec((tm, D_MODEL), row),
        out_shape=jax.ShapeDtypeStruct((N_TOK, D_MODEL), F32),
        scratch_shapes=[
            pltpu.VMEM((tm, D_MODEL), F32),
            pltpu.VMEM((tm, D_MODEL), BF16),
            pltpu.VMEM((tm, D_MODEL), F32),
        ],
        compiler_params=_params("arbitrary", "arbitrary"),
        name="outproj_mlp_final" if final else "outproj_mlp",
    )(x2, o2, wo, mod3, mod3, mod3, mod3, g, w1, w2, fg)


def kernel(x, c, positions, ada_w, ada_b, norm_g, mlp_w1, mlp_w2, gla_w_in, gla_w_a2, gla_b_a,
           gla_b_r, gla_norm_g, gla_w_o, diff_w_in, diff_lambda, diff_subln_g, diff_w_o, final_g):
    bf = lambda a: a.astype(BF16)
    x2 = x.reshape(N_TOK, D_MODEL)
    mod3 = _adaln_mod(c, ada_w, ada_b).reshape(DEPTH * BATCH * N_MOD, 1, D_MODEL)
    fg = final_g.reshape(1, D_MODEL)

    w_in = gla_w_in[0]
    qd, ki, ks, v, gate, dec = _inproj_gla(
        x2, mod3, 0, norm_g[0, 0].reshape(1, D_MODEL),
        bf(w_in[:, :2 * GLA_HK]), bf(w_in[:, 2 * GLA_HK:2 * GLA_HK + D_MODEL]),
        bf(w_in[:, 2 * GLA_HK + D_MODEL:2 * GLA_HK + 2 * D_MODEL]),
        bf(w_in[:, 2 * GLA_HK + 2 * D_MODEL:]),
        gla_w_a2[0], gla_b_a[0].reshape(1, GLA_HK), gla_b_r[0].reshape(1, D_MODEL))
    o = _gla_core(qd, ki, ks, v, gate, dec, gla_norm_g[0].reshape(1, GLA_DV))
    x2 = _outproj_mlp(x2, o.reshape(N_TOK, D_MODEL), mod3, 0, bf(gla_w_o[0]),
                      norm_g[0, 1].reshape(1, D_MODEL), bf(mlp_w1[0]), bf(mlp_w2[0]), fg, False)

    lambda_init = 0.8 - 0.6 * math.exp(-0.3 * 1)
    ct, s1, s2 = _rope_tables(positions)
    w_in = diff_w_in[0]
    q, k, v = _inproj_diff(
        x2, mod3, 1, norm_g[1, 0].reshape(1, D_MODEL),
        bf(w_in[:, :D_MODEL]), bf(w_in[:, D_MODEL:2 * D_MODEL]), bf(w_in[:, 2 * D_MODEL:]),
        ct, s1, s2)
    o = _diff_attention(q, k, v, diff_lambda[0], diff_subln_g[0].reshape(1, DIFF_DV), lambda_init)
    x2 = _outproj_mlp(x2, o.reshape(N_TOK, D_MODEL), mod3, 1, bf(diff_w_o[0]),
                      norm_g[1, 1].reshape(1, D_MODEL), bf(mlp_w1[1]), bf(mlp_w2[1]), fg, True)
    return x2.reshape(BATCH, SEQ, D_MODEL)
```

```python
import functools
import math

import jax
import jax.numpy as jnp
from jax import lax
from jax.experimental import pallas as pl
from jax.experimental.pallas import tpu as pltpu

D_MODEL = 1024
BATCH = 8
SEQ = 2048
DEPTH = 2
D_FF = 4 * D_MODEL
NORM_EPS = 1e-6
GLA_HEADS = 4
GLA_DK = 128
GLA_DV = 256
GLA_HK = GLA_HEADS * GLA_DK
GLA_GATE_RANK = 16
GLA_TAU = 16.0
GLA_CHUNK = 64
DIFF_HEADS = 8
DIFF_DH = 64
DIFF_DV = 128
ROPE_THETA = 500000.0
ROPE_DIM = DIFF_DH // 4
ROPE_HALF = ROPE_DIM // 2
N_TOK = BATCH * SEQ
N_MOD = 6

TM_INPROJ = 512
TM_MLP = 512
TF_MLP = 1024
TN_MOD = 1536
GLA_TC = 256
ATT_TQ = 256
ATT_TK = 256
ATT_HEADS = 8
VMEM_LIMIT = 56 * 1024 * 1024

F32 = jnp.float32
BF16 = jnp.bfloat16
NT_DIMS = (((1,), (1,)), ((), ()))
TN_DIMS = (((0,), (0,)), ((), ()))


def _dot(a, b):
    return jnp.dot(a, b, preferred_element_type=F32)


def _split_bf16(a):
    hi = a.astype(BF16)
    lo = (a - hi.astype(F32)).astype(BF16)
    return hi, lo


def _rms_scale(x):
    return lax.rsqrt(jnp.mean(x * x, axis=-1, keepdims=True) + NORM_EPS)


def _silu(x):
    return x * (1.0 / (1.0 + jnp.exp(-x)))


def _params(*sem):
    return pltpu.CompilerParams(dimension_semantics=sem, vmem_limit_bytes=VMEM_LIMIT)


def _mod_kernel(c_ref, w_ref, b_ref, o_ref):
    c = c_ref[...]
    c_hi, c_lo = _split_bf16(_silu(c))
    w_hi, w_lo = _split_bf16(w_ref[...])
    lhs = jnp.concatenate([c_hi, c_lo], axis=0)
    r = _dot(lhs, w_hi)
    o_ref[...] = r[:BATCH] + r[BATCH:] + _dot(c_hi, w_lo) + b_ref[...]


def _adaln_mod(c, ada_w, ada_b):
    n_out = N_MOD * D_MODEL
    return pl.pallas_call(
        _mod_kernel,
        grid=(DEPTH, n_out // TN_MOD),
        in_specs=[
            pl.BlockSpec((BATCH, D_MODEL), lambda i, j: (0, 0)),
            pl.BlockSpec((None, D_MODEL, TN_MOD), lambda i, j: (i, 0, j)),
            pl.BlockSpec((None, 1, TN_MOD), lambda i, j: (i, 0, j)),
        ],
        out_specs=pl.BlockSpec((None, BATCH, TN_MOD), lambda i, j: (i, 0, j)),
        out_shape=jax.ShapeDtypeStruct((DEPTH, BATCH, n_out), F32),
        compiler_params=_params("arbitrary", "arbitrary"),
        name="adaln_mod",
    )(c, ada_w, ada_b.reshape(DEPTH, 1, n_out))


def _mod_spec(layer, k, tm):
    tiles_per_seq = SEQ // tm

    def index(m, *_):
        return ((layer * BATCH + m // tiles_per_seq) * N_MOD + k, 0, 0)

    return pl.BlockSpec((None, 1, D_MODEL), index)


def _full_spec(shape):
    zeros = (0,) * len(shape)
    return pl.BlockSpec(shape, lambda *_: zeros)


def _modulated_norm(x, g, scale, shift):
    return (x * _rms_scale(x) * g) * (1.0 + scale) + shift


def _inproj_gla_kernel(x_ref, shift_ref, scale_ref, g_ref, wqk_ref, wv_ref, wr_ref, wa_ref,
                       wa2_ref, ba_ref, br_ref,
                       qd_ref, ki_ref, ks_ref, v_ref, gate_ref, dec_ref):
    hb = _modulated_norm(x_ref[...], g_ref[...], scale_ref[...], shift_ref[...]).astype(BF16)
    v_ref[...] = _dot(hb, wv_ref[...]).astype(BF16)
    r = _dot(hb, wr_ref[...]) + br_ref[...]
    gate_ref[...] = _silu(r).astype(BF16)

    a_hi, a_lo = _split_bf16(_dot(hb, wa_ref[...]))
    w_hi, w_lo = _split_bf16(wa2_ref[...])
    la = _dot(a_hi, w_hi) + (_dot(a_lo, w_hi) + _dot(a_hi, w_lo)) + ba_ref[...]
    log_a = (jnp.minimum(la, 0.0) - jnp.log(1.0 + jnp.exp(-jnp.abs(la)))) * (1.0 / GLA_TAU)
    l_hi, l_lo = _split_bf16(log_a)
    l_cat = jnp.concatenate([l_hi, l_lo], axis=1)

    qk = _dot(hb, wqk_ref[...])
    row = lax.broadcasted_iota(jnp.int32, (GLA_CHUNK, GLA_CHUNK), 0)
    col = lax.broadcasted_iota(jnp.int32, (GLA_CHUNK, GLA_CHUNK), 1)
    tril = jnp.where(row >= col, 1.0, 0.0).astype(BF16)
    q_scale = GLA_DK ** -0.5
    for n in range(x_ref.shape[0] // GLA_CHUNK):
        sl = slice(n * GLA_CHUNK, (n + 1) * GLA_CHUNK)
        bc = _dot(tril, l_cat[sl])
        b = bc[:, :GLA_HK] + bc[:, GLA_HK:]
        b_last = b[GLA_CHUNK - 1:GLA_CHUNK, :]
        q = qk[sl, :GLA_HK] * q_scale
        k = qk[sl, GLA_HK:]
        qd_ref[sl, :] = (q * jnp.exp(b)).astype(BF16)
        ki_ref[sl, :] = (k * jnp.exp(-b)).astype(BF16)
        ks_ref[sl, :] = (k * jnp.exp(b_last - b)).astype(BF16)
        dec_ref[n] = jnp.exp(b_last)


def _inproj_gla(x2, mod3, layer, g, wqk, wv, wr, wa, wa2, ba, br):
    tm = TM_INPROJ
    row = lambda m: (m, 0)
    return pl.pallas_call(
        _inproj_gla_kernel,
        grid=(N_TOK // tm,),
        in_specs=[
            pl.BlockSpec((tm, D_MODEL), row),
            _mod_spec(layer, 0, tm),
            _mod_spec(layer, 1, tm),
            _full_spec((1, D_MODEL)),
            _full_spec((D_MODEL, 2 * GLA_HK)),
            _full_spec((D_MODEL, D_MODEL)),
            _full_spec((D_MODEL, D_MODEL)),
            _full_spec((D_MODEL, GLA_GATE_RANK)),
            _full_spec((GLA_GATE_RANK, GLA_HK)),
            _full_spec((1, GLA_HK)),
            _full_spec((1, D_MODEL)),
        ],
        out_specs=[
            pl.BlockSpec((tm, GLA_HK), row),
            pl.BlockSpec((tm, GLA_HK), row),
            pl.BlockSpec((tm, GLA_HK), row),
            pl.BlockSpec((tm, D_MODEL), row),
            pl.BlockSpec((tm, D_MODEL), row),
            pl.BlockSpec((tm // GLA_CHUNK, 1, GLA_HK), lambda m: (m, 0, 0)),
        ],
        out_shape=[
            jax.ShapeDtypeStruct((N_TOK, GLA_HK), BF16),
            jax.ShapeDtypeStruct((N_TOK, GLA_HK), BF16),
            jax.ShapeDtypeStruct((N_TOK, GLA_HK), BF16),
            jax.ShapeDtypeStruct((N_TOK, D_MODEL), BF16),
            jax.ShapeDtypeStruct((N_TOK, D_MODEL), BF16),
            jax.ShapeDtypeStruct((N_TOK // GLA_CHUNK, 1, GLA_HK), F32),
        ],
        compiler_params=_params("arbitrary"),
        name="inproj_gla",
    )(x2, mod3, mod3, g, wqk, wv, wr, wa, wa2, ba, br)


def _gla_kernel(qd_ref, ki_ref, ks_ref, v_ref, gate_ref, dec_ref, ng_ref, o_ref, st_ref):
    @pl.when(pl.program_id(2) == 0)
    def _():
        st_ref[...] = jnp.zeros_like(st_ref)

    row = lax.broadcasted_iota(jnp.int32, (GLA_CHUNK, GLA_CHUNK), 0)
    col = lax.broadcasted_iota(jnp.int32, (GLA_CHUNK, GLA_CHUNK), 1)
    causal = row >= col
    for n in range(qd_ref.shape[0] // GLA_CHUNK):
        sl = slice(n * GLA_CHUNK, (n + 1) * GLA_CHUNK)
        qd = qd_ref[sl, :]
        v = v_ref[sl, :]
        s = lax.dot_general(qd, ki_ref[sl, :], NT_DIMS, preferred_element_type=F32)
        s = jnp.where(causal, s, 0.0).astype(BF16)
        st = st_ref[...]
        o = _dot(s, v) + lax.dot_general(qd, st.astype(BF16), NT_DIMS,
                                         preferred_element_type=F32)
        upd = lax.dot_general(v, ks_ref[sl, :], TN_DIMS, preferred_element_type=F32)
        st_ref[...] = st * dec_ref[n] + upd
        y = o * _rms_scale(o) * ng_ref[...]
        o_ref[sl, :] = (y * gate_ref[sl, :].astype(F32)).astype(BF16)


def _gla_core(qd, ki, ks, v, gate, dec, ng):
    tc = GLA_TC
    qk_spec = pl.BlockSpec((None, tc, GLA_DK), lambda b, h, t: (b, t, h))
    v_spec = pl.BlockSpec((None, tc, GLA_DV), lambda b, h, t: (b, t, h))
    return pl.pallas_call(
        _gla_kernel,
        grid=(BATCH, GLA_HEADS, SEQ // tc),
        in_specs=[
            qk_spec, qk_spec, qk_spec, v_spec, v_spec,
            pl.BlockSpec((None, tc // GLA_CHUNK, 1, GLA_DK), lambda b, h, t: (b, t, 0, h)),
            _full_spec((1, GLA_DV)),
        ],
        out_specs=v_spec,
        out_shape=jax.ShapeDtypeStruct((BATCH, SEQ, D_MODEL), BF16),
        scratch_shapes=[pltpu.VMEM((GLA_DV, GLA_DK), F32)],
        compiler_params=_params("arbitrary", "arbitrary", "arbitrary"),
        name="gla_core",
    )(qd.reshape(BATCH, SEQ, GLA_HK), ki.reshape(BATCH, SEQ, GLA_HK),
      ks.reshape(BATCH, SEQ, GLA_HK), v.reshape(BATCH, SEQ, D_MODEL),
      gate.reshape(BATCH, SEQ, D_MODEL),
      dec.reshape(BATCH, SEQ // GLA_CHUNK, 1, GLA_HK), ng)


def _rope_kernel(pos_ref, freq_ref, cos_ref, sin_ref):
    ang = pos_ref[...] * freq_ref[...]
    cos_ref[...] = jnp.cos(ang)
    sin_ref[...] = jnp.sin(ang)


def _rope_tables(positions):
    n_dense = N_TOK * ROPE_HALF // 128
    inv_freq = ROPE_THETA ** (-jnp.arange(0, ROPE_DIM, 2, dtype=F32) / ROPE_DIM)
    pos = jnp.broadcast_to(positions.astype(F32)[..., None], (BATCH, SEQ, ROPE_HALF))
    freq = jnp.tile(inv_freq, 128 // ROPE_HALF).reshape(1, 128)
    cos, sin = pl.pallas_call(
        _rope_kernel,
        grid=(1,),
        in_specs=[_full_spec((n_dense, 128)), _full_spec((1, 128))],
        out_specs=[_full_spec((n_dense, 128)), _full_spec((n_dense, 128))],
        out_shape=[jax.ShapeDtypeStruct((n_dense, 128), F32)] * 2,
        compiler_params=_params("arbitrary"),
        name="rope_tables",
    )(pos.reshape(n_dense, 128), freq)
    cos = cos.reshape(N_TOK, ROPE_HALF)
    sin = sin.reshape(N_TOK, ROPE_HALF)
    rest = DIFF_DH - ROPE_DIM
    ones = jnp.ones((N_TOK, rest), F32)
    zeros8 = jnp.zeros((N_TOK, ROPE_HALF), F32)
    zeros = jnp.zeros((N_TOK, rest), F32)
    ct = jnp.concatenate([cos, cos, ones] * 2, axis=1)
    s1 = jnp.concatenate([-sin, zeros8, zeros] * 2, axis=1)
    s2 = jnp.concatenate([zeros8, sin, zeros] * 2, axis=1)
    return ct, s1, s2


def _inproj_diff_kernel(x_ref, shift_ref, scale_ref, g_ref, wq_ref, wk_ref, wvt_ref,
                        ct_ref, s1_ref, s2_ref, q_ref, k_ref, vt_ref):
    hb = _modulated_norm(x_ref[...], g_ref[...], scale_ref[...], shift_ref[...]).astype(BF16)
    vt_ref[...] = lax.dot_general(wvt_ref[...], hb, NT_DIMS,
                                  preferred_element_type=F32).astype(BF16)
    ct = ct_ref[...]
    s1 = s1_ref[...]
    s2 = s2_ref[...]
    q_scale = DIFF_DH ** -0.5 * math.log2(math.e)
    for w_ref, out_ref, scl in ((wq_ref, q_ref, q_scale), (wk_ref, k_ref, 1.0)):
        z = _dot(hb, w_ref[...])
        for h in range(DIFF_HEADS):
            sl = slice(h * DIFF_DV, (h + 1) * DIFF_DV)
            xh = z[:, sl]
            rot = (xh * ct + pltpu.roll(xh, DIFF_DV - ROPE_HALF, 1) * s1
                   + pltpu.roll(xh, ROPE_HALF, 1) * s2)
            out_ref[:, sl] = (rot * scl).astype(BF16)


def _inproj_diff(x2, mod3, layer, g, wq, wk, wvt, ct, s1, s2):
    tm = TM_INPROJ
    row = lambda m: (m, 0)
    w_spec = _full_spec((D_MODEL, D_MODEL))
    t_spec = pl.BlockSpec((tm, 128), row)
    o_spec = pl.BlockSpec((tm, D_MODEL), row)
    return pl.pallas_call(
        _inproj_diff_kernel,
        grid=(N_TOK // tm,),
        in_specs=[
            pl.BlockSpec((tm, D_MODEL), row),
            _mod_spec(layer, 0, tm),
            _mod_spec(layer, 1, tm),
            _full_spec((1, D_MODEL)),
            w_spec, w_spec, w_spec, t_spec, t_spec, t_spec,
        ],
        out_specs=[o_spec, o_spec, pl.BlockSpec((D_MODEL, tm), lambda m: (0, m))],
        out_shape=[jax.ShapeDtypeStruct((N_TOK, D_MODEL), BF16)] * 2
        + [jax.ShapeDtypeStruct((D_MODEL, N_TOK), BF16)],
        compiler_params=_params("arbitrary"),
        name="inproj_diff",
    )(x2, mod3, mod3, g, wq, wk, wvt, ct, s1, s2)


def _attn_kernel(q_ref, k_ref, vt_ref, lam_ref, g_ref, o_ref, qs_ref, m_ref, l_ref, acc_ref,
                 *, lambda_init):
    tq, tk = ATT_TQ, ATT_TK
    lf = lam_ref[...]
    lam = (jnp.exp(jnp.sum(lf[0:1] * lf[1:2], axis=-1, keepdims=True))
           - jnp.exp(jnp.sum(lf[2:3] * lf[3:4], axis=-1, keepdims=True)) + lambda_init)
    lane = lax.broadcasted_iota(jnp.int32, (tq, DIFF_DV), 1)
    k_pos = lax.broadcasted_iota(jnp.int32, (tk, 2 * tq), 0)
    q_pos = lax.broadcasted_iota(jnp.int32, (tk, 2 * tq), 1) & (tq - 1)
    heads = [slice(h * DIFF_DV, (h + 1) * DIFF_DV) for h in range(ATT_HEADS)]

    def kv_step(j, diag_offset):
        start = pl.multiple_of(j * tk, tk)
        scores = [lax.dot_general(k_ref[pl.ds(start, tk), hs], qs_ref[h], NT_DIMS,
                                  preferred_element_type=F32) for h, hs in enumerate(heads)]
        for h, hs in enumerate(heads):
            vtb = vt_ref[hs, pl.ds(start, tk)]
            s = scores[h]
            if diag_offset is not None:
                s = jnp.where(k_pos + diag_offset <= q_pos, s, -1e30)
            m_prev = m_ref[h]
            m_new = jnp.maximum(m_prev, jnp.max(s, axis=0, keepdims=True))
            p = jnp.exp2(s - m_new)
            alpha = jnp.exp2(m_prev - m_new)
            l_ref[h] = alpha * l_ref[h] + jnp.sum(p, axis=0, keepdims=True)
            acc_ref[h] = alpha * acc_ref[h] + _dot(vtb, p.astype(BF16))
            m_ref[h] = m_new

    def q_block(qi, carry):
        q0 = pl.multiple_of(qi * tq, tq)
        for h, hs in enumerate(heads):
            q = q_ref[pl.ds(q0, tq), hs]
            qs_ref[h, 0:tq, :] = jnp.where(lane < DIFF_DH, q, jnp.zeros_like(q))
            qs_ref[h, tq:2 * tq, :] = jnp.where(lane >= DIFF_DH, q, jnp.zeros_like(q))
        m_ref[...] = jnp.full_like(m_ref, -1e30)
        l_ref[...] = jnp.zeros_like(l_ref)
        acc_ref[...] = jnp.zeros_like(acc_ref)
        n_full = qi * (tq // tk)

        def body(j, c):
            kv_step(j, None)
            return c

        lax.fori_loop(0, n_full, body, 0)
        for jj in range(tq // tk):
            kv_step(n_full + jj, jj * tk)
        for h, hs in enumerate(heads):
            acc = acc_ref[h]
            inv_l = 1.0 / l_ref[h]
            o = acc[:, :tq] * inv_l[:, :tq] - lam * (acc[:, tq:] * inv_l[:, tq:])
            rms = lax.rsqrt(jnp.mean(o * o, axis=0, keepdims=True) + NORM_EPS)
            y = o * rms * (g_ref[...] * (1.0 - lambda_init))
            o_ref[pl.ds(q0, tq), hs] = y.T.astype(BF16)
        return carry

    lax.fori_loop(0, SEQ // tq, q_block, 0)


def _diff_attention(q, k, vt, lam_vecs, g, lambda_init):
    width = ATT_HEADS * DIFF_DV
    spec = pl.BlockSpec((None, SEQ, width), lambda b, h: (b, 0, h))
    return pl.pallas_call(
        functools.partial(_attn_kernel, lambda_init=lambda_init),
        grid=(BATCH, DIFF_HEADS // ATT_HEADS),
        in_specs=[spec, spec, pl.BlockSpec((width, SEQ), lambda b, h: (h, b)),
                  _full_spec((4, DIFF_DH)), _full_spec((DIFF_DV, 1))],
        out_specs=spec,
        out_shape=jax.ShapeDtypeStruct((BATCH, SEQ, D_MODEL), BF16),
        scratch_shapes=[
            pltpu.VMEM((ATT_HEADS, 2 * ATT_TQ, DIFF_DV), BF16),
            pltpu.VMEM((ATT_HEADS, 1, 2 * ATT_TQ), F32),
            pltpu.VMEM((ATT_HEADS, 1, 2 * ATT_TQ), F32),
            pltpu.VMEM((ATT_HEADS, DIFF_DV, 2 * ATT_TQ), F32),
        ],
        compiler_params=_params("arbitrary", "arbitrary"),
        name="diff_attention",
    )(q.reshape(BATCH, SEQ, D_MODEL), k.reshape(BATCH, SEQ, D_MODEL), vt, lam_vecs, g)


def _mlp_kernel(x_ref, o_ref, wo_ref, gt_ref, sh_ref, sc_ref, gc_ref, g_ref, w1_ref, w2_ref,
                fg_ref, out_ref, x1_ref, h_ref, acc_ref, *, final):
    j = pl.program_id(1)

    @pl.when(j == 0)
    def _():
        x1 = x_ref[...] + gt_ref[...] * _dot(o_ref[...], wo_ref[...])
        x1_ref[...] = x1
        h_ref[...] = _modulated_norm(x1, g_ref[...], sc_ref[...], sh_ref[...]).astype(BF16)
        acc_ref[...] = jnp.zeros_like(acc_ref)

    a = jnp.square(jnp.maximum(_dot(h_ref[...], w1_ref[...]), 0.0)).astype(BF16)
    acc_ref[...] += _dot(a, w2_ref[...])

    @pl.when(j == pl.num_programs(1) - 1)
    def _():
        out = x1_ref[...] + gc_ref[...] * acc_ref[...]
        if final:
            out = out * _rms_scale(out) * fg_ref[...]
        out_ref[...] = out


def _outproj_mlp(x2, o2, mod3, layer, wo, g, w1, w2, fg, final):
    tm, tf = TM_MLP, TF_MLP
    row = lambda m, j: (m, 0)
    return pl.pallas_call(
        functools.partial(_mlp_kernel, final=final),
        grid=(N_TOK // tm, D_FF // tf),
        in_specs=[
            pl.BlockSpec((tm, D_MODEL), row),
            pl.BlockSpec((tm, D_MODEL), row),
            _full_spec((D_MODEL, D_MODEL)),
            _mod_spec(layer, 2, tm),
            _mod_spec(layer, 3, tm),
            _mod_spec(layer, 4, tm),
            _mod_spec(layer, 5, tm),
            _full_spec((1, D_MODEL)),
            pl.BlockSpec((D_MODEL, tf), lambda m, j: (0, j)),
            pl.BlockSpec((tf, D_MODEL), lambda m, j: (j, 0)),
            _full_spec((1, D_MODEL)),
        ],
        out_specs=pl.BlockSpec((tm, D_MODEL), row),
        out_shape=jax.ShapeDtypeStruct((N_TOK, D_MODEL), F32),
        scratch_shapes=[
            pltpu.VMEM((tm, D_MODEL), F32),
            pltpu.VMEM((tm, D_MODEL), BF16),
            pltpu.VMEM((tm, D_MODEL), F32),
        ],
        compiler_params=_params("arbitrary", "arbitrary"),
        name="outproj_mlp_final" if final else "outproj_mlp",
    )(x2, o2, wo, mod3, mod3, mod3, mod3, g, w1, w2, fg)


def kernel(x, c, positions, ada_w, ada_b, norm_g, mlp_w1, mlp_w2, gla_w_in, gla_w_a2, gla_b_a,
           gla_b_r, gla_norm_g, gla_w_o, diff_w_in, diff_lambda, diff_subln_g, diff_w_o, final_g):
    bf = lambda a: a.astype(BF16)
    x2 = x.reshape(N_TOK, D_MODEL)
    mod3 = _adaln_mod(c, ada_w, ada_b).reshape(DEPTH * BATCH * N_MOD, 1, D_MODEL)
    fg = final_g.reshape(1, D_MODEL)

    w_in = gla_w_in[0]
    qd, ki, ks, v, gate, dec = _inproj_gla(
        x2, mod3, 0, norm_g[0, 0].reshape(1, D_MODEL),
        bf(w_in[:, :2 * GLA_HK]), bf(w_in[:, 2 * GLA_HK:2 * GLA_HK + D_MODEL]),
        bf(w_in[:, 2 * GLA_HK + D_MODEL:2 * GLA_HK + 2 * D_MODEL]),
        bf(w_in[:, 2 * GLA_HK + 2 * D_MODEL:]),
        gla_w_a2[0], gla_b_a[0].reshape(1, GLA_HK), gla_b_r[0].reshape(1, D_MODEL))
    o = _gla_core(qd, ki, ks, v, gate, dec, gla_norm_g[0].reshape(1, GLA_DV))
    x2 = _outproj_mlp(x2, o.reshape(N_TOK, D_MODEL), mod3, 0, bf(gla_w_o[0]),
                      norm_g[0, 1].reshape(1, D_MODEL), bf(mlp_w1[0]), bf(mlp_w2[0]), fg, False)

    lambda_init = 0.8 - 0.6 * math.exp(-0.3 * 1)
    ct, s1, s2 = _rope_tables(positions)
    w_in = diff_w_in[0]
    q, k, vt = _inproj_diff(
        x2, mod3, 1, norm_g[1, 0].reshape(1, D_MODEL),
        bf(w_in[:, :D_MODEL]), bf(w_in[:, D_MODEL:2 * D_MODEL]), bf(w_in[:, 2 * D_MODEL:].T),
        ct, s1, s2)
    o = _diff_attention(q, k, vt, diff_lambda[0], diff_subln_g[0].reshape(DIFF_DV, 1), lambda_init)
    x2 = _outproj_mlp(x2, o.reshape(N_TOK, D_MODEL), mod3, 1, bf(diff_w_o[0]),
                      norm_g[1, 1].reshape(1, D_MODEL), bf(mlp_w1[1]), bf(mlp_w2[1]), fg, True)
    return x2.reshape(BATCH, SEQ, D_MODEL)
```

```python
import functools
import math

import jax
import jax.numpy as jnp
import numpy as np
from jax import lax
from jax.experimental import pallas as pl
from jax.experimental.pallas import tpu as pltpu

D_MODEL = 1024
BATCH = 8
SEQ = 2048
DEPTH = 2
D_FF = 4 * D_MODEL
NORM_EPS = 1e-6
GLA_HEADS = 4
GLA_DK = 128
GLA_DV = 256
GLA_HK = GLA_HEADS * GLA_DK
GLA_GATE_RANK = 16
GLA_TAU = 16.0
GLA_CHUNK = 64
DIFF_HEADS = 8
DIFF_DH = 64
DIFF_DV = 128
ROPE_THETA = 500000.0
ROPE_DIM = DIFF_DH // 4
ROPE_HALF = ROPE_DIM // 2
N_TOK = BATCH * SEQ
N_MOD = 6

TM_INPROJ = 512
TM_MLP = 512
TF_MLP = 1024
TN_MOD = 1536
GLA_TC = 512
ATT_TQ = 256
ATT_TK = 256
ATT_HEADS = 8
VMEM_LIMIT = 56 * 1024 * 1024

F32 = jnp.float32
BF16 = jnp.bfloat16
NT_DIMS = (((1,), (1,)), ((), ()))
TN_DIMS = (((0,), (0,)), ((), ()))


def _dot(a, b):
    return jnp.dot(a, b, preferred_element_type=F32)


def _split_bf16(a):
    hi = a.astype(BF16)
    lo = (a - hi.astype(F32)).astype(BF16)
    return hi, lo


def _rms_scale(x):
    return lax.rsqrt(jnp.mean(x * x, axis=-1, keepdims=True) + NORM_EPS)


def _silu(x):
    return x * (1.0 / (1.0 + jnp.exp(-x)))


def _params(*sem):
    return pltpu.CompilerParams(dimension_semantics=sem, vmem_limit_bytes=VMEM_LIMIT)


def _mod_kernel(c_ref, w_ref, b_ref, o_ref):
    c = c_ref[...]
    c_hi, c_lo = _split_bf16(_silu(c))
    w_hi, w_lo = _split_bf16(w_ref[...])
    lhs = jnp.concatenate([c_hi, c_lo], axis=0)
    r = _dot(lhs, w_hi)
    o_ref[...] = r[:BATCH] + r[BATCH:] + _dot(c_hi, w_lo) + b_ref[...]


def _adaln_mod(c, ada_w, ada_b):
    n_out = N_MOD * D_MODEL
    return pl.pallas_call(
        _mod_kernel,
        grid=(DEPTH, n_out // TN_MOD),
        in_specs=[
            pl.BlockSpec((BATCH, D_MODEL), lambda i, j: (0, 0)),
            pl.BlockSpec((None, D_MODEL, TN_MOD), lambda i, j: (i, 0, j)),
            pl.BlockSpec((None, 1, TN_MOD), lambda i, j: (i, 0, j)),
        ],
        out_specs=pl.BlockSpec((None, BATCH, TN_MOD), lambda i, j: (i, 0, j)),
        out_shape=jax.ShapeDtypeStruct((DEPTH, BATCH, n_out), F32),
        compiler_params=_params("arbitrary", "arbitrary"),
        name="adaln_mod",
    )(c, ada_w, ada_b.reshape(DEPTH, 1, n_out))


def _mod_spec(layer, k, tm):
    tiles_per_seq = SEQ // tm

    def index(m, *_):
        return ((layer * BATCH + m // tiles_per_seq) * N_MOD + k, 0, 0)

    return pl.BlockSpec((None, 1, D_MODEL), index)


def _full_spec(shape):
    zeros = (0,) * len(shape)
    return pl.BlockSpec(shape, lambda *_: zeros)


def _modulated_norm(x, g, scale, shift):
    return (x * _rms_scale(x) * g) * (1.0 + scale) + shift


def _inproj_gla_kernel(x_ref, shift_ref, scale_ref, g_ref, wqk_ref, wv_ref, wr_ref, wa_ref,
                       wa2_ref, ba_ref, br_ref,
                       qd_ref, ki_ref, ks_ref, v_ref, gate_ref, dec_ref):
    hb = _modulated_norm(x_ref[...], g_ref[...], scale_ref[...], shift_ref[...]).astype(BF16)
    v_ref[...] = _dot(hb, wv_ref[...]).astype(BF16)
    r = _dot(hb, wr_ref[...]) + br_ref[...]
    gate_ref[...] = _silu(r).astype(BF16)

    a_hi, a_lo = _split_bf16(_dot(hb, wa_ref[...]))
    w_hi, w_lo = _split_bf16(wa2_ref[...])
    la = _dot(a_hi, w_hi) + (_dot(a_lo, w_hi) + _dot(a_hi, w_lo)) + ba_ref[...]
    log_a = (jnp.minimum(la, 0.0) - jnp.log(1.0 + jnp.exp(-jnp.abs(la)))) * (1.0 / GLA_TAU)
    l_hi, l_lo = _split_bf16(log_a)
    l_cat = jnp.concatenate([l_hi, l_lo], axis=1)

    qk = _dot(hb, wqk_ref[...])
    row = lax.broadcasted_iota(jnp.int32, (GLA_CHUNK, GLA_CHUNK), 0)
    col = lax.broadcasted_iota(jnp.int32, (GLA_CHUNK, GLA_CHUNK), 1)
    tril = jnp.where(row >= col, 1.0, 0.0).astype(BF16)
    q_scale = GLA_DK ** -0.5
    for n in range(x_ref.shape[0] // GLA_CHUNK):
        sl = slice(n * GLA_CHUNK, (n + 1) * GLA_CHUNK)
        bc = _dot(tril, l_cat[sl])
        b = bc[:, :GLA_HK] + bc[:, GLA_HK:]
        b_last = b[GLA_CHUNK - 1:GLA_CHUNK, :]
        q = qk[sl, :GLA_HK] * q_scale
        k = qk[sl, GLA_HK:]
        qd_ref[sl, :] = (q * jnp.exp(b)).astype(BF16)
        ki_ref[sl, :] = (k * jnp.exp(-b)).astype(BF16)
        ks_ref[sl, :] = (k * jnp.exp(b_last - b)).astype(BF16)
        dec_ref[n] = jnp.exp(b_last)


def _inproj_gla(x2, mod3, layer, g, w_in, wa, wa2, ba, br):
    tm = TM_INPROJ
    row = lambda m: (m, 0)
    assert 2 * GLA_HK == D_MODEL
    col_block = lambda width, idx: pl.BlockSpec((None, D_MODEL, width), lambda m: (0, 0, idx))
    return pl.pallas_call(
        _inproj_gla_kernel,
        grid=(N_TOK // tm,),
        in_specs=[
            pl.BlockSpec((tm, D_MODEL), row),
            _mod_spec(layer, 0, tm),
            _mod_spec(layer, 1, tm),
            _full_spec((1, D_MODEL)),
            col_block(D_MODEL, 0),
            col_block(D_MODEL, 1),
            col_block(D_MODEL, 2),
            _full_spec((D_MODEL, GLA_GATE_RANK)),
            _full_spec((GLA_GATE_RANK, GLA_HK)),
            _full_spec((1, GLA_HK)),
            _full_spec((1, D_MODEL)),
        ],
        out_specs=[
            pl.BlockSpec((tm, GLA_HK), row),
            pl.BlockSpec((tm, GLA_HK), row),
            pl.BlockSpec((tm, GLA_HK), row),
            pl.BlockSpec((tm, D_MODEL), row),
            pl.BlockSpec((tm, D_MODEL), row),
            pl.BlockSpec((tm // GLA_CHUNK, 1, GLA_HK), lambda m: (m, 0, 0)),
        ],
        out_shape=[
            jax.ShapeDtypeStruct((N_TOK, GLA_HK), BF16),
            jax.ShapeDtypeStruct((N_TOK, GLA_HK), BF16),
            jax.ShapeDtypeStruct((N_TOK, GLA_HK), BF16),
            jax.ShapeDtypeStruct((N_TOK, D_MODEL), BF16),
            jax.ShapeDtypeStruct((N_TOK, D_MODEL), BF16),
            jax.ShapeDtypeStruct((N_TOK // GLA_CHUNK, 1, GLA_HK), F32),
        ],
        compiler_params=_params("arbitrary"),
        name="inproj_gla",
    )(x2, mod3, mod3, g, w_in, w_in, w_in, wa, wa2, ba, br)


def _gla_kernel(qd_ref, ki_ref, ks_ref, v_ref, gate_ref, dec_ref, ng_ref, o_ref, st_ref):
    @pl.when(pl.program_id(1) == 0)
    def _():
        st_ref[...] = jnp.zeros_like(st_ref)

    row = lax.broadcasted_iota(jnp.int32, (GLA_CHUNK, GLA_CHUNK), 0)
    col = lax.broadcasted_iota(jnp.int32, (GLA_CHUNK, GLA_CHUNK), 1)
    causal = row >= col
    n_chunks = qd_ref.shape[0] // GLA_CHUNK
    work = [(h, n) for h in range(GLA_HEADS) for n in range(n_chunks)]
    rows = lambda n: slice(n * GLA_CHUNK, (n + 1) * GLA_CHUNK)
    kcols = lambda h: slice(h * GLA_DK, (h + 1) * GLA_DK)
    vcols = lambda h: slice(h * GLA_DV, (h + 1) * GLA_DV)

    scores = {}
    updates = {}
    for h, n in work:
        scores[h, n] = lax.dot_general(qd_ref[rows(n), kcols(h)], ki_ref[rows(n), kcols(h)],
                                       NT_DIMS, preferred_element_type=F32)
        updates[h, n] = lax.dot_general(v_ref[rows(n), vcols(h)], ks_ref[rows(n), kcols(h)],
                                        TN_DIMS, preferred_element_type=F32)
    states = {}
    for h in range(GLA_HEADS):
        st = st_ref[h]
        for n in range(n_chunks):
            states[h, n] = st.astype(BF16)
            st = st * dec_ref[n][:, kcols(h)] + updates[h, n]
        st_ref[h] = st
    for h, n in work:
        s = jnp.where(causal, scores[h, n], 0.0).astype(BF16)
        o = _dot(s, v_ref[rows(n), vcols(h)]) + lax.dot_general(
            qd_ref[rows(n), kcols(h)], states[h, n], NT_DIMS, preferred_element_type=F32)
        y = o * _rms_scale(o) * ng_ref[...]
        o_ref[rows(n), vcols(h)] = (y * gate_ref[rows(n), vcols(h)].astype(F32)).astype(BF16)


def _gla_core(qd, ki, ks, v, gate, dec, ng):
    tc = GLA_TC
    qk_spec = pl.BlockSpec((None, tc, GLA_HK), lambda b, t: (b, t, 0))
    v_spec = pl.BlockSpec((None, tc, D_MODEL), lambda b, t: (b, t, 0))
    return pl.pallas_call(
        _gla_kernel,
        grid=(BATCH, SEQ // tc),
        in_specs=[
            qk_spec, qk_spec, qk_spec, v_spec, v_spec,
            pl.BlockSpec((None, tc // GLA_CHUNK, 1, GLA_HK), lambda b, t: (b, t, 0, 0)),
            _full_spec((1, GLA_DV)),
        ],
        out_specs=v_spec,
        out_shape=jax.ShapeDtypeStruct((BATCH, SEQ, D_MODEL), BF16),
        scratch_shapes=[pltpu.VMEM((GLA_HEADS, GLA_DV, GLA_DK), F32)],
        compiler_params=_params("arbitrary", "arbitrary"),
        name="gla_core",
    )(qd.reshape(BATCH, SEQ, GLA_HK), ki.reshape(BATCH, SEQ, GLA_HK),
      ks.reshape(BATCH, SEQ, GLA_HK), v.reshape(BATCH, SEQ, D_MODEL),
      gate.reshape(BATCH, SEQ, D_MODEL),
      dec.reshape(BATCH, SEQ // GLA_CHUNK, 1, GLA_HK), ng)


def _rope_kernel(pos_ref, freq_ref, cs_ref):
    ang = freq_ref[...] * pos_ref[...]
    cs_ref[0:ROPE_HALF, :] = jnp.cos(ang)
    cs_ref[ROPE_HALF:, :] = jnp.sin(ang)


def _rope_tables(positions):
    inv_freq = ROPE_THETA ** (-jnp.arange(0, ROPE_DIM, 2, dtype=F32) / ROPE_DIM)
    return pl.pallas_call(
        _rope_kernel,
        grid=(1,),
        in_specs=[_full_spec((1, N_TOK)), _full_spec((ROPE_HALF, 1))],
        out_specs=_full_spec((ROPE_DIM, N_TOK)),
        out_shape=jax.ShapeDtypeStruct((ROPE_DIM, N_TOK), F32),
        compiler_params=_params("arbitrary"),
        name="rope_tables",
    )(positions.astype(F32).reshape(1, N_TOK), inv_freq.reshape(ROPE_HALF, 1))


def _rope_expander():
    e = np.zeros((3, ROPE_DIM, 2 * DIFF_DV), np.float32)
    for lane in range(DIFF_DV):
        d = lane % DIFF_DH
        if d < ROPE_DIM:
            e[:, d % ROPE_HALF, lane] = 1.0
            e[:, ROPE_HALF + d % ROPE_HALF, DIFF_DV + lane] = -1.0 if d < ROPE_HALF else 1.0
    return jnp.asarray(e.reshape(3 * ROPE_DIM, 2 * DIFF_DV), BF16)


def _inproj_diff_kernel(x_ref, shift_ref, scale_ref, g_ref, wq_ref, wk_ref, wvt_ref,
                        cs_ref, e_ref, q_ref, k_ref, vt_ref):
    hb = _modulated_norm(x_ref[...], g_ref[...], scale_ref[...], shift_ref[...]).astype(BF16)
    vt_ref[...] = lax.dot_general(wvt_ref[...], hb, NT_DIMS,
                                  preferred_element_type=F32).astype(BF16)

    cs = cs_ref[...]
    hi = cs.astype(BF16).astype(F32)
    mid = (cs - hi).astype(BF16).astype(F32)
    lo = cs - hi - mid
    parts = jnp.concatenate([hi, mid, lo], axis=0)
    tab = _dot(parts.T.astype(BF16), e_ref[...])
    d = lax.broadcasted_iota(jnp.int32, (1, DIFF_DV), 1) & (DIFF_DH - 1)
    ct = tab[:, :DIFF_DV] + jnp.where(d >= ROPE_DIM, 1.0, 0.0)
    s12 = tab[:, DIFF_DV:]
    first_half = d < ROPE_HALF

    q_scale = DIFF_DH ** -0.5 * math.log2(math.e)
    for w_ref, out_ref, scl in ((wq_ref, q_ref, q_scale), (wk_ref, k_ref, 1.0)):
        z = _dot(hb, w_ref[...])
        for h in range(DIFF_HEADS):
            sl = slice(h * DIFF_DV, (h + 1) * DIFF_DV)
            xh = z[:, sl]
            partner = jnp.where(first_half, pltpu.roll(xh, DIFF_DV - ROPE_HALF, 1),
                                pltpu.roll(xh, ROPE_HALF, 1))
            out_ref[:, sl] = ((xh * ct + partner * s12) * scl).astype(BF16)


def _inproj_diff(x2, mod3, layer, g, w_in, wvt, cs, expander):
    tm = TM_INPROJ
    row = lambda m: (m, 0)
    o_spec = pl.BlockSpec((tm, D_MODEL), row)
    return pl.pallas_call(
        _inproj_diff_kernel,
        grid=(N_TOK // tm,),
        in_specs=[
            pl.BlockSpec((tm, D_MODEL), row),
            _mod_spec(layer, 0, tm),
            _mod_spec(layer, 1, tm),
            _full_spec((1, D_MODEL)),
            pl.BlockSpec((None, D_MODEL, D_MODEL), lambda m: (0, 0, 0)),
            pl.BlockSpec((None, D_MODEL, D_MODEL), lambda m: (0, 0, 1)),
            _full_spec((D_MODEL, D_MODEL)),
            pl.BlockSpec((ROPE_DIM, tm), lambda m: (0, m)),
            _full_spec((3 * ROPE_DIM, 2 * DIFF_DV)),
        ],
        out_specs=[o_spec, o_spec, pl.BlockSpec((D_MODEL, tm), lambda m: (0, m))],
        out_shape=[jax.ShapeDtypeStruct((N_TOK, D_MODEL), BF16)] * 2
        + [jax.ShapeDtypeStruct((D_MODEL, N_TOK), BF16)],
        compiler_params=_params("arbitrary"),
        name="inproj_diff",
    )(x2, mod3, mod3, g, w_in, w_in, wvt, cs, expander)


def _attn_kernel(q_ref, k_ref, vt_ref, lam_ref, g_ref, o_ref, qs_ref, m_ref, l_ref, acc_ref,
                 *, lambda_init):
    tq, tk = ATT_TQ, ATT_TK
    lf = lam_ref[...]
    lam = (jnp.exp(jnp.sum(lf[0:1] * lf[1:2], axis=-1, keepdims=True))
           - jnp.exp(jnp.sum(lf[2:3] * lf[3:4], axis=-1, keepdims=True)) + lambda_init)
    lane = lax.broadcasted_iota(jnp.int32, (tq, DIFF_DV), 1)
    k_pos = lax.broadcasted_iota(jnp.int32, (tk, 2 * tq), 0)
    q_pos = lax.broadcasted_iota(jnp.int32, (tk, 2 * tq), 1) & (tq - 1)
    heads = [slice(h * DIFF_DV, (h + 1) * DIFF_DV) for h in range(ATT_HEADS)]

    def kv_step(j, diag_offset):
        start = pl.multiple_of(j * tk, tk)
        scores = [lax.dot_general(k_ref[pl.ds(start, tk), hs], qs_ref[h], NT_DIMS,
                                  preferred_element_type=F32) for h, hs in enumerate(heads)]
        for h, hs in enumerate(heads):
            vtb = vt_ref[hs, pl.ds(start, tk)]
            s = scores[h]
            if diag_offset is not None:
                s = jnp.where(k_pos + diag_offset <= q_pos, s, -1e30)
            m_prev = m_ref[h]
            m_new = jnp.maximum(m_prev, jnp.max(s, axis=0, keepdims=True))
            p = jnp.exp2(s - m_new)
            alpha = jnp.exp2(m_prev - m_new)
            l_ref[h] = alpha * l_ref[h] + jnp.sum(p, axis=0, keepdims=True)
            acc_ref[h] = alpha * acc_ref[h] + _dot(vtb, p.astype(BF16))
            m_ref[h] = m_new

    def q_block(qi, carry):
        q0 = pl.multiple_of(qi * tq, tq)
        for h, hs in enumerate(heads):
            q = q_ref[pl.ds(q0, tq), hs]
            qs_ref[h, 0:tq, :] = jnp.where(lane < DIFF_DH, q, jnp.zeros_like(q))
            qs_ref[h, tq:2 * tq, :] = jnp.where(lane >= DIFF_DH, q, jnp.zeros_like(q))
        m_ref[...] = jnp.full_like(m_ref, -1e30)
        l_ref[...] = jnp.zeros_like(l_ref)
        acc_ref[...] = jnp.zeros_like(acc_ref)
        n_full = qi * (tq // tk)

        def body(j, c):
            kv_step(j, None)
            return c

        lax.fori_loop(0, n_full, body, 0)
        for jj in range(tq // tk):
            kv_step(n_full + jj, jj * tk)
        for h, hs in enumerate(heads):
            acc = acc_ref[h]
            inv_l = 1.0 / l_ref[h]
            o = acc[:, :tq] * inv_l[:, :tq] - lam * (acc[:, tq:] * inv_l[:, tq:])
            rms = lax.rsqrt(jnp.mean(o * o, axis=0, keepdims=True) + NORM_EPS)
            y = o * rms * (g_ref[...] * (1.0 - lambda_init))
            o_ref[pl.ds(q0, tq), hs] = y.T.astype(BF16)
        return carry

    lax.fori_loop(0, SEQ // tq, q_block, 0)


def _diff_attention(q, k, vt, lam_vecs, g, lambda_init):
    width = ATT_HEADS * DIFF_DV
    spec = pl.BlockSpec((None, SEQ, width), lambda b, h: (b, 0, h))
    return pl.pallas_call(
        functools.partial(_attn_kernel, lambda_init=lambda_init),
        grid=(BATCH, DIFF_HEADS // ATT_HEADS),
        in_specs=[spec, spec, pl.BlockSpec((width, SEQ), lambda b, h: (h, b)),
                  _full_spec((4, DIFF_DH)), _full_spec((DIFF_DV, 1))],
        out_specs=spec,
        out_shape=jax.ShapeDtypeStruct((BATCH, SEQ, D_MODEL), BF16),
        scratch_shapes=[
            pltpu.VMEM((ATT_HEADS, 2 * ATT_TQ, DIFF_DV), BF16),
            pltpu.VMEM((ATT_HEADS, 1, 2 * ATT_TQ), F32),
            pltpu.VMEM((ATT_HEADS, 1, 2 * ATT_TQ), F32),
            pltpu.VMEM((ATT_HEADS, DIFF_DV, 2 * ATT_TQ), F32),
        ],
        compiler_params=_params("arbitrary", "arbitrary"),
        name="diff_attention",
    )(q.reshape(BATCH, SEQ, D_MODEL), k.reshape(BATCH, SEQ, D_MODEL), vt, lam_vecs, g)


def _mlp_kernel(x_ref, o_ref, wo_ref, gt_ref, sh_ref, sc_ref, gc_ref, g_ref, w1_ref, w2_ref,
                fg_ref, out_ref, x1_ref, h_ref, acc_ref, *, final):
    j = pl.program_id(1)

    @pl.when(j == 0)
    def _():
        x1 = x_ref[...] + gt_ref[...] * _dot(o_ref[...], wo_ref[...])
        x1_ref[...] = x1
        h_ref[...] = _modulated_norm(x1, g_ref[...], sc_ref[...], sh_ref[...]).astype(BF16)
        acc_ref[...] = jnp.zeros_like(acc_ref)

    a = jnp.square(jnp.maximum(_dot(h_ref[...], w1_ref[...]), 0.0)).astype(BF16)
    acc_ref[...] += _dot(a, w2_ref[...])

    @pl.when(j == pl.num_programs(1) - 1)
    def _():
        out = x1_ref[...] + gc_ref[...] * acc_ref[...]
        if final:
            out = out * _rms_scale(out) * fg_ref[...]
        out_ref[...] = out


def _outproj_mlp(x2, o2, mod3, layer, wo, g, w1, w2, fg, final):
    tm, tf = TM_MLP, TF_MLP
    row = lambda m, j: (m, 0)
    return pl.pallas_call(
        functools.partial(_mlp_kernel, final=final),
        grid=(N_TOK // tm, D_FF // tf),
        in_specs=[
            pl.BlockSpec((tm, D_MODEL), row),
            pl.BlockSpec((tm, D_MODEL), row),
            pl.BlockSpec((None, D_MODEL, D_MODEL), lambda m, j: (0, 0, 0)),
            _mod_spec(layer, 2, tm),
            _mod_spec(layer, 3, tm),
            _mod_spec(layer, 4, tm),
            _mod_spec(layer, 5, tm),
            _full_spec((1, D_MODEL)),
            pl.BlockSpec((None, D_MODEL, tf), lambda m, j: (layer, 0, j)),
            pl.BlockSpec((None, tf, D_MODEL), lambda m, j: (layer, j, 0)),
            _full_spec((1, D_MODEL)),
        ],
        out_specs=pl.BlockSpec((tm, D_MODEL), row),
        out_shape=jax.ShapeDtypeStruct((N_TOK, D_MODEL), F32),
        scratch_shapes=[
            pltpu.VMEM((tm, D_MODEL), F32),
            pltpu.VMEM((tm, D_MODEL), BF16),
            pltpu.VMEM((tm, D_MODEL), F32),
        ],
        compiler_params=_params("arbitrary", "arbitrary"),
        name="outproj_mlp_final" if final else "outproj_mlp",
    )(x2, o2, wo, mod3, mod3, mod3, mod3, g, w1, w2, fg)


def kernel(x, c, positions, ada_w, ada_b, norm_g, mlp_w1, mlp_w2, gla_w_in, gla_w_a2, gla_b_a,
           gla_b_r, gla_norm_g, gla_w_o, diff_w_in, diff_lambda, diff_subln_g, diff_w_o, final_g):
    bf = lambda a: a.astype(BF16)
    w1, w2 = bf(mlp_w1), bf(mlp_w2)
    x2 = x.reshape(N_TOK, D_MODEL)
    mod3 = _adaln_mod(c, ada_w, ada_b).reshape(DEPTH * BATCH * N_MOD, 1, D_MODEL)
    fg = final_g.reshape(1, D_MODEL)

    qd, ki, ks, v, gate, dec = _inproj_gla(
        x2, mod3, 0, norm_g[0, 0].reshape(1, D_MODEL), bf(gla_w_in),
        bf(gla_w_in[0, :, 3 * D_MODEL:]), gla_w_a2[0], gla_b_a[0].reshape(1, GLA_HK), gla_b_r[0].reshape(1, D_MODEL))
    o = _gla_core(qd, ki, ks, v, gate, dec, gla_norm_g[0].reshape(1, GLA_DV))
    x2 = _outproj_mlp(x2, o.reshape(N_TOK, D_MODEL), mod3, 0, bf(gla_w_o),
                      norm_g[0, 1].reshape(1, D_MODEL), w1, w2, fg, False)

    lambda_init = 0.8 - 0.6 * math.exp(-0.3 * 1)
    cs = _rope_tables(positions)
    q, k, vt = _inproj_diff(
        x2, mod3, 1, norm_g[1, 0].reshape(1, D_MODEL), bf(diff_w_in),
        bf(diff_w_in[0, :, 2 * D_MODEL:].T), cs, _rope_expander())
    o = _diff_attention(q, k, vt, diff_lambda[0], diff_subln_g[0].reshape(DIFF_DV, 1), lambda_init)
    x2 = _outproj_mlp(x2, o.reshape(N_TOK, D_MODEL), mod3, 1, bf(diff_w_o),
                      norm_g[1, 1].reshape(1, D_MODEL), w1, w2, fg, True)
    return x2.reshape(BATCH, SEQ, D_MODEL)
```

```python
import functools
import math

import jax
import jax.numpy as jnp
import numpy as np
from jax import lax
from jax.experimental import pallas as pl
from jax.experimental.pallas import tpu as pltpu

D_MODEL = 1024
BATCH = 8
SEQ = 2048
DEPTH = 2
D_FF = 4 * D_MODEL
NORM_EPS = 1e-6
GLA_HEADS = 4
GLA_DK = 128
GLA_DV = 256
GLA_HK = GLA_HEADS * GLA_DK
GLA_GATE_RANK = 16
GLA_TAU = 16.0
GLA_CHUNK = 64
DIFF_HEADS = 8
DIFF_DH = 64
DIFF_DV = 128
ROPE_THETA = 500000.0
ROPE_DIM = DIFF_DH // 4
ROPE_HALF = ROPE_DIM // 2
N_TOK = BATCH * SEQ
N_MOD = 6

TM_INPROJ = 512
TM_MLP = 512
TF_MLP = 1024
TN_MOD = 1536
GLA_TC = 512
ATT_TQ = 256
ATT_TK = 256
VT_ROWS = DIFF_DV + 16
ATT_HEADS = 8
VMEM_LIMIT = 56 * 1024 * 1024

F32 = jnp.float32
BF16 = jnp.bfloat16
NT_DIMS = (((1,), (1,)), ((), ()))
TN_DIMS = (((0,), (0,)), ((), ()))


def _dot(a, b):
    return jnp.dot(a, b, preferred_element_type=F32)


def _split_bf16(a):
    hi = a.astype(BF16)
    lo = (a - hi.astype(F32)).astype(BF16)
    return hi, lo


def _rms_scale(x):
    return lax.rsqrt(jnp.mean(x * x, axis=-1, keepdims=True) + NORM_EPS)


def _silu(x):
    return x * (1.0 / (1.0 + jnp.exp(-x)))


def _params(*sem):
    return pltpu.CompilerParams(dimension_semantics=sem, vmem_limit_bytes=VMEM_LIMIT)


def _mod_kernel(c_ref, w_ref, b_ref, o_ref):
    c = c_ref[...]
    c_hi, c_lo = _split_bf16(_silu(c))
    w_hi, w_lo = _split_bf16(w_ref[...])
    lhs = jnp.concatenate([c_hi, c_lo], axis=0)
    r = _dot(lhs, w_hi)
    o_ref[...] = r[:BATCH] + r[BATCH:] + _dot(c_hi, w_lo) + b_ref[...]


def _adaln_mod(c, ada_w, ada_b):
    n_out = N_MOD * D_MODEL
    return pl.pallas_call(
        _mod_kernel,
        grid=(DEPTH, n_out // TN_MOD),
        in_specs=[
            pl.BlockSpec((BATCH, D_MODEL), lambda i, j: (0, 0)),
            pl.BlockSpec((None, D_MODEL, TN_MOD), lambda i, j: (i, 0, j)),
            pl.BlockSpec((None, 1, TN_MOD), lambda i, j: (i, 0, j)),
        ],
        out_specs=pl.BlockSpec((None, BATCH, TN_MOD), lambda i, j: (i, 0, j)),
        out_shape=jax.ShapeDtypeStruct((DEPTH, BATCH, n_out), F32),
        compiler_params=_params("arbitrary", "arbitrary"),
        name="adaln_mod",
    )(c, ada_w, ada_b.reshape(DEPTH, 1, n_out))


def _mod_spec(layer, k, tm):
    tiles_per_seq = SEQ // tm

    def index(m, *_):
        return ((layer * BATCH + m // tiles_per_seq) * N_MOD + k, 0, 0)

    return pl.BlockSpec((None, 1, D_MODEL), index)


def _full_spec(shape):
    zeros = (0,) * len(shape)
    return pl.BlockSpec(shape, lambda *_: zeros)


def _modulated_norm(x, g, scale, shift):
    return (x * _rms_scale(x) * g) * (1.0 + scale) + shift


def _inproj_gla_kernel(x_ref, shift_ref, scale_ref, g_ref, wqk_ref, wv_ref, wr_ref, wa_ref,
                       wa2_ref, ba_ref, br_ref,
                       qd_ref, ki_ref, ks_ref, v_ref, gate_ref, dec_ref):
    hb = _modulated_norm(x_ref[...], g_ref[...], scale_ref[...], shift_ref[...]).astype(BF16)
    a_hi, a_lo = _split_bf16(_dot(hb, wa_ref[...]))
    r = _dot(hb, wr_ref[...]) + br_ref[...]
    gate_ref[...] = _silu(r).astype(BF16)

    w_hi, w_lo = _split_bf16(wa2_ref[...])
    la = _dot(jnp.concatenate([a_hi, a_lo, a_hi], axis=1),
              jnp.concatenate([w_hi, w_hi, w_lo], axis=0)) + ba_ref[...]
    qk = _dot(hb, wqk_ref[...])
    log_a = (jnp.minimum(la, 0.0) - jnp.log(1.0 + jnp.exp(-jnp.abs(la)))) * (1.0 / GLA_TAU)
    l_hi, l_lo = _split_bf16(log_a)
    l_cat = jnp.concatenate([l_hi, l_lo], axis=1)

    row = lax.broadcasted_iota(jnp.int32, (GLA_CHUNK, GLA_CHUNK), 0)
    col = lax.broadcasted_iota(jnp.int32, (GLA_CHUNK, GLA_CHUNK), 1)
    tril = jnp.where(row >= col, 1.0, 0.0).astype(BF16)
    q_scale = GLA_DK ** -0.5
    n_chunks = x_ref.shape[0] // GLA_CHUNK
    chunk = lambda n: slice(n * GLA_CHUNK, (n + 1) * GLA_CHUNK)
    cums = [_dot(tril, l_cat[chunk(n)]) for n in range(n_chunks)]
    v_ref[...] = _dot(hb, wv_ref[...]).astype(BF16)
    for n in range(n_chunks):
        sl = chunk(n)
        b = cums[n][:, :GLA_HK] + cums[n][:, GLA_HK:]
        b_last = b[GLA_CHUNK - 1:GLA_CHUNK, :]
        q = qk[sl, :GLA_HK] * q_scale
        k = qk[sl, GLA_HK:]
        qd_ref[sl, :] = (q * jnp.exp(b)).astype(BF16)
        ki_ref[sl, :] = (k * jnp.exp(-b)).astype(BF16)
        ks_ref[sl, :] = (k * jnp.exp(b_last - b)).astype(BF16)
        dec_ref[n] = jnp.exp(b_last)


def _inproj_gla(x2, mod3, layer, g, w_in, wa, wa2, ba, br):
    tm = TM_INPROJ
    row = lambda m: (m, 0)
    assert 2 * GLA_HK == D_MODEL
    col_block = lambda width, idx: pl.BlockSpec((None, D_MODEL, width), lambda m: (0, 0, idx))
    return pl.pallas_call(
        _inproj_gla_kernel,
        grid=(N_TOK // tm,),
        in_specs=[
            pl.BlockSpec((tm, D_MODEL), row),
            _mod_spec(layer, 0, tm),
            _mod_spec(layer, 1, tm),
            _full_spec((1, D_MODEL)),
            col_block(D_MODEL, 0),
            col_block(D_MODEL, 1),
            col_block(D_MODEL, 2),
            _full_spec((D_MODEL, GLA_GATE_RANK)),
            _full_spec((GLA_GATE_RANK, GLA_HK)),
            _full_spec((1, GLA_HK)),
            _full_spec((1, D_MODEL)),
        ],
        out_specs=[
            pl.BlockSpec((tm, GLA_HK), row),
            pl.BlockSpec((tm, GLA_HK), row),
            pl.BlockSpec((tm, GLA_HK), row),
            pl.BlockSpec((tm, D_MODEL), row),
            pl.BlockSpec((tm, D_MODEL), row),
            pl.BlockSpec((tm // GLA_CHUNK, 1, GLA_HK), lambda m: (m, 0, 0)),
        ],
        out_shape=[
            jax.ShapeDtypeStruct((N_TOK, GLA_HK), BF16),
            jax.ShapeDtypeStruct((N_TOK, GLA_HK), BF16),
            jax.ShapeDtypeStruct((N_TOK, GLA_HK), BF16),
            jax.ShapeDtypeStruct((N_TOK, D_MODEL), BF16),
            jax.ShapeDtypeStruct((N_TOK, D_MODEL), BF16),
            jax.ShapeDtypeStruct((N_TOK // GLA_CHUNK, 1, GLA_HK), F32),
        ],
        compiler_params=_params("arbitrary"),
        name="inproj_gla",
    )(x2, mod3, mod3, g, w_in, w_in, w_in, wa, wa2, ba, br)


def _gla_kernel(qd_ref, ki_ref, ks_ref, v_ref, gate_ref, dec_ref, ng_ref, o_ref, st_ref):
    @pl.when(pl.program_id(1) == 0)
    def _():
        st_ref[...] = jnp.zeros_like(st_ref)

    row = lax.broadcasted_iota(jnp.int32, (GLA_CHUNK, GLA_CHUNK), 0)
    col = lax.broadcasted_iota(jnp.int32, (GLA_CHUNK, GLA_CHUNK), 1)
    causal = row >= col
    n_chunks = qd_ref.shape[0] // GLA_CHUNK
    work = [(h, n) for h in range(GLA_HEADS) for n in range(n_chunks)]
    rows = lambda n: slice(n * GLA_CHUNK, (n + 1) * GLA_CHUNK)
    kcols = lambda h: slice(h * GLA_DK, (h + 1) * GLA_DK)
    vcols = lambda h: slice(h * GLA_DV, (h + 1) * GLA_DV)

    scores = {}
    updates = {}
    for h, n in work:
        scores[h, n] = lax.dot_general(qd_ref[rows(n), kcols(h)], ki_ref[rows(n), kcols(h)],
                                       NT_DIMS, preferred_element_type=F32)
        updates[h, n] = lax.dot_general(v_ref[rows(n), vcols(h)], ks_ref[rows(n), kcols(h)],
                                        TN_DIMS, preferred_element_type=F32)
    states = {}
    for h in range(GLA_HEADS):
        st = st_ref[h]
        for n in range(n_chunks):
            states[h, n] = st.astype(BF16)
            st = st * dec_ref[n][:, kcols(h)] + updates[h, n]
        st_ref[h] = st
    for h, n in work:
        s = jnp.where(causal, scores[h, n], 0.0).astype(BF16)
        o = _dot(s, v_ref[rows(n), vcols(h)]) + lax.dot_general(
            qd_ref[rows(n), kcols(h)], states[h, n], NT_DIMS, preferred_element_type=F32)
        y = o * _rms_scale(o) * ng_ref[...]
        o_ref[rows(n), vcols(h)] = (y * gate_ref[rows(n), vcols(h)].astype(F32)).astype(BF16)


def _gla_core(qd, ki, ks, v, gate, dec, ng):
    tc = GLA_TC
    qk_spec = pl.BlockSpec((None, tc, GLA_HK), lambda b, t: (b, t, 0))
    v_spec = pl.BlockSpec((None, tc, D_MODEL), lambda b, t: (b, t, 0))
    return pl.pallas_call(
        _gla_kernel,
        grid=(BATCH, SEQ // tc),
        in_specs=[
            qk_spec, qk_spec, qk_spec, v_spec, v_spec,
            pl.BlockSpec((None, tc // GLA_CHUNK, 1, GLA_HK), lambda b, t: (b, t, 0, 0)),
            _full_spec((1, GLA_DV)),
        ],
        out_specs=v_spec,
        out_shape=jax.ShapeDtypeStruct((BATCH, SEQ, D_MODEL), BF16),
        scratch_shapes=[pltpu.VMEM((GLA_HEADS, GLA_DV, GLA_DK), F32)],
        compiler_params=_params("arbitrary", "arbitrary"),
        name="gla_core",
    )(qd.reshape(BATCH, SEQ, GLA_HK), ki.reshape(BATCH, SEQ, GLA_HK),
      ks.reshape(BATCH, SEQ, GLA_HK), v.reshape(BATCH, SEQ, D_MODEL),
      gate.reshape(BATCH, SEQ, D_MODEL),
      dec.reshape(BATCH, SEQ // GLA_CHUNK, 1, GLA_HK), ng)


def _rope_kernel(pos_ref, freq_ref, cs_ref):
    ang = freq_ref[...] * pos_ref[...]
    cs_ref[0:ROPE_HALF, :] = jnp.cos(ang)
    cs_ref[ROPE_HALF:, :] = jnp.sin(ang)


def _rope_tables(positions):
    inv_freq = ROPE_THETA ** (-jnp.arange(0, ROPE_DIM, 2, dtype=F32) / ROPE_DIM)
    return pl.pallas_call(
        _rope_kernel,
        grid=(1,),
        in_specs=[_full_spec((1, N_TOK)), _full_spec((ROPE_HALF, 1))],
        out_specs=_full_spec((ROPE_DIM, N_TOK)),
        out_shape=jax.ShapeDtypeStruct((ROPE_DIM, N_TOK), F32),
        compiler_params=_params("arbitrary"),
        name="rope_tables",
    )(positions.astype(F32).reshape(1, N_TOK), inv_freq.reshape(ROPE_HALF, 1))


def _rope_expander():
    e = np.zeros((3, ROPE_DIM, 2 * DIFF_DV), np.float32)
    for lane in range(DIFF_DV):
        d = lane % DIFF_DH
        if d < ROPE_DIM:
            e[:, d % ROPE_HALF, lane] = 1.0
            e[:, ROPE_HALF + d % ROPE_HALF, DIFF_DV + lane] = -1.0 if d < ROPE_HALF else 1.0
    return jnp.asarray(e.reshape(3 * ROPE_DIM, 2 * DIFF_DV), BF16)


def _inproj_diff_kernel(x_ref, shift_ref, scale_ref, g_ref, wq_ref, wk_ref, wvt_ref,
                        cs_ref, e_ref, q_ref, k_ref, vt_ref):
    hb = _modulated_norm(x_ref[...], g_ref[...], scale_ref[...], shift_ref[...]).astype(BF16)

    cs = cs_ref[...]
    hi = cs.astype(BF16).astype(F32)
    mid = (cs - hi).astype(BF16).astype(F32)
    lo = cs - hi - mid
    parts = jnp.concatenate([hi, mid, lo], axis=0)
    tab = _dot(parts.T.astype(BF16), e_ref[...])
    d = lax.broadcasted_iota(jnp.int32, (1, DIFF_DV), 1) & (DIFF_DH - 1)
    ct = tab[:, :DIFF_DV] + jnp.where(d >= ROPE_DIM, 1.0, 0.0)
    s12 = tab[:, DIFF_DV:]
    first_half = d < ROPE_HALF

    q_scale = DIFF_DH ** -0.5 * math.log2(math.e)
    for w_ref, out_ref, scl in ((wq_ref, q_ref, q_scale), (wk_ref, k_ref, 1.0)):
        z = _dot(hb, w_ref[...])
        for h in range(DIFF_HEADS):
            sl = slice(h * DIFF_DV, (h + 1) * DIFF_DV)
            xh = z[:, sl]
            partner = jnp.where(first_half, pltpu.roll(xh, DIFF_DV - ROPE_HALF, 1),
                                pltpu.roll(xh, ROPE_HALF, 1))
            out_ref[:, sl] = ((xh * ct + partner * s12) * scl).astype(BF16)
    vt = lax.dot_general(wvt_ref[...], hb, NT_DIMS, preferred_element_type=F32).astype(BF16)
    ones = jnp.ones((VT_ROWS - DIFF_DV, vt.shape[1]), BF16)
    for h in range(DIFF_HEADS):
        vt_ref[h * VT_ROWS:h * VT_ROWS + DIFF_DV, :] = vt[h * DIFF_DV:(h + 1) * DIFF_DV]
        vt_ref[h * VT_ROWS + DIFF_DV:(h + 1) * VT_ROWS, :] = ones


def _inproj_diff(x2, mod3, layer, g, w_in, wvt, cs, expander):
    tm = TM_INPROJ
    row = lambda m: (m, 0)
    o_spec = pl.BlockSpec((tm, D_MODEL), row)
    return pl.pallas_call(
        _inproj_diff_kernel,
        grid=(N_TOK // tm,),
        in_specs=[
            pl.BlockSpec((tm, D_MODEL), row),
            _mod_spec(layer, 0, tm),
            _mod_spec(layer, 1, tm),
            _full_spec((1, D_MODEL)),
            pl.BlockSpec((None, D_MODEL, D_MODEL), lambda m: (0, 0, 0)),
            pl.BlockSpec((None, D_MODEL, D_MODEL), lambda m: (0, 0, 1)),
            _full_spec((D_MODEL, D_MODEL)),
            pl.BlockSpec((ROPE_DIM, tm), lambda m: (0, m)),
            _full_spec((3 * ROPE_DIM, 2 * DIFF_DV)),
        ],
        out_specs=[o_spec, o_spec, pl.BlockSpec((DIFF_HEADS * VT_ROWS, tm), lambda m: (0, m))],
        out_shape=[jax.ShapeDtypeStruct((N_TOK, D_MODEL), BF16)] * 2
        + [jax.ShapeDtypeStruct((DIFF_HEADS * VT_ROWS, N_TOK), BF16)],
        compiler_params=_params("arbitrary"),
        name="inproj_diff",
    )(x2, mod3, mod3, g, w_in, w_in, wvt, cs, expander)


def _attn_kernel(q_ref, k_ref, vt_ref, lam_ref, g_ref, o_ref, qs_ref, m_ref, acc_ref,
                 *, lambda_init):
    tq, tk = ATT_TQ, ATT_TK
    lf = lam_ref[...]
    lam = (jnp.exp(jnp.sum(lf[0:1] * lf[1:2], axis=-1, keepdims=True))
           - jnp.exp(jnp.sum(lf[2:3] * lf[3:4], axis=-1, keepdims=True)) + lambda_init)
    lane = lax.broadcasted_iota(jnp.int32, (tq, DIFF_DV), 1)
    k_pos = lax.broadcasted_iota(jnp.int32, (tk, 2 * tq), 0)
    q_pos = lax.broadcasted_iota(jnp.int32, (tk, 2 * tq), 1) & (tq - 1)
    heads = [slice(h * DIFF_DV, (h + 1) * DIFF_DV) for h in range(ATT_HEADS)]

    def kv_step(j, diag_offset):
        start = pl.multiple_of(j * tk, tk)
        scores = [lax.dot_general(k_ref[pl.ds(start, tk), hs], qs_ref[h], NT_DIMS,
                                  preferred_element_type=F32) for h, hs in enumerate(heads)]
        for h in range(ATT_HEADS):
            vtb = vt_ref[h * VT_ROWS:(h + 1) * VT_ROWS, pl.ds(start, tk)]
            s = scores[h]
            if diag_offset is not None:
                s = jnp.where(k_pos + diag_offset <= q_pos, s, -1e30)
            m_prev = m_ref[h]
            m_new = jnp.maximum(m_prev, jnp.max(s, axis=0, keepdims=True))
            p = jnp.exp2(s - m_new)
            alpha = jnp.exp2(m_prev - m_new)
            acc_ref[h] = alpha * acc_ref[h] + _dot(vtb, p.astype(BF16))
            m_ref[h] = m_new

    def q_block(qi, carry):
        q0 = pl.multiple_of(qi * tq, tq)
        for h, hs in enumerate(heads):
            q = q_ref[pl.ds(q0, tq), hs]
            qs_ref[h, 0:tq, :] = jnp.where(lane < DIFF_DH, q, jnp.zeros_like(q))
            qs_ref[h, tq:2 * tq, :] = jnp.where(lane >= DIFF_DH, q, jnp.zeros_like(q))
        m_ref[...] = jnp.full_like(m_ref, -1e30)
        acc_ref[...] = jnp.zeros_like(acc_ref)
        n_full = qi * (tq // tk)

        def body(j, c):
            kv_step(j, None)
            return c

        lax.fori_loop(0, n_full, body, 0)
        for jj in range(tq // tk):
            kv_step(n_full + jj, jj * tk)
        for h, hs in enumerate(heads):
            acc = acc_ref[h, 0:DIFF_DV, :]
            inv_l = 1.0 / acc_ref[h, DIFF_DV:DIFF_DV + 1, :]
            o = acc[:, :tq] * inv_l[:, :tq] - lam * (acc[:, tq:] * inv_l[:, tq:])
            rms = lax.rsqrt(jnp.mean(o * o, axis=0, keepdims=True) + NORM_EPS)
            y = o * rms * (g_ref[...] * (1.0 - lambda_init))
            o_ref[pl.ds(q0, tq), hs] = y.T.astype(BF16)
        return carry

    lax.fori_loop(0, SEQ // tq, q_block, 0)


def _diff_attention(q, k, vt, lam_vecs, g, lambda_init):
    width = ATT_HEADS * DIFF_DV
    spec = pl.BlockSpec((None, SEQ, width), lambda b, h: (b, 0, h))
    return pl.pallas_call(
        functools.partial(_attn_kernel, lambda_init=lambda_init),
        grid=(BATCH, DIFF_HEADS // ATT_HEADS),
        in_specs=[spec, spec, pl.BlockSpec((ATT_HEADS * VT_ROWS, SEQ), lambda b, h: (h, b)),
                  _full_spec((4, DIFF_DH)), _full_spec((DIFF_DV, 1))],
        out_specs=spec,
        out_shape=jax.ShapeDtypeStruct((BATCH, SEQ, D_MODEL), BF16),
        scratch_shapes=[
            pltpu.VMEM((ATT_HEADS, 2 * ATT_TQ, DIFF_DV), BF16),
            pltpu.VMEM((ATT_HEADS, 1, 2 * ATT_TQ), F32),
            pltpu.VMEM((ATT_HEADS, VT_ROWS, 2 * ATT_TQ), F32),
        ],
        compiler_params=_params("arbitrary", "arbitrary"),
        name="diff_attention",
    )(q.reshape(BATCH, SEQ, D_MODEL), k.reshape(BATCH, SEQ, D_MODEL), vt, lam_vecs, g)


def _mlp_kernel(x_ref, o_ref, wo_ref, gt_ref, sh_ref, sc_ref, gc_ref, g_ref, w1_ref, w2_ref,
                fg_ref, out_ref, *, final):
    x1 = x_ref[...] + gt_ref[...] * _dot(o_ref[...], wo_ref[...])
    hb = _modulated_norm(x1, g_ref[...], sc_ref[...], sh_ref[...]).astype(BF16)
    a = jnp.square(jnp.maximum(_dot(hb, w1_ref[...]), 0.0)).astype(BF16)
    out = x1 + gc_ref[...] * _dot(a, w2_ref[...])
    if final:
        out = out * _rms_scale(out) * fg_ref[...]
    out_ref[...] = out


def _resident_spec(block_shape, index):
    return pl.BlockSpec(block_shape, lambda m: index, pipeline_mode=pl.Buffered(1))


def _outproj_mlp(x2, o2, mod3, layer, wo, g, w1, w2, fg, final):
    tm = TM_MLP
    row = lambda m: (m, 0)
    return pl.pallas_call(
        functools.partial(_mlp_kernel, final=final),
        grid=(N_TOK // tm,),
        in_specs=[
            pl.BlockSpec((tm, D_MODEL), row),
            pl.BlockSpec((tm, D_MODEL), row),
            _resident_spec((None, D_MODEL, D_MODEL), (0, 0, 0)),
            _mod_spec(layer, 2, tm),
            _mod_spec(layer, 3, tm),
            _mod_spec(layer, 4, tm),
            _mod_spec(layer, 5, tm),
            _full_spec((1, D_MODEL)),
            _resident_spec((None, D_MODEL, D_FF), (layer, 0, 0)),
            _resident_spec((None, D_FF, D_MODEL), (layer, 0, 0)),
            _full_spec((1, D_MODEL)),
        ],
        out_specs=pl.BlockSpec((tm, D_MODEL), row),
        out_shape=jax.ShapeDtypeStruct((N_TOK, D_MODEL), F32),
        compiler_params=_params("arbitrary"),
        name="outproj_mlp_final" if final else "outproj_mlp",
    )(x2, o2, wo, mod3, mod3, mod3, mod3, g, w1, w2, fg)


def kernel(x, c, positions, ada_w, ada_b, norm_g, mlp_w1, mlp_w2, gla_w_in, gla_w_a2, gla_b_a,
           gla_b_r, gla_norm_g, gla_w_o, diff_w_in, diff_lambda, diff_subln_g, diff_w_o, final_g):
    bf = lambda a: a.astype(BF16)
    w1, w2 = bf(mlp_w1), bf(mlp_w2)
    x2 = x.reshape(N_TOK, D_MODEL)
    mod3 = _adaln_mod(c, ada_w, ada_b).reshape(DEPTH * BATCH * N_MOD, 1, D_MODEL)
    fg = final_g.reshape(1, D_MODEL)

    qd, ki, ks, v, gate, dec = _inproj_gla(
        x2, mod3, 0, norm_g[0, 0].reshape(1, D_MODEL), bf(gla_w_in),
        bf(gla_w_in[0, :, 3 * D_MODEL:]), gla_w_a2[0], gla_b_a[0].reshape(1, GLA_HK), gla_b_r[0].reshape(1, D_MODEL))
    o = _gla_core(qd, ki, ks, v, gate, dec, gla_norm_g[0].reshape(1, GLA_DV))
    x2 = _outproj_mlp(x2, o.reshape(N_TOK, D_MODEL), mod3, 0, bf(gla_w_o),
                      norm_g[0, 1].reshape(1, D_MODEL), w1, w2, fg, False)

    lambda_init = 0.8 - 0.6 * math.exp(-0.3 * 1)
    cs = _rope_tables(positions)
    q, k, vt = _inproj_diff(
        x2, mod3, 1, norm_g[1, 0].reshape(1, D_MODEL), bf(diff_w_in),
        bf(diff_w_in[0, :, 2 * D_MODEL:].T), cs, _rope_expander())
    o = _diff_attention(q, k, vt, diff_lambda[0], diff_subln_g[0].reshape(DIFF_DV, 1), lambda_init)
    x2 = _outproj_mlp(x2, o.reshape(N_TOK, D_MODEL), mod3, 1, bf(diff_w_o),
                      norm_g[1, 1].reshape(1, D_MODEL), w1, w2, fg, True)
    return x2.reshape(BATCH, SEQ, D_MODEL)
```

```python
import functools
import math

import jax
import jax.numpy as jnp
import numpy as np
from jax import lax
from jax.experimental import pallas as pl
from jax.experimental.pallas import tpu as pltpu

D_MODEL = 1024
BATCH = 8
SEQ = 2048
DEPTH = 2
D_FF = 4 * D_MODEL
NORM_EPS = 1e-6
GLA_HEADS = 4
GLA_DK = 128
GLA_DV = 256
GLA_HK = GLA_HEADS * GLA_DK
GLA_GATE_RANK = 16
GLA_TAU = 16.0
GLA_CHUNK = 64
DIFF_HEADS = 8
DIFF_DH = 64
DIFF_DV = 128
ROPE_THETA = 500000.0
ROPE_DIM = DIFF_DH // 4
ROPE_HALF = ROPE_DIM // 2
N_TOK = BATCH * SEQ
N_MOD = 6

TM_INPROJ = 1024
TM_MLP = 512
TF_MLP = 1024
TN_MOD = 1536
GLA_TC = 512
ATT_TQ = 256
ATT_TK = 256
VT_ROWS = DIFF_DV + 16
ATT_HEADS = 8
VMEM_LIMIT = 56 * 1024 * 1024

F32 = jnp.float32
BF16 = jnp.bfloat16
NT_DIMS = (((1,), (1,)), ((), ()))
TN_DIMS = (((0,), (0,)), ((), ()))


def _dot(a, b):
    return jnp.dot(a, b, preferred_element_type=F32)


def _split_bf16(a):
    hi = a.astype(BF16)
    lo = (a - hi.astype(F32)).astype(BF16)
    return hi, lo


def _rms_scale(x):
    return lax.rsqrt(jnp.mean(x * x, axis=-1, keepdims=True) + NORM_EPS)


def _silu(x):
    return x * (1.0 / (1.0 + jnp.exp(-x)))


def _params(*sem):
    return pltpu.CompilerParams(dimension_semantics=sem, vmem_limit_bytes=VMEM_LIMIT)


def _mod_kernel(c_ref, w_ref, b_ref, o_ref):
    c = c_ref[...]
    c_hi, c_lo = _split_bf16(_silu(c))
    w_hi, w_lo = _split_bf16(w_ref[...])
    lhs = jnp.concatenate([c_hi, c_lo], axis=0)
    r = _dot(lhs, w_hi)
    o_ref[...] = r[:BATCH] + r[BATCH:] + _dot(c_hi, w_lo) + b_ref[...]


def _adaln_mod(c, ada_w, ada_b):
    n_out = N_MOD * D_MODEL
    return pl.pallas_call(
        _mod_kernel,
        grid=(DEPTH, n_out // TN_MOD),
        in_specs=[
            pl.BlockSpec((BATCH, D_MODEL), lambda i, j: (0, 0)),
            pl.BlockSpec((None, D_MODEL, TN_MOD), lambda i, j: (i, 0, j)),
            pl.BlockSpec((None, 1, TN_MOD), lambda i, j: (i, 0, j)),
        ],
        out_specs=pl.BlockSpec((None, BATCH, TN_MOD), lambda i, j: (i, 0, j)),
        out_shape=jax.ShapeDtypeStruct((DEPTH, BATCH, n_out), F32),
        compiler_params=_params("arbitrary", "arbitrary"),
        name="adaln_mod",
    )(c, ada_w, ada_b.reshape(DEPTH, 1, n_out))


def _mod_spec(layer, k, tm):
    tiles_per_seq = SEQ // tm

    def index(m, *_):
        return ((layer * BATCH + m // tiles_per_seq) * N_MOD + k, 0, 0)

    return pl.BlockSpec((None, 1, D_MODEL), index)


def _full_spec(shape):
    zeros = (0,) * len(shape)
    return pl.BlockSpec(shape, lambda *_: zeros)


def _modulated_norm(x, g, scale, shift):
    return (x * _rms_scale(x) * g) * (1.0 + scale) + shift


def _inproj_gla_kernel(x_ref, shift_ref, scale_ref, g_ref, wqk_ref, wv_ref, wr_ref, wa_ref,
                       wa2_ref, ba_ref, br_ref,
                       qd_ref, ki_ref, ks_ref, v_ref, gate_ref, dec_ref):
    hb = _modulated_norm(x_ref[...], g_ref[...], scale_ref[...], shift_ref[...]).astype(BF16)
    a_hi, a_lo = _split_bf16(_dot(hb, wa_ref[...]))
    r = _dot(hb, wr_ref[...]) + br_ref[...]
    gate_ref[...] = _silu(r).astype(BF16)

    w_hi, w_lo = _split_bf16(wa2_ref[...])
    la = _dot(jnp.concatenate([a_hi, a_lo, a_hi], axis=1),
              jnp.concatenate([w_hi, w_hi, w_lo], axis=0)) + ba_ref[...]
    qk = _dot(hb, wqk_ref[...])
    log_a = (jnp.minimum(la, 0.0) - jnp.log(1.0 + jnp.exp(-jnp.abs(la)))) * (1.0 / GLA_TAU)
    l_hi, l_lo = _split_bf16(log_a)
    l_cat = jnp.concatenate([l_hi, l_lo], axis=1)

    row = lax.broadcasted_iota(jnp.int32, (GLA_CHUNK, GLA_CHUNK), 0)
    col = lax.broadcasted_iota(jnp.int32, (GLA_CHUNK, GLA_CHUNK), 1)
    tril = jnp.where(row >= col, 1.0, 0.0).astype(BF16)
    q_scale = GLA_DK ** -0.5
    n_chunks = x_ref.shape[0] // GLA_CHUNK
    chunk = lambda n: slice(n * GLA_CHUNK, (n + 1) * GLA_CHUNK)
    cums = [_dot(tril, l_cat[chunk(n)]) for n in range(n_chunks)]
    v_ref[...] = _dot(hb, wv_ref[...]).astype(BF16)
    for n in range(n_chunks):
        sl = chunk(n)
        b = cums[n][:, :GLA_HK] + cums[n][:, GLA_HK:]
        b_last = b[GLA_CHUNK - 1:GLA_CHUNK, :]
        q = qk[sl, :GLA_HK] * q_scale
        k = qk[sl, GLA_HK:]
        qd_ref[sl, :] = (q * jnp.exp(b)).astype(BF16)
        ki_ref[sl, :] = (k * jnp.exp(-b)).astype(BF16)
        ks_ref[sl, :] = (k * jnp.exp(b_last - b)).astype(BF16)
        dec_ref[n] = jnp.exp(b_last)


def _inproj_gla(x2, mod3, layer, g, w_in, wa, wa2, ba, br):
    tm = TM_INPROJ
    row = lambda m: (m, 0)
    assert 2 * GLA_HK == D_MODEL
    col_block = lambda idx: _resident_spec((None, D_MODEL, D_MODEL), (0, 0, idx))
    return pl.pallas_call(
        _inproj_gla_kernel,
        grid=(N_TOK // tm,),
        in_specs=[
            pl.BlockSpec((tm, D_MODEL), row),
            _mod_spec(layer, 0, tm),
            _mod_spec(layer, 1, tm),
            _full_spec((1, D_MODEL)),
            col_block(0),
            col_block(1),
            col_block(2),
            _full_spec((D_MODEL, GLA_GATE_RANK)),
            _full_spec((GLA_GATE_RANK, GLA_HK)),
            _full_spec((1, GLA_HK)),
            _full_spec((1, D_MODEL)),
        ],
        out_specs=[
            pl.BlockSpec((tm, GLA_HK), row),
            pl.BlockSpec((tm, GLA_HK), row),
            pl.BlockSpec((tm, GLA_HK), row),
            pl.BlockSpec((tm, D_MODEL), row),
            pl.BlockSpec((tm, D_MODEL), row),
            pl.BlockSpec((tm // GLA_CHUNK, 1, GLA_HK), lambda m: (m, 0, 0)),
        ],
        out_shape=[
            jax.ShapeDtypeStruct((N_TOK, GLA_HK), BF16),
            jax.ShapeDtypeStruct((N_TOK, GLA_HK), BF16),
            jax.ShapeDtypeStruct((N_TOK, GLA_HK), BF16),
            jax.ShapeDtypeStruct((N_TOK, D_MODEL), BF16),
            jax.ShapeDtypeStruct((N_TOK, D_MODEL), BF16),
            jax.ShapeDtypeStruct((N_TOK // GLA_CHUNK, 1, GLA_HK), F32),
        ],
        compiler_params=_params("arbitrary"),
        name="inproj_gla",
    )(x2, mod3, mod3, g, w_in, w_in, w_in, wa, wa2, ba, br)


def _gla_kernel(qd_ref, ki_ref, ks_ref, v_ref, gate_ref, dec_ref, ng_ref, o_ref, st_ref):
    @pl.when(pl.program_id(1) == 0)
    def _():
        st_ref[...] = jnp.zeros_like(st_ref)

    row = lax.broadcasted_iota(jnp.int32, (GLA_CHUNK, GLA_CHUNK), 0)
    col = lax.broadcasted_iota(jnp.int32, (GLA_CHUNK, GLA_CHUNK), 1)
    causal = row >= col
    n_chunks = qd_ref.shape[0] // GLA_CHUNK
    work = [(h, n) for h in range(GLA_HEADS) for n in range(n_chunks)]
    rows = lambda n: slice(n * GLA_CHUNK, (n + 1) * GLA_CHUNK)
    kcols = lambda h: slice(h * GLA_DK, (h + 1) * GLA_DK)
    vcols = lambda h: slice(h * GLA_DV, (h + 1) * GLA_DV)

    scores = {}
    updates = {}
    for h, n in work:
        scores[h, n] = lax.dot_general(qd_ref[rows(n), kcols(h)], ki_ref[rows(n), kcols(h)],
                                       NT_DIMS, preferred_element_type=F32)
        updates[h, n] = lax.dot_general(v_ref[rows(n), vcols(h)], ks_ref[rows(n), kcols(h)],
                                        TN_DIMS, preferred_element_type=F32)
    states = {}
    for h in range(GLA_HEADS):
        st = st_ref[h]
        for n in range(n_chunks):
            states[h, n] = st.astype(BF16)
            st = st * dec_ref[n][:, kcols(h)] + updates[h, n]
        st_ref[h] = st
    for h, n in work:
        s = jnp.where(causal, scores[h, n], 0.0).astype(BF16)
        o = _dot(s, v_ref[rows(n), vcols(h)]) + lax.dot_general(
            qd_ref[rows(n), kcols(h)], states[h, n], NT_DIMS, preferred_element_type=F32)
        y = o * _rms_scale(o) * ng_ref[...]
        o_ref[rows(n), vcols(h)] = (y * gate_ref[rows(n), vcols(h)].astype(F32)).astype(BF16)


def _gla_core(qd, ki, ks, v, gate, dec, ng):
    tc = GLA_TC
    qk_spec = pl.BlockSpec((None, tc, GLA_HK), lambda b, t: (b, t, 0))
    v_spec = pl.BlockSpec((None, tc, D_MODEL), lambda b, t: (b, t, 0))
    return pl.pallas_call(
        _gla_kernel,
        grid=(BATCH, SEQ // tc),
        in_specs=[
            qk_spec, qk_spec, qk_spec, v_spec, v_spec,
            pl.BlockSpec((None, tc // GLA_CHUNK, 1, GLA_HK), lambda b, t: (b, t, 0, 0)),
            _full_spec((1, GLA_DV)),
        ],
        out_specs=v_spec,
        out_shape=jax.ShapeDtypeStruct((BATCH, SEQ, D_MODEL), BF16),
        scratch_shapes=[pltpu.VMEM((GLA_HEADS, GLA_DV, GLA_DK), F32)],
        compiler_params=_params("arbitrary", "arbitrary"),
        name="gla_core",
    )(qd.reshape(BATCH, SEQ, GLA_HK), ki.reshape(BATCH, SEQ, GLA_HK),
      ks.reshape(BATCH, SEQ, GLA_HK), v.reshape(BATCH, SEQ, D_MODEL),
      gate.reshape(BATCH, SEQ, D_MODEL),
      dec.reshape(BATCH, SEQ // GLA_CHUNK, 1, GLA_HK), ng)


def _rope_kernel(pos_ref, freq_ref, cs_ref):
    ang = freq_ref[...] * pos_ref[...]
    cs_ref[0:ROPE_HALF, :] = jnp.cos(ang)
    cs_ref[ROPE_HALF:, :] = jnp.sin(ang)


def _rope_tables(positions):
    inv_freq = ROPE_THETA ** (-jnp.arange(0, ROPE_DIM, 2, dtype=F32) / ROPE_DIM)
    return pl.pallas_call(
        _rope_kernel,
        grid=(1,),
        in_specs=[_full_spec((1, N_TOK)), _full_spec((ROPE_HALF, 1))],
        out_specs=_full_spec((ROPE_DIM, N_TOK)),
        out_shape=jax.ShapeDtypeStruct((ROPE_DIM, N_TOK), F32),
        compiler_params=_params("arbitrary"),
        name="rope_tables",
    )(positions.astype(F32).reshape(1, N_TOK), inv_freq.reshape(ROPE_HALF, 1))


def _rope_expander():
    e = np.zeros((3, ROPE_DIM, 2 * DIFF_DV), np.float32)
    for lane in range(DIFF_DV):
        d = lane % DIFF_DH
        if d < ROPE_DIM:
            e[:, d % ROPE_HALF, lane] = 1.0
            e[:, ROPE_HALF + d % ROPE_HALF, DIFF_DV + lane] = -1.0 if d < ROPE_HALF else 1.0
    return jnp.asarray(e.reshape(3 * ROPE_DIM, 2 * DIFF_DV), BF16)


def _inproj_diff_kernel(x_ref, shift_ref, scale_ref, g_ref, wq_ref, wk_ref, wv_ref,
                        cs_ref, e_ref, q_ref, k_ref, vt_ref, wvt_ref):
    @pl.when(pl.program_id(0) == 0)
    def _():
        wvt_ref[...] = wv_ref[...].T

    hb = _modulated_norm(x_ref[...], g_ref[...], scale_ref[...], shift_ref[...]).astype(BF16)

    cs = cs_ref[...]
    hi = cs.astype(BF16).astype(F32)
    mid = (cs - hi).astype(BF16).astype(F32)
    lo = cs - hi - mid
    parts = jnp.concatenate([hi, mid, lo], axis=0)
    tab = _dot(parts.T.astype(BF16), e_ref[...])
    d = lax.broadcasted_iota(jnp.int32, (1, DIFF_DV), 1) & (DIFF_DH - 1)
    ct = tab[:, :DIFF_DV] + jnp.where(d >= ROPE_DIM, 1.0, 0.0)
    s12 = tab[:, DIFF_DV:]
    first_half = d < ROPE_HALF

    q_scale = DIFF_DH ** -0.5 * math.log2(math.e)
    for w_ref, out_ref, scl in ((wq_ref, q_ref, q_scale), (wk_ref, k_ref, 1.0)):
        z = _dot(hb, w_ref[...])
        for h in range(DIFF_HEADS):
            sl = slice(h * DIFF_DV, (h + 1) * DIFF_DV)
            xh = z[:, sl]
            partner = jnp.where(first_half, pltpu.roll(xh, DIFF_DV - ROPE_HALF, 1),
                                pltpu.roll(xh, ROPE_HALF, 1))
            out_ref[:, sl] = ((xh * ct + partner * s12) * scl).astype(BF16)
    vt = lax.dot_general(wvt_ref[...], hb, NT_DIMS, preferred_element_type=F32).astype(BF16)
    ones = jnp.ones((VT_ROWS - DIFF_DV, vt.shape[1]), BF16)
    for h in range(DIFF_HEADS):
        vt_ref[h * VT_ROWS:h * VT_ROWS + DIFF_DV, :] = vt[h * DIFF_DV:(h + 1) * DIFF_DV]
        vt_ref[h * VT_ROWS + DIFF_DV:(h + 1) * VT_ROWS, :] = ones


def _inproj_diff(x2, mod3, layer, g, w_in, cs, expander):
    tm = TM_INPROJ
    row = lambda m: (m, 0)
    o_spec = pl.BlockSpec((tm, D_MODEL), row)
    return pl.pallas_call(
        _inproj_diff_kernel,
        grid=(N_TOK // tm,),
        in_specs=[
            pl.BlockSpec((tm, D_MODEL), row),
            _mod_spec(layer, 0, tm),
            _mod_spec(layer, 1, tm),
            _full_spec((1, D_MODEL)),
            _resident_spec((None, D_MODEL, D_MODEL), (0, 0, 0)),
            _resident_spec((None, D_MODEL, D_MODEL), (0, 0, 1)),
            _resident_spec((None, D_MODEL, D_MODEL), (0, 0, 2)),
            pl.BlockSpec((ROPE_DIM, tm), lambda m: (0, m)),
            _full_spec((3 * ROPE_DIM, 2 * DIFF_DV)),
        ],
        out_specs=[o_spec, o_spec, pl.BlockSpec((DIFF_HEADS * VT_ROWS, tm), lambda m: (0, m))],
        out_shape=[jax.ShapeDtypeStruct((N_TOK, D_MODEL), BF16)] * 2
        + [jax.ShapeDtypeStruct((DIFF_HEADS * VT_ROWS, N_TOK), BF16)],
        scratch_shapes=[pltpu.VMEM((D_MODEL, D_MODEL), BF16)],
        compiler_params=_params("arbitrary"),
        name="inproj_diff",
    )(x2, mod3, mod3, g, w_in, w_in, w_in, cs, expander)


def _attn_kernel(q_ref, k_ref, vt_ref, lam_ref, g_ref, o_ref, qs_ref, s_ref, m_ref, acc_ref,
                 *, lambda_init):
    tq, tk = ATT_TQ, ATT_TK
    assert tq == tk
    lf = lam_ref[...]
    lam = (jnp.exp(jnp.sum(lf[0:1] * lf[1:2], axis=-1, keepdims=True))
           - jnp.exp(jnp.sum(lf[2:3] * lf[3:4], axis=-1, keepdims=True)) + lambda_init)
    lane = lax.broadcasted_iota(jnp.int32, (tq, DIFF_DV), 1)
    k_pos = lax.broadcasted_iota(jnp.int32, (tk, 2 * tq), 0)
    q_pos = lax.broadcasted_iota(jnp.int32, (tk, 2 * tq), 1) & (tq - 1)
    heads = [slice(h * DIFF_DV, (h + 1) * DIFF_DV) for h in range(ATT_HEADS)]

    def scores_for(j, h):
        start = pl.multiple_of(j * tk, tk)
        return lax.dot_general(k_ref[pl.ds(start, tk), heads[h]], qs_ref[h], NT_DIMS,
                               preferred_element_type=F32)

    def kv_step(j, masked, prefetch_scores):
        start = pl.multiple_of(j * tk, tk)
        for h in range(ATT_HEADS):
            vtb = vt_ref[h * VT_ROWS:(h + 1) * VT_ROWS, pl.ds(start, tk)]
            s = s_ref[h]
            if masked:
                s = jnp.where(k_pos <= q_pos, s, -1e30)
            m_prev = m_ref[h]
            m_new = jnp.maximum(m_prev, jnp.max(s, axis=0, keepdims=True))
            p = jnp.exp2(s - m_new)
            alpha = jnp.exp2(m_prev - m_new)
            acc_ref[h] = alpha * acc_ref[h] + _dot(vtb, p.astype(BF16))
            m_ref[h] = m_new
            if prefetch_scores:
                s_ref[h] = scores_for(j + 1, h)

    def q_block(qi, carry):
        q0 = pl.multiple_of(qi * tq, tq)
        for h, hs in enumerate(heads):
            q = q_ref[pl.ds(q0, tq), hs]
            qs_ref[h, 0:tq, :] = jnp.where(lane < DIFF_DH, q, jnp.zeros_like(q))
            qs_ref[h, tq:2 * tq, :] = jnp.where(lane >= DIFF_DH, q, jnp.zeros_like(q))
        m_ref[...] = jnp.full_like(m_ref, -1e30)
        acc_ref[...] = jnp.zeros_like(acc_ref)
        for h in range(ATT_HEADS):
            s_ref[h] = scores_for(0, h)

        def body(j, c):
            kv_step(j, False, True)
            return c

        lax.fori_loop(0, qi, body, 0)
        kv_step(qi, True, False)
        for h, hs in enumerate(heads):
            acc = acc_ref[h, 0:DIFF_DV, :]
            inv_l = 1.0 / acc_ref[h, DIFF_DV:DIFF_DV + 1, :]
            o = acc[:, :tq] * inv_l[:, :tq] - lam * (acc[:, tq:] * inv_l[:, tq:])
            rms = lax.rsqrt(jnp.mean(o * o, axis=0, keepdims=True) + NORM_EPS)
            y = o * rms * (g_ref[...] * (1.0 - lambda_init))
            o_ref[pl.ds(q0, tq), hs] = y.T.astype(BF16)
        return carry

    lax.fori_loop(0, SEQ // tq, q_block, 0)


def _diff_attention(q, k, vt, lam_vecs, g, lambda_init):
    width = ATT_HEADS * DIFF_DV
    spec = pl.BlockSpec((None, SEQ, width), lambda b, h: (b, 0, h))
    return pl.pallas_call(
        functools.partial(_attn_kernel, lambda_init=lambda_init),
        grid=(BATCH, DIFF_HEADS // ATT_HEADS),
        in_specs=[spec, spec, pl.BlockSpec((ATT_HEADS * VT_ROWS, SEQ), lambda b, h: (h, b)),
                  _full_spec((4, DIFF_DH)), _full_spec((DIFF_DV, 1))],
        out_specs=spec,
        out_shape=jax.ShapeDtypeStruct((BATCH, SEQ, D_MODEL), BF16),
        scratch_shapes=[
            pltpu.VMEM((ATT_HEADS, 2 * ATT_TQ, DIFF_DV), BF16),
            pltpu.VMEM((ATT_HEADS, ATT_TK, 2 * ATT_TQ), F32),
            pltpu.VMEM((ATT_HEADS, 1, 2 * ATT_TQ), F32),
            pltpu.VMEM((ATT_HEADS, VT_ROWS, 2 * ATT_TQ), F32),
        ],
        compiler_params=_params("arbitrary", "arbitrary"),
        name="diff_attention",
    )(q.reshape(BATCH, SEQ, D_MODEL), k.reshape(BATCH, SEQ, D_MODEL), vt, lam_vecs, g)


def _mlp_kernel(x_ref, o_ref, wo_ref, gt_ref, sh_ref, sc_ref, gc_ref, g_ref, w1_ref, w2_ref,
                fg_ref, out_ref, *, final):
    x1 = x_ref[...] + gt_ref[...] * _dot(o_ref[...], wo_ref[...])
    hb = _modulated_norm(x1, g_ref[...], sc_ref[...], sh_ref[...]).astype(BF16)
    a = jnp.square(jnp.maximum(_dot(hb, w1_ref[...]), 0.0)).astype(BF16)
    out = x1 + gc_ref[...] * _dot(a, w2_ref[...])
    if final:
        out = out * _rms_scale(out) * fg_ref[...]
    out_ref[...] = out


def _resident_spec(block_shape, index):
    return pl.BlockSpec(block_shape, lambda m: index, pipeline_mode=pl.Buffered(1))


def _outproj_mlp(x2, o2, mod3, layer, wo, g, w1, w2, fg, final):
    tm = TM_MLP
    row = lambda m: (m, 0)
    return pl.pallas_call(
        functools.partial(_mlp_kernel, final=final),
        grid=(N_TOK // tm,),
        in_specs=[
            pl.BlockSpec((tm, D_MODEL), row),
            pl.BlockSpec((tm, D_MODEL), row),
            _resident_spec((None, D_MODEL, D_MODEL), (0, 0, 0)),
            _mod_spec(layer, 2, tm),
            _mod_spec(layer, 3, tm),
            _mod_spec(layer, 4, tm),
            _mod_spec(layer, 5, tm),
            _full_spec((1, D_MODEL)),
            _resident_spec((None, D_MODEL, D_FF), (layer, 0, 0)),
            _resident_spec((None, D_FF, D_MODEL), (layer, 0, 0)),
            _full_spec((1, D_MODEL)),
        ],
        out_specs=pl.BlockSpec((tm, D_MODEL), row),
        out_shape=jax.ShapeDtypeStruct((N_TOK, D_MODEL), F32),
        compiler_params=_params("arbitrary"),
        name="outproj_mlp_final" if final else "outproj_mlp",
    )(x2, o2, wo, mod3, mod3, mod3, mod3, g, w1, w2, fg)


def kernel(x, c, positions, ada_w, ada_b, norm_g, mlp_w1, mlp_w2, gla_w_in, gla_w_a2, gla_b_a,
           gla_b_r, gla_norm_g, gla_w_o, diff_w_in, diff_lambda, diff_subln_g, diff_w_o, final_g):
    bf = lambda a: a.astype(BF16)
    w1, w2 = bf(mlp_w1), bf(mlp_w2)
    x2 = x.reshape(N_TOK, D_MODEL)
    mod3 = _adaln_mod(c, ada_w, ada_b).reshape(DEPTH * BATCH * N_MOD, 1, D_MODEL)
    fg = final_g.reshape(1, D_MODEL)

    qd, ki, ks, v, gate, dec = _inproj_gla(
        x2, mod3, 0, norm_g[0, 0].reshape(1, D_MODEL), bf(gla_w_in[:, :, :3 * D_MODEL]),
        bf(gla_w_in[0, :, 3 * D_MODEL:]), gla_w_a2[0], gla_b_a[0].reshape(1, GLA_HK), gla_b_r[0].reshape(1, D_MODEL))
    o = _gla_core(qd, ki, ks, v, gate, dec, gla_norm_g[0].reshape(1, GLA_DV))
    x2 = _outproj_mlp(x2, o.reshape(N_TOK, D_MODEL), mod3, 0, bf(gla_w_o),
                      norm_g[0, 1].reshape(1, D_MODEL), w1, w2, fg, False)

    lambda_init = 0.8 - 0.6 * math.exp(-0.3 * 1)
    cs = _rope_tables(positions)
    q, k, vt = _inproj_diff(
        x2, mod3, 1, norm_g[1, 0].reshape(1, D_MODEL), bf(diff_w_in), cs, _rope_expander())
    o = _diff_attention(q, k, vt, diff_lambda[0], diff_subln_g[0].reshape(DIFF_DV, 1), lambda_init)
    x2 = _outproj_mlp(x2, o.reshape(N_TOK, D_MODEL), mod3, 1, bf(diff_w_o),
                      norm_g[1, 1].reshape(1, D_MODEL), w1, w2, fg, True)
    return x2.reshape(BATCH, SEQ, D_MODEL)
```

```python
import functools
import math

import jax
import jax.numpy as jnp
import numpy as np
from jax import lax
from jax.experimental import pallas as pl
from jax.experimental.pallas import tpu as pltpu

D_MODEL = 1024
BATCH = 8
SEQ = 2048
DEPTH = 2
D_FF = 4 * D_MODEL
NORM_EPS = 1e-6
GLA_HEADS = 4
GLA_DK = 128
GLA_DV = 256
GLA_HK = GLA_HEADS * GLA_DK
GLA_GATE_RANK = 16
GLA_TAU = 16.0
GLA_CHUNK = 64
DIFF_HEADS = 8
DIFF_DH = 64
DIFF_DV = 128
ROPE_THETA = 500000.0
ROPE_DIM = DIFF_DH // 4
ROPE_HALF = ROPE_DIM // 2
N_TOK = BATCH * SEQ
N_MOD = 6

TM_INPROJ = 1024
TM_MLP = 512
TF_MLP = 1024
TN_MOD = 1536
GLA_TC = 512
ATT_TQ = 256
ATT_TK = 256
VT_ROWS = DIFF_DV + 16
ATT_HEADS = 8
VMEM_LIMIT = 56 * 1024 * 1024

F32 = jnp.float32
BF16 = jnp.bfloat16
NT_DIMS = (((1,), (1,)), ((), ()))
TN_DIMS = (((0,), (0,)), ((), ()))


def _dot(a, b):
    return jnp.dot(a, b, preferred_element_type=F32)


def _split_bf16(a):
    hi = a.astype(BF16)
    lo = (a - hi.astype(F32)).astype(BF16)
    return hi, lo


def _rms_scale(x):
    return lax.rsqrt(jnp.mean(x * x, axis=-1, keepdims=True) + NORM_EPS)


def _silu(x):
    return x * (1.0 / (1.0 + jnp.exp(-x)))


def _params(*sem):
    return pltpu.CompilerParams(dimension_semantics=sem, vmem_limit_bytes=VMEM_LIMIT)


def _mod_kernel(c_ref, w_ref, b_ref, o_ref):
    c = c_ref[...]
    c_hi, c_lo = _split_bf16(_silu(c))
    w_hi, w_lo = _split_bf16(w_ref[...])
    lhs = jnp.concatenate([c_hi, c_lo], axis=0)
    r = _dot(lhs, w_hi)
    o_ref[...] = r[:BATCH] + r[BATCH:] + _dot(c_hi, w_lo) + b_ref[...]


def _adaln_mod(c, ada_w, ada_b):
    n_out = N_MOD * D_MODEL
    return pl.pallas_call(
        _mod_kernel,
        grid=(DEPTH, n_out // TN_MOD),
        in_specs=[
            pl.BlockSpec((BATCH, D_MODEL), lambda i, j: (0, 0)),
            pl.BlockSpec((None, D_MODEL, TN_MOD), lambda i, j: (i, 0, j)),
            pl.BlockSpec((None, 1, TN_MOD), lambda i, j: (i, 0, j)),
        ],
        out_specs=pl.BlockSpec((None, BATCH, TN_MOD), lambda i, j: (i, 0, j)),
        out_shape=jax.ShapeDtypeStruct((DEPTH, BATCH, n_out), F32),
        compiler_params=_params("arbitrary", "arbitrary"),
        name="adaln_mod",
    )(c, ada_w, ada_b.reshape(DEPTH, 1, n_out))


def _mod_spec(layer, k, tm):
    tiles_per_seq = SEQ // tm

    def index(m, *_):
        return ((layer * BATCH + m // tiles_per_seq) * N_MOD + k, 0, 0)

    return pl.BlockSpec((None, 1, D_MODEL), index)


def _full_spec(shape):
    zeros = (0,) * len(shape)
    return pl.BlockSpec(shape, lambda *_: zeros)


def _modulated_norm(x, g, scale, shift):
    return (x * _rms_scale(x) * g) * (1.0 + scale) + shift


def _inproj_gla_kernel(x_ref, shift_ref, scale_ref, g_ref, wqk_ref, wv_ref, wr_ref, wa_ref,
                       wa2_ref, ba_ref, br_ref,
                       qd_ref, ki_ref, ks_ref, v_ref, gate_ref, dec_ref, wbf_ref):
    @pl.when(pl.program_id(0) == 0)
    def _():
        for i, w_ref in enumerate((wqk_ref, wv_ref, wr_ref)):
            wbf_ref[i] = w_ref[...].astype(BF16)

    wqk_ref, wv_ref, wr_ref = wbf_ref.at[0], wbf_ref.at[1], wbf_ref.at[2]
    hb = _modulated_norm(x_ref[...], g_ref[...], scale_ref[...], shift_ref[...]).astype(BF16)
    a_hi, a_lo = _split_bf16(_dot(hb, wa_ref[...].astype(BF16)))
    r = _dot(hb, wr_ref[...]) + br_ref[...]
    gate_ref[...] = _silu(r).astype(BF16)

    w_hi, w_lo = _split_bf16(wa2_ref[...])
    la = _dot(jnp.concatenate([a_hi, a_lo, a_hi], axis=1),
              jnp.concatenate([w_hi, w_hi, w_lo], axis=0)) + ba_ref[...]
    qk = _dot(hb, wqk_ref[...])
    log_a = (jnp.minimum(la, 0.0) - jnp.log(1.0 + jnp.exp(-jnp.abs(la)))) * (1.0 / GLA_TAU)
    l_hi, l_lo = _split_bf16(log_a)
    l_cat = jnp.concatenate([l_hi, l_lo], axis=1)

    row = lax.broadcasted_iota(jnp.int32, (GLA_CHUNK, GLA_CHUNK), 0)
    col = lax.broadcasted_iota(jnp.int32, (GLA_CHUNK, GLA_CHUNK), 1)
    tril = jnp.where(row >= col, 1.0, 0.0).astype(BF16)
    q_scale = GLA_DK ** -0.5
    n_chunks = x_ref.shape[0] // GLA_CHUNK
    chunk = lambda n: slice(n * GLA_CHUNK, (n + 1) * GLA_CHUNK)
    cums = [_dot(tril, l_cat[chunk(n)]) for n in range(n_chunks)]
    v_ref[...] = _dot(hb, wv_ref[...]).astype(BF16)
    for n in range(n_chunks):
        sl = chunk(n)
        b = cums[n][:, :GLA_HK] + cums[n][:, GLA_HK:]
        b_last = b[GLA_CHUNK - 1:GLA_CHUNK, :]
        q = qk[sl, :GLA_HK] * q_scale
        k = qk[sl, GLA_HK:]
        qd_ref[sl, :] = (q * jnp.exp(b)).astype(BF16)
        ki_ref[sl, :] = (k * jnp.exp(-b)).astype(BF16)
        ks_ref[sl, :] = (k * jnp.exp(b_last - b)).astype(BF16)
        dec_ref[n] = jnp.exp(b_last)


def _inproj_gla(x2, mod3, layer, g, w_in, wa, wa2, ba, br):
    tm = TM_INPROJ
    row = lambda m: (m, 0)
    assert 2 * GLA_HK == D_MODEL
    col_block = lambda idx: _resident_spec((None, D_MODEL, D_MODEL), (0, 0, idx))
    return pl.pallas_call(
        _inproj_gla_kernel,
        grid=(N_TOK // tm,),
        in_specs=[
            pl.BlockSpec((tm, D_MODEL), row),
            _mod_spec(layer, 0, tm),
            _mod_spec(layer, 1, tm),
            _full_spec((1, D_MODEL)),
            col_block(0),
            col_block(1),
            col_block(2),
            _full_spec((D_MODEL, GLA_GATE_RANK)),
            _full_spec((GLA_GATE_RANK, GLA_HK)),
            _full_spec((1, GLA_HK)),
            _full_spec((1, D_MODEL)),
        ],
        out_specs=[
            pl.BlockSpec((tm, GLA_HK), row),
            pl.BlockSpec((tm, GLA_HK), row),
            pl.BlockSpec((tm, GLA_HK), row),
            pl.BlockSpec((tm, D_MODEL), row),
            pl.BlockSpec((tm, D_MODEL), row),
            pl.BlockSpec((tm // GLA_CHUNK, 1, GLA_HK), lambda m: (m, 0, 0)),
        ],
        out_shape=[
            jax.ShapeDtypeStruct((N_TOK, GLA_HK), BF16),
            jax.ShapeDtypeStruct((N_TOK, GLA_HK), BF16),
            jax.ShapeDtypeStruct((N_TOK, GLA_HK), BF16),
            jax.ShapeDtypeStruct((N_TOK, D_MODEL), BF16),
            jax.ShapeDtypeStruct((N_TOK, D_MODEL), BF16),
            jax.ShapeDtypeStruct((N_TOK // GLA_CHUNK, 1, GLA_HK), F32),
        ],
        scratch_shapes=[pltpu.VMEM((3, D_MODEL, D_MODEL), BF16)],
        compiler_params=_params("arbitrary"),
        name="inproj_gla",
    )(x2, mod3, mod3, g, w_in, w_in, w_in, wa, wa2, ba, br)


def _gla_kernel(qd_ref, ki_ref, ks_ref, v_ref, gate_ref, dec_ref, ng_ref, *rest, n_cast):
    cast_in, (o_ref, *cast_out), st_ref = rest[:n_cast], rest[n_cast:-1], rest[-1]
    for src, dst in zip(cast_in, cast_out):
        dst[...] = src[...].astype(BF16)

    @pl.when(pl.program_id(1) == 0)
    def _():
        st_ref[...] = jnp.zeros_like(st_ref)

    row = lax.broadcasted_iota(jnp.int32, (GLA_CHUNK, GLA_CHUNK), 0)
    col = lax.broadcasted_iota(jnp.int32, (GLA_CHUNK, GLA_CHUNK), 1)
    causal = row >= col
    n_chunks = qd_ref.shape[0] // GLA_CHUNK
    work = [(h, n) for h in range(GLA_HEADS) for n in range(n_chunks)]
    rows = lambda n: slice(n * GLA_CHUNK, (n + 1) * GLA_CHUNK)
    kcols = lambda h: slice(h * GLA_DK, (h + 1) * GLA_DK)
    vcols = lambda h: slice(h * GLA_DV, (h + 1) * GLA_DV)

    scores = {}
    updates = {}
    for h, n in work:
        scores[h, n] = lax.dot_general(qd_ref[rows(n), kcols(h)], ki_ref[rows(n), kcols(h)],
                                       NT_DIMS, preferred_element_type=F32)
        updates[h, n] = lax.dot_general(v_ref[rows(n), vcols(h)], ks_ref[rows(n), kcols(h)],
                                        TN_DIMS, preferred_element_type=F32)
    states = {}
    for h in range(GLA_HEADS):
        st = st_ref[h]
        for n in range(n_chunks):
            states[h, n] = st.astype(BF16)
            st = st * dec_ref[n][:, kcols(h)] + updates[h, n]
        st_ref[h] = st
    for h, n in work:
        s = jnp.where(causal, scores[h, n], 0.0).astype(BF16)
        o = _dot(s, v_ref[rows(n), vcols(h)]) + lax.dot_general(
            qd_ref[rows(n), kcols(h)], states[h, n], NT_DIMS, preferred_element_type=F32)
        y = o * _rms_scale(o) * ng_ref[...]
        o_ref[rows(n), vcols(h)] = (y * gate_ref[rows(n), vcols(h)].astype(F32)).astype(BF16)


def _gla_core(qd, ki, ks, v, gate, dec, ng, later_weights):
    tc = GLA_TC
    steps_t = SEQ // tc
    n_steps = BATCH * steps_t
    qk_spec = pl.BlockSpec((None, tc, GLA_HK), lambda b, t: (b, t, 0))
    v_spec = pl.BlockSpec((None, tc, D_MODEL), lambda b, t: (b, t, 0))
    flat = [w.reshape(-1, w.shape[-1]) for w in later_weights]
    slab_specs = [pl.BlockSpec((w.shape[0] // n_steps, w.shape[1]),
                               lambda b, t: (b * steps_t + t, 0)) for w in flat]
    out = pl.pallas_call(
        functools.partial(_gla_kernel, n_cast=len(flat)),
        grid=(BATCH, steps_t),
        in_specs=[
            qk_spec, qk_spec, qk_spec, v_spec, v_spec,
            pl.BlockSpec((None, tc // GLA_CHUNK, 1, GLA_HK), lambda b, t: (b, t, 0, 0)),
            _full_spec((1, GLA_DV)),
        ] + slab_specs,
        out_specs=[v_spec] + slab_specs,
        out_shape=[jax.ShapeDtypeStruct((BATCH, SEQ, D_MODEL), BF16)]
        + [jax.ShapeDtypeStruct(w.shape, BF16) for w in flat],
        scratch_shapes=[pltpu.VMEM((GLA_HEADS, GLA_DV, GLA_DK), F32)],
        compiler_params=_params("arbitrary", "arbitrary"),
        name="gla_core",
    )(qd.reshape(BATCH, SEQ, GLA_HK), ki.reshape(BATCH, SEQ, GLA_HK),
      ks.reshape(BATCH, SEQ, GLA_HK), v.reshape(BATCH, SEQ, D_MODEL),
      gate.reshape(BATCH, SEQ, D_MODEL),
      dec.reshape(BATCH, SEQ // GLA_CHUNK, 1, GLA_HK), ng, *flat)
    return out[0], [wb.reshape(w.shape) for wb, w in zip(out[1:], later_weights)]


def _rope_kernel(pos_ref, freq_ref, cs_ref):
    ang = freq_ref[...] * pos_ref[...]
    cs_ref[0:ROPE_HALF, :] = jnp.cos(ang)
    cs_ref[ROPE_HALF:, :] = jnp.sin(ang)


def _rope_tables(positions):
    inv_freq = ROPE_THETA ** (-jnp.arange(0, ROPE_DIM, 2, dtype=F32) / ROPE_DIM)
    return pl.pallas_call(
        _rope_kernel,
        grid=(1,),
        in_specs=[_full_spec((1, N_TOK)), _full_spec((ROPE_HALF, 1))],
        out_specs=_full_spec((ROPE_DIM, N_TOK)),
        out_shape=jax.ShapeDtypeStruct((ROPE_DIM, N_TOK), F32),
        compiler_params=_params("arbitrary"),
        name="rope_tables",
    )(positions.astype(F32).reshape(1, N_TOK), inv_freq.reshape(ROPE_HALF, 1))


def _rope_expander():
    e = np.zeros((3, ROPE_DIM, 2 * DIFF_DV), np.float32)
    for lane in range(DIFF_DV):
        d = lane % DIFF_DH
        if d < ROPE_DIM:
            e[:, d % ROPE_HALF, lane] = 1.0
            e[:, ROPE_HALF + d % ROPE_HALF, DIFF_DV + lane] = -1.0 if d < ROPE_HALF else 1.0
    return jnp.asarray(e.reshape(3 * ROPE_DIM, 2 * DIFF_DV), BF16)


def _inproj_diff_kernel(x_ref, shift_ref, scale_ref, g_ref, wq_ref, wk_ref, wv_ref,
                        cs_ref, e_ref, q_ref, k_ref, vt_ref, wvt_ref):
    @pl.when(pl.program_id(0) == 0)
    def _():
        wvt_ref[...] = wv_ref[...].T

    hb = _modulated_norm(x_ref[...], g_ref[...], scale_ref[...], shift_ref[...]).astype(BF16)

    cs = cs_ref[...]
    hi = cs.astype(BF16).astype(F32)
    mid = (cs - hi).astype(BF16).astype(F32)
    lo = cs - hi - mid
    parts = jnp.concatenate([hi, mid, lo], axis=0)
    tab = _dot(parts.T.astype(BF16), e_ref[...])
    d = lax.broadcasted_iota(jnp.int32, (1, DIFF_DV), 1) & (DIFF_DH - 1)
    ct = tab[:, :DIFF_DV] + jnp.where(d >= ROPE_DIM, 1.0, 0.0)
    s12 = tab[:, DIFF_DV:]
    first_half = d < ROPE_HALF

    q_scale = DIFF_DH ** -0.5 * math.log2(math.e)
    for w_ref, out_ref, scl in ((wq_ref, q_ref, q_scale), (wk_ref, k_ref, 1.0)):
        z = _dot(hb, w_ref[...])
        for h in range(DIFF_HEADS):
            sl = slice(h * DIFF_DV, (h + 1) * DIFF_DV)
            xh = z[:, sl]
            partner = jnp.where(first_half, pltpu.roll(xh, DIFF_DV - ROPE_HALF, 1),
                                pltpu.roll(xh, ROPE_HALF, 1))
            out_ref[:, sl] = ((xh * ct + partner * s12) * scl).astype(BF16)
    vt = lax.dot_general(wvt_ref[...], hb, NT_DIMS, preferred_element_type=F32).astype(BF16)
    ones = jnp.ones((VT_ROWS - DIFF_DV, vt.shape[1]), BF16)
    for h in range(DIFF_HEADS):
        vt_ref[h * VT_ROWS:h * VT_ROWS + DIFF_DV, :] = vt[h * DIFF_DV:(h + 1) * DIFF_DV]
        vt_ref[h * VT_ROWS + DIFF_DV:(h + 1) * VT_ROWS, :] = ones


def _inproj_diff(x2, mod3, layer, g, w_in, cs, expander):
    tm = TM_INPROJ
    row = lambda m: (m, 0)
    o_spec = pl.BlockSpec((tm, D_MODEL), row)
    return pl.pallas_call(
        _inproj_diff_kernel,
        grid=(N_TOK // tm,),
        in_specs=[
            pl.BlockSpec((tm, D_MODEL), row),
            _mod_spec(layer, 0, tm),
            _mod_spec(layer, 1, tm),
            _full_spec((1, D_MODEL)),
            _resident_spec((None, D_MODEL, D_MODEL), (0, 0, 0)),
            _resident_spec((None, D_MODEL, D_MODEL), (0, 0, 1)),
            _resident_spec((None, D_MODEL, D_MODEL), (0, 0, 2)),
            pl.BlockSpec((ROPE_DIM, tm), lambda m: (0, m)),
            _full_spec((3 * ROPE_DIM, 2 * DIFF_DV)),
        ],
        out_specs=[o_spec, o_spec, pl.BlockSpec((DIFF_HEADS * VT_ROWS, tm), lambda m: (0, m))],
        out_shape=[jax.ShapeDtypeStruct((N_TOK, D_MODEL), BF16)] * 2
        + [jax.ShapeDtypeStruct((DIFF_HEADS * VT_ROWS, N_TOK), BF16)],
        scratch_shapes=[pltpu.VMEM((D_MODEL, D_MODEL), BF16)],
        compiler_params=_params("arbitrary"),
        name="inproj_diff",
    )(x2, mod3, mod3, g, w_in, w_in, w_in, cs, expander)


def _attn_kernel(q_ref, k_ref, vt_ref, lam_ref, g_ref, o_ref, qs_ref, s_ref, m_ref, acc_ref,
                 *, lambda_init):
    tq, tk = ATT_TQ, ATT_TK
    assert tq == tk
    lf = lam_ref[...]
    lam = (jnp.exp(jnp.sum(lf[0:1] * lf[1:2], axis=-1, keepdims=True))
           - jnp.exp(jnp.sum(lf[2:3] * lf[3:4], axis=-1, keepdims=True)) + lambda_init)
    lane = lax.broadcasted_iota(jnp.int32, (tq, DIFF_DV), 1)
    k_pos = lax.broadcasted_iota(jnp.int32, (tk, 2 * tq), 0)
    q_pos = lax.broadcasted_iota(jnp.int32, (tk, 2 * tq), 1) & (tq - 1)
    heads = [slice(h * DIFF_DV, (h + 1) * DIFF_DV) for h in range(ATT_HEADS)]

    def stage_queries(qi, slot):
        q0 = pl.multiple_of(qi * tq, tq)
        for h, hs in enumerate(heads):
            q = q_ref[pl.ds(q0, tq), hs]
            qs_ref[slot, h, 0:tq, :] = jnp.where(lane < DIFF_DH, q, jnp.zeros_like(q))
            qs_ref[slot, h, tq:2 * tq, :] = jnp.where(lane >= DIFF_DH, q, jnp.zeros_like(q))

    def scores_for(j, h, slot):
        start = pl.multiple_of(j * tk, tk)
        return lax.dot_general(k_ref[pl.ds(start, tk), heads[h]], qs_ref[slot, h], NT_DIMS,
                               preferred_element_type=F32)

    def reset_state():
        m_ref[...] = jnp.full_like(m_ref, -1e30)
        acc_ref[...] = jnp.zeros_like(acc_ref)

    def kv_step(j, masked, next_j, next_slot):
        start = pl.multiple_of(j * tk, tk)
        for h in range(ATT_HEADS):
            vtb = vt_ref[h * VT_ROWS:(h + 1) * VT_ROWS, pl.ds(start, tk)]
            s = s_ref[h]
            if masked:
                s = jnp.where(k_pos <= q_pos, s, -1e30)
            m_prev = m_ref[h]
            m_new = jnp.maximum(m_prev, jnp.max(s, axis=0, keepdims=True))
            p = jnp.exp2(s - m_new)
            alpha = jnp.exp2(m_prev - m_new)
            acc_ref[h] = alpha * acc_ref[h] + _dot(vtb, p.astype(BF16))
            m_ref[h] = m_new
            s_ref[h] = scores_for(next_j, h, next_slot)

    n_q = SEQ // tq
    stage_queries(0, 0)
    reset_state()
    for h in range(ATT_HEADS):
        s_ref[h] = scores_for(0, h, 0)

    def q_block(qi, carry):
        q0 = pl.multiple_of(qi * tq, tq)
        slot = qi & 1

        def body(j, c):
            kv_step(j, False, j + 1, slot)
            return c

        lax.fori_loop(0, qi, body, 0)
        stage_queries(jnp.minimum(qi + 1, n_q - 1), 1 - slot)
        kv_step(qi, True, 0, 1 - slot)
        for h, hs in enumerate(heads):
            acc = acc_ref[h, 0:DIFF_DV, :]
            inv_l = 1.0 / acc_ref[h, DIFF_DV:DIFF_DV + 1, :]
            o = acc[:, :tq] * inv_l[:, :tq] - lam * (acc[:, tq:] * inv_l[:, tq:])
            rms = lax.rsqrt(jnp.mean(o * o, axis=0, keepdims=True) + NORM_EPS)
            y = o * rms * (g_ref[...] * (1.0 - lambda_init))
            o_ref[pl.ds(q0, tq), hs] = y.T.astype(BF16)
        reset_state()
        return carry

    lax.fori_loop(0, n_q, q_block, 0)


def _diff_attention(q, k, vt, lam_vecs, g, lambda_init):
    width = ATT_HEADS * DIFF_DV
    spec = pl.BlockSpec((None, SEQ, width), lambda b, h: (b, 0, h))
    return pl.pallas_call(
        functools.partial(_attn_kernel, lambda_init=lambda_init),
        grid=(BATCH, DIFF_HEADS // ATT_HEADS),
        in_specs=[spec, spec, pl.BlockSpec((ATT_HEADS * VT_ROWS, SEQ), lambda b, h: (h, b)),
                  _full_spec((4, DIFF_DH)), _full_spec((DIFF_DV, 1))],
        out_specs=spec,
        out_shape=jax.ShapeDtypeStruct((BATCH, SEQ, D_MODEL), BF16),
        scratch_shapes=[
            pltpu.VMEM((2, ATT_HEADS, 2 * ATT_TQ, DIFF_DV), BF16),
            pltpu.VMEM((ATT_HEADS, ATT_TK, 2 * ATT_TQ), F32),
            pltpu.VMEM((ATT_HEADS, 1, 2 * ATT_TQ), F32),
            pltpu.VMEM((ATT_HEADS, VT_ROWS, 2 * ATT_TQ), F32),
        ],
        compiler_params=_params("arbitrary", "arbitrary"),
        name="diff_attention",
    )(q.reshape(BATCH, SEQ, D_MODEL), k.reshape(BATCH, SEQ, D_MODEL), vt, lam_vecs, g)


def _mlp_kernel(x_ref, o_ref, wo_ref, gt_ref, sh_ref, sc_ref, gc_ref, g_ref, w1_ref, w2_ref,
                fg_ref, out_ref, *, final):
    x1 = x_ref[...] + gt_ref[...] * _dot(o_ref[...], wo_ref[...])
    hb = _modulated_norm(x1, g_ref[...], sc_ref[...], sh_ref[...]).astype(BF16)
    a = jnp.square(jnp.maximum(_dot(hb, w1_ref[...]), 0.0)).astype(BF16)
    out = x1 + gc_ref[...] * _dot(a, w2_ref[...])
    if final:
        out = out * _rms_scale(out) * fg_ref[...]
    out_ref[...] = out


def _resident_spec(block_shape, index):
    return pl.BlockSpec(block_shape, lambda m: index, pipeline_mode=pl.Buffered(1))


def _outproj_mlp(x2, o2, mod3, layer, wo, g, w1, w2, fg, final):
    tm = TM_MLP
    row = lambda m: (m, 0)
    return pl.pallas_call(
        functools.partial(_mlp_kernel, final=final),
        grid=(N_TOK // tm,),
        in_specs=[
            pl.BlockSpec((tm, D_MODEL), row),
            pl.BlockSpec((tm, D_MODEL), row),
            _resident_spec((None, D_MODEL, D_MODEL), (0, 0, 0)),
            _mod_spec(layer, 2, tm),
            _mod_spec(layer, 3, tm),
            _mod_spec(layer, 4, tm),
            _mod_spec(layer, 5, tm),
            _full_spec((1, D_MODEL)),
            _resident_spec((None, D_MODEL, D_FF), (layer, 0, 0)),
            _resident_spec((None, D_FF, D_MODEL), (layer, 0, 0)),
            _full_spec((1, D_MODEL)),
        ],
        out_specs=pl.BlockSpec((tm, D_MODEL), row),
        out_shape=jax.ShapeDtypeStruct((N_TOK, D_MODEL), F32),
        compiler_params=_params("arbitrary"),
        name="outproj_mlp_final" if final else "outproj_mlp",
    )(x2, o2, wo, mod3, mod3, mod3, mod3, g, w1, w2, fg)


def kernel(x, c, positions, ada_w, ada_b, norm_g, mlp_w1, mlp_w2, gla_w_in, gla_w_a2, gla_b_a,
           gla_b_r, gla_norm_g, gla_w_o, diff_w_in, diff_lambda, diff_subln_g, diff_w_o, final_g):
    bf = lambda a: a.astype(BF16)
    x2 = x.reshape(N_TOK, D_MODEL)
    mod3 = _adaln_mod(c, ada_w, ada_b).reshape(DEPTH * BATCH * N_MOD, 1, D_MODEL)
    fg = final_g.reshape(1, D_MODEL)

    qd, ki, ks, v, gate, dec = _inproj_gla(
        x2, mod3, 0, norm_g[0, 0].reshape(1, D_MODEL), gla_w_in,
        gla_w_in[0, :, 3 * D_MODEL:], gla_w_a2[0], gla_b_a[0].reshape(1, GLA_HK), gla_b_r[0].reshape(1, D_MODEL))
    o, (w1, w2, gla_wo, diff_win, diff_wo) = _gla_core(
        qd, ki, ks, v, gate, dec, gla_norm_g[0].reshape(1, GLA_DV),
        [mlp_w1, mlp_w2, gla_w_o, diff_w_in, diff_w_o])
    x2 = _outproj_mlp(x2, o.reshape(N_TOK, D_MODEL), mod3, 0, gla_wo,
                      norm_g[0, 1].reshape(1, D_MODEL), w1, w2, fg, False)

    lambda_init = 0.8 - 0.6 * math.exp(-0.3 * 1)
    cs = _rope_tables(positions)
    q, k, vt = _inproj_diff(
        x2, mod3, 1, norm_g[1, 0].reshape(1, D_MODEL), diff_win, cs, _rope_expander())
    o = _diff_attention(q, k, vt, diff_lambda[0], diff_subln_g[0].reshape(DIFF_DV, 1), lambda_init)
    x2 = _outproj_mlp(x2, o.reshape(N_TOK, D_MODEL), mod3, 1, diff_wo,
                      norm_g[1, 1].reshape(1, D_MODEL), w1, w2, fg, True)
    return x2.reshape(BATCH, SEQ, D_MODEL)
```

```python
import functools
import math

import jax
import jax.numpy as jnp
import numpy as np
from jax import lax
from jax.experimental import pallas as pl
from jax.experimental.pallas import tpu as pltpu

D_MODEL = 1024
BATCH = 8
SEQ = 2048
DEPTH = 2
D_FF = 4 * D_MODEL
NORM_EPS = 1e-6
GLA_HEADS = 4
GLA_DK = 128
GLA_DV = 256
GLA_HK = GLA_HEADS * GLA_DK
GLA_GATE_RANK = 16
GLA_TAU = 16.0
GLA_CHUNK = 64
DIFF_HEADS = 8
DIFF_DH = 64
DIFF_DV = 128
ROPE_THETA = 500000.0
ROPE_DIM = DIFF_DH // 4
ROPE_HALF = ROPE_DIM // 2
N_TOK = BATCH * SEQ
N_MOD = 6

TM_INPROJ = 1024
TM_MLP = 512
TF_MLP = 1024
TN_MOD = 1536
GLA_TC = 512
ATT_TQ = 256
ATT_TK = 256
VT_ROWS = DIFF_DV + 16
ATT_HEADS = 8
VMEM_LIMIT = 56 * 1024 * 1024

F32 = jnp.float32
BF16 = jnp.bfloat16
MASKED = -1e30
NT_DIMS = (((1,), (1,)), ((), ()))
TN_DIMS = (((0,), (0,)), ((), ()))


def _dot(a, b):
    return jnp.dot(a, b, preferred_element_type=F32)


def _split_bf16(a):
    hi = a.astype(BF16)
    lo = (a - hi.astype(F32)).astype(BF16)
    return hi, lo


def _rms_scale(x):
    return lax.rsqrt(jnp.mean(x * x, axis=-1, keepdims=True) + NORM_EPS)


def _silu(x):
    return x * (1.0 / (1.0 + jnp.exp(-x)))


def _params(*sem):
    return pltpu.CompilerParams(dimension_semantics=sem, vmem_limit_bytes=VMEM_LIMIT)


def _mod_kernel(c_ref, w_ref, b_ref, o_ref):
    c = c_ref[...]
    c_hi, c_lo = _split_bf16(_silu(c))
    w_hi, w_lo = _split_bf16(w_ref[...])
    lhs = jnp.concatenate([c_hi, c_lo], axis=0)
    r = _dot(lhs, w_hi)
    o_ref[...] = r[:BATCH] + r[BATCH:] + _dot(c_hi, w_lo) + b_ref[...]


def _adaln_mod(c, ada_w, ada_b):
    n_out = N_MOD * D_MODEL
    return pl.pallas_call(
        _mod_kernel,
        grid=(DEPTH, n_out // TN_MOD),
        in_specs=[
            pl.BlockSpec((BATCH, D_MODEL), lambda i, j: (0, 0)),
            pl.BlockSpec((None, D_MODEL, TN_MOD), lambda i, j: (i, 0, j)),
            pl.BlockSpec((None, 1, TN_MOD), lambda i, j: (i, 0, j)),
        ],
        out_specs=pl.BlockSpec((None, BATCH, TN_MOD), lambda i, j: (i, 0, j)),
        out_shape=jax.ShapeDtypeStruct((DEPTH, BATCH, n_out), F32),
        compiler_params=_params("arbitrary", "arbitrary"),
        name="adaln_mod",
    )(c, ada_w, ada_b.reshape(DEPTH, 1, n_out))


def _mod_spec(layer, k, tm):
    tiles_per_seq = SEQ // tm

    def index(m, *_):
        return ((layer * BATCH + m // tiles_per_seq) * N_MOD + k, 0, 0)

    return pl.BlockSpec((None, 1, D_MODEL), index)


def _full_spec(shape):
    zeros = (0,) * len(shape)
    return pl.BlockSpec(shape, lambda *_: zeros)


def _modulated_norm(x, g, scale, shift):
    return (x * _rms_scale(x) * g) * (1.0 + scale) + shift


def _inproj_gla_kernel(x_ref, shift_ref, scale_ref, g_ref, wqk_ref, wv_ref, wr_ref, wa_ref,
                       wa2_ref, ba_ref, br_ref,
                       qd_ref, ki_ref, ks_ref, v_ref, gate_ref, dec_ref, wbf_ref):
    @pl.when(pl.program_id(0) == 0)
    def _():
        for i, w_ref in enumerate((wqk_ref, wv_ref, wr_ref)):
            wbf_ref[i] = w_ref[...].astype(BF16)

    wqk_ref, wv_ref, wr_ref = wbf_ref.at[0], wbf_ref.at[1], wbf_ref.at[2]
    hb = _modulated_norm(x_ref[...], g_ref[...], scale_ref[...], shift_ref[...]).astype(BF16)
    a_hi, a_lo = _split_bf16(_dot(hb, wa_ref[...].astype(BF16)))
    r = _dot(hb, wr_ref[...]) + br_ref[...]
    gate_ref[...] = _silu(r).astype(BF16)

    w_hi, w_lo = _split_bf16(wa2_ref[...])
    la = _dot(jnp.concatenate([a_hi, a_lo, a_hi], axis=1),
              jnp.concatenate([w_hi, w_hi, w_lo], axis=0)) + ba_ref[...]
    qk = _dot(hb, wqk_ref[...])
    log_a = (jnp.minimum(la, 0.0) - jnp.log(1.0 + jnp.exp(-jnp.abs(la)))) * (1.0 / GLA_TAU)
    l_hi, l_lo = _split_bf16(log_a)
    l_cat = jnp.concatenate([l_hi, l_lo], axis=1)

    row = lax.broadcasted_iota(jnp.int32, (GLA_CHUNK, GLA_CHUNK), 0)
    col = lax.broadcasted_iota(jnp.int32, (GLA_CHUNK, GLA_CHUNK), 1)
    tril = jnp.where(row >= col, 1.0, 0.0).astype(BF16)
    q_scale = GLA_DK ** -0.5
    n_chunks = x_ref.shape[0] // GLA_CHUNK
    chunk = lambda n: slice(n * GLA_CHUNK, (n + 1) * GLA_CHUNK)
    cums = [_dot(tril, l_cat[chunk(n)]) for n in range(n_chunks)]
    v_ref[...] = _dot(hb, wv_ref[...]).astype(BF16)
    for n in range(n_chunks):
        sl = chunk(n)
        b = cums[n][:, :GLA_HK] + cums[n][:, GLA_HK:]
        b_last = b[GLA_CHUNK - 1:GLA_CHUNK, :]
        q = qk[sl, :GLA_HK] * q_scale
        k = qk[sl, GLA_HK:]
        qd_ref[sl, :] = (q * jnp.exp(b)).astype(BF16)
        ki_ref[sl, :] = (k * jnp.exp(-b)).astype(BF16)
        ks_ref[sl, :] = (k * jnp.exp(b_last - b)).astype(BF16)
        dec_ref[n] = jnp.exp(b_last)


def _inproj_gla(x2, mod3, layer, g, w_in, wa, wa2, ba, br):
    tm = TM_INPROJ
    row = lambda m: (m, 0)
    assert 2 * GLA_HK == D_MODEL
    col_block = lambda idx: _resident_spec((None, D_MODEL, D_MODEL), (0, 0, idx))
    return pl.pallas_call(
        _inproj_gla_kernel,
        grid=(N_TOK // tm,),
        in_specs=[
            pl.BlockSpec((tm, D_MODEL), row),
            _mod_spec(layer, 0, tm),
            _mod_spec(layer, 1, tm),
            _full_spec((1, D_MODEL)),
            col_block(0),
            col_block(1),
            col_block(2),
            _full_spec((D_MODEL, GLA_GATE_RANK)),
            _full_spec((GLA_GATE_RANK, GLA_HK)),
            _full_spec((1, GLA_HK)),
            _full_spec((1, D_MODEL)),
        ],
        out_specs=[
            pl.BlockSpec((tm, GLA_HK), row),
            pl.BlockSpec((tm, GLA_HK), row),
            pl.BlockSpec((tm, GLA_HK), row),
            pl.BlockSpec((tm, D_MODEL), row),
            pl.BlockSpec((tm, D_MODEL), row),
            pl.BlockSpec((tm // GLA_CHUNK, 1, GLA_HK), lambda m: (m, 0, 0)),
        ],
        out_shape=[
            jax.ShapeDtypeStruct((N_TOK, GLA_HK), BF16),
            jax.ShapeDtypeStruct((N_TOK, GLA_HK), BF16),
            jax.ShapeDtypeStruct((N_TOK, GLA_HK), BF16),
            jax.ShapeDtypeStruct((N_TOK, D_MODEL), BF16),
            jax.ShapeDtypeStruct((N_TOK, D_MODEL), BF16),
            jax.ShapeDtypeStruct((N_TOK // GLA_CHUNK, 1, GLA_HK), F32),
        ],
        scratch_shapes=[pltpu.VMEM((3, D_MODEL, D_MODEL), BF16)],
        compiler_params=_params("arbitrary"),
        name="inproj_gla",
    )(x2, mod3, mod3, g, w_in, w_in, w_in, wa, wa2, ba, br)


def _gla_kernel(qd_ref, ki_ref, ks_ref, v_ref, gate_ref, dec_ref, ng_ref, *rest, n_cast):
    cast_in, (o_ref, *cast_out), st_ref = rest[:n_cast], rest[n_cast:-1], rest[-1]
    for src, dst in zip(cast_in, cast_out):
        dst[...] = src[...].astype(BF16)

    @pl.when(pl.program_id(1) == 0)
    def _():
        st_ref[...] = jnp.zeros_like(st_ref)

    row = lax.broadcasted_iota(jnp.int32, (GLA_CHUNK, GLA_CHUNK), 0)
    col = lax.broadcasted_iota(jnp.int32, (GLA_CHUNK, GLA_CHUNK), 1)
    causal = row >= col
    n_chunks = qd_ref.shape[0] // GLA_CHUNK
    work = [(h, n) for h in range(GLA_HEADS) for n in range(n_chunks)]
    rows = lambda n: slice(n * GLA_CHUNK, (n + 1) * GLA_CHUNK)
    kcols = lambda h: slice(h * GLA_DK, (h + 1) * GLA_DK)
    vcols = lambda h: slice(h * GLA_DV, (h + 1) * GLA_DV)

    scores = {}
    updates = {}
    for h, n in work:
        scores[h, n] = lax.dot_general(qd_ref[rows(n), kcols(h)], ki_ref[rows(n), kcols(h)],
                                       NT_DIMS, preferred_element_type=F32)
        updates[h, n] = lax.dot_general(v_ref[rows(n), vcols(h)], ks_ref[rows(n), kcols(h)],
                                        TN_DIMS, preferred_element_type=F32)
    states = {}
    for h in range(GLA_HEADS):
        st = st_ref[h]
        for n in range(n_chunks):
            states[h, n] = st.astype(BF16)
            st = st * dec_ref[n][:, kcols(h)] + updates[h, n]
        st_ref[h] = st
    for h, n in work:
        s = jnp.where(causal, scores[h, n], 0.0).astype(BF16)
        o = _dot(s, v_ref[rows(n), vcols(h)]) + lax.dot_general(
            qd_ref[rows(n), kcols(h)], states[h, n], NT_DIMS, preferred_element_type=F32)
        y = o * _rms_scale(o) * ng_ref[...]
        o_ref[rows(n), vcols(h)] = (y * gate_ref[rows(n), vcols(h)].astype(F32)).astype(BF16)


def _gla_core(qd, ki, ks, v, gate, dec, ng, later_weights):
    tc = GLA_TC
    steps_t = SEQ // tc
    n_steps = BATCH * steps_t
    qk_spec = pl.BlockSpec((None, tc, GLA_HK), lambda b, t: (b, t, 0))
    v_spec = pl.BlockSpec((None, tc, D_MODEL), lambda b, t: (b, t, 0))
    flat = [w.reshape(-1, w.shape[-1]) for w in later_weights]
    slab_specs = [pl.BlockSpec((w.shape[0] // n_steps, w.shape[1]),
                               lambda b, t: (b * steps_t + t, 0)) for w in flat]
    out = pl.pallas_call(
        functools.partial(_gla_kernel, n_cast=len(flat)),
        grid=(BATCH, steps_t),
        in_specs=[
            qk_spec, qk_spec, qk_spec, v_spec, v_spec,
            pl.BlockSpec((None, tc // GLA_CHUNK, 1, GLA_HK), lambda b, t: (b, t, 0, 0)),
            _full_spec((1, GLA_DV)),
        ] + slab_specs,
        out_specs=[v_spec] + slab_specs,
        out_shape=[jax.ShapeDtypeStruct((BATCH, SEQ, D_MODEL), BF16)]
        + [jax.ShapeDtypeStruct(w.shape, BF16) for w in flat],
        scratch_shapes=[pltpu.VMEM((GLA_HEADS, GLA_DV, GLA_DK), F32)],
        compiler_params=_params("arbitrary", "arbitrary"),
        name="gla_core",
    )(qd.reshape(BATCH, SEQ, GLA_HK), ki.reshape(BATCH, SEQ, GLA_HK),
      ks.reshape(BATCH, SEQ, GLA_HK), v.reshape(BATCH, SEQ, D_MODEL),
      gate.reshape(BATCH, SEQ, D_MODEL),
      dec.reshape(BATCH, SEQ // GLA_CHUNK, 1, GLA_HK), ng, *flat)
    return out[0], [wb.reshape(w.shape) for wb, w in zip(out[1:], later_weights)]


def _rope_kernel(pos_ref, freq_ref, cs_ref):
    ang = freq_ref[...] * pos_ref[...]
    cs_ref[0:ROPE_HALF, :] = jnp.cos(ang)
    cs_ref[ROPE_HALF:, :] = jnp.sin(ang)


def _rope_tables(positions):
    inv_freq = ROPE_THETA ** (-jnp.arange(0, ROPE_DIM, 2, dtype=F32) / ROPE_DIM)
    return pl.pallas_call(
        _rope_kernel,
        grid=(1,),
        in_specs=[_full_spec((1, N_TOK)), _full_spec((ROPE_HALF, 1))],
        out_specs=_full_spec((ROPE_DIM, N_TOK)),
        out_shape=jax.ShapeDtypeStruct((ROPE_DIM, N_TOK), F32),
        compiler_params=_params("arbitrary"),
        name="rope_tables",
    )(positions.astype(F32).reshape(1, N_TOK), inv_freq.reshape(ROPE_HALF, 1))


def _rope_expander():
    e = np.zeros((3, ROPE_DIM, 2 * DIFF_DV), np.float32)
    for lane in range(DIFF_DV):
        d = lane % DIFF_DH
        if d < ROPE_DIM:
            e[:, d % ROPE_HALF, lane] = 1.0
            e[:, ROPE_HALF + d % ROPE_HALF, DIFF_DV + lane] = -1.0 if d < ROPE_HALF else 1.0
    return jnp.asarray(e.reshape(3 * ROPE_DIM, 2 * DIFF_DV), BF16)


def _inproj_diff_kernel(x_ref, shift_ref, scale_ref, g_ref, wq_ref, wk_ref, wv_ref,
                        cs_ref, e_ref, q_ref, k_ref, vt_ref, wvt_ref):
    @pl.when(pl.program_id(0) == 0)
    def _():
        wvt_ref[...] = wv_ref[...].T

    hb = _modulated_norm(x_ref[...], g_ref[...], scale_ref[...], shift_ref[...]).astype(BF16)

    cs = cs_ref[...]
    hi = cs.astype(BF16).astype(F32)
    mid = (cs - hi).astype(BF16).astype(F32)
    lo = cs - hi - mid
    parts = jnp.concatenate([hi, mid, lo], axis=0)
    tab = _dot(parts.T.astype(BF16), e_ref[...])
    d = lax.broadcasted_iota(jnp.int32, (1, DIFF_DV), 1) & (DIFF_DH - 1)
    ct = tab[:, :DIFF_DV] + jnp.where(d >= ROPE_DIM, 1.0, 0.0)
    s12 = tab[:, DIFF_DV:]
    first_half = d < ROPE_HALF

    q_scale = DIFF_DH ** -0.5 * math.log2(math.e)
    for w_ref, out_ref, scl in ((wq_ref, q_ref, q_scale), (wk_ref, k_ref, 1.0)):
        z = _dot(hb, w_ref[...])
        for h in range(DIFF_HEADS):
            sl = slice(h * DIFF_DV, (h + 1) * DIFF_DV)
            xh = z[:, sl]
            partner = jnp.where(first_half, pltpu.roll(xh, DIFF_DV - ROPE_HALF, 1),
                                pltpu.roll(xh, ROPE_HALF, 1))
            out_ref[:, sl] = ((xh * ct + partner * s12) * scl).astype(BF16)
    vt = lax.dot_general(wvt_ref[...], hb, NT_DIMS, preferred_element_type=F32).astype(BF16)
    ones = jnp.ones((VT_ROWS - DIFF_DV, vt.shape[1]), BF16)
    for h in range(DIFF_HEADS):
        vt_ref[h * VT_ROWS:h * VT_ROWS + DIFF_DV, :] = vt[h * DIFF_DV:(h + 1) * DIFF_DV]
        vt_ref[h * VT_ROWS + DIFF_DV:(h + 1) * VT_ROWS, :] = ones


def _inproj_diff(x2, mod3, layer, g, w_in, cs, expander):
    tm = TM_INPROJ
    row = lambda m: (m, 0)
    o_spec = pl.BlockSpec((tm, D_MODEL), row)
    return pl.pallas_call(
        _inproj_diff_kernel,
        grid=(N_TOK // tm,),
        in_specs=[
            pl.BlockSpec((tm, D_MODEL), row),
            _mod_spec(layer, 0, tm),
            _mod_spec(layer, 1, tm),
            _full_spec((1, D_MODEL)),
            _resident_spec((None, D_MODEL, D_MODEL), (0, 0, 0)),
            _resident_spec((None, D_MODEL, D_MODEL), (0, 0, 1)),
            _resident_spec((None, D_MODEL, D_MODEL), (0, 0, 2)),
            pl.BlockSpec((ROPE_DIM, tm), lambda m: (0, m)),
            _full_spec((3 * ROPE_DIM, 2 * DIFF_DV)),
        ],
        out_specs=[o_spec, o_spec, pl.BlockSpec((DIFF_HEADS * VT_ROWS, tm), lambda m: (0, m))],
        out_shape=[jax.ShapeDtypeStruct((N_TOK, D_MODEL), BF16)] * 2
        + [jax.ShapeDtypeStruct((DIFF_HEADS * VT_ROWS, N_TOK), BF16)],
        scratch_shapes=[pltpu.VMEM((D_MODEL, D_MODEL), BF16)],
        compiler_params=_params("arbitrary"),
        name="inproj_diff",
    )(x2, mod3, mod3, g, w_in, w_in, w_in, cs, expander)


def _attn_kernel(q_ref, k_ref, vt_ref, lam_ref, g_ref, o_ref, qs_ref, s_ref, m_ref, acc_ref,
                 *, lambda_init):
    tq, tk = ATT_TQ, ATT_TK
    assert tq == tk
    lf = lam_ref[...]
    lam = (jnp.exp(jnp.sum(lf[0:1] * lf[1:2], axis=-1, keepdims=True))
           - jnp.exp(jnp.sum(lf[2:3] * lf[3:4], axis=-1, keepdims=True)) + lambda_init)
    lane = lax.broadcasted_iota(jnp.int32, (tq, DIFF_DV), 1)
    k_pos = lax.broadcasted_iota(jnp.int32, (tk, 2 * tq), 0)
    q_pos = lax.broadcasted_iota(jnp.int32, (tk, 2 * tq), 1) & (tq - 1)
    heads = [slice(h * DIFF_DV, (h + 1) * DIFF_DV) for h in range(ATT_HEADS)]

    def stage_queries(qi, slot):
        q0 = pl.multiple_of(qi * tq, tq)
        for h, hs in enumerate(heads):
            q = q_ref[pl.ds(q0, tq), hs]
            qs_ref[slot, h, 0:tq, :] = jnp.where(lane < DIFF_DH, q, jnp.zeros_like(q))
            qs_ref[slot, h, tq:2 * tq, :] = jnp.where(lane >= DIFF_DH, q, jnp.zeros_like(q))

    def scores_for(j, h, slot):
        start = pl.multiple_of(j * tk, tk)
        return lax.dot_general(k_ref[pl.ds(start, tk), heads[h]], qs_ref[slot, h], NT_DIMS,
                               preferred_element_type=F32)

    def reset_state():
        m_ref[...] = jnp.full_like(m_ref, MASKED)
        acc_ref[...] = jnp.zeros_like(acc_ref)

    def kv_step(j, masked, next_j, next_slot):
        start = pl.multiple_of(j * tk, tk)
        for h in range(ATT_HEADS):
            vtb = vt_ref[h * VT_ROWS:(h + 1) * VT_ROWS, pl.ds(start, tk)]
            s = s_ref[h]
            next_s = scores_for(next_j, h, next_slot)
            if masked:
                s = jnp.where(k_pos <= q_pos, s, MASKED)
            m_prev = m_ref[h]
            m_new = jnp.maximum(m_prev, jnp.max(s, axis=0, keepdims=True))
            p = jnp.exp2(s - m_new)
            alpha = jnp.exp2(m_prev.astype(F32) - m_new.astype(F32))
            acc_ref[h] = alpha * acc_ref[h] + _dot(vtb, p)
            m_ref[h] = m_new
            s_ref[h] = next_s.astype(BF16)

    n_q = SEQ // tq
    stage_queries(0, 0)
    reset_state()
    for h in range(ATT_HEADS):
        s_ref[h] = scores_for(0, h, 0).astype(BF16)

    def q_block(qi, carry):
        q0 = pl.multiple_of(qi * tq, tq)
        slot = qi & 1

        def body(j, c):
            kv_step(j, False, j + 1, slot)
            return c

        lax.fori_loop(0, qi, body, 0)
        stage_queries(jnp.minimum(qi + 1, n_q - 1), 1 - slot)
        kv_step(qi, True, 0, 1 - slot)
        for h, hs in enumerate(heads):
            acc = acc_ref[h, 0:DIFF_DV, :]
            inv_l = 1.0 / acc_ref[h, DIFF_DV:DIFF_DV + 1, :]
            o = acc[:, :tq] * inv_l[:, :tq] - lam * (acc[:, tq:] * inv_l[:, tq:])
            rms = lax.rsqrt(jnp.mean(o * o, axis=0, keepdims=True) + NORM_EPS)
            y = o * rms * (g_ref[...] * (1.0 - lambda_init))
            o_ref[pl.ds(q0, tq), hs] = y.T.astype(BF16)
        reset_state()
        return carry

    lax.fori_loop(0, n_q, q_block, 0)


def _diff_attention(q, k, vt, lam_vecs, g, lambda_init):
    width = ATT_HEADS * DIFF_DV
    spec = pl.BlockSpec((None, SEQ, width), lambda b, h: (b, 0, h))
    return pl.pallas_call(
        functools.partial(_attn_kernel, lambda_init=lambda_init),
        grid=(BATCH, DIFF_HEADS // ATT_HEADS),
        in_specs=[spec, spec, pl.BlockSpec((ATT_HEADS * VT_ROWS, SEQ), lambda b, h: (h, b)),
                  _full_spec((4, DIFF_DH)), _full_spec((DIFF_DV, 1))],
        out_specs=spec,
        out_shape=jax.ShapeDtypeStruct((BATCH, SEQ, D_MODEL), BF16),
        scratch_shapes=[
            pltpu.VMEM((2, ATT_HEADS, 2 * ATT_TQ, DIFF_DV), BF16),
            pltpu.VMEM((ATT_HEADS, ATT_TK, 2 * ATT_TQ), BF16),
            pltpu.VMEM((ATT_HEADS, 1, 2 * ATT_TQ), BF16),
            pltpu.VMEM((ATT_HEADS, VT_ROWS, 2 * ATT_TQ), F32),
        ],
        compiler_params=_params("arbitrary", "arbitrary"),
        name="diff_attention",
    )(q.reshape(BATCH, SEQ, D_MODEL), k.reshape(BATCH, SEQ, D_MODEL), vt, lam_vecs, g)


def _mlp_kernel(x_ref, o_ref, wo_ref, gt_ref, sh_ref, sc_ref, gc_ref, g_ref, w1_ref, w2_ref,
                fg_ref, out_ref, *, final):
    x1 = x_ref[...] + gt_ref[...] * _dot(o_ref[...], wo_ref[...])
    hb = _modulated_norm(x1, g_ref[...], sc_ref[...], sh_ref[...]).astype(BF16)
    a = jnp.square(jnp.maximum(_dot(hb, w1_ref[...]), 0.0)).astype(BF16)
    out = x1 + gc_ref[...] * _dot(a, w2_ref[...])
    if final:
        out = out * _rms_scale(out) * fg_ref[...]
    out_ref[...] = out


def _resident_spec(block_shape, index):
    return pl.BlockSpec(block_shape, lambda m: index, pipeline_mode=pl.Buffered(1))


def _outproj_mlp(x2, o2, mod3, layer, wo, g, w1, w2, fg, final):
    tm = TM_MLP
    row = lambda m: (m, 0)
    return pl.pallas_call(
        functools.partial(_mlp_kernel, final=final),
        grid=(N_TOK // tm,),
        in_specs=[
            pl.BlockSpec((tm, D_MODEL), row),
            pl.BlockSpec((tm, D_MODEL), row),
            _resident_spec((None, D_MODEL, D_MODEL), (0, 0, 0)),
            _mod_spec(layer, 2, tm),
            _mod_spec(layer, 3, tm),
            _mod_spec(layer, 4, tm),
            _mod_spec(layer, 5, tm),
            _full_spec((1, D_MODEL)),
            _resident_spec((None, D_MODEL, D_FF), (layer, 0, 0)),
            _resident_spec((None, D_FF, D_MODEL), (layer, 0, 0)),
            _full_spec((1, D_MODEL)),
        ],
        out_specs=pl.BlockSpec((tm, D_MODEL), row),
        out_shape=jax.ShapeDtypeStruct((N_TOK, D_MODEL), F32),
        compiler_params=_params("arbitrary"),
        name="outproj_mlp_final" if final else "outproj_mlp",
    )(x2, o2, wo, mod3, mod3, mod3, mod3, g, w1, w2, fg)


def kernel(x, c, positions, ada_w, ada_b, norm_g, mlp_w1, mlp_w2, gla_w_in, gla_w_a2, gla_b_a,
           gla_b_r, gla_norm_g, gla_w_o, diff_w_in, diff_lambda, diff_subln_g, diff_w_o, final_g):
    bf = lambda a: a.astype(BF16)
    x2 = x.reshape(N_TOK, D_MODEL)
    mod3 = _adaln_mod(c, ada_w, ada_b).reshape(DEPTH * BATCH * N_MOD, 1, D_MODEL)
    fg = final_g.reshape(1, D_MODEL)

    qd, ki, ks, v, gate, dec = _inproj_gla(
        x2, mod3, 0, norm_g[0, 0].reshape(1, D_MODEL), gla_w_in,
        gla_w_in[0, :, 3 * D_MODEL:], gla_w_a2[0], gla_b_a[0].reshape(1, GLA_HK), gla_b_r[0].reshape(1, D_MODEL))
    o, (w1, w2, gla_wo, diff_win, diff_wo) = _gla_core(
        qd, ki, ks, v, gate, dec, gla_norm_g[0].reshape(1, GLA_DV),
        [mlp_w1, mlp_w2, gla_w_o, diff_w_in, diff_w_o])
    x2 = _outproj_mlp(x2, o.reshape(N_TOK, D_MODEL), mod3, 0, gla_wo,
                      norm_g[0, 1].reshape(1, D_MODEL), w1, w2, fg, False)

    lambda_init = 0.8 - 0.6 * math.exp(-0.3 * 1)
    cs = _rope_tables(positions)
    q, k, vt = _inproj_diff(
        x2, mod3, 1, norm_g[1, 0].reshape(1, D_MODEL), diff_win, cs, _rope_expander())
    o = _diff_attention(q, k, vt, diff_lambda[0], diff_subln_g[0].reshape(DIFF_DV, 1), lambda_init)
    x2 = _outproj_mlp(x2, o.reshape(N_TOK, D_MODEL), mod3, 1, diff_wo,
                      norm_g[1, 1].reshape(1, D_MODEL), w1, w2, fg, True)
    return x2.reshape(BATCH, SEQ, D_MODEL)
```

```python
import functools
import math

import jax
import jax.numpy as jnp
import numpy as np
from jax import lax
from jax.experimental import pallas as pl
from jax.experimental.pallas import tpu as pltpu

D_MODEL = 1024
BATCH = 8
SEQ = 2048
DEPTH = 2
D_FF = 4 * D_MODEL
NORM_EPS = 1e-6
GLA_HEADS = 4
GLA_DK = 128
GLA_DV = 256
GLA_HK = GLA_HEADS * GLA_DK
GLA_GATE_RANK = 16
GLA_TAU = 16.0
GLA_CHUNK = 64
DIFF_HEADS = 8
DIFF_DH = 64
DIFF_DV = 128
ROPE_THETA = 500000.0
ROPE_DIM = DIFF_DH // 4
ROPE_HALF = ROPE_DIM // 2
N_TOK = BATCH * SEQ
N_MOD = 6

TM_INPROJ = 1024
TM_MLP = 512
TF_MLP = 1024
TN_MOD = 1536
GLA_TC = 512
ATT_TQ = 256
ATT_TK = 256
VT_ROWS = DIFF_DV + 16
ATT_HEADS = 8
VMEM_LIMIT = 56 * 1024 * 1024

F32 = jnp.float32
BF16 = jnp.bfloat16
MASKED = -1e30
NT_DIMS = (((1,), (1,)), ((), ()))
TN_DIMS = (((0,), (0,)), ((), ()))


def _dot(a, b):
    return jnp.dot(a, b, preferred_element_type=F32)


def _split_bf16(a):
    hi = a.astype(BF16)
    lo = (a - hi.astype(F32)).astype(BF16)
    return hi, lo


def _rms_scale(x):
    return lax.rsqrt(jnp.mean(x * x, axis=-1, keepdims=True) + NORM_EPS)


def _silu(x):
    return x * (1.0 / (1.0 + jnp.exp(-x)))


def _params(*sem):
    return pltpu.CompilerParams(dimension_semantics=sem, vmem_limit_bytes=VMEM_LIMIT)


def _mod_kernel(c_ref, w_ref, b_ref, o_ref):
    c = c_ref[...]
    c_hi, c_lo = _split_bf16(_silu(c))
    w_hi, w_lo = _split_bf16(w_ref[...])
    lhs = jnp.concatenate([c_hi, c_lo], axis=0)
    r = _dot(lhs, w_hi)
    o_ref[...] = r[:BATCH] + r[BATCH:] + _dot(c_hi, w_lo) + b_ref[...]


def _adaln_mod(c, ada_w, ada_b):
    n_out = N_MOD * D_MODEL
    return pl.pallas_call(
        _mod_kernel,
        grid=(DEPTH, n_out // TN_MOD),
        in_specs=[
            pl.BlockSpec((BATCH, D_MODEL), lambda i, j: (0, 0)),
            pl.BlockSpec((None, D_MODEL, TN_MOD), lambda i, j: (i, 0, j)),
            pl.BlockSpec((None, 1, TN_MOD), lambda i, j: (i, 0, j)),
        ],
        out_specs=pl.BlockSpec((None, BATCH, TN_MOD), lambda i, j: (i, 0, j)),
        out_shape=jax.ShapeDtypeStruct((DEPTH, BATCH, n_out), F32),
        compiler_params=_params("arbitrary", "arbitrary"),
        name="adaln_mod",
    )(c, ada_w, ada_b.reshape(DEPTH, 1, n_out))


def _mod_spec(layer, k, tm):
    tiles_per_seq = SEQ // tm

    def index(m, *_):
        return ((layer * BATCH + m // tiles_per_seq) * N_MOD + k, 0, 0)

    return pl.BlockSpec((None, 1, D_MODEL), index)


def _full_spec(shape):
    zeros = (0,) * len(shape)
    return pl.BlockSpec(shape, lambda *_: zeros)


def _modulated_norm(x, g, scale, shift):
    return (x * _rms_scale(x) * g) * (1.0 + scale) + shift


def _inproj_gla_kernel(x_ref, shift_ref, scale_ref, g_ref, wqk_ref, wv_ref, wr_ref, wa_ref,
                       wa2_ref, ba_ref, br_ref,
                       qd_ref, ki_ref, ks_ref, v_ref, gate_ref, dec_ref, wbf_ref):
    @pl.when(pl.program_id(0) == 0)
    def _():
        for i, w_ref in enumerate((wqk_ref, wv_ref, wr_ref)):
            wbf_ref[i] = w_ref[...].astype(BF16)

    wqk_ref, wv_ref, wr_ref = wbf_ref.at[0], wbf_ref.at[1], wbf_ref.at[2]
    hb = _modulated_norm(x_ref[...], g_ref[...], scale_ref[...], shift_ref[...]).astype(BF16)
    a_hi, a_lo = _split_bf16(_dot(hb, wa_ref[...].astype(BF16)))
    r = _dot(hb, wr_ref[...]) + br_ref[...]
    gate_ref[...] = _silu(r).astype(BF16)

    w_hi, w_lo = _split_bf16(wa2_ref[...])
    la = _dot(jnp.concatenate([a_hi, a_lo, a_hi], axis=1),
              jnp.concatenate([w_hi, w_hi, w_lo], axis=0)) + ba_ref[...]
    qk = _dot(hb, wqk_ref[...])
    log_a = (jnp.minimum(la, 0.0) - jnp.log(1.0 + jnp.exp(-jnp.abs(la)))) * (1.0 / GLA_TAU)
    l_hi, l_lo = _split_bf16(log_a)
    l_cat = jnp.concatenate([l_hi, l_lo], axis=1)

    row = lax.broadcasted_iota(jnp.int32, (GLA_CHUNK, GLA_CHUNK), 0)
    col = lax.broadcasted_iota(jnp.int32, (GLA_CHUNK, GLA_CHUNK), 1)
    tril = jnp.where(row >= col, 1.0, 0.0).astype(BF16)
    q_scale = GLA_DK ** -0.5
    n_chunks = x_ref.shape[0] // GLA_CHUNK
    chunk = lambda n: slice(n * GLA_CHUNK, (n + 1) * GLA_CHUNK)
    cums = [_dot(tril, l_cat[chunk(n)]) for n in range(n_chunks)]
    v_ref[...] = _dot(hb, wv_ref[...]).astype(BF16)
    for n in range(n_chunks):
        sl = chunk(n)
        b = cums[n][:, :GLA_HK] + cums[n][:, GLA_HK:]
        b_last = b[GLA_CHUNK - 1:GLA_CHUNK, :]
        q = qk[sl, :GLA_HK] * q_scale
        k = qk[sl, GLA_HK:]
        qd_ref[sl, :] = (q * jnp.exp(b)).astype(BF16)
        ki_ref[sl, :] = (k * jnp.exp(-b)).astype(BF16)
        ks_ref[sl, :] = (k * jnp.exp(b_last - b)).astype(BF16)
        dec_ref[n] = jnp.exp(b_last)


def _inproj_gla(x2, mod3, layer, g, w_in, wa, wa2, ba, br):
    tm = TM_INPROJ
    row = lambda m: (m, 0)
    assert 2 * GLA_HK == D_MODEL
    col_block = lambda idx: _resident_spec((None, D_MODEL, D_MODEL), (0, 0, idx))
    return pl.pallas_call(
        _inproj_gla_kernel,
        grid=(N_TOK // tm,),
        in_specs=[
            pl.BlockSpec((tm, D_MODEL), row),
            _mod_spec(layer, 0, tm),
            _mod_spec(layer, 1, tm),
            _full_spec((1, D_MODEL)),
            col_block(0),
            col_block(1),
            col_block(2),
            _full_spec((D_MODEL, GLA_GATE_RANK)),
            _full_spec((GLA_GATE_RANK, GLA_HK)),
            _full_spec((1, GLA_HK)),
            _full_spec((1, D_MODEL)),
        ],
        out_specs=[
            pl.BlockSpec((tm, GLA_HK), row),
            pl.BlockSpec((tm, GLA_HK), row),
            pl.BlockSpec((tm, GLA_HK), row),
            pl.BlockSpec((tm, D_MODEL), row),
            pl.BlockSpec((tm, D_MODEL), row),
            pl.BlockSpec((tm // GLA_CHUNK, 1, GLA_HK), lambda m: (m, 0, 0)),
        ],
        out_shape=[
            jax.ShapeDtypeStruct((N_TOK, GLA_HK), BF16),
            jax.ShapeDtypeStruct((N_TOK, GLA_HK), BF16),
            jax.ShapeDtypeStruct((N_TOK, GLA_HK), BF16),
            jax.ShapeDtypeStruct((N_TOK, D_MODEL), BF16),
            jax.ShapeDtypeStruct((N_TOK, D_MODEL), BF16),
            jax.ShapeDtypeStruct((N_TOK // GLA_CHUNK, 1, GLA_HK), F32),
        ],
        scratch_shapes=[pltpu.VMEM((3, D_MODEL, D_MODEL), BF16)],
        compiler_params=_params("arbitrary"),
        name="inproj_gla",
    )(x2, mod3, mod3, g, w_in, w_in, w_in, wa, wa2, ba, br)


def _gla_kernel(qd_ref, ki_ref, ks_ref, v_ref, gate_ref, dec_ref, ng_ref, *rest, n_cast):
    cast_in, (o_ref, *cast_out), st_ref = rest[:n_cast], rest[n_cast:-1], rest[-1]
    for src, dst in zip(cast_in, cast_out):
        dst[...] = src[...].astype(BF16)

    @pl.when(pl.program_id(1) == 0)
    def _():
        st_ref[...] = jnp.zeros_like(st_ref)

    row = lax.broadcasted_iota(jnp.int32, (GLA_CHUNK, GLA_CHUNK), 0)
    col = lax.broadcasted_iota(jnp.int32, (GLA_CHUNK, GLA_CHUNK), 1)
    causal = row >= col
    n_chunks = qd_ref.shape[0] // GLA_CHUNK
    work = [(h, n) for h in range(GLA_HEADS) for n in range(n_chunks)]
    rows = lambda n: slice(n * GLA_CHUNK, (n + 1) * GLA_CHUNK)
    kcols = lambda h: slice(h * GLA_DK, (h + 1) * GLA_DK)
    vcols = lambda h: slice(h * GLA_DV, (h + 1) * GLA_DV)

    scores = {}
    updates = {}
    for h, n in work:
        scores[h, n] = lax.dot_general(qd_ref[rows(n), kcols(h)], ki_ref[rows(n), kcols(h)],
                                       NT_DIMS, preferred_element_type=F32)
        updates[h, n] = lax.dot_general(v_ref[rows(n), vcols(h)], ks_ref[rows(n), kcols(h)],
                                        TN_DIMS, preferred_element_type=F32)
    states = {}
    for h in range(GLA_HEADS):
        st = st_ref[h]
        for n in range(n_chunks):
            states[h, n] = st.astype(BF16)
            st = st * dec_ref[n][:, kcols(h)] + updates[h, n]
        st_ref[h] = st
    for h, n in work:
        s = jnp.where(causal, scores[h, n], 0.0).astype(BF16)
        o = _dot(s, v_ref[rows(n), vcols(h)]) + lax.dot_general(
            qd_ref[rows(n), kcols(h)], states[h, n], NT_DIMS, preferred_element_type=F32)
        y = o * _rms_scale(o) * ng_ref[...]
        o_ref[rows(n), vcols(h)] = (y * gate_ref[rows(n), vcols(h)].astype(F32)).astype(BF16)


def _cast_slabs(jobs, n_steps, step_of):
    flats, in_specs, out_specs, out_shapes = [], [], [], []
    for w, layer in jobs:
        _, n_rows, cols = w.shape
        slab = n_rows // n_steps
        assert slab * n_steps == n_rows and slab % 16 == 0
        first = layer * n_steps
        flats.append(w.reshape(-1, cols))
        in_specs.append(pl.BlockSpec((slab, cols), lambda *g, first=first: (first + step_of(*g), 0)))
        out_specs.append(pl.BlockSpec((slab, cols), lambda *g: (step_of(*g), 0)))
        out_shapes.append(jax.ShapeDtypeStruct((n_rows, cols), BF16))
    return flats, in_specs, out_specs, out_shapes


def _gla_core(qd, ki, ks, v, gate, dec, ng, cast_jobs):
    tc = GLA_TC
    steps_t = SEQ // tc
    qk_spec = pl.BlockSpec((None, tc, GLA_HK), lambda b, t: (b, t, 0))
    v_spec = pl.BlockSpec((None, tc, D_MODEL), lambda b, t: (b, t, 0))
    flats, cast_in, cast_out, cast_shapes = _cast_slabs(
        cast_jobs, BATCH * steps_t, lambda b, t: b * steps_t + t)
    out = pl.pallas_call(
        functools.partial(_gla_kernel, n_cast=len(flats)),
        grid=(BATCH, steps_t),
        in_specs=[
            qk_spec, qk_spec, qk_spec, v_spec, v_spec,
            pl.BlockSpec((None, tc // GLA_CHUNK, 1, GLA_HK), lambda b, t: (b, t, 0, 0)),
            _full_spec((1, GLA_DV)),
        ] + cast_in,
        out_specs=[v_spec] + cast_out,
        out_shape=[jax.ShapeDtypeStruct((BATCH, SEQ, D_MODEL), BF16)] + cast_shapes,
        scratch_shapes=[pltpu.VMEM((GLA_HEADS, GLA_DV, GLA_DK), F32)],
        compiler_params=_params("arbitrary", "arbitrary"),
        name="gla_core",
    )(qd.reshape(BATCH, SEQ, GLA_HK), ki.reshape(BATCH, SEQ, GLA_HK),
      ks.reshape(BATCH, SEQ, GLA_HK), v.reshape(BATCH, SEQ, D_MODEL),
      gate.reshape(BATCH, SEQ, D_MODEL),
      dec.reshape(BATCH, SEQ // GLA_CHUNK, 1, GLA_HK), ng, *flats)
    return out[0], out[1:]


def _rope_kernel(pos_ref, freq_ref, cs_ref):
    ang = freq_ref[...] * pos_ref[...]
    cs_ref[0:ROPE_HALF, :] = jnp.cos(ang)
    cs_ref[ROPE_HALF:, :] = jnp.sin(ang)


def _rope_tables(positions):
    inv_freq = ROPE_THETA ** (-jnp.arange(0, ROPE_DIM, 2, dtype=F32) / ROPE_DIM)
    return pl.pallas_call(
        _rope_kernel,
        grid=(1,),
        in_specs=[_full_spec((1, N_TOK)), _full_spec((ROPE_HALF, 1))],
        out_specs=_full_spec((ROPE_DIM, N_TOK)),
        out_shape=jax.ShapeDtypeStruct((ROPE_DIM, N_TOK), F32),
        compiler_params=_params("arbitrary"),
        name="rope_tables",
    )(positions.astype(F32).reshape(1, N_TOK), inv_freq.reshape(ROPE_HALF, 1))


def _rope_expander():
    e = np.zeros((3, ROPE_DIM, 2 * DIFF_DV), np.float32)
    for lane in range(DIFF_DV):
        d = lane % DIFF_DH
        if d < ROPE_DIM:
            e[:, d % ROPE_HALF, lane] = 1.0
            e[:, ROPE_HALF + d % ROPE_HALF, DIFF_DV + lane] = -1.0 if d < ROPE_HALF else 1.0
    return jnp.asarray(e.reshape(3 * ROPE_DIM, 2 * DIFF_DV), BF16)


def _inproj_diff_kernel(x_ref, shift_ref, scale_ref, g_ref, wq_ref, wk_ref, wv_ref,
                        cs_ref, e_ref, q_ref, k_ref, vt_ref, wvt_ref):
    @pl.when(pl.program_id(0) == 0)
    def _():
        wvt_ref[...] = wv_ref[...].T

    hb = _modulated_norm(x_ref[...], g_ref[...], scale_ref[...], shift_ref[...]).astype(BF16)

    cs = cs_ref[...]
    hi = cs.astype(BF16).astype(F32)
    mid = (cs - hi).astype(BF16).astype(F32)
    lo = cs - hi - mid
    parts = jnp.concatenate([hi, mid, lo], axis=0)
    tab = _dot(parts.T.astype(BF16), e_ref[...])
    d = lax.broadcasted_iota(jnp.int32, (1, DIFF_DV), 1) & (DIFF_DH - 1)
    ct = tab[:, :DIFF_DV] + jnp.where(d >= ROPE_DIM, 1.0, 0.0)
    s12 = tab[:, DIFF_DV:]
    first_half = d < ROPE_HALF

    q_scale = DIFF_DH ** -0.5 * math.log2(math.e)
    for w_ref, out_ref, scl in ((wq_ref, q_ref, q_scale), (wk_ref, k_ref, 1.0)):
        z = _dot(hb, w_ref[...])
        for h in range(DIFF_HEADS):
            sl = slice(h * DIFF_DV, (h + 1) * DIFF_DV)
            xh = z[:, sl]
            partner = jnp.where(first_half, pltpu.roll(xh, DIFF_DV - ROPE_HALF, 1),
                                pltpu.roll(xh, ROPE_HALF, 1))
            out_ref[:, sl] = ((xh * ct + partner * s12) * scl).astype(BF16)
    vt = lax.dot_general(wvt_ref[...], hb, NT_DIMS, preferred_element_type=F32).astype(BF16)
    ones = jnp.ones((VT_ROWS - DIFF_DV, vt.shape[1]), BF16)
    for h in range(DIFF_HEADS):
        vt_ref[h * VT_ROWS:h * VT_ROWS + DIFF_DV, :] = vt[h * DIFF_DV:(h + 1) * DIFF_DV]
        vt_ref[h * VT_ROWS + DIFF_DV:(h + 1) * VT_ROWS, :] = ones


def _inproj_diff(x2, mod3, layer, g, w_in, cs, expander):
    tm = TM_INPROJ
    row = lambda m: (m, 0)
    o_spec = pl.BlockSpec((tm, D_MODEL), row)
    return pl.pallas_call(
        _inproj_diff_kernel,
        grid=(N_TOK // tm,),
        in_specs=[
            pl.BlockSpec((tm, D_MODEL), row),
            _mod_spec(layer, 0, tm),
            _mod_spec(layer, 1, tm),
            _full_spec((1, D_MODEL)),
            _resident_spec((D_MODEL, D_MODEL), (0, 0)),
            _resident_spec((D_MODEL, D_MODEL), (0, 1)),
            _resident_spec((D_MODEL, D_MODEL), (0, 2)),
            pl.BlockSpec((ROPE_DIM, tm), lambda m: (0, m)),
            _full_spec((3 * ROPE_DIM, 2 * DIFF_DV)),
        ],
        out_specs=[o_spec, o_spec, pl.BlockSpec((DIFF_HEADS * VT_ROWS, tm), lambda m: (0, m))],
        out_shape=[jax.ShapeDtypeStruct((N_TOK, D_MODEL), BF16)] * 2
        + [jax.ShapeDtypeStruct((DIFF_HEADS * VT_ROWS, N_TOK), BF16)],
        scratch_shapes=[pltpu.VMEM((D_MODEL, D_MODEL), BF16)],
        compiler_params=_params("arbitrary"),
        name="inproj_diff",
    )(x2, mod3, mod3, g, w_in, w_in, w_in, cs, expander)


def _attn_kernel(q_ref, k_ref, vt_ref, lam_ref, g_ref, o_ref, qs_ref, s_ref, m_ref, acc_ref,
                 *, lambda_init):
    tq, tk = ATT_TQ, ATT_TK
    assert tq == tk
    lf = lam_ref[...]
    lam = (jnp.exp(jnp.sum(lf[0:1] * lf[1:2], axis=-1, keepdims=True))
           - jnp.exp(jnp.sum(lf[2:3] * lf[3:4], axis=-1, keepdims=True)) + lambda_init)
    lane = lax.broadcasted_iota(jnp.int32, (tq, DIFF_DV), 1)
    k_pos = lax.broadcasted_iota(jnp.int32, (tk, 2 * tq), 0)
    q_pos = lax.broadcasted_iota(jnp.int32, (tk, 2 * tq), 1) & (tq - 1)
    heads = [slice(h * DIFF_DV, (h + 1) * DIFF_DV) for h in range(ATT_HEADS)]

    def stage_queries(qi, slot):
        q0 = pl.multiple_of(qi * tq, tq)
        for h, hs in enumerate(heads):
            q = q_ref[pl.ds(q0, tq), hs]
            qs_ref[slot, h, 0:tq, :] = jnp.where(lane < DIFF_DH, q, jnp.zeros_like(q))
            qs_ref[slot, h, tq:2 * tq, :] = jnp.where(lane >= DIFF_DH, q, jnp.zeros_like(q))

    def scores_for(j, h, slot):
        start = pl.multiple_of(j * tk, tk)
        return lax.dot_general(k_ref[pl.ds(start, tk), heads[h]], qs_ref[slot, h], NT_DIMS,
                               preferred_element_type=F32)

    def reset_state():
        m_ref[...] = jnp.full_like(m_ref, MASKED)
        acc_ref[...] = jnp.zeros_like(acc_ref)

    def kv_step(j, masked, next_j, next_slot):
        start = pl.multiple_of(j * tk, tk)
        for h in range(ATT_HEADS):
            vtb = vt_ref[h * VT_ROWS:(h + 1) * VT_ROWS, pl.ds(start, tk)]
            s = s_ref[h]
            next_s = scores_for(next_j, h, next_slot)
            if masked:
                s = jnp.where(k_pos <= q_pos, s, MASKED)
            m_prev = m_ref[h]
            m_new = jnp.maximum(m_prev, jnp.max(s, axis=0, keepdims=True))
            p = jnp.exp2(s - m_new)
            alpha = jnp.exp2(m_prev - m_new)
            acc_ref[h] = alpha * acc_ref[h] + _dot(vtb, p.astype(BF16))
            m_ref[h] = m_new
            s_ref[h] = next_s

    n_q = SEQ // tq
    stage_queries(0, 0)
    reset_state()
    for h in range(ATT_HEADS):
        s_ref[h] = scores_for(0, h, 0)

    def q_block(qi, carry):
        q0 = pl.multiple_of(qi * tq, tq)
        slot = qi & 1

        def body(j, c):
            kv_step(j, False, j + 1, slot)
            return c

        lax.fori_loop(0, qi, body, 0)
        stage_queries(jnp.minimum(qi + 1, n_q - 1), 1 - slot)
        kv_step(qi, True, 0, 1 - slot)
        for h, hs in enumerate(heads):
            acc = acc_ref[h, 0:DIFF_DV, :]
            inv_l = 1.0 / acc_ref[h, DIFF_DV:DIFF_DV + 1, :]
            o = acc[:, :tq] * inv_l[:, :tq] - lam * (acc[:, tq:] * inv_l[:, tq:])
            rms = lax.rsqrt(jnp.mean(o * o, axis=0, keepdims=True) + NORM_EPS)
            y = o * rms * (g_ref[...] * (1.0 - lambda_init))
            o_ref[pl.ds(q0, tq), hs] = y.T.astype(BF16)
        reset_state()
        return carry

    lax.fori_loop(0, n_q, q_block, 0)


def _diff_attention(q, k, vt, lam_vecs, g, lambda_init):
    width = ATT_HEADS * DIFF_DV
    spec = pl.BlockSpec((None, SEQ, width), lambda b, h: (b, 0, h))
    return pl.pallas_call(
        functools.partial(_attn_kernel, lambda_init=lambda_init),
        grid=(BATCH, DIFF_HEADS // ATT_HEADS),
        in_specs=[spec, spec, pl.BlockSpec((ATT_HEADS * VT_ROWS, SEQ), lambda b, h: (h, b)),
                  _full_spec((4, DIFF_DH)), _full_spec((DIFF_DV, 1))],
        out_specs=spec,
        out_shape=jax.ShapeDtypeStruct((BATCH, SEQ, D_MODEL), BF16),
        scratch_shapes=[
            pltpu.VMEM((2, ATT_HEADS, 2 * ATT_TQ, DIFF_DV), BF16),
            pltpu.VMEM((ATT_HEADS, ATT_TK, 2 * ATT_TQ), F32),
            pltpu.VMEM((ATT_HEADS, 1, 2 * ATT_TQ), F32),
            pltpu.VMEM((ATT_HEADS, VT_ROWS, 2 * ATT_TQ), F32),
        ],
        compiler_params=_params("arbitrary", "arbitrary"),
        name="diff_attention",
    )(q.reshape(BATCH, SEQ, D_MODEL), k.reshape(BATCH, SEQ, D_MODEL), vt, lam_vecs, g)


def _mlp_kernel(x_ref, o_ref, wo_ref, gt_ref, sh_ref, sc_ref, gc_ref, g_ref, w1_ref, w2_ref,
                fg_ref, *rest, final, n_cast):
    cast_in, (out_ref, *cast_out) = rest[:n_cast], rest[n_cast:]
    for src, dst in zip(cast_in, cast_out):
        dst[...] = src[...].astype(BF16)

    x1 = x_ref[...] + gt_ref[...] * _dot(o_ref[...], wo_ref[...])
    hb = _modulated_norm(x1, g_ref[...], sc_ref[...], sh_ref[...]).astype(BF16)
    a = jnp.square(jnp.maximum(_dot(hb, w1_ref[...]), 0.0)).astype(BF16)
    out = x1 + gc_ref[...] * _dot(a, w2_ref[...])
    if final:
        out = out * _rms_scale(out) * fg_ref[...]
    out_ref[...] = out


def _resident_spec(block_shape, index):
    return pl.BlockSpec(block_shape, lambda m: index, pipeline_mode=pl.Buffered(1))


def _outproj_mlp(x2, o2, mod3, layer, wo, g, w1, w2, fg, final, cast_jobs=()):
    tm = TM_MLP
    n_steps = N_TOK // tm
    row = lambda m: (m, 0)
    flats, cast_in, cast_out, cast_shapes = _cast_slabs(cast_jobs, n_steps, lambda m: m)
    out = pl.pallas_call(
        functools.partial(_mlp_kernel, final=final, n_cast=len(flats)),
        grid=(n_steps,),
        in_specs=[
            pl.BlockSpec((tm, D_MODEL), row),
            pl.BlockSpec((tm, D_MODEL), row),
            _resident_spec((D_MODEL, D_MODEL), (0, 0)),
            _mod_spec(layer, 2, tm),
            _mod_spec(layer, 3, tm),
            _mod_spec(layer, 4, tm),
            _mod_spec(layer, 5, tm),
            _full_spec((1, D_MODEL)),
            _resident_spec((D_MODEL, D_FF), (0, 0)),
            _resident_spec((D_FF, D_MODEL), (0, 0)),
            _full_spec((1, D_MODEL)),
        ] + cast_in,
        out_specs=[pl.BlockSpec((tm, D_MODEL), row)] + cast_out,
        out_shape=[jax.ShapeDtypeStruct((N_TOK, D_MODEL), F32)] + cast_shapes,
        compiler_params=_params("arbitrary"),
        name="outproj_mlp_final" if final else "outproj_mlp",
    )(x2, o2, wo, mod3, mod3, mod3, mod3, g, w1, w2, fg, *flats)
    return out[0], out[1:]


def kernel(x, c, positions, ada_w, ada_b, norm_g, mlp_w1, mlp_w2, gla_w_in, gla_w_a2, gla_b_a,
           gla_b_r, gla_norm_g, gla_w_o, diff_w_in, diff_lambda, diff_subln_g, diff_w_o, final_g):
    x2 = x.reshape(N_TOK, D_MODEL)
    mod3 = _adaln_mod(c, ada_w, ada_b).reshape(DEPTH * BATCH * N_MOD, 1, D_MODEL)
    fg = final_g.reshape(1, D_MODEL)

    qd, ki, ks, v, gate, dec = _inproj_gla(
        x2, mod3, 0, norm_g[0, 0].reshape(1, D_MODEL), gla_w_in,
        gla_w_in[0, :, 3 * D_MODEL:], gla_w_a2[0], gla_b_a[0].reshape(1, GLA_HK), gla_b_r[0].reshape(1, D_MODEL))
    o, (w1, w2, wo) = _gla_core(
        qd, ki, ks, v, gate, dec, gla_norm_g[0].reshape(1, GLA_DV),
        [(mlp_w1, 0), (mlp_w2, 0), (gla_w_o, 0)])
    x2, (w1, w2, diff_win, wo) = _outproj_mlp(
        x2, o.reshape(N_TOK, D_MODEL), mod3, 0, wo, norm_g[0, 1].reshape(1, D_MODEL), w1, w2, fg,
        False, [(mlp_w1, 1), (mlp_w2, 1), (diff_w_in, 0), (diff_w_o, 0)])

    lambda_init = 0.8 - 0.6 * math.exp(-0.3 * 1)
    cs = _rope_tables(positions)
    q, k, vt = _inproj_diff(
        x2, mod3, 1, norm_g[1, 0].reshape(1, D_MODEL), diff_win, cs, _rope_expander())
    o = _diff_attention(q, k, vt, diff_lambda[0], diff_subln_g[0].reshape(DIFF_DV, 1), lambda_init)
    x2, _ = _outproj_mlp(x2, o.reshape(N_TOK, D_MODEL), mod3, 1, wo,
                         norm_g[1, 1].reshape(1, D_MODEL), w1, w2, fg, True)
    return x2.reshape(BATCH, SEQ, D_MODEL)
```

```python
import functools
import math

import jax
import jax.numpy as jnp
import numpy as np
from jax import lax
from jax.experimental import pallas as pl
from jax.experimental.pallas import tpu as pltpu

D_MODEL = 1024
BATCH = 8
SEQ = 2048
DEPTH = 2
D_FF = 4 * D_MODEL
NORM_EPS = 1e-6
GLA_HEADS = 4
GLA_DK = 128
GLA_DV = 256
GLA_HK = GLA_HEADS * GLA_DK
GLA_GATE_RANK = 16
GLA_TAU = 16.0
GLA_CHUNK = 64
DIFF_HEADS = 8
DIFF_DH = 64
DIFF_DV = 128
ROPE_THETA = 500000.0
ROPE_DIM = DIFF_DH // 4
ROPE_HALF = ROPE_DIM // 2
N_TOK = BATCH * SEQ
N_MOD = 6

TM_INPROJ = 1024
TM_MLP = 512
TF_MLP = 1024
TN_MOD = 1536
GLA_TC = 512
ATT_TQ = 256
ATT_TK = 256
VT_ROWS = DIFF_DV + 16
ATT_HEADS = 8
VMEM_LIMIT = 56 * 1024 * 1024

F32 = jnp.float32
BF16 = jnp.bfloat16
MASKED = -1e30
NT_DIMS = (((1,), (1,)), ((), ()))
TN_DIMS = (((0,), (0,)), ((), ()))


def _dot(a, b):
    return jnp.dot(a, b, preferred_element_type=F32)


def _split_bf16(a):
    hi = a.astype(BF16)
    lo = (a - hi.astype(F32)).astype(BF16)
    return hi, lo


def _rms_scale(x):
    return lax.rsqrt(jnp.mean(x * x, axis=-1, keepdims=True) + NORM_EPS)


def _silu(x):
    return x * (1.0 / (1.0 + jnp.exp(-x)))


def _params(*sem):
    return pltpu.CompilerParams(dimension_semantics=sem, vmem_limit_bytes=VMEM_LIMIT)


def _mod_kernel(c_ref, w_ref, b_ref, o_ref):
    c = c_ref[...]
    c_hi, c_lo = _split_bf16(_silu(c))
    w_hi, w_lo = _split_bf16(w_ref[...])
    lhs = jnp.concatenate([c_hi, c_lo], axis=0)
    r = _dot(lhs, w_hi)
    o_ref[...] = r[:BATCH] + r[BATCH:] + _dot(c_hi, w_lo) + b_ref[...]


def _adaln_mod(c, ada_w, ada_b):
    n_out = N_MOD * D_MODEL
    return pl.pallas_call(
        _mod_kernel,
        grid=(DEPTH, n_out // TN_MOD),
        in_specs=[
            pl.BlockSpec((BATCH, D_MODEL), lambda i, j: (0, 0)),
            pl.BlockSpec((None, D_MODEL, TN_MOD), lambda i, j: (i, 0, j)),
            pl.BlockSpec((None, 1, TN_MOD), lambda i, j: (i, 0, j)),
        ],
        out_specs=pl.BlockSpec((None, BATCH, TN_MOD), lambda i, j: (i, 0, j)),
        out_shape=jax.ShapeDtypeStruct((DEPTH, BATCH, n_out), F32),
        compiler_params=_params("arbitrary", "arbitrary"),
        name="adaln_mod",
    )(c, ada_w, ada_b.reshape(DEPTH, 1, n_out))


def _mod_spec(layer, k, tm):
    tiles_per_seq = SEQ // tm

    def index(m, *_):
        return ((layer * BATCH + m // tiles_per_seq) * N_MOD + k, 0, 0)

    return pl.BlockSpec((None, 1, D_MODEL), index)


def _full_spec(shape):
    zeros = (0,) * len(shape)
    return pl.BlockSpec(shape, lambda *_: zeros)


def _modulated_norm(x, g, scale, shift):
    return (x * _rms_scale(x) * g) * (1.0 + scale) + shift


def _inproj_gla_kernel(x_ref, shift_ref, scale_ref, g_ref, wqk_ref, wv_ref, wr_ref, wa_ref,
                       wa2_ref, ba_ref, br_ref,
                       qd_ref, ki_ref, v_ref, gate_ref, dec_ref, wbf_ref):
    @pl.when(pl.program_id(0) == 0)
    def _():
        for i, w_ref in enumerate((wqk_ref, wv_ref, wr_ref)):
            wbf_ref[i] = w_ref[...].astype(BF16)

    wqk_ref, wv_ref, wr_ref = wbf_ref.at[0], wbf_ref.at[1], wbf_ref.at[2]
    hb = _modulated_norm(x_ref[...], g_ref[...], scale_ref[...], shift_ref[...]).astype(BF16)
    a_hi, a_lo = _split_bf16(_dot(hb, wa_ref[...].astype(BF16)))
    r = _dot(hb, wr_ref[...]) + br_ref[...]
    gate_ref[...] = _silu(r).astype(BF16)

    w_hi, w_lo = _split_bf16(wa2_ref[...])
    la = _dot(jnp.concatenate([a_hi, a_lo, a_hi], axis=1),
              jnp.concatenate([w_hi, w_hi, w_lo], axis=0)) + ba_ref[...]
    qk = _dot(hb, wqk_ref[...])
    log_a = (jnp.minimum(la, 0.0) - jnp.log(1.0 + jnp.exp(-jnp.abs(la)))) * (1.0 / GLA_TAU)
    l_hi, l_lo = _split_bf16(log_a)
    l_cat = jnp.concatenate([l_hi, l_lo], axis=1)

    row = lax.broadcasted_iota(jnp.int32, (GLA_CHUNK, GLA_CHUNK), 0)
    col = lax.broadcasted_iota(jnp.int32, (GLA_CHUNK, GLA_CHUNK), 1)
    tril = jnp.where(row >= col, 1.0, 0.0).astype(BF16)
    q_scale = GLA_DK ** -0.5
    n_chunks = x_ref.shape[0] // GLA_CHUNK
    chunk = lambda n: slice(n * GLA_CHUNK, (n + 1) * GLA_CHUNK)
    cums = [_dot(tril, l_cat[chunk(n)]) for n in range(n_chunks)]
    v_ref[...] = _dot(hb, wv_ref[...]).astype(BF16)
    for n in range(n_chunks):
        sl = chunk(n)
        b = cums[n][:, :GLA_HK] + cums[n][:, GLA_HK:]
        b_last = b[GLA_CHUNK - 1:GLA_CHUNK, :]
        q = qk[sl, :GLA_HK] * q_scale
        k = qk[sl, GLA_HK:]
        qd_ref[sl, :] = (q * jnp.exp(b)).astype(BF16)
        ki_ref[sl, :] = (k * jnp.exp(-b)).astype(BF16)
        dec_ref[n] = jnp.exp(b_last)


def _inproj_gla(x2, mod3, layer, g, w_in, wa, wa2, ba, br):
    tm = TM_INPROJ
    row = lambda m: (m, 0)
    assert 2 * GLA_HK == D_MODEL
    col_block = lambda idx: _resident_spec((None, D_MODEL, D_MODEL), (0, 0, idx))
    return pl.pallas_call(
        _inproj_gla_kernel,
        grid=(N_TOK // tm,),
        in_specs=[
            pl.BlockSpec((tm, D_MODEL), row),
            _mod_spec(layer, 0, tm),
            _mod_spec(layer, 1, tm),
            _full_spec((1, D_MODEL)),
            col_block(0),
            col_block(1),
            col_block(2),
            _full_spec((D_MODEL, GLA_GATE_RANK)),
            _full_spec((GLA_GATE_RANK, GLA_HK)),
            _full_spec((1, GLA_HK)),
            _full_spec((1, D_MODEL)),
        ],
        out_specs=[
            pl.BlockSpec((tm, GLA_HK), row),
            pl.BlockSpec((tm, GLA_HK), row),
            pl.BlockSpec((tm, D_MODEL), row),
            pl.BlockSpec((tm, D_MODEL), row),
            pl.BlockSpec((tm // GLA_CHUNK, 1, GLA_HK), lambda m: (m, 0, 0)),
        ],
        out_shape=[
            jax.ShapeDtypeStruct((N_TOK, GLA_HK), BF16),
            jax.ShapeDtypeStruct((N_TOK, GLA_HK), BF16),
            jax.ShapeDtypeStruct((N_TOK, D_MODEL), BF16),
            jax.ShapeDtypeStruct((N_TOK, D_MODEL), BF16),
            jax.ShapeDtypeStruct((N_TOK // GLA_CHUNK, 1, GLA_HK), F32),
        ],
        scratch_shapes=[pltpu.VMEM((3, D_MODEL, D_MODEL), BF16)],
        compiler_params=_params("arbitrary"),
        name="inproj_gla",
    )(x2, mod3, mod3, g, w_in, w_in, w_in, wa, wa2, ba, br)


def _gla_kernel(qd_ref, ki_ref, v_ref, gate_ref, dec_ref, ng_ref, *rest, n_cast):
    cast_in, (o_ref, *cast_out), st_ref = rest[:n_cast], rest[n_cast:-1], rest[-1]
    for src, dst in zip(cast_in, cast_out):
        dst[...] = src[...].astype(BF16)

    @pl.when(pl.program_id(1) == 0)
    def _():
        st_ref[...] = jnp.zeros_like(st_ref)

    row = lax.broadcasted_iota(jnp.int32, (GLA_CHUNK, GLA_CHUNK), 0)
    col = lax.broadcasted_iota(jnp.int32, (GLA_CHUNK, GLA_CHUNK), 1)
    causal = row >= col
    n_chunks = qd_ref.shape[0] // GLA_CHUNK
    work = [(h, n) for h in range(GLA_HEADS) for n in range(n_chunks)]
    rows = lambda n: slice(n * GLA_CHUNK, (n + 1) * GLA_CHUNK)
    kcols = lambda h: slice(h * GLA_DK, (h + 1) * GLA_DK)
    vcols = lambda h: slice(h * GLA_DV, (h + 1) * GLA_DV)

    scores = {}
    updates = {}
    for h, n in work:
        scores[h, n] = lax.dot_general(qd_ref[rows(n), kcols(h)], ki_ref[rows(n), kcols(h)],
                                       NT_DIMS, preferred_element_type=F32)
        updates[h, n] = lax.dot_general(v_ref[rows(n), vcols(h)], ki_ref[rows(n), kcols(h)],
                                        TN_DIMS, preferred_element_type=F32)
    states = {}
    for h in range(GLA_HEADS):
        st = st_ref[h]
        for n in range(n_chunks):
            states[h, n] = st.astype(BF16)
            st = (st + updates[h, n]) * dec_ref[n][:, kcols(h)]
        st_ref[h] = st
    for h, n in work:
        s = jnp.where(causal, scores[h, n], 0.0).astype(BF16)
        o = _dot(s, v_ref[rows(n), vcols(h)]) + lax.dot_general(
            qd_ref[rows(n), kcols(h)], states[h, n], NT_DIMS, preferred_element_type=F32)
        y = o * _rms_scale(o) * ng_ref[...]
        o_ref[rows(n), vcols(h)] = (y * gate_ref[rows(n), vcols(h)].astype(F32)).astype(BF16)


def _cast_slabs(jobs, n_steps, step_of):
    flats, in_specs, out_specs, out_shapes = [], [], [], []
    for w, layer in jobs:
        _, n_rows, cols = w.shape
        slab = n_rows // n_steps
        assert slab * n_steps == n_rows and slab % 16 == 0
        first = layer * n_steps
        flats.append(w.reshape(-1, cols))
        in_specs.append(pl.BlockSpec((slab, cols), lambda *g, first=first: (first + step_of(*g), 0)))
        out_specs.append(pl.BlockSpec((slab, cols), lambda *g: (step_of(*g), 0)))
        out_shapes.append(jax.ShapeDtypeStruct((n_rows, cols), BF16))
    return flats, in_specs, out_specs, out_shapes


def _gla_core(qd, ki, v, gate, dec, ng, cast_jobs):
    tc = GLA_TC
    steps_t = SEQ // tc
    qk_spec = pl.BlockSpec((None, tc, GLA_HK), lambda b, t: (b, t, 0))
    v_spec = pl.BlockSpec((None, tc, D_MODEL), lambda b, t: (b, t, 0))
    flats, cast_in, cast_out, cast_shapes = _cast_slabs(
        cast_jobs, BATCH * steps_t, lambda b, t: b * steps_t + t)
    out = pl.pallas_call(
        functools.partial(_gla_kernel, n_cast=len(flats)),
        grid=(BATCH, steps_t),
        in_specs=[
            qk_spec, qk_spec, v_spec, v_spec,
            pl.BlockSpec((None, tc // GLA_CHUNK, 1, GLA_HK), lambda b, t: (b, t, 0, 0)),
            _full_spec((1, GLA_DV)),
        ] + cast_in,
        out_specs=[v_spec] + cast_out,
        out_shape=[jax.ShapeDtypeStruct((BATCH, SEQ, D_MODEL), BF16)] + cast_shapes,
        scratch_shapes=[pltpu.VMEM((GLA_HEADS, GLA_DV, GLA_DK), F32)],
        compiler_params=_params("arbitrary", "arbitrary"),
        name="gla_core",
    )(qd.reshape(BATCH, SEQ, GLA_HK), ki.reshape(BATCH, SEQ, GLA_HK),
      v.reshape(BATCH, SEQ, D_MODEL),
      gate.reshape(BATCH, SEQ, D_MODEL),
      dec.reshape(BATCH, SEQ // GLA_CHUNK, 1, GLA_HK), ng, *flats)
    return out[0], out[1:]


def _rope_kernel(pos_ref, freq_ref, cs_ref):
    ang = freq_ref[...] * pos_ref[...]
    cs_ref[0:ROPE_HALF, :] = jnp.cos(ang)
    cs_ref[ROPE_HALF:, :] = jnp.sin(ang)


def _rope_tables(positions):
    inv_freq = ROPE_THETA ** (-jnp.arange(0, ROPE_DIM, 2, dtype=F32) / ROPE_DIM)
    return pl.pallas_call(
        _rope_kernel,
        grid=(1,),
        in_specs=[_full_spec((1, N_TOK)), _full_spec((ROPE_HALF, 1))],
        out_specs=_full_spec((ROPE_DIM, N_TOK)),
        out_shape=jax.ShapeDtypeStruct((ROPE_DIM, N_TOK), F32),
        compiler_params=_params("arbitrary"),
        name="rope_tables",
    )(positions.astype(F32).reshape(1, N_TOK), inv_freq.reshape(ROPE_HALF, 1))


def _rope_expander():
    e = np.zeros((3, ROPE_DIM, 2 * DIFF_DV), np.float32)
    for lane in range(DIFF_DV):
        d = lane % DIFF_DH
        if d < ROPE_DIM:
            e[:, d % ROPE_HALF, lane] = 1.0
            e[:, ROPE_HALF + d % ROPE_HALF, DIFF_DV + lane] = -1.0 if d < ROPE_HALF else 1.0
    return jnp.asarray(e.reshape(3 * ROPE_DIM, 2 * DIFF_DV), BF16)


def _inproj_diff_kernel(x_ref, shift_ref, scale_ref, g_ref, wq_ref, wk_ref, wv_ref,
                        cs_ref, e_ref, q_ref, k_ref, vt_ref, wvt_ref):
    @pl.when(pl.program_id(0) == 0)
    def _():
        wvt_ref[...] = wv_ref[...].T

    hb = _modulated_norm(x_ref[...], g_ref[...], scale_ref[...], shift_ref[...]).astype(BF16)

    cs = cs_ref[...]
    hi = cs.astype(BF16).astype(F32)
    mid = (cs - hi).astype(BF16).astype(F32)
    lo = cs - hi - mid
    parts = jnp.concatenate([hi, mid, lo], axis=0)
    tab = _dot(parts.T.astype(BF16), e_ref[...])
    d = lax.broadcasted_iota(jnp.int32, (1, DIFF_DV), 1) & (DIFF_DH - 1)
    ct = tab[:, :DIFF_DV] + jnp.where(d >= ROPE_DIM, 1.0, 0.0)
    s12 = tab[:, DIFF_DV:]
    first_half = d < ROPE_HALF

    q_scale = DIFF_DH ** -0.5 * math.log2(math.e)
    for w_ref, out_ref, scl in ((wq_ref, q_ref, q_scale), (wk_ref, k_ref, 1.0)):
        z = _dot(hb, w_ref[...])
        for h in range(DIFF_HEADS):
            sl = slice(h * DIFF_DV, (h + 1) * DIFF_DV)
            xh = z[:, sl]
            partner = jnp.where(first_half, pltpu.roll(xh, DIFF_DV - ROPE_HALF, 1),
                                pltpu.roll(xh, ROPE_HALF, 1))
            out_ref[:, sl] = ((xh * ct + partner * s12) * scl).astype(BF16)
    vt = lax.dot_general(wvt_ref[...], hb, NT_DIMS, preferred_element_type=F32).astype(BF16)
    ones = jnp.ones((VT_ROWS - DIFF_DV, vt.shape[1]), BF16)
    for h in range(DIFF_HEADS):
        vt_ref[h * VT_ROWS:h * VT_ROWS + DIFF_DV, :] = vt[h * DIFF_DV:(h + 1) * DIFF_DV]
        vt_ref[h * VT_ROWS + DIFF_DV:(h + 1) * VT_ROWS, :] = ones


def _inproj_diff(x2, mod3, layer, g, w_in, cs, expander):
    tm = TM_INPROJ
    row = lambda m: (m, 0)
    o_spec = pl.BlockSpec((tm, D_MODEL), row)
    return pl.pallas_call(
        _inproj_diff_kernel,
        grid=(N_TOK // tm,),
        in_specs=[
            pl.BlockSpec((tm, D_MODEL), row),
            _mod_spec(layer, 0, tm),
            _mod_spec(layer, 1, tm),
            _full_spec((1, D_MODEL)),
            _resident_spec((D_MODEL, D_MODEL), (0, 0)),
            _resident_spec((D_MODEL, D_MODEL), (0, 1)),
            _resident_spec((D_MODEL, D_MODEL), (0, 2)),
            pl.BlockSpec((ROPE_DIM, tm), lambda m: (0, m)),
            _full_spec((3 * ROPE_DIM, 2 * DIFF_DV)),
        ],
        out_specs=[o_spec, o_spec, pl.BlockSpec((DIFF_HEADS * VT_ROWS, tm), lambda m: (0, m))],
        out_shape=[jax.ShapeDtypeStruct((N_TOK, D_MODEL), BF16)] * 2
        + [jax.ShapeDtypeStruct((DIFF_HEADS * VT_ROWS, N_TOK), BF16)],
        scratch_shapes=[pltpu.VMEM((D_MODEL, D_MODEL), BF16)],
        compiler_params=_params("arbitrary"),
        name="inproj_diff",
    )(x2, mod3, mod3, g, w_in, w_in, w_in, cs, expander)


def _attn_kernel(q_ref, k_ref, vt_ref, lam_ref, g_ref, o_ref, qs_ref, s_ref, m_ref, acc_ref,
                 *, lambda_init):
    tq, tk = ATT_TQ, ATT_TK
    assert tq == tk
    lf = lam_ref[...]
    lam = (jnp.exp(jnp.sum(lf[0:1] * lf[1:2], axis=-1, keepdims=True))
           - jnp.exp(jnp.sum(lf[2:3] * lf[3:4], axis=-1, keepdims=True)) + lambda_init)
    feat = lax.broadcasted_iota(jnp.int32, (DIFF_DV, tq), 0)
    k_pos = lax.broadcasted_iota(jnp.int32, (tk, 2 * tq), 0)
    q_pos = lax.broadcasted_iota(jnp.int32, (tk, 2 * tq), 1) & (tq - 1)
    heads = [slice(h * DIFF_DV, (h + 1) * DIFF_DV) for h in range(ATT_HEADS)]

    def stage_queries(qi, slot):
        q0 = pl.multiple_of(qi * tq, tq)
        for h, hs in enumerate(heads):
            qt = q_ref[pl.ds(q0, tq), hs].astype(F32).T
            qs_ref[slot, h, :, 0:tq] = jnp.where(feat < DIFF_DH, qt, 0.0).astype(BF16)
            qs_ref[slot, h, :, tq:2 * tq] = jnp.where(feat >= DIFF_DH, qt, 0.0).astype(BF16)

    def scores_for(j, h, slot):
        start = pl.multiple_of(j * tk, tk)
        return _dot(k_ref[pl.ds(start, tk), heads[h]], qs_ref[slot, h])

    def reset_state():
        m_ref[...] = jnp.full_like(m_ref, MASKED)
        acc_ref[...] = jnp.zeros_like(acc_ref)

    def kv_step(j, masked, next_j, next_slot):
        start = pl.multiple_of(j * tk, tk)
        for h in range(ATT_HEADS):
            vtb = vt_ref[h * VT_ROWS:(h + 1) * VT_ROWS, pl.ds(start, tk)]
            s = s_ref[h]
            next_s = scores_for(next_j, h, next_slot)
            if masked:
                s = jnp.where(k_pos <= q_pos, s, MASKED)
            m_prev = m_ref[h]
            m_new = jnp.maximum(m_prev, jnp.max(s, axis=0, keepdims=True))
            p = jnp.exp2(s - m_new)
            alpha = jnp.exp2(m_prev - m_new)
            acc_ref[h] = alpha * acc_ref[h] + _dot(vtb, p.astype(BF16))
            m_ref[h] = m_new
            s_ref[h] = next_s

    n_q = SEQ // tq
    stage_queries(0, 0)
    reset_state()
    for h in range(ATT_HEADS):
        s_ref[h] = scores_for(0, h, 0)

    def q_block(qi, carry):
        q0 = pl.multiple_of(qi * tq, tq)
        slot = qi & 1

        def body(j, c):
            kv_step(j, False, j + 1, slot)
            return c

        lax.fori_loop(0, qi, body, 0)
        stage_queries(jnp.minimum(qi + 1, n_q - 1), 1 - slot)
        kv_step(qi, True, 0, 1 - slot)
        for h, hs in enumerate(heads):
            acc = acc_ref[h, 0:DIFF_DV, :]
            inv_l = 1.0 / acc_ref[h, DIFF_DV:DIFF_DV + 1, :]
            o = acc[:, :tq] * inv_l[:, :tq] - lam * (acc[:, tq:] * inv_l[:, tq:])
            rms = lax.rsqrt(jnp.mean(o * o, axis=0, keepdims=True) + NORM_EPS)
            y = o * rms * (g_ref[...] * (1.0 - lambda_init))
            o_ref[pl.ds(q0, tq), hs] = y.T.astype(BF16)
        reset_state()
        return carry

    lax.fori_loop(0, n_q, q_block, 0)


def _diff_attention(q, k, vt, lam_vecs, g, lambda_init):
    width = ATT_HEADS * DIFF_DV
    spec = pl.BlockSpec((None, SEQ, width), lambda b, h: (b, 0, h))
    return pl.pallas_call(
        functools.partial(_attn_kernel, lambda_init=lambda_init),
        grid=(BATCH, DIFF_HEADS // ATT_HEADS),
        in_specs=[spec, spec, pl.BlockSpec((ATT_HEADS * VT_ROWS, SEQ), lambda b, h: (h, b)),
                  _full_spec((4, DIFF_DH)), _full_spec((DIFF_DV, 1))],
        out_specs=spec,
        out_shape=jax.ShapeDtypeStruct((BATCH, SEQ, D_MODEL), BF16),
        scratch_shapes=[
            pltpu.VMEM((2, ATT_HEADS, DIFF_DV, 2 * ATT_TQ), BF16),
            pltpu.VMEM((ATT_HEADS, ATT_TK, 2 * ATT_TQ), F32),
            pltpu.VMEM((ATT_HEADS, 1, 2 * ATT_TQ), F32),
            pltpu.VMEM((ATT_HEADS, VT_ROWS, 2 * ATT_TQ), F32),
        ],
        compiler_params=_params("arbitrary", "arbitrary"),
        name="diff_attention",
    )(q.reshape(BATCH, SEQ, D_MODEL), k.reshape(BATCH, SEQ, D_MODEL), vt, lam_vecs, g)


def _mlp_kernel(x_ref, o_ref, wo_ref, gt_ref, sh_ref, sc_ref, gc_ref, g_ref, w1_ref, w2_ref,
                fg_ref, *rest, final, n_cast):
    cast_in, (out_ref, *cast_out) = rest[:n_cast], rest[n_cast:]
    for src, dst in zip(cast_in, cast_out):
        dst[...] = src[...].astype(BF16)

    x1 = x_ref[...] + gt_ref[...] * _dot(o_ref[...], wo_ref[...])
    hb = _modulated_norm(x1, g_ref[...], sc_ref[...], sh_ref[...]).astype(BF16)
    a = jnp.square(jnp.maximum(_dot(hb, w1_ref[...]), 0.0)).astype(BF16)
    out = x1 + gc_ref[...] * _dot(a, w2_ref[...])
    if final:
        out = out * _rms_scale(out) * fg_ref[...]
    out_ref[...] = out


def _resident_spec(block_shape, index):
    return pl.BlockSpec(block_shape, lambda m: index, pipeline_mode=pl.Buffered(1))


def _outproj_mlp(x2, o2, mod3, layer, wo, g, w1, w2, fg, final, cast_jobs=()):
    tm = TM_MLP
    n_steps = N_TOK // tm
    row = lambda m: (m, 0)
    flats, cast_in, cast_out, cast_shapes = _cast_slabs(cast_jobs, n_steps, lambda m: m)
    out = pl.pallas_call(
        functools.partial(_mlp_kernel, final=final, n_cast=len(flats)),
        grid=(n_steps,),
        in_specs=[
            pl.BlockSpec((tm, D_MODEL), row),
            pl.BlockSpec((tm, D_MODEL), row),
            _resident_spec((D_MODEL, D_MODEL), (0, 0)),
            _mod_spec(layer, 2, tm),
            _mod_spec(layer, 3, tm),
            _mod_spec(layer, 4, tm),
            _mod_spec(layer, 5, tm),
            _full_spec((1, D_MODEL)),
            _resident_spec((D_MODEL, D_FF), (0, 0)),
            _resident_spec((D_FF, D_MODEL), (0, 0)),
            _full_spec((1, D_MODEL)),
        ] + cast_in,
        out_specs=[pl.BlockSpec((tm, D_MODEL), row)] + cast_out,
        out_shape=[jax.ShapeDtypeStruct((N_TOK, D_MODEL), F32)] + cast_shapes,
        compiler_params=_params("arbitrary"),
        name="outproj_mlp_final" if final else "outproj_mlp",
    )(x2, o2, wo, mod3, mod3, mod3, mod3, g, w1, w2, fg, *flats)
    return out[0], out[1:]


def kernel(x, c, positions, ada_w, ada_b, norm_g, mlp_w1, mlp_w2, gla_w_in, gla_w_a2, gla_b_a,
           gla_b_r, gla_norm_g, gla_w_o, diff_w_in, diff_lambda, diff_subln_g, diff_w_o, final_g):
    x2 = x.reshape(N_TOK, D_MODEL)
    mod3 = _adaln_mod(c, ada_w, ada_b).reshape(DEPTH * BATCH * N_MOD, 1, D_MODEL)
    fg = final_g.reshape(1, D_MODEL)

    qd, ki, v, gate, dec = _inproj_gla(
        x2, mod3, 0, norm_g[0, 0].reshape(1, D_MODEL), gla_w_in, gla_w_in[0, :, 3 * D_MODEL:],
        gla_w_a2[0], gla_b_a[0].reshape(1, GLA_HK), gla_b_r[0].reshape(1, D_MODEL))
    o, (w1, w2, wo) = _gla_core(
        qd, ki, v, gate, dec, gla_norm_g[0].reshape(1, GLA_DV),
        [(mlp_w1, 0), (mlp_w2, 0), (gla_w_o, 0)])
    x2, (w1, w2, diff_win, wo) = _outproj_mlp(
        x2, o.reshape(N_TOK, D_MODEL), mod3, 0, wo, norm_g[0, 1].reshape(1, D_MODEL), w1, w2, fg,
        False, [(mlp_w1, 1), (mlp_w2, 1), (diff_w_in, 0), (diff_w_o, 0)])

    lambda_init = 0.8 - 0.6 * math.exp(-0.3 * 1)
    cs = _rope_tables(positions)
    q, k, vt = _inproj_diff(
        x2, mod3, 1, norm_g[1, 0].reshape(1, D_MODEL), diff_win, cs, _rope_expander())
    o = _diff_attention(q, k, vt, diff_lambda[0], diff_subln_g[0].reshape(DIFF_DV, 1), lambda_init)
    x2, _ = _outproj_mlp(x2, o.reshape(N_TOK, D_MODEL), mod3, 1, wo,
                         norm_g[1, 1].reshape(1, D_MODEL), w1, w2, fg, True)
    return x2.reshape(BATCH, SEQ, D_MODEL)
```

```python
import functools
import math

import jax
import jax.numpy as jnp
import numpy as np
from jax import lax
from jax.experimental import pallas as pl
from jax.experimental.pallas import tpu as pltpu

D_MODEL = 1024
BATCH = 8
SEQ = 2048
DEPTH = 2
D_FF = 4 * D_MODEL
NORM_EPS = 1e-6
GLA_HEADS = 4
GLA_DK = 128
GLA_DV = 256
GLA_HK = GLA_HEADS * GLA_DK
GLA_GATE_RANK = 16
GLA_TAU = 16.0
GLA_CHUNK = 64
DIFF_HEADS = 8
DIFF_DH = 64
DIFF_DV = 128
ROPE_THETA = 500000.0
ROPE_DIM = DIFF_DH // 4
ROPE_HALF = ROPE_DIM // 2
N_TOK = BATCH * SEQ
N_MOD = 6

TM_INPROJ = 1024
INPROJ_ROW_GROUPS = 2
TM_MLP = 512
TF_MLP = 1024
TN_MOD = 1536
GLA_TC = 512
ATT_TQ = 256
ATT_TK = 256
VT_ROWS = DIFF_DV + 16
ATT_HEADS = 8
VMEM_LIMIT = 56 * 1024 * 1024

F32 = jnp.float32
BF16 = jnp.bfloat16
MASKED = -1e30
NT_DIMS = (((1,), (1,)), ((), ()))
TN_DIMS = (((0,), (0,)), ((), ()))


def _dot(a, b):
    return jnp.dot(a, b, preferred_element_type=F32)


def _split_bf16(a):
    hi = a.astype(BF16)
    lo = (a - hi.astype(F32)).astype(BF16)
    return hi, lo


def _rms_scale(x):
    return lax.rsqrt(jnp.mean(x * x, axis=-1, keepdims=True) + NORM_EPS)


def _silu(x):
    return x * (1.0 / (1.0 + jnp.exp(-x)))


def _params(*sem):
    return pltpu.CompilerParams(dimension_semantics=sem, vmem_limit_bytes=VMEM_LIMIT)


def _mod_block(c_ref, w_ref, b_ref, o_ref):
    c_hi, c_lo = _split_bf16(_silu(c_ref[...]))
    w_hi, w_lo = _split_bf16(w_ref[...])
    lhs = jnp.concatenate([c_hi, c_lo], axis=0)
    r = _dot(lhs, w_hi)
    o_ref[...] = r[:BATCH] + r[BATCH:] + _dot(c_hi, w_lo) + b_ref[...]


def _adaln_mod(c, ada_w, ada_b):
    n_out = N_MOD * D_MODEL
    return pl.pallas_call(
        _mod_block,
        grid=(DEPTH, n_out // TN_MOD),
        in_specs=[
            _full_spec((BATCH, D_MODEL)),
            pl.BlockSpec((None, D_MODEL, TN_MOD), lambda i, j: (i, 0, j)),
            pl.BlockSpec((None, 1, TN_MOD), lambda i, j: (i, 0, j)),
        ],
        out_specs=pl.BlockSpec((None, BATCH, TN_MOD), lambda i, j: (i, 0, j)),
        out_shape=jax.ShapeDtypeStruct((DEPTH, BATCH, n_out), F32),
        compiler_params=_params("arbitrary", "arbitrary"),
        name="adaln_mod",
    )(c, ada_w, ada_b.reshape(DEPTH, 1, n_out))


def _mod_spec(k, tm):
    tiles_per_seq = SEQ // tm
    return pl.BlockSpec((None, 1, D_MODEL), lambda m, *_: ((m // tiles_per_seq) * N_MOD + k, 0, 0))


def _full_spec(shape):
    zeros = (0,) * len(shape)
    return pl.BlockSpec(shape, lambda *_: zeros)


def _modulated_norm(x, g, scale, shift):
    return (x * _rms_scale(x) * g) * (1.0 + scale) + shift


def _inproj_gla_kernel(x_ref, shift_ref, scale_ref, g_ref, wqk_ref, wv_ref, wr_ref, wa_ref,
                       wa2_ref, ba_ref, br_ref,
                       qd_ref, ki_ref, v_ref, gate_ref, dec_ref, wbf_ref):
    @pl.when(pl.program_id(0) == 0)
    def _():
        for i, w_ref in enumerate((wqk_ref, wv_ref, wr_ref)):
            wbf_ref[i] = w_ref[...].astype(BF16)

    wqk_ref, wv_ref, wr_ref = wbf_ref.at[0], wbf_ref.at[1], wbf_ref.at[2]
    row = lax.broadcasted_iota(jnp.int32, (GLA_CHUNK, GLA_CHUNK), 0)
    col = lax.broadcasted_iota(jnp.int32, (GLA_CHUNK, GLA_CHUNK), 1)
    tril = jnp.where(row >= col, 1.0, 0.0).astype(BF16)
    w_hi, w_lo = _split_bf16(wa2_ref[...])
    w_gate = jnp.concatenate([w_hi, w_hi, w_lo], axis=0)
    q_scale = GLA_DK ** -0.5
    group_rows = x_ref.shape[0] // INPROJ_ROW_GROUPS
    chunks_per_group = group_rows // GLA_CHUNK

    for grp in range(INPROJ_ROW_GROUPS):
        rows = slice(grp * group_rows, (grp + 1) * group_rows)
        hb = _modulated_norm(x_ref[rows, :], g_ref[...], scale_ref[...],
                             shift_ref[...]).astype(BF16)
        a_hi, a_lo = _split_bf16(_dot(hb, wa_ref[...].astype(BF16)))
        r = _dot(hb, wr_ref[...]) + br_ref[...]
        gate_ref[rows, :] = _silu(r).astype(BF16)

        la = _dot(jnp.concatenate([a_hi, a_lo, a_hi], axis=1), w_gate) + ba_ref[...]
        qk = _dot(hb, wqk_ref[...])
        log_a = (jnp.minimum(la, 0.0) - jnp.log(1.0 + jnp.exp(-jnp.abs(la)))) * (1.0 / GLA_TAU)
        l_hi, l_lo = _split_bf16(log_a)
        l_cat = jnp.concatenate([l_hi, l_lo], axis=1)

        chunk = lambda n: slice(n * GLA_CHUNK, (n + 1) * GLA_CHUNK)
        cums = [_dot(tril, l_cat[chunk(n)]) for n in range(chunks_per_group)]
        v_ref[rows, :] = _dot(hb, wv_ref[...]).astype(BF16)
        for n in range(chunks_per_group):
            sl = chunk(n)
            out_rows = slice(grp * group_rows + n * GLA_CHUNK,
                             grp * group_rows + (n + 1) * GLA_CHUNK)
            b = cums[n][:, :GLA_HK] + cums[n][:, GLA_HK:]
            b_last = b[GLA_CHUNK - 1:GLA_CHUNK, :]
            q = qk[sl, :GLA_HK] * q_scale
            k = qk[sl, GLA_HK:]
            qd_ref[out_rows, :] = (q * jnp.exp(b)).astype(BF16)
            ki_ref[out_rows, :] = (k * jnp.exp(-b)).astype(BF16)
            dec_ref[grp * chunks_per_group + n] = jnp.exp(b_last)


def _inproj_gla(x2, mod3, g, w_in, wa, wa2, ba, br):
    tm = TM_INPROJ
    row = lambda m: (m, 0)
    assert 2 * GLA_HK == D_MODEL
    col_block = lambda idx: _resident_spec((D_MODEL, D_MODEL), (0, idx))
    return pl.pallas_call(
        _inproj_gla_kernel,
        grid=(N_TOK // tm,),
        in_specs=[
            pl.BlockSpec((tm, D_MODEL), row),
            _mod_spec(0, tm),
            _mod_spec(1, tm),
            _full_spec((1, D_MODEL)),
            col_block(0),
            col_block(1),
            col_block(2),
            _full_spec((D_MODEL, GLA_GATE_RANK)),
            _full_spec((GLA_GATE_RANK, GLA_HK)),
            _full_spec((1, GLA_HK)),
            _full_spec((1, D_MODEL)),
        ],
        out_specs=[
            pl.BlockSpec((tm, GLA_HK), row),
            pl.BlockSpec((tm, GLA_HK), row),
            pl.BlockSpec((tm, D_MODEL), row),
            pl.BlockSpec((tm, D_MODEL), row),
            pl.BlockSpec((tm // GLA_CHUNK, 1, GLA_HK), lambda m: (m, 0, 0)),
        ],
        out_shape=[
            jax.ShapeDtypeStruct((N_TOK, GLA_HK), BF16),
            jax.ShapeDtypeStruct((N_TOK, GLA_HK), BF16),
            jax.ShapeDtypeStruct((N_TOK, D_MODEL), BF16),
            jax.ShapeDtypeStruct((N_TOK, D_MODEL), BF16),
            jax.ShapeDtypeStruct((N_TOK // GLA_CHUNK, 1, GLA_HK), F32),
        ],
        scratch_shapes=[pltpu.VMEM((3, D_MODEL, D_MODEL), BF16)],
        compiler_params=_params("arbitrary"),
        name="inproj_gla",
    )(x2, mod3, mod3, g, w_in, w_in, w_in, wa, wa2, ba, br)


def _gla_kernel(qd_ref, ki_ref, v_ref, gate_ref, dec_ref, ng_ref, *rest, n_cast):
    cast_in, (o_ref, *cast_out), st_ref = rest[:n_cast], rest[n_cast:-1], rest[-1]
    for src, dst in zip(cast_in, cast_out):
        dst[...] = src[...].astype(BF16)

    @pl.when(pl.program_id(1) == 0)
    def _():
        st_ref[...] = jnp.zeros_like(st_ref)

    row = lax.broadcasted_iota(jnp.int32, (GLA_CHUNK, GLA_CHUNK), 0)
    col = lax.broadcasted_iota(jnp.int32, (GLA_CHUNK, GLA_CHUNK), 1)
    causal = row >= col
    n_chunks = qd_ref.shape[0] // GLA_CHUNK
    work = [(h, n) for h in range(GLA_HEADS) for n in range(n_chunks)]
    rows = lambda n: slice(n * GLA_CHUNK, (n + 1) * GLA_CHUNK)
    kcols = lambda h: slice(h * GLA_DK, (h + 1) * GLA_DK)
    vcols = lambda h: slice(h * GLA_DV, (h + 1) * GLA_DV)

    scores = {}
    updates = {}
    for h, n in work:
        scores[h, n] = lax.dot_general(qd_ref[rows(n), kcols(h)], ki_ref[rows(n), kcols(h)],
                                       NT_DIMS, preferred_element_type=F32)
        updates[h, n] = lax.dot_general(v_ref[rows(n), vcols(h)], ki_ref[rows(n), kcols(h)],
                                        TN_DIMS, preferred_element_type=F32)
    states = {}
    for h in range(GLA_HEADS):
        st = st_ref[h]
        for n in range(n_chunks):
            states[h, n] = st.astype(BF16)
            st = (st + updates[h, n]) * dec_ref[n][:, kcols(h)]
        st_ref[h] = st
    for h, n in work:
        s = jnp.where(causal, scores[h, n], 0.0).astype(BF16)
        o = _dot(s, v_ref[rows(n), vcols(h)]) + lax.dot_general(
            qd_ref[rows(n), kcols(h)], states[h, n], NT_DIMS, preferred_element_type=F32)
        y = o * _rms_scale(o) * ng_ref[...]
        o_ref[rows(n), vcols(h)] = (y * gate_ref[rows(n), vcols(h)].astype(F32)).astype(BF16)


def _cast_slabs(jobs, n_steps, step_of):
    flats, in_specs, out_specs, out_shapes = [], [], [], []
    for w, layer in jobs:
        _, n_rows, cols = w.shape
        slab = n_rows // n_steps
        assert slab * n_steps == n_rows and slab % 16 == 0
        first = layer * n_steps
        flats.append(w.reshape(-1, cols))
        in_specs.append(pl.BlockSpec((slab, cols), lambda *g, first=first: (first + step_of(*g), 0)))
        out_specs.append(pl.BlockSpec((slab, cols), lambda *g: (step_of(*g), 0)))
        out_shapes.append(jax.ShapeDtypeStruct((n_rows, cols), BF16))
    return flats, in_specs, out_specs, out_shapes


def _gla_core(qd, ki, v, gate, dec, ng, cast_jobs):
    tc = GLA_TC
    steps_t = SEQ // tc
    qk_spec = pl.BlockSpec((None, tc, GLA_HK), lambda b, t: (b, t, 0))
    v_spec = pl.BlockSpec((None, tc, D_MODEL), lambda b, t: (b, t, 0))
    flats, cast_in, cast_out, cast_shapes = _cast_slabs(
        cast_jobs, BATCH * steps_t, lambda b, t: b * steps_t + t)
    out = pl.pallas_call(
        functools.partial(_gla_kernel, n_cast=len(flats)),
        grid=(BATCH, steps_t),
        in_specs=[
            qk_spec, qk_spec, v_spec, v_spec,
            pl.BlockSpec((None, tc // GLA_CHUNK, 1, GLA_HK), lambda b, t: (b, t, 0, 0)),
            _full_spec((1, GLA_DV)),
        ] + cast_in,
        out_specs=[v_spec] + cast_out,
        out_shape=[jax.ShapeDtypeStruct((BATCH, SEQ, D_MODEL), BF16)] + cast_shapes,
        scratch_shapes=[pltpu.VMEM((GLA_HEADS, GLA_DV, GLA_DK), F32)],
        compiler_params=_params("arbitrary", "arbitrary"),
        name="gla_core",
    )(qd.reshape(BATCH, SEQ, GLA_HK), ki.reshape(BATCH, SEQ, GLA_HK),
      v.reshape(BATCH, SEQ, D_MODEL),
      gate.reshape(BATCH, SEQ, D_MODEL),
      dec.reshape(BATCH, SEQ // GLA_CHUNK, 1, GLA_HK), ng, *flats)
    return out[0], out[1:]


def _rope_kernel(pos_ref, freq_ref, cs_ref):
    ang = freq_ref[...] * pos_ref[...]
    cs_ref[0:ROPE_HALF, :] = jnp.cos(ang)
    cs_ref[ROPE_HALF:, :] = jnp.sin(ang)


def _rope_tables(positions):
    inv_freq = ROPE_THETA ** (-jnp.arange(0, ROPE_DIM, 2, dtype=F32) / ROPE_DIM)
    return pl.pallas_call(
        _rope_kernel,
        grid=(1,),
        in_specs=[_full_spec((1, N_TOK)), _full_spec((ROPE_HALF, 1))],
        out_specs=_full_spec((ROPE_DIM, N_TOK)),
        out_shape=jax.ShapeDtypeStruct((ROPE_DIM, N_TOK), F32),
        compiler_params=_params("arbitrary"),
        name="rope_tables",
    )(positions.astype(F32).reshape(1, N_TOK), inv_freq.reshape(ROPE_HALF, 1))


def _rope_expander():
    e = np.zeros((3, ROPE_DIM, 2 * DIFF_DV), np.float32)
    for lane in range(DIFF_DV):
        d = lane % DIFF_DH
        if d < ROPE_DIM:
            e[:, d % ROPE_HALF, lane] = 1.0
            e[:, ROPE_HALF + d % ROPE_HALF, DIFF_DV + lane] = -1.0 if d < ROPE_HALF else 1.0
    return jnp.asarray(e.reshape(3 * ROPE_DIM, 2 * DIFF_DV), BF16)


def _inproj_diff_kernel(x_ref, shift_ref, scale_ref, g_ref, wq_ref, wk_ref, wv_ref,
                        cs_ref, e_ref, q_ref, k_ref, vt_ref, wvt_ref):
    @pl.when(pl.program_id(0) == 0)
    def _():
        wvt_ref[...] = wv_ref[...].T

    d = lax.broadcasted_iota(jnp.int32, (1, DIFF_DV), 1) & (DIFF_DH - 1)
    first_half = d < ROPE_HALF
    q_scale = DIFF_DH ** -0.5 * math.log2(math.e)
    group_rows = x_ref.shape[0] // INPROJ_ROW_GROUPS

    for grp in range(INPROJ_ROW_GROUPS):
        rows = slice(grp * group_rows, (grp + 1) * group_rows)
        hb = _modulated_norm(x_ref[rows, :], g_ref[...], scale_ref[...],
                             shift_ref[...]).astype(BF16)

        cs = cs_ref[:, rows]
        hi = cs.astype(BF16).astype(F32)
        mid = (cs - hi).astype(BF16).astype(F32)
        lo = cs - hi - mid
        parts = jnp.concatenate([hi, mid, lo], axis=0)
        tab = _dot(parts.T.astype(BF16), e_ref[...])
        ct = tab[:, :DIFF_DV] + jnp.where(d >= ROPE_DIM, 1.0, 0.0)
        s12 = tab[:, DIFF_DV:]

        for w_ref, out_ref, scl in ((wq_ref, q_ref, q_scale), (wk_ref, k_ref, 1.0)):
            z = _dot(hb, w_ref[...])
            for h in range(DIFF_HEADS):
                sl = slice(h * DIFF_DV, (h + 1) * DIFF_DV)
                xh = z[:, sl]
                partner = jnp.where(first_half, pltpu.roll(xh, DIFF_DV - ROPE_HALF, 1),
                                    pltpu.roll(xh, ROPE_HALF, 1))
                out_ref[rows, sl] = ((xh * ct + partner * s12) * scl).astype(BF16)
        vt = lax.dot_general(wvt_ref[...], hb, NT_DIMS, preferred_element_type=F32).astype(BF16)
        ones = jnp.ones((VT_ROWS - DIFF_DV, group_rows), BF16)
        for h in range(DIFF_HEADS):
            vt_ref[h * VT_ROWS:h * VT_ROWS + DIFF_DV, rows] = vt[h * DIFF_DV:(h + 1) * DIFF_DV]
            vt_ref[h * VT_ROWS + DIFF_DV:(h + 1) * VT_ROWS, rows] = ones


def _inproj_diff(x2, mod3, g, w_in, cs, expander):
    tm = TM_INPROJ
    row = lambda m: (m, 0)
    o_spec = pl.BlockSpec((tm, D_MODEL), row)
    return pl.pallas_call(
        _inproj_diff_kernel,
        grid=(N_TOK // tm,),
        in_specs=[
            pl.BlockSpec((tm, D_MODEL), row),
            _mod_spec(0, tm),
            _mod_spec(1, tm),
            _full_spec((1, D_MODEL)),
            _resident_spec((D_MODEL, D_MODEL), (0, 0)),
            _resident_spec((D_MODEL, D_MODEL), (0, 1)),
            _resident_spec((D_MODEL, D_MODEL), (0, 2)),
            pl.BlockSpec((ROPE_DIM, tm), lambda m: (0, m)),
            _full_spec((3 * ROPE_DIM, 2 * DIFF_DV)),
        ],
        out_specs=[o_spec, o_spec, pl.BlockSpec((DIFF_HEADS * VT_ROWS, tm), lambda m: (0, m))],
        out_shape=[jax.ShapeDtypeStruct((N_TOK, D_MODEL), BF16)] * 2
        + [jax.ShapeDtypeStruct((DIFF_HEADS * VT_ROWS, N_TOK), BF16)],
        scratch_shapes=[pltpu.VMEM((D_MODEL, D_MODEL), BF16)],
        compiler_params=_params("arbitrary"),
        name="inproj_diff",
    )(x2, mod3, mod3, g, w_in, w_in, w_in, cs, expander)


def _attn_kernel(q_ref, k_ref, vt_ref, lam_ref, g_ref, o_ref, qs_ref, s_ref, m_ref, acc_ref,
                 *, lambda_init):
    tq, tk = ATT_TQ, ATT_TK
    assert tq == tk
    lf = lam_ref[...]
    lam = (jnp.exp(jnp.sum(lf[0:1] * lf[1:2], axis=-1, keepdims=True))
           - jnp.exp(jnp.sum(lf[2:3] * lf[3:4], axis=-1, keepdims=True)) + lambda_init)
    feat = lax.broadcasted_iota(jnp.int32, (DIFF_DV, tq), 0)
    k_pos = lax.broadcasted_iota(jnp.int32, (tk, 2 * tq), 0)
    q_pos = lax.broadcasted_iota(jnp.int32, (tk, 2 * tq), 1) & (tq - 1)
    heads = [slice(h * DIFF_DV, (h + 1) * DIFF_DV) for h in range(ATT_HEADS)]

    def stage_queries(qi, slot):
        q0 = pl.multiple_of(qi * tq, tq)
        for h, hs in enumerate(heads):
            qt = q_ref[pl.ds(q0, tq), hs].astype(F32).T
            qs_ref[slot, h, :, 0:tq] = jnp.where(feat < DIFF_DH, qt, 0.0).astype(BF16)
            qs_ref[slot, h, :, tq:2 * tq] = jnp.where(feat >= DIFF_DH, qt, 0.0).astype(BF16)

    def scores_for(j, h, slot):
        start = pl.multiple_of(j * tk, tk)
        return _dot(k_ref[pl.ds(start, tk), heads[h]], qs_ref[slot, h])

    def reset_state():
        m_ref[...] = jnp.full_like(m_ref, MASKED)
        acc_ref[...] = jnp.zeros_like(acc_ref)

    def kv_step(j, masked, next_j, next_slot):
        start = pl.multiple_of(j * tk, tk)
        for h in range(ATT_HEADS):
            vtb = vt_ref[h * VT_ROWS:(h + 1) * VT_ROWS, pl.ds(start, tk)]
            s = s_ref[h]
            next_s = scores_for(next_j, h, next_slot)
            if masked:
                s = jnp.where(k_pos <= q_pos, s, MASKED)
            m_prev = m_ref[h]
            m_new = jnp.maximum(m_prev, jnp.max(s, axis=0, keepdims=True))
            p = jnp.exp2(s - m_new)
            alpha = jnp.exp2(m_prev - m_new)
            acc_ref[h] = alpha * acc_ref[h] + _dot(vtb, p.astype(BF16))
            m_ref[h] = m_new
            s_ref[h] = next_s

    n_q = SEQ // tq
    stage_queries(0, 0)
    reset_state()
    for h in range(ATT_HEADS):
        s_ref[h] = scores_for(0, h, 0)

    def q_block(qi, carry):
        q0 = pl.multiple_of(qi * tq, tq)
        slot = qi & 1

        def body(j, c):
            kv_step(j, False, j + 1, slot)
            return c

        lax.fori_loop(0, qi, body, 0)
        stage_queries(jnp.minimum(qi + 1, n_q - 1), 1 - slot)
        kv_step(qi, True, 0, 1 - slot)
        for h, hs in enumerate(heads):
            acc = acc_ref[h, 0:DIFF_DV, :]
            inv_l = 1.0 / acc_ref[h, DIFF_DV:DIFF_DV + 1, :]
            o = acc[:, :tq] * inv_l[:, :tq] - lam * (acc[:, tq:] * inv_l[:, tq:])
            rms = lax.rsqrt(jnp.mean(o * o, axis=0, keepdims=True) + NORM_EPS)
            y = o * rms * (g_ref[...] * (1.0 - lambda_init))
            o_ref[pl.ds(q0, tq), hs] = y.T.astype(BF16)
        reset_state()
        return carry

    lax.fori_loop(0, n_q, q_block, 0)


def _diff_attention(q, k, vt, lam_vecs, g, lambda_init):
    width = ATT_HEADS * DIFF_DV
    spec = pl.BlockSpec((None, SEQ, width), lambda b, h: (b, 0, h))
    return pl.pallas_call(
        functools.partial(_attn_kernel, lambda_init=lambda_init),
        grid=(BATCH, DIFF_HEADS // ATT_HEADS),
        in_specs=[spec, spec, pl.BlockSpec((ATT_HEADS * VT_ROWS, SEQ), lambda b, h: (h, b)),
                  _full_spec((4, DIFF_DH)), _full_spec((DIFF_DV, 1))],
        out_specs=spec,
        out_shape=jax.ShapeDtypeStruct((BATCH, SEQ, D_MODEL), BF16),
        scratch_shapes=[
            pltpu.VMEM((2, ATT_HEADS, DIFF_DV, 2 * ATT_TQ), BF16),
            pltpu.VMEM((ATT_HEADS, ATT_TK, 2 * ATT_TQ), F32),
            pltpu.VMEM((ATT_HEADS, 1, 2 * ATT_TQ), F32),
            pltpu.VMEM((ATT_HEADS, VT_ROWS, 2 * ATT_TQ), F32),
        ],
        compiler_params=_params("arbitrary", "arbitrary"),
        name="diff_attention",
    )(q.reshape(BATCH, SEQ, D_MODEL), k.reshape(BATCH, SEQ, D_MODEL), vt, lam_vecs, g)


def _mlp_kernel(x_ref, o_ref, wo_ref, gt_ref, sh_ref, sc_ref, gc_ref, g_ref, w1_ref, w2_ref,
                fg_ref, *rest, final, n_cast):
    cast_in, (out_ref, *cast_out) = rest[:n_cast], rest[n_cast:]
    for src, dst in zip(cast_in, cast_out):
        dst[...] = src[...].astype(BF16)

    x1 = x_ref[...] + gt_ref[...] * _dot(o_ref[...], wo_ref[...])
    hb = _modulated_norm(x1, g_ref[...], sc_ref[...], sh_ref[...]).astype(BF16)
    a = jnp.square(jnp.maximum(_dot(hb, w1_ref[...]), 0.0)).astype(BF16)
    out = x1 + gc_ref[...] * _dot(a, w2_ref[...])
    if final:
        out = out * _rms_scale(out) * fg_ref[...]
    out_ref[...] = out


def _resident_spec(block_shape, index):
    return pl.BlockSpec(block_shape, lambda m: index, pipeline_mode=pl.Buffered(1))


def _outproj_mlp(x2, o2, mod3, wo, g, w1, w2, fg, final, cast_jobs=()):
    tm = TM_MLP
    n_steps = N_TOK // tm
    row = lambda m: (m, 0)
    flats, cast_in, cast_out, cast_shapes = _cast_slabs(cast_jobs, n_steps, lambda m: m)
    out = pl.pallas_call(
        functools.partial(_mlp_kernel, final=final, n_cast=len(flats)),
        grid=(n_steps,),
        in_specs=[
            pl.BlockSpec((tm, D_MODEL), row),
            pl.BlockSpec((tm, D_MODEL), row),
            _resident_spec((D_MODEL, D_MODEL), (0, 0)),
            _mod_spec(2, tm),
            _mod_spec(3, tm),
            _mod_spec(4, tm),
            _mod_spec(5, tm),
            _full_spec((1, D_MODEL)),
            _resident_spec((D_MODEL, D_FF), (0, 0)),
            _resident_spec((D_FF, D_MODEL), (0, 0)),
            _full_spec((1, D_MODEL)),
        ] + cast_in,
        out_specs=[pl.BlockSpec((tm, D_MODEL), row)] + cast_out,
        out_shape=[jax.ShapeDtypeStruct((N_TOK, D_MODEL), F32)] + cast_shapes,
        compiler_params=_params("arbitrary"),
        name="outproj_mlp_final" if final else "outproj_mlp",
    )(x2, o2, wo, mod3, mod3, mod3, mod3, g, w1, w2, fg, *flats)
    return out[0], out[1:]


def kernel(x, c, positions, ada_w, ada_b, norm_g, mlp_w1, mlp_w2, gla_w_in, gla_w_a2, gla_b_a,
           gla_b_r, gla_norm_g, gla_w_o, diff_w_in, diff_lambda, diff_subln_g, diff_w_o, final_g):
    x2 = x.reshape(N_TOK, D_MODEL)
    mod = _adaln_mod(c, ada_w, ada_b)
    mod_table = lambda layer: mod[layer].reshape(BATCH * N_MOD, 1, D_MODEL)
    mod3 = mod_table(0)
    fg = final_g.reshape(1, D_MODEL)

    qd, ki, v, gate, dec = _inproj_gla(
        x2, mod3, norm_g[0, 0].reshape(1, D_MODEL), gla_w_in[0], gla_w_in[0, :, 3 * D_MODEL:],
        gla_w_a2[0], gla_b_a[0].reshape(1, GLA_HK), gla_b_r[0].reshape(1, D_MODEL))
    o, (w1, w2, wo) = _gla_core(
        qd, ki, v, gate, dec, gla_norm_g[0].reshape(1, GLA_DV),
        [(mlp_w1, 0), (mlp_w2, 0), (gla_w_o, 0)])
    x2, (w1, w2, diff_win, wo) = _outproj_mlp(
        x2, o.reshape(N_TOK, D_MODEL), mod3, wo, norm_g[0, 1].reshape(1, D_MODEL), w1, w2, fg,
        False, [(mlp_w1, 1), (mlp_w2, 1), (diff_w_in, 0), (diff_w_o, 0)])
    mod3 = mod_table(1)

    lambda_init = 0.8 - 0.6 * math.exp(-0.3 * 1)
    cs = _rope_tables(positions)
    q, k, vt = _inproj_diff(
        x2, mod3, norm_g[1, 0].reshape(1, D_MODEL), diff_win, cs, _rope_expander())
    o = _diff_attention(q, k, vt, diff_lambda[0], diff_subln_g[0].reshape(DIFF_DV, 1), lambda_init)
    x2, _ = _outproj_mlp(x2, o.reshape(N_TOK, D_MODEL), mod3, wo,
                         norm_g[1, 1].reshape(1, D_MODEL), w1, w2, fg, True)
    return x2.reshape(BATCH, SEQ, D_MODEL)
```

```python
import functools
import math

import jax
import jax.numpy as jnp
import numpy as np
from jax import lax
from jax.experimental import pallas as pl
from jax.experimental.pallas import tpu as pltpu

D_MODEL = 1024
BATCH = 8
SEQ = 2048
DEPTH = 2
D_FF = 4 * D_MODEL
NORM_EPS = 1e-6
GLA_HEADS = 4
GLA_DK = 128
GLA_DV = 256
GLA_HK = GLA_HEADS * GLA_DK
GLA_GATE_RANK = 16
GLA_TAU = 16.0
GLA_CHUNK = 64
DIFF_HEADS = 8
DIFF_DH = 64
DIFF_DV = 128
ROPE_THETA = 500000.0
ROPE_DIM = DIFF_DH // 4
ROPE_HALF = ROPE_DIM // 2
N_TOK = BATCH * SEQ
N_MOD = 6

TM_INPROJ = 1024
INPROJ_ROW_GROUPS = 2
TM_MLP = 512
TF_MLP = 1024
TN_MOD = 1536
GLA_TC = 512
ATT_TQ = 256
ATT_TK = 256
VT_ROWS = DIFF_DV + 16
ATT_HEADS = 8
VMEM_LIMIT = 56 * 1024 * 1024

F32 = jnp.float32
BF16 = jnp.bfloat16
MASKED = -1e30
NT_DIMS = (((1,), (1,)), ((), ()))
TN_DIMS = (((0,), (0,)), ((), ()))


def _dot(a, b):
    return jnp.dot(a, b, preferred_element_type=F32)


def _split_bf16(a):
    hi = a.astype(BF16)
    lo = (a - hi.astype(F32)).astype(BF16)
    return hi, lo


def _rms_scale(x):
    return lax.rsqrt(jnp.mean(x * x, axis=-1, keepdims=True) + NORM_EPS)


def _silu(x):
    return x * (1.0 / (1.0 + jnp.exp(-x)))


def _params(*sem):
    return pltpu.CompilerParams(dimension_semantics=sem, vmem_limit_bytes=VMEM_LIMIT)


def _mod_block(c_ref, w_ref, b_ref, o_ref):
    c_hi, c_lo = _split_bf16(_silu(c_ref[...]))
    w_hi, w_lo = _split_bf16(w_ref[...])
    lhs = jnp.concatenate([c_hi, c_lo], axis=0)
    r = _dot(lhs, w_hi)
    o_ref[...] = r[:BATCH] + r[BATCH:] + _dot(c_hi, w_lo) + b_ref[...]


def _adaln_mod(c, ada_w, ada_b):
    n_out = N_MOD * D_MODEL
    return pl.pallas_call(
        _mod_block,
        grid=(DEPTH, n_out // TN_MOD),
        in_specs=[
            _full_spec((BATCH, D_MODEL)),
            pl.BlockSpec((None, D_MODEL, TN_MOD), lambda i, j: (i, 0, j)),
            pl.BlockSpec((None, 1, TN_MOD), lambda i, j: (i, 0, j)),
        ],
        out_specs=pl.BlockSpec((None, BATCH, TN_MOD), lambda i, j: (i, 0, j)),
        out_shape=jax.ShapeDtypeStruct((DEPTH, BATCH, n_out), F32),
        compiler_params=_params("arbitrary", "arbitrary"),
        name="adaln_mod",
    )(c, ada_w, ada_b.reshape(DEPTH, 1, n_out))


def _mod_spec(k, tm):
    tiles_per_seq = SEQ // tm
    return pl.BlockSpec((None, 1, D_MODEL), lambda m, *_: ((m // tiles_per_seq) * N_MOD + k, 0, 0))


def _full_spec(shape):
    zeros = (0,) * len(shape)
    return pl.BlockSpec(shape, lambda *_: zeros)


def _modulated_norm(x, g, scale, shift):
    return (x * _rms_scale(x) * g) * (1.0 + scale) + shift


def _inproj_gla_kernel(x_ref, shift_ref, scale_ref, g_ref, wqk_ref, wv_ref, wr_ref, wa_ref,
                       wa2_ref, ba_ref, br_ref,
                       qd_ref, ki_ref, v_ref, gate_ref, dec_ref, wbf_ref):
    @pl.when(pl.program_id(0) == 0)
    def _():
        for i, w_ref in enumerate((wqk_ref, wv_ref, wr_ref)):
            wbf_ref[i] = w_ref[...].T.astype(BF16)

    wqk_ref, wv_ref, wr_ref = wbf_ref.at[0], wbf_ref.at[1], wbf_ref.at[2]
    row = lax.broadcasted_iota(jnp.int32, (GLA_CHUNK, GLA_CHUNK), 0)
    col = lax.broadcasted_iota(jnp.int32, (GLA_CHUNK, GLA_CHUNK), 1)
    tril = jnp.where(row >= col, 1.0, 0.0).astype(BF16)
    w_hi, w_lo = _split_bf16(wa2_ref[...])
    w_gate = jnp.concatenate([w_hi, w_hi, w_lo], axis=0)
    q_scale = GLA_DK ** -0.5
    group_rows = x_ref.shape[0] // INPROJ_ROW_GROUPS
    chunks_per_group = group_rows // GLA_CHUNK

    for grp in range(INPROJ_ROW_GROUPS):
        rows = slice(grp * group_rows, (grp + 1) * group_rows)
        hb = _modulated_norm(x_ref[rows, :], g_ref[...], scale_ref[...],
                             shift_ref[...]).astype(BF16)
        a_hi, a_lo = _split_bf16(lax.dot_general(
            hb, wa_ref[...].astype(BF16), NT_DIMS, preferred_element_type=F32))
        r = _dot(hb, wr_ref[...]) + br_ref[...]
        gate_ref[rows, :] = _silu(r).astype(BF16)

        la = _dot(jnp.concatenate([a_hi, a_lo, a_hi], axis=1), w_gate) + ba_ref[...]
        qk = _dot(hb, wqk_ref[...])
        log_a = (jnp.minimum(la, 0.0) - jnp.log(1.0 + jnp.exp(-jnp.abs(la)))) * (1.0 / GLA_TAU)
        l_hi, l_lo = _split_bf16(log_a)
        l_cat = jnp.concatenate([l_hi, l_lo], axis=1)

        chunk = lambda n: slice(n * GLA_CHUNK, (n + 1) * GLA_CHUNK)
        cums = [_dot(tril, l_cat[chunk(n)]) for n in range(chunks_per_group)]
        v_ref[rows, :] = _dot(hb, wv_ref[...]).astype(BF16)
        for n in range(chunks_per_group):
            sl = chunk(n)
            out_rows = slice(grp * group_rows + n * GLA_CHUNK,
                             grp * group_rows + (n + 1) * GLA_CHUNK)
            b = cums[n][:, :GLA_HK] + cums[n][:, GLA_HK:]
            b_last = b[GLA_CHUNK - 1:GLA_CHUNK, :]
            q = qk[sl, :GLA_HK] * q_scale
            k = qk[sl, GLA_HK:]
            qd_ref[out_rows, :] = (q * jnp.exp(b)).astype(BF16)
            ki_ref[out_rows, :] = (k * jnp.exp(-b)).astype(BF16)
            dec_ref[grp * chunks_per_group + n] = jnp.exp(b_last)


def _inproj_gla(x2, mod3, g, w_in_t, wa2, ba, br):
    tm = TM_INPROJ
    row = lambda m: (m, 0)
    assert 2 * GLA_HK == D_MODEL
    row_block = lambda idx: _resident_spec((D_MODEL, D_MODEL), (idx, 0))
    return pl.pallas_call(
        _inproj_gla_kernel,
        grid=(N_TOK // tm,),
        in_specs=[
            pl.BlockSpec((tm, D_MODEL), row),
            _mod_spec(0, tm),
            _mod_spec(1, tm),
            _full_spec((1, D_MODEL)),
            row_block(0),
            row_block(1),
            row_block(2),
            _resident_spec((GLA_GATE_RANK, D_MODEL), (3 * D_MODEL // GLA_GATE_RANK, 0)),
            _full_spec((GLA_GATE_RANK, GLA_HK)),
            _full_spec((1, GLA_HK)),
            _full_spec((1, D_MODEL)),
        ],
        out_specs=[
            pl.BlockSpec((tm, GLA_HK), row),
            pl.BlockSpec((tm, GLA_HK), row),
            pl.BlockSpec((tm, D_MODEL), row),
            pl.BlockSpec((tm, D_MODEL), row),
            pl.BlockSpec((tm // GLA_CHUNK, 1, GLA_HK), lambda m: (m, 0, 0)),
        ],
        out_shape=[
            jax.ShapeDtypeStruct((N_TOK, GLA_HK), BF16),
            jax.ShapeDtypeStruct((N_TOK, GLA_HK), BF16),
            jax.ShapeDtypeStruct((N_TOK, D_MODEL), BF16),
            jax.ShapeDtypeStruct((N_TOK, D_MODEL), BF16),
            jax.ShapeDtypeStruct((N_TOK // GLA_CHUNK, 1, GLA_HK), F32),
        ],
        scratch_shapes=[pltpu.VMEM((3, D_MODEL, D_MODEL), BF16)],
        compiler_params=_params("arbitrary"),
        name="inproj_gla",
    )(x2, mod3, mod3, g, w_in_t, w_in_t, w_in_t, w_in_t, wa2, ba, br)


def _gla_kernel(qd_ref, ki_ref, v_ref, gate_ref, dec_ref, ng_ref, *rest, n_cast):
    cast_in, (o_ref, *cast_out), st_ref = rest[:n_cast], rest[n_cast:-1], rest[-1]
    for src, dst in zip(cast_in, cast_out):
        dst[...] = src[...].astype(BF16)

    @pl.when(pl.program_id(1) == 0)
    def _():
        st_ref[...] = jnp.zeros_like(st_ref)

    row = lax.broadcasted_iota(jnp.int32, (GLA_CHUNK, GLA_CHUNK), 0)
    col = lax.broadcasted_iota(jnp.int32, (GLA_CHUNK, GLA_CHUNK), 1)
    causal = row >= col
    n_chunks = qd_ref.shape[0] // GLA_CHUNK
    work = [(h, n) for h in range(GLA_HEADS) for n in range(n_chunks)]
    rows = lambda n: slice(n * GLA_CHUNK, (n + 1) * GLA_CHUNK)
    kcols = lambda h: slice(h * GLA_DK, (h + 1) * GLA_DK)
    vcols = lambda h: slice(h * GLA_DV, (h + 1) * GLA_DV)

    scores = {}
    updates = {}
    for h, n in work:
        scores[h, n] = lax.dot_general(qd_ref[rows(n), kcols(h)], ki_ref[rows(n), kcols(h)],
                                       NT_DIMS, preferred_element_type=F32)
        updates[h, n] = lax.dot_general(v_ref[rows(n), vcols(h)], ki_ref[rows(n), kcols(h)],
                                        TN_DIMS, preferred_element_type=F32)
    states = {}
    for h in range(GLA_HEADS):
        st = st_ref[h]
        for n in range(n_chunks):
            states[h, n] = st.astype(BF16)
            st = (st + updates[h, n]) * dec_ref[n][:, kcols(h)]
        st_ref[h] = st
    for h, n in work:
        s = jnp.where(causal, scores[h, n], 0.0).astype(BF16)
        o = _dot(s, v_ref[rows(n), vcols(h)]) + lax.dot_general(
            qd_ref[rows(n), kcols(h)], states[h, n], NT_DIMS, preferred_element_type=F32)
        y = o * _rms_scale(o) * ng_ref[...]
        o_ref[rows(n), vcols(h)] = (y * gate_ref[rows(n), vcols(h)].astype(F32)).astype(BF16)


def _cast_slabs(jobs, n_steps, step_of):
    flats, in_specs, out_specs, out_shapes = [], [], [], []
    for w, layer in jobs:
        _, n_rows, cols = w.shape
        slab = n_rows // n_steps
        assert slab * n_steps == n_rows and slab % 16 == 0
        first = layer * n_steps
        flats.append(w.reshape(-1, cols))
        in_specs.append(pl.BlockSpec((slab, cols), lambda *g, first=first: (first + step_of(*g), 0)))
        out_specs.append(pl.BlockSpec((slab, cols), lambda *g: (step_of(*g), 0)))
        out_shapes.append(jax.ShapeDtypeStruct((n_rows, cols), BF16))
    return flats, in_specs, out_specs, out_shapes


def _gla_core(qd, ki, v, gate, dec, ng, cast_jobs):
    tc = GLA_TC
    steps_t = SEQ // tc
    qk_spec = pl.BlockSpec((None, tc, GLA_HK), lambda b, t: (b, t, 0))
    v_spec = pl.BlockSpec((None, tc, D_MODEL), lambda b, t: (b, t, 0))
    flats, cast_in, cast_out, cast_shapes = _cast_slabs(
        cast_jobs, BATCH * steps_t, lambda b, t: b * steps_t + t)
    out = pl.pallas_call(
        functools.partial(_gla_kernel, n_cast=len(flats)),
        grid=(BATCH, steps_t),
        in_specs=[
            qk_spec, qk_spec, v_spec, v_spec,
            pl.BlockSpec((None, tc // GLA_CHUNK, 1, GLA_HK), lambda b, t: (b, t, 0, 0)),
            _full_spec((1, GLA_DV)),
        ] + cast_in,
        out_specs=[v_spec] + cast_out,
        out_shape=[jax.ShapeDtypeStruct((BATCH, SEQ, D_MODEL), BF16)] + cast_shapes,
        scratch_shapes=[pltpu.VMEM((GLA_HEADS, GLA_DV, GLA_DK), F32)],
        compiler_params=_params("arbitrary", "arbitrary"),
        name="gla_core",
    )(qd.reshape(BATCH, SEQ, GLA_HK), ki.reshape(BATCH, SEQ, GLA_HK),
      v.reshape(BATCH, SEQ, D_MODEL),
      gate.reshape(BATCH, SEQ, D_MODEL),
      dec.reshape(BATCH, SEQ // GLA_CHUNK, 1, GLA_HK), ng, *flats)
    return out[0], out[1:]


def _rope_kernel(pos_ref, freq_ref, cs_ref):
    ang = freq_ref[...] * pos_ref[...]
    cs_ref[0:ROPE_HALF, :] = jnp.cos(ang)
    cs_ref[ROPE_HALF:, :] = jnp.sin(ang)


def _rope_tables(positions):
    inv_freq = ROPE_THETA ** (-jnp.arange(0, ROPE_DIM, 2, dtype=F32) / ROPE_DIM)
    return pl.pallas_call(
        _rope_kernel,
        grid=(1,),
        in_specs=[_full_spec((1, N_TOK)), _full_spec((ROPE_HALF, 1))],
        out_specs=_full_spec((ROPE_DIM, N_TOK)),
        out_shape=jax.ShapeDtypeStruct((ROPE_DIM, N_TOK), F32),
        compiler_params=_params("arbitrary"),
        name="rope_tables",
    )(positions.astype(F32).reshape(1, N_TOK), inv_freq.reshape(ROPE_HALF, 1))


def _rope_expander():
    e = np.zeros((3, ROPE_DIM, 2 * DIFF_DV), np.float32)
    for lane in range(DIFF_DV):
        d = lane % DIFF_DH
        if d < ROPE_DIM:
            e[:, d % ROPE_HALF, lane] = 1.0
            e[:, ROPE_HALF + d % ROPE_HALF, DIFF_DV + lane] = -1.0 if d < ROPE_HALF else 1.0
    return jnp.asarray(e.reshape(3 * ROPE_DIM, 2 * DIFF_DV), BF16)


def _inproj_diff_kernel(x_ref, shift_ref, scale_ref, g_ref, wq_ref, wk_ref, wv_ref,
                        cs_ref, e_ref, q_ref, k_ref, vt_ref, wvt_ref):
    @pl.when(pl.program_id(0) == 0)
    def _():
        wvt_ref[...] = wv_ref[...].T

    d = lax.broadcasted_iota(jnp.int32, (1, DIFF_DV), 1) & (DIFF_DH - 1)
    first_half = d < ROPE_HALF
    q_scale = DIFF_DH ** -0.5 * math.log2(math.e)
    group_rows = x_ref.shape[0] // INPROJ_ROW_GROUPS

    for grp in range(INPROJ_ROW_GROUPS):
        rows = slice(grp * group_rows, (grp + 1) * group_rows)
        hb = _modulated_norm(x_ref[rows, :], g_ref[...], scale_ref[...],
                             shift_ref[...]).astype(BF16)

        cs = cs_ref[:, rows]
        hi = cs.astype(BF16).astype(F32)
        mid = (cs - hi).astype(BF16).astype(F32)
        lo = cs - hi - mid
        parts = jnp.concatenate([hi, mid, lo], axis=0)
        tab = _dot(parts.T.astype(BF16), e_ref[...])
        ct = tab[:, :DIFF_DV] + jnp.where(d >= ROPE_DIM, 1.0, 0.0)
        s12 = tab[:, DIFF_DV:]

        for w_ref, out_ref, scl in ((wq_ref, q_ref, q_scale), (wk_ref, k_ref, 1.0)):
            z = _dot(hb, w_ref[...])
            for h in range(DIFF_HEADS):
                sl = slice(h * DIFF_DV, (h + 1) * DIFF_DV)
                xh = z[:, sl]
                partner = jnp.where(first_half, pltpu.roll(xh, DIFF_DV - ROPE_HALF, 1),
                                    pltpu.roll(xh, ROPE_HALF, 1))
                out_ref[rows, sl] = ((xh * ct + partner * s12) * scl).astype(BF16)
        vt = lax.dot_general(wvt_ref[...], hb, NT_DIMS, preferred_element_type=F32).astype(BF16)
        ones = jnp.ones((VT_ROWS - DIFF_DV, group_rows), BF16)
        for h in range(DIFF_HEADS):
            vt_ref[h * VT_ROWS:h * VT_ROWS + DIFF_DV, rows] = vt[h * DIFF_DV:(h + 1) * DIFF_DV]
            vt_ref[h * VT_ROWS + DIFF_DV:(h + 1) * VT_ROWS, rows] = ones


def _inproj_diff(x2, mod3, g, w_in, cs, expander):
    tm = TM_INPROJ
    row = lambda m: (m, 0)
    o_spec = pl.BlockSpec((tm, D_MODEL), row)
    return pl.pallas_call(
        _inproj_diff_kernel,
        grid=(N_TOK // tm,),
        in_specs=[
            pl.BlockSpec((tm, D_MODEL), row),
            _mod_spec(0, tm),
            _mod_spec(1, tm),
            _full_spec((1, D_MODEL)),
            _resident_spec((D_MODEL, D_MODEL), (0, 0)),
            _resident_spec((D_MODEL, D_MODEL), (0, 1)),
            _resident_spec((D_MODEL, D_MODEL), (0, 2)),
            pl.BlockSpec((ROPE_DIM, tm), lambda m: (0, m)),
            _full_spec((3 * ROPE_DIM, 2 * DIFF_DV)),
        ],
        out_specs=[o_spec, o_spec, pl.BlockSpec((DIFF_HEADS * VT_ROWS, tm), lambda m: (0, m))],
        out_shape=[jax.ShapeDtypeStruct((N_TOK, D_MODEL), BF16)] * 2
        + [jax.ShapeDtypeStruct((DIFF_HEADS * VT_ROWS, N_TOK), BF16)],
        scratch_shapes=[pltpu.VMEM((D_MODEL, D_MODEL), BF16)],
        compiler_params=_params("arbitrary"),
        name="inproj_diff",
    )(x2, mod3, mod3, g, w_in, w_in, w_in, cs, expander)


def _attn_kernel(q_ref, k_ref, vt_ref, lam_ref, g_ref, o_ref, qs_ref, s_ref, m_ref, acc_ref,
                 *, lambda_init):
    tq, tk = ATT_TQ, ATT_TK
    assert tq == tk
    lf = lam_ref[...]
    lam = (jnp.exp(jnp.sum(lf[0:1] * lf[1:2], axis=-1, keepdims=True))
           - jnp.exp(jnp.sum(lf[2:3] * lf[3:4], axis=-1, keepdims=True)) + lambda_init)
    feat = lax.broadcasted_iota(jnp.int32, (DIFF_DV, tq), 0)
    k_pos = lax.broadcasted_iota(jnp.int32, (tk, 2 * tq), 0)
    q_pos = lax.broadcasted_iota(jnp.int32, (tk, 2 * tq), 1) & (tq - 1)
    heads = [slice(h * DIFF_DV, (h + 1) * DIFF_DV) for h in range(ATT_HEADS)]

    def stage_queries(qi, slot):
        q0 = pl.multiple_of(qi * tq, tq)
        for h, hs in enumerate(heads):
            qt = q_ref[pl.ds(q0, tq), hs].astype(F32).T
            qs_ref[slot, h, :, 0:tq] = jnp.where(feat < DIFF_DH, qt, 0.0).astype(BF16)
            qs_ref[slot, h, :, tq:2 * tq] = jnp.where(feat >= DIFF_DH, qt, 0.0).astype(BF16)

    def scores_for(j, h, slot):
        start = pl.multiple_of(j * tk, tk)
        return _dot(k_ref[pl.ds(start, tk), heads[h]], qs_ref[slot, h])

    def reset_state():
        m_ref[...] = jnp.full_like(m_ref, MASKED)
        acc_ref[...] = jnp.zeros_like(acc_ref)

    def kv_step(j, masked, next_j, next_slot):
        start = pl.multiple_of(j * tk, tk)
        for h in range(ATT_HEADS):
            vtb = vt_ref[h * VT_ROWS:(h + 1) * VT_ROWS, pl.ds(start, tk)]
            s = s_ref[h]
            next_s = scores_for(next_j, h, next_slot)
            if masked:
                s = jnp.where(k_pos <= q_pos, s, MASKED)
            m_prev = m_ref[h]
            m_new = jnp.maximum(m_prev, jnp.max(s, axis=0, keepdims=True))
            p = jnp.exp2(s - m_new)
            alpha = jnp.exp2(m_prev - m_new)
            acc_ref[h] = alpha * acc_ref[h] + _dot(vtb, p.astype(BF16))
            m_ref[h] = m_new
            s_ref[h] = next_s

    n_q = SEQ // tq
    stage_queries(0, 0)
    reset_state()
    for h in range(ATT_HEADS):
        s_ref[h] = scores_for(0, h, 0)

    def q_block(qi, carry):
        q0 = pl.multiple_of(qi * tq, tq)
        slot = qi & 1

        def body(j, c):
            kv_step(j, False, j + 1, slot)
            return c

        lax.fori_loop(0, qi, body, 0)
        stage_queries(jnp.minimum(qi + 1, n_q - 1), 1 - slot)
        kv_step(qi, True, 0, 1 - slot)
        for h, hs in enumerate(heads):
            acc = acc_ref[h, 0:DIFF_DV, :]
            inv_l = 1.0 / acc_ref[h, DIFF_DV:DIFF_DV + 1, :]
            o = acc[:, :tq] * inv_l[:, :tq] - lam * (acc[:, tq:] * inv_l[:, tq:])
            rms = lax.rsqrt(jnp.mean(o * o, axis=0, keepdims=True) + NORM_EPS)
            y = o * rms * (g_ref[...] * (1.0 - lambda_init))
            o_ref[pl.ds(q0, tq), hs] = y.T.astype(BF16)
        reset_state()
        return carry

    lax.fori_loop(0, n_q, q_block, 0)


def _diff_attention(q, k, vt, lam_vecs, g, lambda_init):
    width = ATT_HEADS * DIFF_DV
    spec = pl.BlockSpec((None, SEQ, width), lambda b, h: (b, 0, h))
    return pl.pallas_call(
        functools.partial(_attn_kernel, lambda_init=lambda_init),
        grid=(BATCH, DIFF_HEADS // ATT_HEADS),
        in_specs=[spec, spec, pl.BlockSpec((ATT_HEADS * VT_ROWS, SEQ), lambda b, h: (h, b)),
                  _full_spec((4, DIFF_DH)), _full_spec((DIFF_DV, 1))],
        out_specs=spec,
        out_shape=jax.ShapeDtypeStruct((BATCH, SEQ, D_MODEL), BF16),
        scratch_shapes=[
            pltpu.VMEM((2, ATT_HEADS, DIFF_DV, 2 * ATT_TQ), BF16),
            pltpu.VMEM((ATT_HEADS, ATT_TK, 2 * ATT_TQ), F32),
            pltpu.VMEM((ATT_HEADS, 1, 2 * ATT_TQ), F32),
            pltpu.VMEM((ATT_HEADS, VT_ROWS, 2 * ATT_TQ), F32),
        ],
        compiler_params=_params("arbitrary", "arbitrary"),
        name="diff_attention",
    )(q.reshape(BATCH, SEQ, D_MODEL), k.reshape(BATCH, SEQ, D_MODEL), vt, lam_vecs, g)


def _mlp_kernel(x_ref, o_ref, wo_ref, gt_ref, sh_ref, sc_ref, gc_ref, g_ref, w1_ref, w2_ref,
                fg_ref, *rest, final, n_cast):
    cast_in, (out_ref, *cast_out) = rest[:n_cast], rest[n_cast:]
    for src, dst in zip(cast_in, cast_out):
        dst[...] = src[...].astype(BF16)

    x1 = x_ref[...] + gt_ref[...] * _dot(o_ref[...], wo_ref[...])
    hb = _modulated_norm(x1, g_ref[...], sc_ref[...], sh_ref[...]).astype(BF16)
    a = jnp.square(jnp.maximum(_dot(hb, w1_ref[...]), 0.0)).astype(BF16)
    out = x1 + gc_ref[...] * _dot(a, w2_ref[...])
    if final:
        out = out * _rms_scale(out) * fg_ref[...]
    out_ref[...] = out


def _resident_spec(block_shape, index):
    return pl.BlockSpec(block_shape, lambda m: index, pipeline_mode=pl.Buffered(1))


def _outproj_mlp(x2, o2, mod3, wo, g, w1, w2, fg, final, cast_jobs=()):
    tm = TM_MLP
    n_steps = N_TOK // tm
    row = lambda m: (m, 0)
    flats, cast_in, cast_out, cast_shapes = _cast_slabs(cast_jobs, n_steps, lambda m: m)
    out = pl.pallas_call(
        functools.partial(_mlp_kernel, final=final, n_cast=len(flats)),
        grid=(n_steps,),
        in_specs=[
            pl.BlockSpec((tm, D_MODEL), row),
            pl.BlockSpec((tm, D_MODEL), row),
            _resident_spec((D_MODEL, D_MODEL), (0, 0)),
            _mod_spec(2, tm),
            _mod_spec(3, tm),
            _mod_spec(4, tm),
            _mod_spec(5, tm),
            _full_spec((1, D_MODEL)),
            _resident_spec((D_MODEL, D_FF), (0, 0)),
            _resident_spec((D_FF, D_MODEL), (0, 0)),
            _full_spec((1, D_MODEL)),
        ] + cast_in,
        out_specs=[pl.BlockSpec((tm, D_MODEL), row)] + cast_out,
        out_shape=[jax.ShapeDtypeStruct((N_TOK, D_MODEL), F32)] + cast_shapes,
        compiler_params=_params("arbitrary"),
        name="outproj_mlp_final" if final else "outproj_mlp",
    )(x2, o2, wo, mod3, mod3, mod3, mod3, g, w1, w2, fg, *flats)
    return out[0], out[1:]


def kernel(x, c, positions, ada_w, ada_b, norm_g, mlp_w1, mlp_w2, gla_w_in, gla_w_a2, gla_b_a,
           gla_b_r, gla_norm_g, gla_w_o, diff_w_in, diff_lambda, diff_subln_g, diff_w_o, final_g):
    x2 = x.reshape(N_TOK, D_MODEL)
    mod = _adaln_mod(c, ada_w, ada_b)
    mod_table = lambda layer: mod[layer].reshape(BATCH * N_MOD, 1, D_MODEL)
    mod3 = mod_table(0)
    fg = final_g.reshape(1, D_MODEL)

    qd, ki, v, gate, dec = _inproj_gla(
        x2, mod3, norm_g[0, 0].reshape(1, D_MODEL), gla_w_in[0].T, gla_w_a2[0], gla_b_a[0].reshape(1, GLA_HK), gla_b_r[0].reshape(1, D_MODEL))
    o, (w1, w2, wo) = _gla_core(
        qd, ki, v, gate, dec, gla_norm_g[0].reshape(1, GLA_DV),
        [(mlp_w1, 0), (mlp_w2, 0), (gla_w_o, 0)])
    x2, (w1, w2, diff_win, wo) = _outproj_mlp(
        x2, o.reshape(N_TOK, D_MODEL), mod3, wo, norm_g[0, 1].reshape(1, D_MODEL), w1, w2, fg,
        False, [(mlp_w1, 1), (mlp_w2, 1), (diff_w_in, 0), (diff_w_o, 0)])
    mod3 = mod_table(1)

    lambda_init = 0.8 - 0.6 * math.exp(-0.3 * 1)
    cs = _rope_tables(positions)
    q, k, vt = _inproj_diff(
        x2, mod3, norm_g[1, 0].reshape(1, D_MODEL), diff_win, cs, _rope_expander())
    o = _diff_attention(q, k, vt, diff_lambda[0], diff_subln_g[0].reshape(DIFF_DV, 1), lambda_init)
    x2, _ = _outproj_mlp(x2, o.reshape(N_TOK, D_MODEL), mod3, wo,
                         norm_g[1, 1].reshape(1, D_MODEL), w1, w2, fg, True)
    return x2.reshape(BATCH, SEQ, D_MODEL)
```

```python
import functools
import math

import jax
import jax.numpy as jnp
import numpy as np
from jax import lax
from jax.experimental import pallas as pl
from jax.experimental.pallas import tpu as pltpu

D_MODEL = 1024
BATCH = 8
SEQ = 2048
DEPTH = 2
D_FF = 4 * D_MODEL
NORM_EPS = 1e-6
GLA_HEADS = 4
GLA_DK = 128
GLA_DV = 256
GLA_HK = GLA_HEADS * GLA_DK
GLA_GATE_RANK = 16
GLA_TAU = 16.0
GLA_CHUNK = 64
DIFF_HEADS = 8
DIFF_DH = 64
DIFF_DV = 128
ROPE_THETA = 500000.0
ROPE_DIM = DIFF_DH // 4
ROPE_HALF = ROPE_DIM // 2
N_TOK = BATCH * SEQ
N_MOD = 6

TM_INPROJ = 1024
INPROJ_ROW_GROUPS = 2
TM_MLP = 512
TM_GLA = 512
TF_MLP = 1024
TN_MOD = 1536
GLA_TC = 512
ATT_TQ = 256
ATT_TK = 256
VT_ROWS = DIFF_DV + 16
ATT_HEADS = 8
VMEM_LIMIT = 56 * 1024 * 1024

F32 = jnp.float32
BF16 = jnp.bfloat16
MASKED = -1e30
NT_DIMS = (((1,), (1,)), ((), ()))
TN_DIMS = (((0,), (0,)), ((), ()))


def _dot(a, b):
    return jnp.dot(a, b, preferred_element_type=F32)


def _split_bf16(a):
    hi = a.astype(BF16)
    lo = (a - hi.astype(F32)).astype(BF16)
    return hi, lo


def _rms_scale(x):
    return lax.rsqrt(jnp.mean(x * x, axis=-1, keepdims=True) + NORM_EPS)


def _silu(x):
    return x * (1.0 / (1.0 + jnp.exp(-x)))


def _params(*sem):
    return pltpu.CompilerParams(dimension_semantics=sem, vmem_limit_bytes=VMEM_LIMIT)


def _mod_block(c_ref, w_ref, b_ref, o_ref):
    c_hi, c_lo = _split_bf16(_silu(c_ref[...]))
    w_hi, w_lo = _split_bf16(w_ref[...])
    lhs = jnp.concatenate([c_hi, c_lo], axis=0)
    r = _dot(lhs, w_hi)
    o_ref[...] = r[:BATCH] + r[BATCH:] + _dot(c_hi, w_lo) + b_ref[...]


def _adaln_mod(c, ada_w, ada_b):
    n_out = N_MOD * D_MODEL
    return pl.pallas_call(
        _mod_block,
        grid=(DEPTH, n_out // TN_MOD),
        in_specs=[
            _full_spec((BATCH, D_MODEL)),
            pl.BlockSpec((None, D_MODEL, TN_MOD), lambda i, j: (i, 0, j)),
            pl.BlockSpec((None, 1, TN_MOD), lambda i, j: (i, 0, j)),
        ],
        out_specs=pl.BlockSpec((None, BATCH, TN_MOD), lambda i, j: (i, 0, j)),
        out_shape=jax.ShapeDtypeStruct((DEPTH, BATCH, n_out), F32),
        compiler_params=_params("arbitrary", "arbitrary"),
        name="adaln_mod",
    )(c, ada_w, ada_b.reshape(DEPTH, 1, n_out))


def _mod_spec(k, tm):
    tiles_per_seq = SEQ // tm
    return pl.BlockSpec((None, 1, D_MODEL), lambda m, *_: ((m // tiles_per_seq) * N_MOD + k, 0, 0))


def _full_spec(shape):
    zeros = (0,) * len(shape)
    return pl.BlockSpec(shape, lambda *_: zeros)


def _modulated_norm(x, g, scale, shift):
    return (x * _rms_scale(x) * g) * (1.0 + scale) + shift


def _inproj_gla_kernel(x_ref, shift_ref, scale_ref, g_ref, wqk_ref, wv_ref, wr_ref, wa_ref,
                       wa2_ref, ba_ref, br_ref,
                       qd_ref, ki_ref, v_ref, gate_ref, dec_ref, wbf_ref):
    @pl.when(pl.program_id(0) == 0)
    def _():
        for i, w_ref in enumerate((wqk_ref, wv_ref, wr_ref)):
            wbf_ref[i] = w_ref[...].T.astype(BF16)

    wqk_ref, wv_ref, wr_ref = wbf_ref.at[0], wbf_ref.at[1], wbf_ref.at[2]
    row = lax.broadcasted_iota(jnp.int32, (GLA_CHUNK, GLA_CHUNK), 0)
    col = lax.broadcasted_iota(jnp.int32, (GLA_CHUNK, GLA_CHUNK), 1)
    tril = jnp.where(row >= col, 1.0, 0.0).astype(BF16)
    w_hi, w_lo = _split_bf16(wa2_ref[...])
    w_gate = jnp.concatenate([w_hi, w_hi, w_lo], axis=0)
    q_scale = GLA_DK ** -0.5
    group_rows = x_ref.shape[0] // INPROJ_ROW_GROUPS
    chunks_per_group = group_rows // GLA_CHUNK

    for grp in range(INPROJ_ROW_GROUPS):
        rows = slice(grp * group_rows, (grp + 1) * group_rows)
        hb = _modulated_norm(x_ref[rows, :], g_ref[...], scale_ref[...],
                             shift_ref[...]).astype(BF16)
        a_hi, a_lo = _split_bf16(lax.dot_general(
            hb, wa_ref[...].astype(BF16), NT_DIMS, preferred_element_type=F32))
        r = _dot(hb, wr_ref[...]) + br_ref[...]
        gate_ref[rows, :] = _silu(r).astype(BF16)

        la = _dot(jnp.concatenate([a_hi, a_lo, a_hi], axis=1), w_gate) + ba_ref[...]
        qk = _dot(hb, wqk_ref[...])
        log_a = (jnp.minimum(la, 0.0) - jnp.log(1.0 + jnp.exp(-jnp.abs(la)))) * (1.0 / GLA_TAU)
        l_hi, l_lo = _split_bf16(log_a)
        l_cat = jnp.concatenate([l_hi, l_lo], axis=1)

        chunk = lambda n: slice(n * GLA_CHUNK, (n + 1) * GLA_CHUNK)
        cums = [_dot(tril, l_cat[chunk(n)]) for n in range(chunks_per_group)]
        v_ref[rows, :] = _dot(hb, wv_ref[...]).astype(BF16)
        for n in range(chunks_per_group):
            sl = chunk(n)
            out_rows = slice(grp * group_rows + n * GLA_CHUNK,
                             grp * group_rows + (n + 1) * GLA_CHUNK)
            b = cums[n][:, :GLA_HK] + cums[n][:, GLA_HK:]
            b_last = b[GLA_CHUNK - 1:GLA_CHUNK, :]
            q = qk[sl, :GLA_HK] * q_scale
            k = qk[sl, GLA_HK:]
            qd_ref[out_rows, :] = (q * jnp.exp(b)).astype(BF16)
            ki_ref[out_rows, :] = (k * jnp.exp(-b)).astype(BF16)
            dec_ref[grp * chunks_per_group + n] = jnp.exp(b_last)


def _inproj_gla(x2, mod3, g, w_in_t, wa2, ba, br):
    tm = TM_INPROJ
    row = lambda m: (m, 0)
    assert 2 * GLA_HK == D_MODEL
    row_block = lambda idx: _resident_spec((D_MODEL, D_MODEL), (idx, 0))
    return pl.pallas_call(
        _inproj_gla_kernel,
        grid=(N_TOK // tm,),
        in_specs=[
            pl.BlockSpec((tm, D_MODEL), row),
            _mod_spec(0, tm),
            _mod_spec(1, tm),
            _full_spec((1, D_MODEL)),
            row_block(0),
            row_block(1),
            row_block(2),
            _resident_spec((GLA_GATE_RANK, D_MODEL), (3 * D_MODEL // GLA_GATE_RANK, 0)),
            _full_spec((GLA_GATE_RANK, GLA_HK)),
            _full_spec((1, GLA_HK)),
            _full_spec((1, D_MODEL)),
        ],
        out_specs=[
            pl.BlockSpec((tm, GLA_HK), row),
            pl.BlockSpec((tm, GLA_HK), row),
            pl.BlockSpec((tm, D_MODEL), row),
            pl.BlockSpec((tm, D_MODEL), row),
            pl.BlockSpec((tm // GLA_CHUNK, 1, GLA_HK), lambda m: (m, 0, 0)),
        ],
        out_shape=[
            jax.ShapeDtypeStruct((N_TOK, GLA_HK), BF16),
            jax.ShapeDtypeStruct((N_TOK, GLA_HK), BF16),
            jax.ShapeDtypeStruct((N_TOK, D_MODEL), BF16),
            jax.ShapeDtypeStruct((N_TOK, D_MODEL), BF16),
            jax.ShapeDtypeStruct((N_TOK // GLA_CHUNK, 1, GLA_HK), F32),
        ],
        scratch_shapes=[pltpu.VMEM((3, D_MODEL, D_MODEL), BF16)],
        compiler_params=_params("arbitrary"),
        name="inproj_gla",
    )(x2, mod3, mod3, g, w_in_t, w_in_t, w_in_t, w_in_t, wa2, ba, br)


def _gla_kernel(qd_ref, ki_ref, v_ref, gate_ref, dec_ref, ng_ref, *rest, n_cast):
    cast_in, (o_ref, *cast_out), st_ref = rest[:n_cast], rest[n_cast:-1], rest[-1]
    for src, dst in zip(cast_in, cast_out):
        dst[...] = src[...].astype(BF16)

    @pl.when(pl.program_id(1) == 0)
    def _():
        st_ref[...] = jnp.zeros_like(st_ref)

    row = lax.broadcasted_iota(jnp.int32, (GLA_CHUNK, GLA_CHUNK), 0)
    col = lax.broadcasted_iota(jnp.int32, (GLA_CHUNK, GLA_CHUNK), 1)
    causal = row >= col
    n_chunks = qd_ref.shape[0] // GLA_CHUNK
    work = [(h, n) for h in range(GLA_HEADS) for n in range(n_chunks)]
    rows = lambda n: slice(n * GLA_CHUNK, (n + 1) * GLA_CHUNK)
    kcols = lambda h: slice(h * GLA_DK, (h + 1) * GLA_DK)
    vcols = lambda h: slice(h * GLA_DV, (h + 1) * GLA_DV)

    scores = {}
    updates = {}
    for h, n in work:
        scores[h, n] = lax.dot_general(qd_ref[rows(n), kcols(h)], ki_ref[rows(n), kcols(h)],
                                       NT_DIMS, preferred_element_type=F32)
        updates[h, n] = lax.dot_general(v_ref[rows(n), vcols(h)], ki_ref[rows(n), kcols(h)],
                                        TN_DIMS, preferred_element_type=F32)
    states = {}
    for h in range(GLA_HEADS):
        st = st_ref[h]
        for n in range(n_chunks):
            states[h, n] = st.astype(BF16)
            st = (st + updates[h, n]) * dec_ref[n][:, kcols(h)]
        st_ref[h] = st
    for h, n in work:
        s = jnp.where(causal, scores[h, n], 0.0).astype(BF16)
        o = _dot(s, v_ref[rows(n), vcols(h)]) + lax.dot_general(
            qd_ref[rows(n), kcols(h)], states[h, n], NT_DIMS, preferred_element_type=F32)
        y = o * _rms_scale(o) * ng_ref[...]
        o_ref[rows(n), vcols(h)] = (y * gate_ref[rows(n), vcols(h)].astype(F32)).astype(BF16)


def _gla_layer_kernel(x_ref, shift_ref, scale_ref, g_ref, wqk_ref, wv_ref, wr_ref, wa_ref,
                      wa2_ref, ba_ref, br_ref, ng_ref, *rest, n_cast, tiles_per_seq):
    cast_in, rest = rest[:n_cast], rest[n_cast:]
    o_ref, cast_out = rest[0], rest[1:1 + n_cast]
    wbf_ref, st_ref, qd_ref, ki_ref, v_ref, gate_ref, dec_ref = rest[1 + n_cast:]
    for src, dst in zip(cast_in, cast_out):
        dst[...] = src[...].astype(BF16)

    @pl.when(pl.program_id(0) == 0)
    def _():
        for i, w_ref in enumerate((wqk_ref, wv_ref, wr_ref)):
            wbf_ref[i] = w_ref[...].T.astype(BF16)

    @pl.when(pl.program_id(0) % tiles_per_seq == 0)
    def _():
        st_ref[...] = jnp.zeros_like(st_ref)

    wqk_ref, wv_ref, wr_ref = wbf_ref.at[0], wbf_ref.at[1], wbf_ref.at[2]
    row = lax.broadcasted_iota(jnp.int32, (GLA_CHUNK, GLA_CHUNK), 0)
    col = lax.broadcasted_iota(jnp.int32, (GLA_CHUNK, GLA_CHUNK), 1)
    causal = row >= col
    tril = jnp.where(causal, 1.0, 0.0).astype(BF16)
    w_hi, w_lo = _split_bf16(wa2_ref[...])
    w_gate = jnp.concatenate([w_hi, w_hi, w_lo], axis=0)
    q_scale = GLA_DK ** -0.5
    n_chunks = x_ref.shape[0] // GLA_CHUNK
    chunk = lambda n: slice(n * GLA_CHUNK, (n + 1) * GLA_CHUNK)

    hb = _modulated_norm(x_ref[...], g_ref[...], scale_ref[...], shift_ref[...]).astype(BF16)
    a_hi, a_lo = _split_bf16(lax.dot_general(
        hb, wa_ref[...].astype(BF16), NT_DIMS, preferred_element_type=F32))
    r = _dot(hb, wr_ref[...]) + br_ref[...]
    gate_ref[...] = _silu(r).astype(BF16)
    la = _dot(jnp.concatenate([a_hi, a_lo, a_hi], axis=1), w_gate) + ba_ref[...]
    qk = _dot(hb, wqk_ref[...])
    log_a = (jnp.minimum(la, 0.0) - jnp.log(1.0 + jnp.exp(-jnp.abs(la)))) * (1.0 / GLA_TAU)
    l_hi, l_lo = _split_bf16(log_a)
    l_cat = jnp.concatenate([l_hi, l_lo], axis=1)
    cums = [_dot(tril, l_cat[chunk(n)]) for n in range(n_chunks)]
    v_ref[...] = _dot(hb, wv_ref[...]).astype(BF16)
    for n in range(n_chunks):
        b = cums[n][:, :GLA_HK] + cums[n][:, GLA_HK:]
        b_last = b[GLA_CHUNK - 1:GLA_CHUNK, :]
        qd_ref[chunk(n), :] = (qk[chunk(n), :GLA_HK] * q_scale * jnp.exp(b)).astype(BF16)
        ki_ref[chunk(n), :] = (qk[chunk(n), GLA_HK:] * jnp.exp(-b)).astype(BF16)
        dec_ref[n] = jnp.exp(b_last)

    work = [(h, n) for h in range(GLA_HEADS) for n in range(n_chunks)]
    kcols = lambda h: slice(h * GLA_DK, (h + 1) * GLA_DK)
    vcols = lambda h: slice(h * GLA_DV, (h + 1) * GLA_DV)
    scores = {}
    updates = {}
    for h, n in work:
        scores[h, n] = lax.dot_general(qd_ref[chunk(n), kcols(h)], ki_ref[chunk(n), kcols(h)],
                                       NT_DIMS, preferred_element_type=F32)
        updates[h, n] = lax.dot_general(v_ref[chunk(n), vcols(h)], ki_ref[chunk(n), kcols(h)],
                                        TN_DIMS, preferred_element_type=F32)
    states = {}
    for h in range(GLA_HEADS):
        st = st_ref[h]
        for n in range(n_chunks):
            states[h, n] = st.astype(BF16)
            st = (st + updates[h, n]) * dec_ref[n][:, kcols(h)]
        st_ref[h] = st
    for h, n in work:
        s = jnp.where(causal, scores[h, n], 0.0).astype(BF16)
        o = _dot(s, v_ref[chunk(n), vcols(h)]) + lax.dot_general(
            qd_ref[chunk(n), kcols(h)], states[h, n], NT_DIMS, preferred_element_type=F32)
        y = o * _rms_scale(o) * ng_ref[...]
        o_ref[chunk(n), vcols(h)] = (y * gate_ref[chunk(n), vcols(h)].astype(F32)).astype(BF16)


def _gla_layer(x2, mod3, g, w_in_t, wa2, ba, br, ng, cast_jobs):
    tm = TM_GLA
    n_steps = N_TOK // tm
    row = lambda m: (m, 0)
    assert 2 * GLA_HK == D_MODEL
    row_block = lambda idx: _resident_spec((D_MODEL, D_MODEL), (idx, 0))
    flats, cast_in, cast_out, cast_shapes = _cast_slabs(cast_jobs, n_steps, lambda m: m)
    out = pl.pallas_call(
        functools.partial(_gla_layer_kernel, n_cast=len(flats), tiles_per_seq=SEQ // tm),
        grid=(n_steps,),
        in_specs=[
            pl.BlockSpec((tm, D_MODEL), row),
            _mod_spec(0, tm),
            _mod_spec(1, tm),
            _full_spec((1, D_MODEL)),
            row_block(0),
            row_block(1),
            row_block(2),
            _resident_spec((GLA_GATE_RANK, D_MODEL), (3 * D_MODEL // GLA_GATE_RANK, 0)),
            _full_spec((GLA_GATE_RANK, GLA_HK)),
            _full_spec((1, GLA_HK)),
            _full_spec((1, D_MODEL)),
            _full_spec((1, GLA_DV)),
        ] + cast_in,
        out_specs=[pl.BlockSpec((tm, D_MODEL), row)] + cast_out,
        out_shape=[jax.ShapeDtypeStruct((N_TOK, D_MODEL), BF16)] + cast_shapes,
        scratch_shapes=[
            pltpu.VMEM((3, D_MODEL, D_MODEL), BF16),
            pltpu.VMEM((GLA_HEADS, GLA_DV, GLA_DK), F32),
            pltpu.VMEM((tm, GLA_HK), BF16),
            pltpu.VMEM((tm, GLA_HK), BF16),
            pltpu.VMEM((tm, D_MODEL), BF16),
            pltpu.VMEM((tm, D_MODEL), BF16),
            pltpu.VMEM((tm // GLA_CHUNK, 1, GLA_HK), F32),
        ],
        compiler_params=_params("arbitrary"),
        name="gla_layer",
    )(x2, mod3, mod3, g, w_in_t, w_in_t, w_in_t, w_in_t, wa2, ba, br, ng, *flats)
    return out[0], out[1:]


def _cast_slabs(jobs, n_steps, step_of):
    flats, in_specs, out_specs, out_shapes = [], [], [], []
    for w, layer in jobs:
        _, n_rows, cols = w.shape
        slab = n_rows // n_steps
        assert slab * n_steps == n_rows and slab % 16 == 0
        first = layer * n_steps
        flats.append(w.reshape(-1, cols))
        in_specs.append(pl.BlockSpec((slab, cols), lambda *g, first=first: (first + step_of(*g), 0)))
        out_specs.append(pl.BlockSpec((slab, cols), lambda *g: (step_of(*g), 0)))
        out_shapes.append(jax.ShapeDtypeStruct((n_rows, cols), BF16))
    return flats, in_specs, out_specs, out_shapes


def _gla_core(qd, ki, v, gate, dec, ng, cast_jobs):
    tc = GLA_TC
    steps_t = SEQ // tc
    qk_spec = pl.BlockSpec((None, tc, GLA_HK), lambda b, t: (b, t, 0))
    v_spec = pl.BlockSpec((None, tc, D_MODEL), lambda b, t: (b, t, 0))
    flats, cast_in, cast_out, cast_shapes = _cast_slabs(
        cast_jobs, BATCH * steps_t, lambda b, t: b * steps_t + t)
    out = pl.pallas_call(
        functools.partial(_gla_kernel, n_cast=len(flats)),
        grid=(BATCH, steps_t),
        in_specs=[
            qk_spec, qk_spec, v_spec, v_spec,
            pl.BlockSpec((None, tc // GLA_CHUNK, 1, GLA_HK), lambda b, t: (b, t, 0, 0)),
            _full_spec((1, GLA_DV)),
        ] + cast_in,
        out_specs=[v_spec] + cast_out,
        out_shape=[jax.ShapeDtypeStruct((BATCH, SEQ, D_MODEL), BF16)] + cast_shapes,
        scratch_shapes=[pltpu.VMEM((GLA_HEADS, GLA_DV, GLA_DK), F32)],
        compiler_params=_params("arbitrary", "arbitrary"),
        name="gla_core",
    )(qd.reshape(BATCH, SEQ, GLA_HK), ki.reshape(BATCH, SEQ, GLA_HK),
      v.reshape(BATCH, SEQ, D_MODEL),
      gate.reshape(BATCH, SEQ, D_MODEL),
      dec.reshape(BATCH, SEQ // GLA_CHUNK, 1, GLA_HK), ng, *flats)
    return out[0], out[1:]


def _rope_kernel(pos_ref, freq_ref, cs_ref):
    ang = freq_ref[...] * pos_ref[...]
    cs_ref[0:ROPE_HALF, :] = jnp.cos(ang)
    cs_ref[ROPE_HALF:, :] = jnp.sin(ang)


def _rope_tables(positions):
    inv_freq = ROPE_THETA ** (-jnp.arange(0, ROPE_DIM, 2, dtype=F32) / ROPE_DIM)
    return pl.pallas_call(
        _rope_kernel,
        grid=(1,),
        in_specs=[_full_spec((1, N_TOK)), _full_spec((ROPE_HALF, 1))],
        out_specs=_full_spec((ROPE_DIM, N_TOK)),
        out_shape=jax.ShapeDtypeStruct((ROPE_DIM, N_TOK), F32),
        compiler_params=_params("arbitrary"),
        name="rope_tables",
    )(positions.astype(F32).reshape(1, N_TOK), inv_freq.reshape(ROPE_HALF, 1))


def _rope_expander():
    e = np.zeros((3, ROPE_DIM, 2 * DIFF_DV), np.float32)
    for lane in range(DIFF_DV):
        d = lane % DIFF_DH
        if d < ROPE_DIM:
            e[:, d % ROPE_HALF, lane] = 1.0
            e[:, ROPE_HALF + d % ROPE_HALF, DIFF_DV + lane] = -1.0 if d < ROPE_HALF else 1.0
    return jnp.asarray(e.reshape(3 * ROPE_DIM, 2 * DIFF_DV), BF16)


def _inproj_diff_kernel(x_ref, shift_ref, scale_ref, g_ref, wq_ref, wk_ref, wv_ref,
                        cs_ref, e_ref, q_ref, k_ref, vt_ref, wvt_ref):
    @pl.when(pl.program_id(0) == 0)
    def _():
        wvt_ref[...] = wv_ref[...].T

    d = lax.broadcasted_iota(jnp.int32, (1, DIFF_DV), 1) & (DIFF_DH - 1)
    first_half = d < ROPE_HALF
    q_scale = DIFF_DH ** -0.5 * math.log2(math.e)
    group_rows = x_ref.shape[0] // INPROJ_ROW_GROUPS

    for grp in range(INPROJ_ROW_GROUPS):
        rows = slice(grp * group_rows, (grp + 1) * group_rows)
        hb = _modulated_norm(x_ref[rows, :], g_ref[...], scale_ref[...],
                             shift_ref[...]).astype(BF16)

        cs = cs_ref[:, rows]
        hi = cs.astype(BF16).astype(F32)
        mid = (cs - hi).astype(BF16).astype(F32)
        lo = cs - hi - mid
        parts = jnp.concatenate([hi, mid, lo], axis=0)
        tab = _dot(parts.T.astype(BF16), e_ref[...])
        ct = tab[:, :DIFF_DV] + jnp.where(d >= ROPE_DIM, 1.0, 0.0)
        s12 = tab[:, DIFF_DV:]

        for w_ref, out_ref, scl in ((wq_ref, q_ref, q_scale), (wk_ref, k_ref, 1.0)):
            z = _dot(hb, w_ref[...])
            for h in range(DIFF_HEADS):
                sl = slice(h * DIFF_DV, (h + 1) * DIFF_DV)
                xh = z[:, sl]
                partner = jnp.where(first_half, pltpu.roll(xh, DIFF_DV - ROPE_HALF, 1),
                                    pltpu.roll(xh, ROPE_HALF, 1))
                out_ref[rows, sl] = ((xh * ct + partner * s12) * scl).astype(BF16)
        vt = lax.dot_general(wvt_ref[...], hb, NT_DIMS, preferred_element_type=F32).astype(BF16)
        ones = jnp.ones((VT_ROWS - DIFF_DV, group_rows), BF16)
        for h in range(DIFF_HEADS):
            vt_ref[h * VT_ROWS:h * VT_ROWS + DIFF_DV, rows] = vt[h * DIFF_DV:(h + 1) * DIFF_DV]
            vt_ref[h * VT_ROWS + DIFF_DV:(h + 1) * VT_ROWS, rows] = ones


def _inproj_diff(x2, mod3, g, w_in, cs, expander):
    tm = TM_INPROJ
    row = lambda m: (m, 0)
    o_spec = pl.BlockSpec((tm, D_MODEL), row)
    return pl.pallas_call(
        _inproj_diff_kernel,
        grid=(N_TOK // tm,),
        in_specs=[
            pl.BlockSpec((tm, D_MODEL), row),
            _mod_spec(0, tm),
            _mod_spec(1, tm),
            _full_spec((1, D_MODEL)),
            _resident_spec((D_MODEL, D_MODEL), (0, 0)),
            _resident_spec((D_MODEL, D_MODEL), (0, 1)),
            _resident_spec((D_MODEL, D_MODEL), (0, 2)),
            pl.BlockSpec((ROPE_DIM, tm), lambda m: (0, m)),
            _full_spec((3 * ROPE_DIM, 2 * DIFF_DV)),
        ],
        out_specs=[o_spec, o_spec, pl.BlockSpec((DIFF_HEADS * VT_ROWS, tm), lambda m: (0, m))],
        out_shape=[jax.ShapeDtypeStruct((N_TOK, D_MODEL), BF16)] * 2
        + [jax.ShapeDtypeStruct((DIFF_HEADS * VT_ROWS, N_TOK), BF16)],
        scratch_shapes=[pltpu.VMEM((D_MODEL, D_MODEL), BF16)],
        compiler_params=_params("arbitrary"),
        name="inproj_diff",
    )(x2, mod3, mod3, g, w_in, w_in, w_in, cs, expander)


def _attn_kernel(q_ref, k_ref, vt_ref, lam_ref, g_ref, o_ref, qs_ref, s_ref, m_ref, acc_ref,
                 *, lambda_init):
    tq, tk = ATT_TQ, ATT_TK
    assert tq == tk
    lf = lam_ref[...]
    lam = (jnp.exp(jnp.sum(lf[0:1] * lf[1:2], axis=-1, keepdims=True))
           - jnp.exp(jnp.sum(lf[2:3] * lf[3:4], axis=-1, keepdims=True)) + lambda_init)
    feat = lax.broadcasted_iota(jnp.int32, (DIFF_DV, tq), 0)
    k_pos = lax.broadcasted_iota(jnp.int32, (tk, 2 * tq), 0)
    q_pos = lax.broadcasted_iota(jnp.int32, (tk, 2 * tq), 1) & (tq - 1)
    heads = [slice(h * DIFF_DV, (h + 1) * DIFF_DV) for h in range(ATT_HEADS)]

    def stage_queries(qi, slot):
        q0 = pl.multiple_of(qi * tq, tq)
        for h, hs in enumerate(heads):
            qt = q_ref[pl.ds(q0, tq), hs].astype(F32).T
            qs_ref[slot, h, :, 0:tq] = jnp.where(feat < DIFF_DH, qt, 0.0).astype(BF16)
            qs_ref[slot, h, :, tq:2 * tq] = jnp.where(feat >= DIFF_DH, qt, 0.0).astype(BF16)

    def scores_for(j, h, slot):
        start = pl.multiple_of(j * tk, tk)
        return _dot(k_ref[pl.ds(start, tk), heads[h]], qs_ref[slot, h])

    def reset_state():
        m_ref[...] = jnp.full_like(m_ref, MASKED)
        acc_ref[...] = jnp.zeros_like(acc_ref)

    def kv_step(j, masked, next_j, next_slot):
        start = pl.multiple_of(j * tk, tk)
        for h in range(ATT_HEADS):
            vtb = vt_ref[h * VT_ROWS:(h + 1) * VT_ROWS, pl.ds(start, tk)]
            s = s_ref[h]
            next_s = scores_for(next_j, h, next_slot)
            if masked:
                s = jnp.where(k_pos <= q_pos, s, MASKED)
            m_prev = m_ref[h]
            m_new = jnp.maximum(m_prev, jnp.max(s, axis=0, keepdims=True))
            p = jnp.exp2(s - m_new)
            alpha = jnp.exp2(m_prev - m_new)
            acc_ref[h] = alpha * acc_ref[h] + _dot(vtb, p.astype(BF16))
            m_ref[h] = m_new
            s_ref[h] = next_s

    n_q = SEQ // tq
    stage_queries(0, 0)
    reset_state()
    for h in range(ATT_HEADS):
        s_ref[h] = scores_for(0, h, 0)

    def q_block(qi, carry):
        q0 = pl.multiple_of(qi * tq, tq)
        slot = qi & 1

        def body(j, c):
            kv_step(j, False, j + 1, slot)
            return c

        lax.fori_loop(0, qi, body, 0)
        stage_queries(jnp.minimum(qi + 1, n_q - 1), 1 - slot)
        kv_step(qi, True, 0, 1 - slot)
        for h, hs in enumerate(heads):
            acc = acc_ref[h, 0:DIFF_DV, :]
            inv_l = 1.0 / acc_ref[h, DIFF_DV:DIFF_DV + 1, :]
            o = acc[:, :tq] * inv_l[:, :tq] - lam * (acc[:, tq:] * inv_l[:, tq:])
            rms = lax.rsqrt(jnp.mean(o * o, axis=0, keepdims=True) + NORM_EPS)
            y = o * rms * (g_ref[...] * (1.0 - lambda_init))
            o_ref[pl.ds(q0, tq), hs] = y.T.astype(BF16)
        reset_state()
        return carry

    lax.fori_loop(0, n_q, q_block, 0)


def _diff_attention(q, k, vt, lam_vecs, g, lambda_init):
    width = ATT_HEADS * DIFF_DV
    spec = pl.BlockSpec((None, SEQ, width), lambda b, h: (b, 0, h))
    return pl.pallas_call(
        functools.partial(_attn_kernel, lambda_init=lambda_init),
        grid=(BATCH, DIFF_HEADS // ATT_HEADS),
        in_specs=[spec, spec, pl.BlockSpec((ATT_HEADS * VT_ROWS, SEQ), lambda b, h: (h, b)),
                  _full_spec((4, DIFF_DH)), _full_spec((DIFF_DV, 1))],
        out_specs=spec,
        out_shape=jax.ShapeDtypeStruct((BATCH, SEQ, D_MODEL), BF16),
        scratch_shapes=[
            pltpu.VMEM((2, ATT_HEADS, DIFF_DV, 2 * ATT_TQ), BF16),
            pltpu.VMEM((ATT_HEADS, ATT_TK, 2 * ATT_TQ), F32),
            pltpu.VMEM((ATT_HEADS, 1, 2 * ATT_TQ), F32),
            pltpu.VMEM((ATT_HEADS, VT_ROWS, 2 * ATT_TQ), F32),
        ],
        compiler_params=_params("arbitrary", "arbitrary"),
        name="diff_attention",
    )(q.reshape(BATCH, SEQ, D_MODEL), k.reshape(BATCH, SEQ, D_MODEL), vt, lam_vecs, g)


def _mlp_kernel(x_ref, o_ref, wo_ref, gt_ref, sh_ref, sc_ref, gc_ref, g_ref, w1_ref, w2_ref,
                fg_ref, *rest, final, n_cast):
    cast_in, (out_ref, *cast_out) = rest[:n_cast], rest[n_cast:]
    for src, dst in zip(cast_in, cast_out):
        dst[...] = src[...].astype(BF16)

    x1 = x_ref[...] + gt_ref[...] * _dot(o_ref[...], wo_ref[...])
    hb = _modulated_norm(x1, g_ref[...], sc_ref[...], sh_ref[...]).astype(BF16)
    a = jnp.square(jnp.maximum(_dot(hb, w1_ref[...]), 0.0)).astype(BF16)
    out = x1 + gc_ref[...] * _dot(a, w2_ref[...])
    if final:
        out = out * _rms_scale(out) * fg_ref[...]
    out_ref[...] = out


def _resident_spec(block_shape, index):
    return pl.BlockSpec(block_shape, lambda m: index, pipeline_mode=pl.Buffered(1))


def _outproj_mlp(x2, o2, mod3, wo, g, w1, w2, fg, final, cast_jobs=()):
    tm = TM_MLP
    n_steps = N_TOK // tm
    row = lambda m: (m, 0)
    flats, cast_in, cast_out, cast_shapes = _cast_slabs(cast_jobs, n_steps, lambda m: m)
    out = pl.pallas_call(
        functools.partial(_mlp_kernel, final=final, n_cast=len(flats)),
        grid=(n_steps,),
        in_specs=[
            pl.BlockSpec((tm, D_MODEL), row),
            pl.BlockSpec((tm, D_MODEL), row),
            _resident_spec((D_MODEL, D_MODEL), (0, 0)),
            _mod_spec(2, tm),
            _mod_spec(3, tm),
            _mod_spec(4, tm),
            _mod_spec(5, tm),
            _full_spec((1, D_MODEL)),
            _resident_spec((D_MODEL, D_FF), (0, 0)),
            _resident_spec((D_FF, D_MODEL), (0, 0)),
            _full_spec((1, D_MODEL)),
        ] + cast_in,
        out_specs=[pl.BlockSpec((tm, D_MODEL), row)] + cast_out,
        out_shape=[jax.ShapeDtypeStruct((N_TOK, D_MODEL), F32)] + cast_shapes,
        compiler_params=_params("arbitrary"),
        name="outproj_mlp_final" if final else "outproj_mlp",
    )(x2, o2, wo, mod3, mod3, mod3, mod3, g, w1, w2, fg, *flats)
    return out[0], out[1:]


def kernel(x, c, positions, ada_w, ada_b, norm_g, mlp_w1, mlp_w2, gla_w_in, gla_w_a2, gla_b_a,
           gla_b_r, gla_norm_g, gla_w_o, diff_w_in, diff_lambda, diff_subln_g, diff_w_o, final_g):
    x2 = x.reshape(N_TOK, D_MODEL)
    mod = _adaln_mod(c, ada_w, ada_b)
    mod_table = lambda layer: mod[layer].reshape(BATCH * N_MOD, 1, D_MODEL)
    mod3 = mod_table(0)
    fg = final_g.reshape(1, D_MODEL)

    o, (w1, w2, wo) = _gla_layer(
        x2, mod3, norm_g[0, 0].reshape(1, D_MODEL), gla_w_in[0].T, gla_w_a2[0],
        gla_b_a[0].reshape(1, GLA_HK), gla_b_r[0].reshape(1, D_MODEL),
        gla_norm_g[0].reshape(1, GLA_DV), [(mlp_w1, 0), (mlp_w2, 0), (gla_w_o, 0)])
    x2, (w1, w2, diff_win, wo) = _outproj_mlp(
        x2, o.reshape(N_TOK, D_MODEL), mod3, wo, norm_g[0, 1].reshape(1, D_MODEL), w1, w2, fg,
        False, [(mlp_w1, 1), (mlp_w2, 1), (diff_w_in, 0), (diff_w_o, 0)])
    mod3 = mod_table(1)

    lambda_init = 0.8 - 0.6 * math.exp(-0.3 * 1)
    cs = _rope_tables(positions)
    q, k, vt = _inproj_diff(
        x2, mod3, norm_g[1, 0].reshape(1, D_MODEL), diff_win, cs, _rope_expander())
    o = _diff_attention(q, k, vt, diff_lambda[0], diff_subln_g[0].reshape(DIFF_DV, 1), lambda_init)
    x2, _ = _outproj_mlp(x2, o.reshape(N_TOK, D_MODEL), mod3, wo,
                         norm_g[1, 1].reshape(1, D_MODEL), w1, w2, fg, True)
    return x2.reshape(BATCH, SEQ, D_MODEL)
```

```python
import functools
import math

import jax
import jax.numpy as jnp
import numpy as np
from jax import lax
from jax.experimental import pallas as pl
from jax.experimental.pallas import tpu as pltpu

D_MODEL = 1024
BATCH = 8
SEQ = 2048
DEPTH = 2
D_FF = 4 * D_MODEL
NORM_EPS = 1e-6
GLA_HEADS = 4
GLA_DK = 128
GLA_DV = 256
GLA_HK = GLA_HEADS * GLA_DK
GLA_GATE_RANK = 16
GLA_TAU = 16.0
GLA_CHUNK = 64
DIFF_HEADS = 8
DIFF_DH = 64
DIFF_DV = 128
ROPE_THETA = 500000.0
ROPE_DIM = DIFF_DH // 4
ROPE_HALF = ROPE_DIM // 2
N_TOK = BATCH * SEQ
N_MOD = 6

TM_INPROJ = 1024
INPROJ_ROW_GROUPS = 2
TM_MLP = 512
TM_GLA = 512
TN_MOD = 1536
ATT_TQ = 256
ATT_TK = 256
VT_ROWS = DIFF_DV + 16
ATT_HEADS = 8
VMEM_LIMIT = 56 * 1024 * 1024

F32 = jnp.float32
BF16 = jnp.bfloat16
MASKED = -1e30
NT_DIMS = (((1,), (1,)), ((), ()))
TN_DIMS = (((0,), (0,)), ((), ()))


def _dot(a, b):
    return jnp.dot(a, b, preferred_element_type=F32)


def _split_bf16(a):
    hi = a.astype(BF16)
    lo = (a - hi.astype(F32)).astype(BF16)
    return hi, lo


def _rms_scale(x):
    return lax.rsqrt(jnp.mean(x * x, axis=-1, keepdims=True) + NORM_EPS)


def _silu(x):
    return x * (1.0 / (1.0 + jnp.exp(-x)))


def _params(*sem):
    return pltpu.CompilerParams(dimension_semantics=sem, vmem_limit_bytes=VMEM_LIMIT)


def _full_spec(shape):
    zeros = (0,) * len(shape)
    return pl.BlockSpec(shape, lambda *_: zeros)


def _resident_spec(block_shape, index):
    return pl.BlockSpec(block_shape, lambda m: index, pipeline_mode=pl.Buffered(1))


def _mod_spec(k, tm):
    tiles_per_seq = SEQ // tm
    return pl.BlockSpec((None, 1, D_MODEL), lambda m, *_: ((m // tiles_per_seq) * N_MOD + k, 0, 0))


def _modulated_norm(x, g, scale, shift):
    return (x * _rms_scale(x) * g) * (1.0 + scale) + shift


def _cast_slabs(jobs, n_steps, step_of):
    flats, in_specs, out_specs, out_shapes = [], [], [], []
    for w, layer in jobs:
        _, n_rows, cols = w.shape
        slab = n_rows // n_steps
        assert slab * n_steps == n_rows and slab % 16 == 0
        first = layer * n_steps
        flats.append(w.reshape(-1, cols))
        in_specs.append(pl.BlockSpec((slab, cols), lambda *g, first=first: (first + step_of(*g), 0)))
        out_specs.append(pl.BlockSpec((slab, cols), lambda *g: (step_of(*g), 0)))
        out_shapes.append(jax.ShapeDtypeStruct((n_rows, cols), BF16))
    return flats, in_specs, out_specs, out_shapes


def _mod_block(c_ref, w_ref, b_ref, o_ref):
    c_hi, c_lo = _split_bf16(_silu(c_ref[...]))
    w_hi, w_lo = _split_bf16(w_ref[...])
    lhs = jnp.concatenate([c_hi, c_lo], axis=0)
    r = _dot(lhs, w_hi)
    o_ref[...] = r[:BATCH] + r[BATCH:] + _dot(c_hi, w_lo) + b_ref[...]


def _adaln_mod(c, ada_w, ada_b):
    n_out = N_MOD * D_MODEL
    return pl.pallas_call(
        _mod_block,
        grid=(DEPTH, n_out // TN_MOD),
        in_specs=[
            _full_spec((BATCH, D_MODEL)),
            pl.BlockSpec((None, D_MODEL, TN_MOD), lambda i, j: (i, 0, j)),
            pl.BlockSpec((None, 1, TN_MOD), lambda i, j: (i, 0, j)),
        ],
        out_specs=pl.BlockSpec((None, BATCH, TN_MOD), lambda i, j: (i, 0, j)),
        out_shape=jax.ShapeDtypeStruct((DEPTH, BATCH, n_out), F32),
        compiler_params=_params("arbitrary", "arbitrary"),
        name="adaln_mod",
    )(c, ada_w, ada_b.reshape(DEPTH, 1, n_out))


def _gla_layer_kernel(x_ref, shift_ref, scale_ref, g_ref, wqk_ref, wv_ref, wr_ref, wa_ref,
                      wa2_ref, ba_ref, br_ref, ng_ref, *rest, n_cast, tiles_per_seq):
    cast_in, rest = rest[:n_cast], rest[n_cast:]
    o_ref, cast_out = rest[0], rest[1:1 + n_cast]
    wbf_ref, st_ref, qd_ref, ki_ref, v_ref, gate_ref, dec_ref = rest[1 + n_cast:]
    for src, dst in zip(cast_in, cast_out):
        dst[...] = src[...].astype(BF16)

    @pl.when(pl.program_id(0) == 0)
    def _():
        for i, w_ref in enumerate((wqk_ref, wv_ref, wr_ref)):
            wbf_ref[i] = w_ref[...].T.astype(BF16)

    @pl.when(pl.program_id(0) % tiles_per_seq == 0)
    def _():
        st_ref[...] = jnp.zeros_like(st_ref)

    wqk_ref, wv_ref, wr_ref = wbf_ref.at[0], wbf_ref.at[1], wbf_ref.at[2]
    row = lax.broadcasted_iota(jnp.int32, (GLA_CHUNK, GLA_CHUNK), 0)
    col = lax.broadcasted_iota(jnp.int32, (GLA_CHUNK, GLA_CHUNK), 1)
    causal = row >= col
    tril = jnp.where(causal, 1.0, 0.0).astype(BF16)
    w_hi, w_lo = _split_bf16(wa2_ref[...])
    w_gate = jnp.concatenate([w_hi, w_hi, w_lo], axis=0)
    q_scale = GLA_DK ** -0.5
    n_chunks = x_ref.shape[0] // GLA_CHUNK
    chunk = lambda n: slice(n * GLA_CHUNK, (n + 1) * GLA_CHUNK)

    hb = _modulated_norm(x_ref[...], g_ref[...], scale_ref[...], shift_ref[...]).astype(BF16)
    a_hi, a_lo = _split_bf16(lax.dot_general(
        hb, wa_ref[...].astype(BF16), NT_DIMS, preferred_element_type=F32))
    r = _dot(hb, wr_ref[...]) + br_ref[...]
    gate_ref[...] = _silu(r).astype(BF16)
    la = _dot(jnp.concatenate([a_hi, a_lo, a_hi], axis=1), w_gate) + ba_ref[...]
    qk = _dot(hb, wqk_ref[...])
    log_a = (jnp.minimum(la, 0.0) - jnp.log(1.0 + jnp.exp(-jnp.abs(la)))) * (1.0 / GLA_TAU)
    l_hi, l_lo = _split_bf16(log_a)
    l_cat = jnp.concatenate([l_hi, l_lo], axis=1)
    cums = [_dot(tril, l_cat[chunk(n)]) for n in range(n_chunks)]
    v_ref[...] = _dot(hb, wv_ref[...]).astype(BF16)
    for n in range(n_chunks):
        b = cums[n][:, :GLA_HK] + cums[n][:, GLA_HK:]
        b_last = b[GLA_CHUNK - 1:GLA_CHUNK, :]
        qd_ref[chunk(n), :] = (qk[chunk(n), :GLA_HK] * q_scale * jnp.exp(b)).astype(BF16)
        ki_ref[chunk(n), :] = (qk[chunk(n), GLA_HK:] * jnp.exp(-b)).astype(BF16)
        dec_ref[n] = jnp.exp(b_last)

    work = [(h, n) for h in range(GLA_HEADS) for n in range(n_chunks)]
    kcols = lambda h: slice(h * GLA_DK, (h + 1) * GLA_DK)
    vcols = lambda h: slice(h * GLA_DV, (h + 1) * GLA_DV)
    scores = {}
    updates = {}
    for h, n in work:
        scores[h, n] = lax.dot_general(qd_ref[chunk(n), kcols(h)], ki_ref[chunk(n), kcols(h)],
                                       NT_DIMS, preferred_element_type=F32)
        updates[h, n] = lax.dot_general(ki_ref[chunk(n), kcols(h)], v_ref[chunk(n), vcols(h)],
                                        TN_DIMS, preferred_element_type=F32)
    states = {}
    for h in range(GLA_HEADS):
        st = st_ref[h]
        for n in range(n_chunks):
            states[h, n] = st.astype(BF16)
            dec_col = jnp.broadcast_to(dec_ref[n][:, kcols(h)], (8, GLA_DK)).T[:, 0:1]
            st = (st + updates[h, n]) * dec_col
        st_ref[h] = st
    for h, n in work:
        s = jnp.where(causal, scores[h, n], 0.0).astype(BF16)
        o = _dot(s, v_ref[chunk(n), vcols(h)]) + _dot(qd_ref[chunk(n), kcols(h)], states[h, n])
        y = o * _rms_scale(o) * ng_ref[...]
        o_ref[chunk(n), vcols(h)] = (y * gate_ref[chunk(n), vcols(h)].astype(F32)).astype(BF16)


def _gla_layer(x2, mod3, g, w_in_t, wa2, ba, br, ng, cast_jobs):
    tm = TM_GLA
    n_steps = N_TOK // tm
    row = lambda m: (m, 0)
    assert 2 * GLA_HK == D_MODEL
    row_block = lambda idx: _resident_spec((D_MODEL, D_MODEL), (idx, 0))
    flats, cast_in, cast_out, cast_shapes = _cast_slabs(cast_jobs, n_steps, lambda m: m)
    out = pl.pallas_call(
        functools.partial(_gla_layer_kernel, n_cast=len(flats), tiles_per_seq=SEQ // tm),
        grid=(n_steps,),
        in_specs=[
            pl.BlockSpec((tm, D_MODEL), row),
            _mod_spec(0, tm),
            _mod_spec(1, tm),
            _full_spec((1, D_MODEL)),
            row_block(0),
            row_block(1),
            row_block(2),
            _resident_spec((GLA_GATE_RANK, D_MODEL), (3 * D_MODEL // GLA_GATE_RANK, 0)),
            _full_spec((GLA_GATE_RANK, GLA_HK)),
            _full_spec((1, GLA_HK)),
            _full_spec((1, D_MODEL)),
            _full_spec((1, GLA_DV)),
        ] + cast_in,
        out_specs=[pl.BlockSpec((tm, D_MODEL), row)] + cast_out,
        out_shape=[jax.ShapeDtypeStruct((N_TOK, D_MODEL), BF16)] + cast_shapes,
        scratch_shapes=[
            pltpu.VMEM((3, D_MODEL, D_MODEL), BF16),
            pltpu.VMEM((GLA_HEADS, GLA_DK, GLA_DV), F32),
            pltpu.VMEM((tm, GLA_HK), BF16),
            pltpu.VMEM((tm, GLA_HK), BF16),
            pltpu.VMEM((tm, D_MODEL), BF16),
            pltpu.VMEM((tm, D_MODEL), BF16),
            pltpu.VMEM((tm // GLA_CHUNK, 1, GLA_HK), F32),
        ],
        compiler_params=_params("arbitrary"),
        name="gla_layer",
    )(x2, mod3, mod3, g, w_in_t, w_in_t, w_in_t, w_in_t, wa2, ba, br, ng, *flats)
    return out[0], out[1:]


def _rope_kernel(pos_ref, freq_ref, cs_ref):
    ang = freq_ref[...] * pos_ref[...]
    cs_ref[0:ROPE_HALF, :] = jnp.cos(ang)
    cs_ref[ROPE_HALF:, :] = jnp.sin(ang)


def _rope_tables(positions):
    inv_freq = ROPE_THETA ** (-jnp.arange(0, ROPE_DIM, 2, dtype=F32) / ROPE_DIM)
    return pl.pallas_call(
        _rope_kernel,
        grid=(1,),
        in_specs=[_full_spec((1, N_TOK)), _full_spec((ROPE_HALF, 1))],
        out_specs=_full_spec((ROPE_DIM, N_TOK)),
        out_shape=jax.ShapeDtypeStruct((ROPE_DIM, N_TOK), F32),
        compiler_params=_params("arbitrary"),
        name="rope_tables",
    )(positions.astype(F32).reshape(1, N_TOK), inv_freq.reshape(ROPE_HALF, 1))


def _rope_expander():
    e = np.zeros((3, ROPE_DIM, 2 * DIFF_DV), np.float32)
    for lane in range(DIFF_DV):
        d = lane % DIFF_DH
        if d < ROPE_DIM:
            e[:, d % ROPE_HALF, lane] = 1.0
            e[:, ROPE_HALF + d % ROPE_HALF, DIFF_DV + lane] = -1.0 if d < ROPE_HALF else 1.0
    return jnp.asarray(e.reshape(3 * ROPE_DIM, 2 * DIFF_DV), BF16)


def _inproj_diff_kernel(x_ref, shift_ref, scale_ref, g_ref, wq_ref, wk_ref, wv_ref,
                        cs_ref, e_ref, q_ref, k_ref, vt_ref, wvt_ref):
    @pl.when(pl.program_id(0) == 0)
    def _():
        wvt_ref[...] = wv_ref[...].T

    d = lax.broadcasted_iota(jnp.int32, (1, DIFF_DV), 1) & (DIFF_DH - 1)
    first_half = d < ROPE_HALF
    q_scale = DIFF_DH ** -0.5 * math.log2(math.e)
    group_rows = x_ref.shape[0] // INPROJ_ROW_GROUPS

    for grp in range(INPROJ_ROW_GROUPS):
        rows = slice(grp * group_rows, (grp + 1) * group_rows)
        hb = _modulated_norm(x_ref[rows, :], g_ref[...], scale_ref[...],
                             shift_ref[...]).astype(BF16)

        cs = cs_ref[:, rows]
        hi = cs.astype(BF16).astype(F32)
        mid = (cs - hi).astype(BF16).astype(F32)
        lo = cs - hi - mid
        parts = jnp.concatenate([hi, mid, lo], axis=0)
        tab = _dot(parts.T.astype(BF16), e_ref[...])
        ct = tab[:, :DIFF_DV] + jnp.where(d >= ROPE_DIM, 1.0, 0.0)
        s12 = tab[:, DIFF_DV:]

        for w_ref, out_ref, scl in ((wq_ref, q_ref, q_scale), (wk_ref, k_ref, 1.0)):
            z = _dot(hb, w_ref[...])
            for h in range(DIFF_HEADS):
                sl = slice(h * DIFF_DV, (h + 1) * DIFF_DV)
                xh = z[:, sl]
                partner = jnp.where(first_half, pltpu.roll(xh, DIFF_DV - ROPE_HALF, 1),
                                    pltpu.roll(xh, ROPE_HALF, 1))
                out_ref[rows, sl] = ((xh * ct + partner * s12) * scl).astype(BF16)
        vt = lax.dot_general(wvt_ref[...], hb, NT_DIMS, preferred_element_type=F32).astype(BF16)
        ones = jnp.ones((VT_ROWS - DIFF_DV, group_rows), BF16)
        for h in range(DIFF_HEADS):
            vt_ref[h * VT_ROWS:h * VT_ROWS + DIFF_DV, rows] = vt[h * DIFF_DV:(h + 1) * DIFF_DV]
            vt_ref[h * VT_ROWS + DIFF_DV:(h + 1) * VT_ROWS, rows] = ones


def _inproj_diff(x2, mod3, g, w_in, cs, expander):
    tm = TM_INPROJ
    row = lambda m: (m, 0)
    o_spec = pl.BlockSpec((tm, D_MODEL), row)
    return pl.pallas_call(
        _inproj_diff_kernel,
        grid=(N_TOK // tm,),
        in_specs=[
            pl.BlockSpec((tm, D_MODEL), row),
            _mod_spec(0, tm),
            _mod_spec(1, tm),
            _full_spec((1, D_MODEL)),
            _resident_spec((D_MODEL, D_MODEL), (0, 0)),
            _resident_spec((D_MODEL, D_MODEL), (0, 1)),
            _resident_spec((D_MODEL, D_MODEL), (0, 2)),
            pl.BlockSpec((ROPE_DIM, tm), lambda m: (0, m)),
            _full_spec((3 * ROPE_DIM, 2 * DIFF_DV)),
        ],
        out_specs=[o_spec, o_spec, pl.BlockSpec((DIFF_HEADS * VT_ROWS, tm), lambda m: (0, m))],
        out_shape=[jax.ShapeDtypeStruct((N_TOK, D_MODEL), BF16)] * 2
        + [jax.ShapeDtypeStruct((DIFF_HEADS * VT_ROWS, N_TOK), BF16)],
        scratch_shapes=[pltpu.VMEM((D_MODEL, D_MODEL), BF16)],
        compiler_params=_params("arbitrary"),
        name="inproj_diff",
    )(x2, mod3, mod3, g, w_in, w_in, w_in, cs, expander)


def _attn_kernel(q_ref, k_ref, vt_ref, lam_ref, g_ref, o_ref, qs_ref, s_ref, m_ref, acc_ref,
                 *, lambda_init):
    tq, tk = ATT_TQ, ATT_TK
    assert tq == tk
    lf = lam_ref[...]
    lam = (jnp.exp(jnp.sum(lf[0:1] * lf[1:2], axis=-1, keepdims=True))
           - jnp.exp(jnp.sum(lf[2:3] * lf[3:4], axis=-1, keepdims=True)) + lambda_init)
    feat = lax.broadcasted_iota(jnp.int32, (DIFF_DV, tq), 0)
    k_pos = lax.broadcasted_iota(jnp.int32, (tk, 2 * tq), 0)
    q_pos = lax.broadcasted_iota(jnp.int32, (tk, 2 * tq), 1) & (tq - 1)
    heads = [slice(h * DIFF_DV, (h + 1) * DIFF_DV) for h in range(ATT_HEADS)]

    def stage_queries(qi, slot):
        q0 = pl.multiple_of(qi * tq, tq)
        for h, hs in enumerate(heads):
            qt = q_ref[pl.ds(q0, tq), hs].astype(F32).T
            qs_ref[slot, h, :, 0:tq] = jnp.where(feat < DIFF_DH, qt, 0.0).astype(BF16)
            qs_ref[slot, h, :, tq:2 * tq] = jnp.where(feat >= DIFF_DH, qt, 0.0).astype(BF16)

    def scores_for(j, h, slot):
        start = pl.multiple_of(j * tk, tk)
        return _dot(k_ref[pl.ds(start, tk), heads[h]], qs_ref[slot, h])

    def reset_state():
        m_ref[...] = jnp.full_like(m_ref, MASKED)
        acc_ref[...] = jnp.zeros_like(acc_ref)

    def kv_step(j, masked, next_j, next_slot):
        start = pl.multiple_of(j * tk, tk)
        for h in range(ATT_HEADS):
            vtb = vt_ref[h * VT_ROWS:(h + 1) * VT_ROWS, pl.ds(start, tk)]
            s = s_ref[h]
            next_s = scores_for(next_j, h, next_slot)
            if masked:
                s = jnp.where(k_pos <= q_pos, s, MASKED)
            m_prev = m_ref[h]
            m_new = jnp.maximum(m_prev, jnp.max(s, axis=0, keepdims=True))
            p = jnp.exp2(s - m_new)
            alpha = jnp.exp2(m_prev - m_new)
            acc_ref[h] = alpha * acc_ref[h] + _dot(vtb, p.astype(BF16))
            m_ref[h] = m_new
            s_ref[h] = next_s

    n_q = SEQ // tq
    stage_queries(0, 0)
    reset_state()
    for h in range(ATT_HEADS):
        s_ref[h] = scores_for(0, h, 0)

    def q_block(qi, carry):
        q0 = pl.multiple_of(qi * tq, tq)
        slot = qi & 1

        def body(j, c):
            kv_step(j, False, j + 1, slot)
            return c

        lax.fori_loop(0, qi, body, 0)
        stage_queries(jnp.minimum(qi + 1, n_q - 1), 1 - slot)
        kv_step(qi, True, 0, 1 - slot)
        for h, hs in enumerate(heads):
            acc = acc_ref[h, 0:DIFF_DV, :]
            inv_l = 1.0 / acc_ref[h, DIFF_DV:DIFF_DV + 1, :]
            o = acc[:, :tq] * inv_l[:, :tq] - lam * (acc[:, tq:] * inv_l[:, tq:])
            rms = lax.rsqrt(jnp.mean(o * o, axis=0, keepdims=True) + NORM_EPS)
            y = o * rms * (g_ref[...] * (1.0 - lambda_init))
            o_ref[pl.ds(q0, tq), hs] = y.T.astype(BF16)
        reset_state()
        return carry

    lax.fori_loop(0, n_q, q_block, 0)


def _diff_attention(q, k, vt, lam_vecs, g, lambda_init):
    width = ATT_HEADS * DIFF_DV
    spec = pl.BlockSpec((None, SEQ, width), lambda b, h: (b, 0, h))
    return pl.pallas_call(
        functools.partial(_attn_kernel, lambda_init=lambda_init),
        grid=(BATCH, DIFF_HEADS // ATT_HEADS),
        in_specs=[spec, spec, pl.BlockSpec((ATT_HEADS * VT_ROWS, SEQ), lambda b, h: (h, b)),
                  _full_spec((4, DIFF_DH)), _full_spec((DIFF_DV, 1))],
        out_specs=spec,
        out_shape=jax.ShapeDtypeStruct((BATCH, SEQ, D_MODEL), BF16),
        scratch_shapes=[
            pltpu.VMEM((2, ATT_HEADS, DIFF_DV, 2 * ATT_TQ), BF16),
            pltpu.VMEM((ATT_HEADS, ATT_TK, 2 * ATT_TQ), F32),
            pltpu.VMEM((ATT_HEADS, 1, 2 * ATT_TQ), F32),
            pltpu.VMEM((ATT_HEADS, VT_ROWS, 2 * ATT_TQ), F32),
        ],
        compiler_params=_params("arbitrary", "arbitrary"),
        name="diff_attention",
    )(q.reshape(BATCH, SEQ, D_MODEL), k.reshape(BATCH, SEQ, D_MODEL), vt, lam_vecs, g)


def _mlp_kernel(x_ref, o_ref, wo_ref, gt_ref, sh_ref, sc_ref, gc_ref, g_ref, w1_ref, w2_ref,
                fg_ref, *rest, final, n_cast):
    cast_in, (out_ref, *cast_out) = rest[:n_cast], rest[n_cast:]
    for src, dst in zip(cast_in, cast_out):
        dst[...] = src[...].astype(BF16)

    x1 = x_ref[...] + gt_ref[...] * _dot(o_ref[...], wo_ref[...])
    hb = _modulated_norm(x1, g_ref[...], sc_ref[...], sh_ref[...]).astype(BF16)
    a = jnp.square(jnp.maximum(_dot(hb, w1_ref[...]), 0.0)).astype(BF16)
    out = x1 + gc_ref[...] * _dot(a, w2_ref[...])
    if final:
        out = out * _rms_scale(out) * fg_ref[...]
    out_ref[...] = out


def _outproj_mlp(x2, o2, mod3, wo, g, w1, w2, fg, final, cast_jobs=()):
    tm = TM_MLP
    n_steps = N_TOK // tm
    row = lambda m: (m, 0)
    flats, cast_in, cast_out, cast_shapes = _cast_slabs(cast_jobs, n_steps, lambda m: m)
    out = pl.pallas_call(
        functools.partial(_mlp_kernel, final=final, n_cast=len(flats)),
        grid=(n_steps,),
        in_specs=[
            pl.BlockSpec((tm, D_MODEL), row),
            pl.BlockSpec((tm, D_MODEL), row),
            _resident_spec((D_MODEL, D_MODEL), (0, 0)),
            _mod_spec(2, tm),
            _mod_spec(3, tm),
            _mod_spec(4, tm),
            _mod_spec(5, tm),
            _full_spec((1, D_MODEL)),
            _resident_spec((D_MODEL, D_FF), (0, 0)),
            _resident_spec((D_FF, D_MODEL), (0, 0)),
            _full_spec((1, D_MODEL)),
        ] + cast_in,
        out_specs=[pl.BlockSpec((tm, D_MODEL), row)] + cast_out,
        out_shape=[jax.ShapeDtypeStruct((N_TOK, D_MODEL), F32)] + cast_shapes,
        compiler_params=_params("arbitrary"),
        name="outproj_mlp_final" if final else "outproj_mlp",
    )(x2, o2, wo, mod3, mod3, mod3, mod3, g, w1, w2, fg, *flats)
    return out[0], out[1:]


def kernel(x, c, positions, ada_w, ada_b, norm_g, mlp_w1, mlp_w2, gla_w_in, gla_w_a2, gla_b_a,
           gla_b_r, gla_norm_g, gla_w_o, diff_w_in, diff_lambda, diff_subln_g, diff_w_o, final_g):
    x2 = x.reshape(N_TOK, D_MODEL)
    mod = _adaln_mod(c, ada_w, ada_b)
    mod_table = lambda layer: mod[layer].reshape(BATCH * N_MOD, 1, D_MODEL)
    mod3 = mod_table(0)
    fg = final_g.reshape(1, D_MODEL)

    o, (w1, w2, wo) = _gla_layer(
        x2, mod3, norm_g[0, 0].reshape(1, D_MODEL), gla_w_in[0].T, gla_w_a2[0],
        gla_b_a[0].reshape(1, GLA_HK), gla_b_r[0].reshape(1, D_MODEL),
        gla_norm_g[0].reshape(1, GLA_DV), [(mlp_w1, 0), (mlp_w2, 0), (gla_w_o, 0)])
    x2, (w1, w2, diff_win, wo) = _outproj_mlp(
        x2, o, mod3, wo, norm_g[0, 1].reshape(1, D_MODEL), w1, w2, fg,
        False, [(mlp_w1, 1), (mlp_w2, 1), (diff_w_in, 0), (diff_w_o, 0)])
    mod3 = mod_table(1)

    lambda_init = 0.8 - 0.6 * math.exp(-0.3 * 1)
    cs = _rope_tables(positions)
    q, k, vt = _inproj_diff(
        x2, mod3, norm_g[1, 0].reshape(1, D_MODEL), diff_win, cs, _rope_expander())
    o = _diff_attention(q, k, vt, diff_lambda[0], diff_subln_g[0].reshape(DIFF_DV, 1), lambda_init)
    x2, _ = _outproj_mlp(x2, o.reshape(N_TOK, D_MODEL), mod3, wo,
                         norm_g[1, 1].reshape(1, D_MODEL), w1, w2, fg, True)
    return x2.reshape(BATCH, SEQ, D_MODEL)
```

```python
import functools
import math

import jax
import jax.numpy as jnp
import numpy as np
from jax import lax
from jax.experimental import pallas as pl
from jax.experimental.pallas import tpu as pltpu

D_MODEL = 1024
BATCH = 8
SEQ = 2048
DEPTH = 2
D_FF = 4 * D_MODEL
NORM_EPS = 1e-6
GLA_HEADS = 4
GLA_DK = 128
GLA_DV = 256
GLA_HK = GLA_HEADS * GLA_DK
GLA_GATE_RANK = 16
GLA_TAU = 16.0
GLA_CHUNK = 64
DIFF_HEADS = 8
DIFF_DH = 64
DIFF_DV = 128
ROPE_THETA = 500000.0
ROPE_DIM = DIFF_DH // 4
ROPE_HALF = ROPE_DIM // 2
N_TOK = BATCH * SEQ
N_MOD = 6

TM_INPROJ = 1024
INPROJ_ROW_GROUPS = 2
TM_MLP = 512
TM_GLA = 512
TN_MOD = 1536
ATT_TQ = 256
ATT_TK = 256
VT_ROWS = DIFF_DV + 16
ATT_HEADS = 8
VMEM_LIMIT = 56 * 1024 * 1024

F32 = jnp.float32
BF16 = jnp.bfloat16
MASKED = -1e30
NT_DIMS = (((1,), (1,)), ((), ()))
TN_DIMS = (((0,), (0,)), ((), ()))


def _dot(a, b):
    return jnp.dot(a, b, preferred_element_type=F32)


def _split_bf16(a):
    hi = a.astype(BF16)
    lo = (a - hi.astype(F32)).astype(BF16)
    return hi, lo


def _rms_scale(x):
    return lax.rsqrt(jnp.mean(x * x, axis=-1, keepdims=True) + NORM_EPS)


def _silu(x):
    return x * (1.0 / (1.0 + jnp.exp(-x)))


def _params(*sem):
    return pltpu.CompilerParams(dimension_semantics=sem, vmem_limit_bytes=VMEM_LIMIT)


def _full_spec(shape):
    zeros = (0,) * len(shape)
    return pl.BlockSpec(shape, lambda *_: zeros)


def _resident_spec(block_shape, index):
    return pl.BlockSpec(block_shape, lambda m: index, pipeline_mode=pl.Buffered(1))


def _mod_spec(k, tm):
    tiles_per_seq = SEQ // tm
    return pl.BlockSpec((None, 1, D_MODEL), lambda m, *_: ((m // tiles_per_seq) * N_MOD + k, 0, 0))


def _modulated_norm(x, g, scale, shift):
    return (x * _rms_scale(x) * g) * (1.0 + scale) + shift


def _cast_slabs(jobs, n_steps, step_of):
    flats, in_specs, out_specs, out_shapes = [], [], [], []
    for w, layer in jobs:
        _, n_rows, cols = w.shape
        slab = n_rows // n_steps
        assert slab * n_steps == n_rows and slab % 16 == 0
        first = layer * n_steps
        flats.append(w.reshape(-1, cols))
        in_specs.append(pl.BlockSpec((slab, cols), lambda *g, first=first: (first + step_of(*g), 0)))
        out_specs.append(pl.BlockSpec((slab, cols), lambda *g: (step_of(*g), 0)))
        out_shapes.append(jax.ShapeDtypeStruct((n_rows, cols), BF16))
    return flats, in_specs, out_specs, out_shapes


def _mod_block(c_ref, w_ref, b_ref, o_ref):
    c_hi, c_lo = _split_bf16(_silu(c_ref[...]))
    w_hi, w_lo = _split_bf16(w_ref[...])
    lhs = jnp.concatenate([c_hi, c_lo], axis=0)
    r = _dot(lhs, w_hi)
    o_ref[...] = r[:BATCH] + r[BATCH:] + _dot(c_hi, w_lo) + b_ref[...]


def _adaln_mod(c, ada_w, ada_b):
    n_out = N_MOD * D_MODEL
    return pl.pallas_call(
        _mod_block,
        grid=(DEPTH, n_out // TN_MOD),
        in_specs=[
            _full_spec((BATCH, D_MODEL)),
            pl.BlockSpec((None, D_MODEL, TN_MOD), lambda i, j: (i, 0, j)),
            pl.BlockSpec((None, 1, TN_MOD), lambda i, j: (i, 0, j)),
        ],
        out_specs=pl.BlockSpec((None, BATCH, TN_MOD), lambda i, j: (i, 0, j)),
        out_shape=jax.ShapeDtypeStruct((DEPTH, BATCH, n_out), F32),
        compiler_params=_params("arbitrary", "arbitrary"),
        name="adaln_mod",
    )(c, ada_w, ada_b.reshape(DEPTH, 1, n_out))


def _gla_layer_kernel(x_ref, shift_ref, scale_ref, g_ref, wqk_ref, wv_ref, wr_ref, wa_ref,
                      wa2_ref, ba_ref, br_ref, ng_ref, *rest, n_cast, tiles_per_seq):
    cast_in, rest = rest[:n_cast], rest[n_cast:]
    o_ref, cast_out = rest[0], rest[1:1 + n_cast]
    wbf_ref, st_ref, qd_ref, ki_ref, v_ref, gate_ref, dec_ref = rest[1 + n_cast:]
    for src, dst in zip(cast_in, cast_out):
        dst[...] = src[...].astype(BF16)

    @pl.when(pl.program_id(0) == 0)
    def _():
        for i, w_ref in enumerate((wqk_ref, wv_ref, wr_ref)):
            wbf_ref[i] = w_ref[...].T.astype(BF16)

    @pl.when(pl.program_id(0) % tiles_per_seq == 0)
    def _():
        st_ref[...] = jnp.zeros_like(st_ref)

    wqk_ref, wv_ref, wr_ref = wbf_ref.at[0], wbf_ref.at[1], wbf_ref.at[2]
    row = lax.broadcasted_iota(jnp.int32, (GLA_CHUNK, GLA_CHUNK), 0)
    col = lax.broadcasted_iota(jnp.int32, (GLA_CHUNK, GLA_CHUNK), 1)
    causal = row >= col
    tril = jnp.where(causal, 1.0, 0.0).astype(BF16)
    w_hi, w_lo = _split_bf16(wa2_ref[...])
    w_gate = jnp.concatenate([w_hi, w_hi, w_lo], axis=0)
    q_scale = GLA_DK ** -0.5
    n_chunks = x_ref.shape[0] // GLA_CHUNK
    chunk = lambda n: slice(n * GLA_CHUNK, (n + 1) * GLA_CHUNK)

    hb = _modulated_norm(x_ref[...], g_ref[...], scale_ref[...], shift_ref[...]).astype(BF16)
    a_hi, a_lo = _split_bf16(lax.dot_general(
        hb, wa_ref[...].astype(BF16), NT_DIMS, preferred_element_type=F32))
    r = _dot(hb, wr_ref[...]) + br_ref[...]
    gate_ref[...] = _silu(r).astype(BF16)
    la = _dot(jnp.concatenate([a_hi, a_lo, a_hi], axis=1), w_gate) + ba_ref[...]
    qk = _dot(hb, wqk_ref[...])
    log_a = (jnp.minimum(la, 0.0) - jnp.log(1.0 + jnp.exp(-jnp.abs(la)))) * (1.0 / GLA_TAU)
    l_hi = log_a.astype(BF16)
    cums = [_dot(tril, l_hi[chunk(n)]) for n in range(n_chunks)]
    v_ref[...] = _dot(hb, wv_ref[...]).astype(BF16)
    for n in range(n_chunks):
        b = cums[n]
        b_last = b[GLA_CHUNK - 1:GLA_CHUNK, :]
        qd_ref[chunk(n), :] = (qk[chunk(n), :GLA_HK] * q_scale * jnp.exp(b)).astype(BF16)
        ki_ref[chunk(n), :] = (qk[chunk(n), GLA_HK:] * jnp.exp(-b)).astype(BF16)
        dec_ref[n] = jnp.exp(b_last)

    work = [(h, n) for h in range(GLA_HEADS) for n in range(n_chunks)]
    kcols = lambda h: slice(h * GLA_DK, (h + 1) * GLA_DK)
    vcols = lambda h: slice(h * GLA_DV, (h + 1) * GLA_DV)
    scores = {}
    updates = {}
    for h, n in work:
        scores[h, n] = lax.dot_general(qd_ref[chunk(n), kcols(h)], ki_ref[chunk(n), kcols(h)],
                                       NT_DIMS, preferred_element_type=F32)
        updates[h, n] = lax.dot_general(ki_ref[chunk(n), kcols(h)], v_ref[chunk(n), vcols(h)],
                                        TN_DIMS, preferred_element_type=F32)
    states = {}
    for h in range(GLA_HEADS):
        st = st_ref[h]
        for n in range(n_chunks):
            states[h, n] = st.astype(BF16)
            dec_col = jnp.broadcast_to(dec_ref[n][:, kcols(h)], (8, GLA_DK)).T[:, 0:1]
            st = (st + updates[h, n]) * dec_col
        st_ref[h] = st
    for h, n in work:
        s = jnp.where(causal, scores[h, n], 0.0).astype(BF16)
        o = _dot(s, v_ref[chunk(n), vcols(h)]) + _dot(qd_ref[chunk(n), kcols(h)], states[h, n])
        y = o * _rms_scale(o) * ng_ref[...]
        o_ref[chunk(n), vcols(h)] = (y * gate_ref[chunk(n), vcols(h)].astype(F32)).astype(BF16)


def _gla_layer(x2, mod3, g, w_in_t, wa2, ba, br, ng, cast_jobs):
    tm = TM_GLA
    n_steps = N_TOK // tm
    row = lambda m: (m, 0)
    assert 2 * GLA_HK == D_MODEL
    row_block = lambda idx: _resident_spec((D_MODEL, D_MODEL), (idx, 0))
    flats, cast_in, cast_out, cast_shapes = _cast_slabs(cast_jobs, n_steps, lambda m: m)
    out = pl.pallas_call(
        functools.partial(_gla_layer_kernel, n_cast=len(flats), tiles_per_seq=SEQ // tm),
        grid=(n_steps,),
        in_specs=[
            pl.BlockSpec((tm, D_MODEL), row),
            _mod_spec(0, tm),
            _mod_spec(1, tm),
            _full_spec((1, D_MODEL)),
            row_block(0),
            row_block(1),
            row_block(2),
            _resident_spec((GLA_GATE_RANK, D_MODEL), (3 * D_MODEL // GLA_GATE_RANK, 0)),
            _full_spec((GLA_GATE_RANK, GLA_HK)),
            _full_spec((1, GLA_HK)),
            _full_spec((1, D_MODEL)),
            _full_spec((1, GLA_DV)),
        ] + cast_in,
        out_specs=[pl.BlockSpec((tm, D_MODEL), row)] + cast_out,
        out_shape=[jax.ShapeDtypeStruct((N_TOK, D_MODEL), BF16)] + cast_shapes,
        scratch_shapes=[
            pltpu.VMEM((3, D_MODEL, D_MODEL), BF16),
            pltpu.VMEM((GLA_HEADS, GLA_DK, GLA_DV), F32),
            pltpu.VMEM((tm, GLA_HK), BF16),
            pltpu.VMEM((tm, GLA_HK), BF16),
            pltpu.VMEM((tm, D_MODEL), BF16),
            pltpu.VMEM((tm, D_MODEL), BF16),
            pltpu.VMEM((tm // GLA_CHUNK, 1, GLA_HK), F32),
        ],
        compiler_params=_params("arbitrary"),
        name="gla_layer",
    )(x2, mod3, mod3, g, w_in_t, w_in_t, w_in_t, w_in_t, wa2, ba, br, ng, *flats)
    return out[0], out[1:]


def _rope_kernel(pos_ref, freq_ref, cs_ref):
    ang = freq_ref[...] * pos_ref[...]
    cs_ref[0:ROPE_HALF, :] = jnp.cos(ang)
    cs_ref[ROPE_HALF:, :] = jnp.sin(ang)


def _rope_tables(positions):
    inv_freq = ROPE_THETA ** (-jnp.arange(0, ROPE_DIM, 2, dtype=F32) / ROPE_DIM)
    return pl.pallas_call(
        _rope_kernel,
        grid=(1,),
        in_specs=[_full_spec((1, N_TOK)), _full_spec((ROPE_HALF, 1))],
        out_specs=_full_spec((ROPE_DIM, N_TOK)),
        out_shape=jax.ShapeDtypeStruct((ROPE_DIM, N_TOK), F32),
        compiler_params=_params("arbitrary"),
        name="rope_tables",
    )(positions.astype(F32).reshape(1, N_TOK), inv_freq.reshape(ROPE_HALF, 1))


def _rope_expander():
    e = np.zeros((3, ROPE_DIM, 2 * DIFF_DV), np.float32)
    for lane in range(DIFF_DV):
        d = lane % DIFF_DH
        if d < ROPE_DIM:
            e[:, d % ROPE_HALF, lane] = 1.0
            e[:, ROPE_HALF + d % ROPE_HALF, DIFF_DV + lane] = -1.0 if d < ROPE_HALF else 1.0
    return jnp.asarray(e.reshape(3 * ROPE_DIM, 2 * DIFF_DV), BF16)


def _inproj_diff_kernel(x_ref, shift_ref, scale_ref, g_ref, wq_ref, wk_ref, wv_ref,
                        cs_ref, e_ref, q_ref, k_ref, vt_ref, wvt_ref):
    @pl.when(pl.program_id(0) == 0)
    def _():
        wvt_ref[...] = wv_ref[...].T

    d = lax.broadcasted_iota(jnp.int32, (1, DIFF_DV), 1) & (DIFF_DH - 1)
    first_half = d < ROPE_HALF
    q_scale = DIFF_DH ** -0.5 * math.log2(math.e)
    group_rows = x_ref.shape[0] // INPROJ_ROW_GROUPS

    for grp in range(INPROJ_ROW_GROUPS):
        rows = slice(grp * group_rows, (grp + 1) * group_rows)
        hb = _modulated_norm(x_ref[rows, :], g_ref[...], scale_ref[...],
                             shift_ref[...]).astype(BF16)

        cs = cs_ref[:, rows]
        hi = cs.astype(BF16).astype(F32)
        mid = (cs - hi).astype(BF16).astype(F32)
        lo = cs - hi - mid
        parts = jnp.concatenate([hi, mid, lo], axis=0)
        tab = _dot(parts.T.astype(BF16), e_ref[...])
        ct = tab[:, :DIFF_DV] + jnp.where(d >= ROPE_DIM, 1.0, 0.0)
        s12 = tab[:, DIFF_DV:]

        for w_ref, out_ref, scl in ((wq_ref, q_ref, q_scale), (wk_ref, k_ref, 1.0)):
            z = _dot(hb, w_ref[...])
            for h in range(DIFF_HEADS):
                sl = slice(h * DIFF_DV, (h + 1) * DIFF_DV)
                xh = z[:, sl]
                partner = jnp.where(first_half, pltpu.roll(xh, DIFF_DV - ROPE_HALF, 1),
                                    pltpu.roll(xh, ROPE_HALF, 1))
                out_ref[rows, sl] = ((xh * ct + partner * s12) * scl).astype(BF16)
        vt = lax.dot_general(wvt_ref[...], hb, NT_DIMS, preferred_element_type=F32).astype(BF16)
        ones = jnp.ones((VT_ROWS - DIFF_DV, group_rows), BF16)
        for h in range(DIFF_HEADS):
            vt_ref[h * VT_ROWS:h * VT_ROWS + DIFF_DV, rows] = vt[h * DIFF_DV:(h + 1) * DIFF_DV]
            vt_ref[h * VT_ROWS + DIFF_DV:(h + 1) * VT_ROWS, rows] = ones


def _inproj_diff(x2, mod3, g, w_in, cs, expander):
    tm = TM_INPROJ
    row = lambda m: (m, 0)
    o_spec = pl.BlockSpec((tm, D_MODEL), row)
    return pl.pallas_call(
        _inproj_diff_kernel,
        grid=(N_TOK // tm,),
        in_specs=[
            pl.BlockSpec((tm, D_MODEL), row),
            _mod_spec(0, tm),
            _mod_spec(1, tm),
            _full_spec((1, D_MODEL)),
            _resident_spec((D_MODEL, D_MODEL), (0, 0)),
            _resident_spec((D_MODEL, D_MODEL), (0, 1)),
            _resident_spec((D_MODEL, D_MODEL), (0, 2)),
            pl.BlockSpec((ROPE_DIM, tm), lambda m: (0, m)),
            _full_spec((3 * ROPE_DIM, 2 * DIFF_DV)),
        ],
        out_specs=[o_spec, o_spec, pl.BlockSpec((DIFF_HEADS * VT_ROWS, tm), lambda m: (0, m))],
        out_shape=[jax.ShapeDtypeStruct((N_TOK, D_MODEL), BF16)] * 2
        + [jax.ShapeDtypeStruct((DIFF_HEADS * VT_ROWS, N_TOK), BF16)],
        scratch_shapes=[pltpu.VMEM((D_MODEL, D_MODEL), BF16)],
        compiler_params=_params("arbitrary"),
        name="inproj_diff",
    )(x2, mod3, mod3, g, w_in, w_in, w_in, cs, expander)


def _attn_kernel(q_ref, k_ref, vt_ref, lam_ref, g_ref, o_ref, qs_ref, s_ref, m_ref, acc_ref,
                 *, lambda_init):
    tq, tk = ATT_TQ, ATT_TK
    assert tq == tk
    lf = lam_ref[...]
    lam = (jnp.exp(jnp.sum(lf[0:1] * lf[1:2], axis=-1, keepdims=True))
           - jnp.exp(jnp.sum(lf[2:3] * lf[3:4], axis=-1, keepdims=True)) + lambda_init)
    feat = lax.broadcasted_iota(jnp.int32, (DIFF_DV, tq), 0)
    k_pos = lax.broadcasted_iota(jnp.int32, (tk, 2 * tq), 0)
    q_pos = lax.broadcasted_iota(jnp.int32, (tk, 2 * tq), 1) & (tq - 1)
    heads = [slice(h * DIFF_DV, (h + 1) * DIFF_DV) for h in range(ATT_HEADS)]

    def stage_queries(qi, slot):
        q0 = pl.multiple_of(qi * tq, tq)
        for h, hs in enumerate(heads):
            qt = q_ref[pl.ds(q0, tq), hs].astype(F32).T
            qs_ref[slot, h, :, 0:tq] = jnp.where(feat < DIFF_DH, qt, 0.0).astype(BF16)
            qs_ref[slot, h, :, tq:2 * tq] = jnp.where(feat >= DIFF_DH, qt, 0.0).astype(BF16)

    def scores_for(j, h, slot):
        start = pl.multiple_of(j * tk, tk)
        return _dot(k_ref[pl.ds(start, tk), heads[h]], qs_ref[slot, h])

    def reset_state():
        m_ref[...] = jnp.full_like(m_ref, MASKED)

    def kv_step(j, masked, next_j, next_slot):
        start = pl.multiple_of(j * tk, tk)
        for h in range(ATT_HEADS):
            vtb = vt_ref[h * VT_ROWS:(h + 1) * VT_ROWS, pl.ds(start, tk)]
            s = s_ref[h]
            next_s = scores_for(next_j, h, next_slot)
            if masked:
                s = jnp.where(k_pos <= q_pos, s, MASKED)
            m_prev = m_ref[h]
            m_new = jnp.maximum(m_prev, jnp.max(s, axis=0, keepdims=True))
            p = jnp.exp2(s - m_new)
            alpha = jnp.exp2(m_prev - m_new)
            acc_ref[h] = alpha * acc_ref[h] + _dot(vtb, p.astype(BF16))
            m_ref[h] = m_new
            s_ref[h] = next_s

    n_q = SEQ // tq
    stage_queries(0, 0)
    acc_ref[...] = jnp.zeros_like(acc_ref)
    reset_state()
    for h in range(ATT_HEADS):
        s_ref[h] = scores_for(0, h, 0)

    def q_block(qi, carry):
        q0 = pl.multiple_of(qi * tq, tq)
        slot = qi & 1

        def body(j, c):
            kv_step(j, False, j + 1, slot)
            return c

        lax.fori_loop(0, qi, body, 0)
        stage_queries(jnp.minimum(qi + 1, n_q - 1), 1 - slot)
        kv_step(qi, True, 0, 1 - slot)
        for h, hs in enumerate(heads):
            acc = acc_ref[h, 0:DIFF_DV, :]
            inv_l = 1.0 / acc_ref[h, DIFF_DV:DIFF_DV + 1, :]
            o = acc[:, :tq] * inv_l[:, :tq] - lam * (acc[:, tq:] * inv_l[:, tq:])
            rms = lax.rsqrt(jnp.mean(o * o, axis=0, keepdims=True) + NORM_EPS)
            y = o * rms * (g_ref[...] * (1.0 - lambda_init))
            o_ref[pl.ds(q0, tq), hs] = y.T.astype(BF16)
        reset_state()
        return carry

    lax.fori_loop(0, n_q, q_block, 0)


def _diff_attention(q, k, vt, lam_vecs, g, lambda_init):
    width = ATT_HEADS * DIFF_DV
    spec = pl.BlockSpec((None, SEQ, width), lambda b, h: (b, 0, h))
    return pl.pallas_call(
        functools.partial(_attn_kernel, lambda_init=lambda_init),
        grid=(BATCH, DIFF_HEADS // ATT_HEADS),
        in_specs=[spec, spec, pl.BlockSpec((ATT_HEADS * VT_ROWS, SEQ), lambda b, h: (h, b)),
                  _full_spec((4, DIFF_DH)), _full_spec((DIFF_DV, 1))],
        out_specs=spec,
        out_shape=jax.ShapeDtypeStruct((BATCH, SEQ, D_MODEL), BF16),
        scratch_shapes=[
            pltpu.VMEM((2, ATT_HEADS, DIFF_DV, 2 * ATT_TQ), BF16),
            pltpu.VMEM((ATT_HEADS, ATT_TK, 2 * ATT_TQ), F32),
            pltpu.VMEM((ATT_HEADS, 1, 2 * ATT_TQ), F32),
            pltpu.VMEM((ATT_HEADS, VT_ROWS, 2 * ATT_TQ), F32),
        ],
        compiler_params=_params("arbitrary", "arbitrary"),
        name="diff_attention",
    )(q.reshape(BATCH, SEQ, D_MODEL), k.reshape(BATCH, SEQ, D_MODEL), vt, lam_vecs, g)


def _mlp_kernel(x_ref, o_ref, wo_ref, gt_ref, sh_ref, sc_ref, gc_ref, g_ref, w1_ref, w2_ref,
                fg_ref, *rest, final, n_cast):
    cast_in, (out_ref, *cast_out) = rest[:n_cast], rest[n_cast:]
    for src, dst in zip(cast_in, cast_out):
        dst[...] = src[...].astype(BF16)

    x1 = x_ref[...] + gt_ref[...] * _dot(o_ref[...], wo_ref[...])
    hb = _modulated_norm(x1, g_ref[...], sc_ref[...], sh_ref[...]).astype(BF16)
    a = jnp.square(jnp.maximum(_dot(hb, w1_ref[...]), 0.0)).astype(BF16)
    out = x1 + gc_ref[...] * _dot(a, w2_ref[...])
    if final:
        out = out * _rms_scale(out) * fg_ref[...]
    out_ref[...] = out


def _outproj_mlp(x2, o2, mod3, wo, g, w1, w2, fg, final, cast_jobs=()):
    tm = TM_MLP
    n_steps = N_TOK // tm
    row = lambda m: (m, 0)
    flats, cast_in, cast_out, cast_shapes = _cast_slabs(cast_jobs, n_steps, lambda m: m)
    out = pl.pallas_call(
        functools.partial(_mlp_kernel, final=final, n_cast=len(flats)),
        grid=(n_steps,),
        in_specs=[
            pl.BlockSpec((tm, D_MODEL), row),
            pl.BlockSpec((tm, D_MODEL), row),
            _resident_spec((D_MODEL, D_MODEL), (0, 0)),
            _mod_spec(2, tm),
            _mod_spec(3, tm),
            _mod_spec(4, tm),
            _mod_spec(5, tm),
            _full_spec((1, D_MODEL)),
            _resident_spec((D_MODEL, D_FF), (0, 0)),
            _resident_spec((D_FF, D_MODEL), (0, 0)),
            _full_spec((1, D_MODEL)),
        ] + cast_in,
        out_specs=[pl.BlockSpec((tm, D_MODEL), row)] + cast_out,
        out_shape=[jax.ShapeDtypeStruct((N_TOK, D_MODEL), F32)] + cast_shapes,
        compiler_params=_params("arbitrary"),
        name="outproj_mlp_final" if final else "outproj_mlp",
    )(x2, o2, wo, mod3, mod3, mod3, mod3, g, w1, w2, fg, *flats)
    return out[0], out[1:]


def kernel(x, c, positions, ada_w, ada_b, norm_g, mlp_w1, mlp_w2, gla_w_in, gla_w_a2, gla_b_a,
           gla_b_r, gla_norm_g, gla_w_o, diff_w_in, diff_lambda, diff_subln_g, diff_w_o, final_g):
    x2 = x.reshape(N_TOK, D_MODEL)
    mod = _adaln_mod(c, ada_w, ada_b)
    mod_table = lambda layer: mod[layer].reshape(BATCH * N_MOD, 1, D_MODEL)
    mod3 = mod_table(0)
    fg = final_g.reshape(1, D_MODEL)

    o, (w1, w2, wo) = _gla_layer(
        x2, mod3, norm_g[0, 0].reshape(1, D_MODEL), gla_w_in[0].T, gla_w_a2[0],
        gla_b_a[0].reshape(1, GLA_HK), gla_b_r[0].reshape(1, D_MODEL),
        gla_norm_g[0].reshape(1, GLA_DV), [(mlp_w1, 0), (mlp_w2, 0), (gla_w_o, 0)])
    x2, (w1, w2, diff_win, wo) = _outproj_mlp(
        x2, o, mod3, wo, norm_g[0, 1].reshape(1, D_MODEL), w1, w2, fg,
        False, [(mlp_w1, 1), (mlp_w2, 1), (diff_w_in, 0), (diff_w_o, 0)])
    mod3 = mod_table(1)

    lambda_init = 0.8 - 0.6 * math.exp(-0.3 * 1)
    cs = _rope_tables(positions)
    q, k, vt = _inproj_diff(
        x2, mod3, norm_g[1, 0].reshape(1, D_MODEL), diff_win, cs, _rope_expander())
    o = _diff_attention(q, k, vt, diff_lambda[0], diff_subln_g[0].reshape(DIFF_DV, 1), lambda_init)
    x2, _ = _outproj_mlp(x2, o.reshape(N_TOK, D_MODEL), mod3, wo,
                         norm_g[1, 1].reshape(1, D_MODEL), w1, w2, fg, True)
    return x2.reshape(BATCH, SEQ, D_MODEL)
```

```python
import functools
import math

import jax
import jax.numpy as jnp
import numpy as np
from jax import lax
from jax.experimental import pallas as pl
from jax.experimental.pallas import tpu as pltpu

D_MODEL = 1024
BATCH = 8
SEQ = 2048
DEPTH = 2
D_FF = 4 * D_MODEL
NORM_EPS = 1e-6
GLA_HEADS = 4
GLA_DK = 128
GLA_DV = 256
GLA_HK = GLA_HEADS * GLA_DK
GLA_GATE_RANK = 16
GLA_TAU = 16.0
GLA_CHUNK = 64
DIFF_HEADS = 8
DIFF_DH = 64
DIFF_DV = 128
ROPE_THETA = 500000.0
ROPE_DIM = DIFF_DH // 4
ROPE_HALF = ROPE_DIM // 2
N_TOK = BATCH * SEQ
N_MOD = 6

TM_INPROJ = 1024
INPROJ_ROW_GROUPS = 2
TM_MLP = 512
TM_GLA = 1024
TN_MOD = 768
ATT_TQ = 256
ATT_TK = 256
VT_ROWS = DIFF_DV + 16
ATT_HEADS = 8
VMEM_LIMIT = 56 * 1024 * 1024

F32 = jnp.float32
BF16 = jnp.bfloat16
MASKED = -1e30
NT_DIMS = (((1,), (1,)), ((), ()))
TN_DIMS = (((0,), (0,)), ((), ()))


def _dot(a, b):
    return jnp.dot(a, b, preferred_element_type=F32)


def _split_bf16(a):
    hi = a.astype(BF16)
    lo = (a - hi.astype(F32)).astype(BF16)
    return hi, lo


def _rms_scale(x):
    return lax.rsqrt(jnp.mean(x * x, axis=-1, keepdims=True) + NORM_EPS)


def _silu(x):
    return x * (1.0 / (1.0 + jnp.exp(-x)))


def _params(*sem):
    return pltpu.CompilerParams(dimension_semantics=sem, vmem_limit_bytes=VMEM_LIMIT)


def _full_spec(shape):
    zeros = (0,) * len(shape)
    return pl.BlockSpec(shape, lambda *_: zeros)


def _resident_spec(block_shape, index):
    return pl.BlockSpec(block_shape, lambda m: index, pipeline_mode=pl.Buffered(1))


def _mod_spec(k, tm):
    tiles_per_seq = SEQ // tm
    return pl.BlockSpec((None, 1, D_MODEL), lambda m, *_: ((m // tiles_per_seq) * N_MOD + k, 0, 0))


def _modulated_norm(x, g, scale, shift):
    return (x * _rms_scale(x) * g) * (1.0 + scale) + shift


def _cast_slabs(jobs, n_steps, step_of):
    flats, in_specs, out_specs, out_shapes = [], [], [], []
    for w, layer in jobs:
        _, n_rows, cols = w.shape
        slab = n_rows // n_steps
        assert slab * n_steps == n_rows and slab % 16 == 0
        first = layer * n_steps
        flats.append(w.reshape(-1, cols))
        in_specs.append(pl.BlockSpec((slab, cols), lambda *g, first=first: (first + step_of(*g), 0)))
        out_specs.append(pl.BlockSpec((slab, cols), lambda *g: (step_of(*g), 0)))
        out_shapes.append(jax.ShapeDtypeStruct((n_rows, cols), BF16))
    return flats, in_specs, out_specs, out_shapes


def _mod_block(c_ref, w_ref, b_ref, o_ref):
    c_hi, c_lo = _split_bf16(_silu(c_ref[...]))
    w_hi, w_lo = _split_bf16(w_ref[...])
    lhs = jnp.concatenate([c_hi, c_lo], axis=0)
    r = _dot(lhs, w_hi)
    o_ref[...] = r[:BATCH] + r[BATCH:] + _dot(c_hi, w_lo) + b_ref[...]


def _adaln_mod(c, ada_w, ada_b):
    n_out = N_MOD * D_MODEL
    return pl.pallas_call(
        _mod_block,
        grid=(DEPTH, n_out // TN_MOD),
        in_specs=[
            _full_spec((BATCH, D_MODEL)),
            pl.BlockSpec((None, D_MODEL, TN_MOD), lambda i, j: (i, 0, j)),
            pl.BlockSpec((None, 1, TN_MOD), lambda i, j: (i, 0, j)),
        ],
        out_specs=pl.BlockSpec((None, BATCH, TN_MOD), lambda i, j: (i, 0, j)),
        out_shape=jax.ShapeDtypeStruct((DEPTH, BATCH, n_out), F32),
        compiler_params=_params("arbitrary", "arbitrary"),
        name="adaln_mod",
    )(c, ada_w, ada_b.reshape(DEPTH, 1, n_out))


def _gla_layer_kernel(x_ref, shift_ref, scale_ref, g_ref, wqk_ref, wv_ref, wr_ref, wa_ref,
                      wa2_ref, ba_ref, br_ref, ng_ref, *rest, n_cast, tiles_per_seq):
    cast_in, rest = rest[:n_cast], rest[n_cast:]
    o_ref, cast_out = rest[0], rest[1:1 + n_cast]
    wbf_ref, st_ref, qd_ref, ki_ref, v_ref, gate_ref, dec_ref = rest[1 + n_cast:]
    for src, dst in zip(cast_in, cast_out):
        dst[...] = src[...].astype(BF16)

    @pl.when(pl.program_id(0) == 0)
    def _():
        for i, w_ref in enumerate((wqk_ref, wv_ref, wr_ref)):
            wbf_ref[i] = w_ref[...].T.astype(BF16)

    @pl.when(pl.program_id(0) % tiles_per_seq == 0)
    def _():
        st_ref[...] = jnp.zeros_like(st_ref)

    wqk_ref, wv_ref, wr_ref = wbf_ref.at[0], wbf_ref.at[1], wbf_ref.at[2]
    row = lax.broadcasted_iota(jnp.int32, (GLA_CHUNK, GLA_CHUNK), 0)
    col = lax.broadcasted_iota(jnp.int32, (GLA_CHUNK, GLA_CHUNK), 1)
    causal = row >= col
    tril = jnp.where(causal, 1.0, 0.0).astype(BF16)
    w_hi, w_lo = _split_bf16(wa2_ref[...])
    w_gate = jnp.concatenate([w_hi, w_hi, w_lo], axis=0)
    q_scale = GLA_DK ** -0.5
    n_chunks = x_ref.shape[0] // GLA_CHUNK
    chunk = lambda n: slice(n * GLA_CHUNK, (n + 1) * GLA_CHUNK)

    hb = _modulated_norm(x_ref[...], g_ref[...], scale_ref[...], shift_ref[...]).astype(BF16)
    a_hi, a_lo = _split_bf16(lax.dot_general(
        hb, wa_ref[...].astype(BF16), NT_DIMS, preferred_element_type=F32))
    r = _dot(hb, wr_ref[...]) + br_ref[...]
    gate_ref[...] = _silu(r).astype(BF16)
    la = _dot(jnp.concatenate([a_hi, a_lo, a_hi], axis=1), w_gate) + ba_ref[...]
    qk = _dot(hb, wqk_ref[...])
    log_a = (jnp.minimum(la, 0.0) - jnp.log(1.0 + jnp.exp(-jnp.abs(la)))) * (1.0 / GLA_TAU)
    l_hi = log_a.astype(BF16)
    cums = [_dot(tril, l_hi[chunk(n)]) for n in range(n_chunks)]
    v_ref[...] = _dot(hb, wv_ref[...]).astype(BF16)
    for n in range(n_chunks):
        b = cums[n]
        b_last = b[GLA_CHUNK - 1:GLA_CHUNK, :]
        qd_ref[chunk(n), :] = (qk[chunk(n), :GLA_HK] * q_scale * jnp.exp(b)).astype(BF16)
        ki_ref[chunk(n), :] = (qk[chunk(n), GLA_HK:] * jnp.exp(-b)).astype(BF16)
        dec_ref[n] = jnp.exp(b_last)

    work = [(h, n) for h in range(GLA_HEADS) for n in range(n_chunks)]
    kcols = lambda h: slice(h * GLA_DK, (h + 1) * GLA_DK)
    vcols = lambda h: slice(h * GLA_DV, (h + 1) * GLA_DV)
    scores = {}
    updates = {}
    for h, n in work:
        scores[h, n] = lax.dot_general(qd_ref[chunk(n), kcols(h)], ki_ref[chunk(n), kcols(h)],
                                       NT_DIMS, preferred_element_type=F32)
        updates[h, n] = lax.dot_general(ki_ref[chunk(n), kcols(h)], v_ref[chunk(n), vcols(h)],
                                        TN_DIMS, preferred_element_type=F32)
    states = {}
    for h in range(GLA_HEADS):
        st = st_ref[h]
        for n in range(n_chunks):
            states[h, n] = st.astype(BF16)
            dec_col = jnp.broadcast_to(dec_ref[n][:, kcols(h)], (8, GLA_DK)).T[:, 0:1]
            st = (st + updates[h, n]) * dec_col
        st_ref[h] = st
    for h, n in work:
        s = jnp.where(causal, scores[h, n], 0.0).astype(BF16)
        o = _dot(s, v_ref[chunk(n), vcols(h)]) + _dot(qd_ref[chunk(n), kcols(h)], states[h, n])
        y = o * _rms_scale(o) * ng_ref[...]
        o_ref[chunk(n), vcols(h)] = (y * gate_ref[chunk(n), vcols(h)].astype(F32)).astype(BF16)


def _gla_layer(x2, mod3, g, w_in_t, wa2, ba, br, ng, cast_jobs):
    tm = TM_GLA
    n_steps = N_TOK // tm
    row = lambda m: (m, 0)
    assert 2 * GLA_HK == D_MODEL
    row_block = lambda idx: _resident_spec((D_MODEL, D_MODEL), (idx, 0))
    flats, cast_in, cast_out, cast_shapes = _cast_slabs(cast_jobs, n_steps, lambda m: m)
    out = pl.pallas_call(
        functools.partial(_gla_layer_kernel, n_cast=len(flats), tiles_per_seq=SEQ // tm),
        grid=(n_steps,),
        in_specs=[
            pl.BlockSpec((tm, D_MODEL), row),
            _mod_spec(0, tm),
            _mod_spec(1, tm),
            _full_spec((1, D_MODEL)),
            row_block(0),
            row_block(1),
            row_block(2),
            _resident_spec((GLA_GATE_RANK, D_MODEL), (3 * D_MODEL // GLA_GATE_RANK, 0)),
            _full_spec((GLA_GATE_RANK, GLA_HK)),
            _full_spec((1, GLA_HK)),
            _full_spec((1, D_MODEL)),
            _full_spec((1, GLA_DV)),
        ] + cast_in,
        out_specs=[pl.BlockSpec((tm, D_MODEL), row)] + cast_out,
        out_shape=[jax.ShapeDtypeStruct((N_TOK, D_MODEL), BF16)] + cast_shapes,
        scratch_shapes=[
            pltpu.VMEM((3, D_MODEL, D_MODEL), BF16),
            pltpu.VMEM((GLA_HEADS, GLA_DK, GLA_DV), F32),
            pltpu.VMEM((tm, GLA_HK), BF16),
            pltpu.VMEM((tm, GLA_HK), BF16),
            pltpu.VMEM((tm, D_MODEL), BF16),
            pltpu.VMEM((tm, D_MODEL), BF16),
            pltpu.VMEM((tm // GLA_CHUNK, 1, GLA_HK), F32),
        ],
        compiler_params=_params("arbitrary"),
        name="gla_layer",
    )(x2, mod3, mod3, g, w_in_t, w_in_t, w_in_t, w_in_t, wa2, ba, br, ng, *flats)
    return out[0], out[1:]


def _rope_kernel(pos_ref, freq_ref, cs_ref):
    ang = freq_ref[...] * pos_ref[...]
    cs_ref[0:ROPE_HALF, :] = jnp.cos(ang)
    cs_ref[ROPE_HALF:, :] = jnp.sin(ang)


def _rope_tables(positions):
    inv_freq = ROPE_THETA ** (-jnp.arange(0, ROPE_DIM, 2, dtype=F32) / ROPE_DIM)
    return pl.pallas_call(
        _rope_kernel,
        grid=(1,),
        in_specs=[_full_spec((1, N_TOK)), _full_spec((ROPE_HALF, 1))],
        out_specs=_full_spec((ROPE_DIM, N_TOK)),
        out_shape=jax.ShapeDtypeStruct((ROPE_DIM, N_TOK), F32),
        compiler_params=_params("arbitrary"),
        name="rope_tables",
    )(positions.astype(F32).reshape(1, N_TOK), inv_freq.reshape(ROPE_HALF, 1))


def _rope_expander():
    e = np.zeros((3, ROPE_DIM, 2 * DIFF_DV), np.float32)
    for lane in range(DIFF_DV):
        d = lane % DIFF_DH
        if d < ROPE_DIM:
            e[:, d % ROPE_HALF, lane] = 1.0
            e[:, ROPE_HALF + d % ROPE_HALF, DIFF_DV + lane] = -1.0 if d < ROPE_HALF else 1.0
    return jnp.asarray(e.reshape(3 * ROPE_DIM, 2 * DIFF_DV), BF16)


def _inproj_diff_kernel(x_ref, shift_ref, scale_ref, g_ref, wq_ref, wk_ref, wv_ref,
                        cs_ref, e_ref, q_ref, k_ref, vt_ref, wvt_ref):
    @pl.when(pl.program_id(0) == 0)
    def _():
        wvt_ref[...] = wv_ref[...].T

    d = lax.broadcasted_iota(jnp.int32, (1, DIFF_DV), 1) & (DIFF_DH - 1)
    first_half = d < ROPE_HALF
    q_scale = DIFF_DH ** -0.5 * math.log2(math.e)
    group_rows = x_ref.shape[0] // INPROJ_ROW_GROUPS

    for grp in range(INPROJ_ROW_GROUPS):
        rows = slice(grp * group_rows, (grp + 1) * group_rows)
        hb = _modulated_norm(x_ref[rows, :], g_ref[...], scale_ref[...],
                             shift_ref[...]).astype(BF16)

        cs = cs_ref[:, rows]
        hi = cs.astype(BF16).astype(F32)
        mid = (cs - hi).astype(BF16).astype(F32)
        lo = cs - hi - mid
        parts = jnp.concatenate([hi, mid, lo], axis=0)
        tab = _dot(parts.T.astype(BF16), e_ref[...])
        ct = tab[:, :DIFF_DV] + jnp.where(d >= ROPE_DIM, 1.0, 0.0)
        s12 = tab[:, DIFF_DV:]

        for w_ref, out_ref, scl in ((wq_ref, q_ref, q_scale), (wk_ref, k_ref, 1.0)):
            z = _dot(hb, w_ref[...])
            for h in range(DIFF_HEADS):
                sl = slice(h * DIFF_DV, (h + 1) * DIFF_DV)
                xh = z[:, sl]
                partner = jnp.where(first_half, pltpu.roll(xh, DIFF_DV - ROPE_HALF, 1),
                                    pltpu.roll(xh, ROPE_HALF, 1))
                out_ref[rows, sl] = ((xh * ct + partner * s12) * scl).astype(BF16)
        vt = lax.dot_general(wvt_ref[...], hb, NT_DIMS, preferred_element_type=F32).astype(BF16)
        ones = jnp.ones((VT_ROWS - DIFF_DV, group_rows), BF16)
        for h in range(DIFF_HEADS):
            vt_ref[h * VT_ROWS:h * VT_ROWS + DIFF_DV, rows] = vt[h * DIFF_DV:(h + 1) * DIFF_DV]
            vt_ref[h * VT_ROWS + DIFF_DV:(h + 1) * VT_ROWS, rows] = ones


def _inproj_diff(x2, mod3, g, w_in, cs, expander):
    tm = TM_INPROJ
    row = lambda m: (m, 0)
    o_spec = pl.BlockSpec((tm, D_MODEL), row)
    return pl.pallas_call(
        _inproj_diff_kernel,
        grid=(N_TOK // tm,),
        in_specs=[
            pl.BlockSpec((tm, D_MODEL), row),
            _mod_spec(0, tm),
            _mod_spec(1, tm),
            _full_spec((1, D_MODEL)),
            _resident_spec((D_MODEL, D_MODEL), (0, 0)),
            _resident_spec((D_MODEL, D_MODEL), (0, 1)),
            _resident_spec((D_MODEL, D_MODEL), (0, 2)),
            pl.BlockSpec((ROPE_DIM, tm), lambda m: (0, m)),
            _full_spec((3 * ROPE_DIM, 2 * DIFF_DV)),
        ],
        out_specs=[o_spec, o_spec, pl.BlockSpec((DIFF_HEADS * VT_ROWS, tm), lambda m: (0, m))],
        out_shape=[jax.ShapeDtypeStruct((N_TOK, D_MODEL), BF16)] * 2
        + [jax.ShapeDtypeStruct((DIFF_HEADS * VT_ROWS, N_TOK), BF16)],
        scratch_shapes=[pltpu.VMEM((D_MODEL, D_MODEL), BF16)],
        compiler_params=_params("arbitrary"),
        name="inproj_diff",
    )(x2, mod3, mod3, g, w_in, w_in, w_in, cs, expander)


def _attn_kernel(q_ref, k_ref, vt_ref, lam_ref, g_ref, o_ref, qs_ref, s_ref, m_ref, acc_ref,
                 *, lambda_init):
    tq, tk = ATT_TQ, ATT_TK
    assert tq == tk
    lf = lam_ref[...]
    lam = (jnp.exp(jnp.sum(lf[0:1] * lf[1:2], axis=-1, keepdims=True))
           - jnp.exp(jnp.sum(lf[2:3] * lf[3:4], axis=-1, keepdims=True)) + lambda_init)
    feat = lax.broadcasted_iota(jnp.int32, (DIFF_DV, tq), 0)
    k_pos = lax.broadcasted_iota(jnp.int32, (tk, 2 * tq), 0)
    q_pos = lax.broadcasted_iota(jnp.int32, (tk, 2 * tq), 1) & (tq - 1)
    heads = [slice(h * DIFF_DV, (h + 1) * DIFF_DV) for h in range(ATT_HEADS)]

    def stage_queries(qi, slot):
        q0 = pl.multiple_of(qi * tq, tq)
        for h, hs in enumerate(heads):
            qt = q_ref[pl.ds(q0, tq), hs].astype(F32).T
            qs_ref[slot, h, :, 0:tq] = jnp.where(feat < DIFF_DH, qt, 0.0).astype(BF16)
            qs_ref[slot, h, :, tq:2 * tq] = jnp.where(feat >= DIFF_DH, qt, 0.0).astype(BF16)

    def scores_for(j, h, slot):
        start = pl.multiple_of(j * tk, tk)
        return _dot(k_ref[pl.ds(start, tk), heads[h]], qs_ref[slot, h])

    def reset_state():
        m_ref[...] = jnp.full_like(m_ref, MASKED)

    def kv_step(j, masked, next_j, next_slot):
        start = pl.multiple_of(j * tk, tk)
        for h in range(ATT_HEADS):
            vtb = vt_ref[h * VT_ROWS:(h + 1) * VT_ROWS, pl.ds(start, tk)]
            s = s_ref[h]
            next_s = scores_for(next_j, h, next_slot)
            if masked:
                s = jnp.where(k_pos <= q_pos, s, MASKED)
            m_prev = m_ref[h]
            m_new = jnp.maximum(m_prev, jnp.max(s, axis=0, keepdims=True))
            p = jnp.exp2(s - m_new)
            alpha = jnp.exp2(m_prev - m_new)
            acc_ref[h] = alpha * acc_ref[h] + _dot(vtb, p.astype(BF16))
            m_ref[h] = m_new
            s_ref[h] = next_s

    n_q = SEQ // tq
    stage_queries(0, 0)
    acc_ref[...] = jnp.zeros_like(acc_ref)
    reset_state()
    for h in range(ATT_HEADS):
        s_ref[h] = scores_for(0, h, 0)

    def q_block(qi, carry):
        q0 = pl.multiple_of(qi * tq, tq)
        slot = qi & 1

        def body(j, c):
            kv_step(j, False, j + 1, slot)
            return c

        lax.fori_loop(0, qi, body, 0)
        stage_queries(jnp.minimum(qi + 1, n_q - 1), 1 - slot)
        kv_step(qi, True, 0, 1 - slot)
        for h, hs in enumerate(heads):
            acc = acc_ref[h, 0:DIFF_DV, :]
            inv_l = 1.0 / acc_ref[h, DIFF_DV:DIFF_DV + 1, :]
            o = acc[:, :tq] * inv_l[:, :tq] - lam * (acc[:, tq:] * inv_l[:, tq:])
            rms = lax.rsqrt(jnp.mean(o * o, axis=0, keepdims=True) + NORM_EPS)
            y = o * rms * (g_ref[...] * (1.0 - lambda_init))
            o_ref[pl.ds(q0, tq), hs] = y.T.astype(BF16)
        reset_state()
        return carry

    lax.fori_loop(0, n_q, q_block, 0)


def _diff_attention(q, k, vt, lam_vecs, g, lambda_init):
    width = ATT_HEADS * DIFF_DV
    spec = pl.BlockSpec((None, SEQ, width), lambda b, h: (b, 0, h))
    return pl.pallas_call(
        functools.partial(_attn_kernel, lambda_init=lambda_init),
        grid=(BATCH, DIFF_HEADS // ATT_HEADS),
        in_specs=[spec, spec, pl.BlockSpec((ATT_HEADS * VT_ROWS, SEQ), lambda b, h: (h, b)),
                  _full_spec((4, DIFF_DH)), _full_spec((DIFF_DV, 1))],
        out_specs=spec,
        out_shape=jax.ShapeDtypeStruct((BATCH, SEQ, D_MODEL), BF16),
        scratch_shapes=[
            pltpu.VMEM((2, ATT_HEADS, DIFF_DV, 2 * ATT_TQ), BF16),
            pltpu.VMEM((ATT_HEADS, ATT_TK, 2 * ATT_TQ), F32),
            pltpu.VMEM((ATT_HEADS, 1, 2 * ATT_TQ), F32),
            pltpu.VMEM((ATT_HEADS, VT_ROWS, 2 * ATT_TQ), F32),
        ],
        compiler_params=_params("arbitrary", "arbitrary"),
        name="diff_attention",
    )(q.reshape(BATCH, SEQ, D_MODEL), k.reshape(BATCH, SEQ, D_MODEL), vt, lam_vecs, g)


def _mlp_kernel(x_ref, o_ref, wo_ref, gt_ref, sh_ref, sc_ref, gc_ref, g_ref, w1_ref, w2_ref,
                fg_ref, *rest, final, n_cast):
    cast_in, (out_ref, *cast_out) = rest[:n_cast], rest[n_cast:]
    for src, dst in zip(cast_in, cast_out):
        dst[...] = src[...].astype(BF16)

    x1 = x_ref[...] + gt_ref[...] * _dot(o_ref[...], wo_ref[...])
    hb = _modulated_norm(x1, g_ref[...], sc_ref[...], sh_ref[...]).astype(BF16)
    a = jnp.square(jnp.maximum(_dot(hb, w1_ref[...]), 0.0)).astype(BF16)
    out = x1 + gc_ref[...] * _dot(a, w2_ref[...])
    if final:
        out = out * _rms_scale(out) * fg_ref[...]
    out_ref[...] = out


def _outproj_mlp(x2, o2, mod3, wo, g, w1, w2, fg, final, cast_jobs=()):
    tm = TM_MLP
    n_steps = N_TOK // tm
    row = lambda m: (m, 0)
    flats, cast_in, cast_out, cast_shapes = _cast_slabs(cast_jobs, n_steps, lambda m: m)
    out = pl.pallas_call(
        functools.partial(_mlp_kernel, final=final, n_cast=len(flats)),
        grid=(n_steps,),
        in_specs=[
            pl.BlockSpec((tm, D_MODEL), row),
            pl.BlockSpec((tm, D_MODEL), row),
            _resident_spec((D_MODEL, D_MODEL), (0, 0)),
            _mod_spec(2, tm),
            _mod_spec(3, tm),
            _mod_spec(4, tm),
            _mod_spec(5, tm),
            _full_spec((1, D_MODEL)),
            _resident_spec((D_MODEL, D_FF), (0, 0)),
            _resident_spec((D_FF, D_MODEL), (0, 0)),
            _full_spec((1, D_MODEL)),
        ] + cast_in,
        out_specs=[pl.BlockSpec((tm, D_MODEL), row)] + cast_out,
        out_shape=[jax.ShapeDtypeStruct((N_TOK, D_MODEL), F32)] + cast_shapes,
        compiler_params=_params("arbitrary"),
        name="outproj_mlp_final" if final else "outproj_mlp",
    )(x2, o2, wo, mod3, mod3, mod3, mod3, g, w1, w2, fg, *flats)
    return out[0], out[1:]


def kernel(x, c, positions, ada_w, ada_b, norm_g, mlp_w1, mlp_w2, gla_w_in, gla_w_a2, gla_b_a,
           gla_b_r, gla_norm_g, gla_w_o, diff_w_in, diff_lambda, diff_subln_g, diff_w_o, final_g):
    x2 = x.reshape(N_TOK, D_MODEL)
    mod = _adaln_mod(c, ada_w, ada_b)
    mod_table = lambda layer: mod[layer].reshape(BATCH * N_MOD, 1, D_MODEL)
    mod3 = mod_table(0)
    fg = final_g.reshape(1, D_MODEL)

    o, (w1, w2, wo) = _gla_layer(
        x2, mod3, norm_g[0, 0].reshape(1, D_MODEL), gla_w_in[0].T, gla_w_a2[0],
        gla_b_a[0].reshape(1, GLA_HK), gla_b_r[0].reshape(1, D_MODEL),
        gla_norm_g[0].reshape(1, GLA_DV), [(mlp_w1, 0), (mlp_w2, 0), (gla_w_o, 0)])
    x2, (w1, w2, diff_win, wo) = _outproj_mlp(
        x2, o, mod3, wo, norm_g[0, 1].reshape(1, D_MODEL), w1, w2, fg,
        False, [(mlp_w1, 1), (mlp_w2, 1), (diff_w_in, 0), (diff_w_o, 0)])
    mod3 = mod_table(1)

    lambda_init = 0.8 - 0.6 * math.exp(-0.3 * 1)
    cs = _rope_tables(positions)
    q, k, vt = _inproj_diff(
        x2, mod3, norm_g[1, 0].reshape(1, D_MODEL), diff_win, cs, _rope_expander())
    o = _diff_attention(q, k, vt, diff_lambda[0], diff_subln_g[0].reshape(DIFF_DV, 1), lambda_init)
    x2, _ = _outproj_mlp(x2, o.reshape(N_TOK, D_MODEL), mod3, wo,
                         norm_g[1, 1].reshape(1, D_MODEL), w1, w2, fg, True)
    return x2.reshape(BATCH, SEQ, D_MODEL)
```

```python
import functools
import math

import jax
import jax.numpy as jnp
import numpy as np
from jax import lax
from jax.experimental import pallas as pl
from jax.experimental.pallas import tpu as pltpu

D_MODEL = 1024
BATCH = 8
SEQ = 2048
DEPTH = 2
D_FF = 4 * D_MODEL
NORM_EPS = 1e-6
GLA_HEADS = 4
GLA_DK = 128
GLA_DV = 256
GLA_HK = GLA_HEADS * GLA_DK
GLA_GATE_RANK = 16
GLA_TAU = 16.0
GLA_CHUNK = 64
DIFF_HEADS = 8
DIFF_DH = 64
DIFF_DV = 128
ROPE_THETA = 500000.0
ROPE_DIM = DIFF_DH // 4
ROPE_HALF = ROPE_DIM // 2
N_TOK = BATCH * SEQ
N_MOD = 6

TM_INPROJ = 1024
INPROJ_ROW_GROUPS = 2
TM_MLP = 512
TM_GLA = 1024
GLA_ROW_GROUPS = 2
TN_MOD = 1536
ATT_TQ = 256
ATT_TK = 256
VT_ROWS = DIFF_DV + 16
ATT_HEADS = 8
VMEM_LIMIT = 56 * 1024 * 1024

F32 = jnp.float32
BF16 = jnp.bfloat16
MASKED = -1e30
NT_DIMS = (((1,), (1,)), ((), ()))
TN_DIMS = (((0,), (0,)), ((), ()))


def _dot(a, b):
    return jnp.dot(a, b, preferred_element_type=F32)


def _split_bf16(a):
    hi = a.astype(BF16)
    lo = (a - hi.astype(F32)).astype(BF16)
    return hi, lo


def _rms_scale(x):
    return lax.rsqrt(jnp.mean(x * x, axis=-1, keepdims=True) + NORM_EPS)


def _silu(x):
    return x * (1.0 / (1.0 + jnp.exp(-x)))


def _params(*sem):
    return pltpu.CompilerParams(dimension_semantics=sem, vmem_limit_bytes=VMEM_LIMIT)


def _full_spec(shape):
    zeros = (0,) * len(shape)
    return pl.BlockSpec(shape, lambda *_: zeros)


def _resident_spec(block_shape, index):
    return pl.BlockSpec(block_shape, lambda m: index, pipeline_mode=pl.Buffered(1))


def _mod_spec(k, tm):
    tiles_per_seq = SEQ // tm
    return pl.BlockSpec((None, 1, D_MODEL), lambda m, *_: ((m // tiles_per_seq) * N_MOD + k, 0, 0))


def _modulated_norm(x, g, scale, shift):
    return (x * _rms_scale(x) * g) * (1.0 + scale) + shift


def _cast_slabs(jobs, n_steps, step_of):
    flats, in_specs, out_specs, out_shapes = [], [], [], []
    for w, layer in jobs:
        _, n_rows, cols = w.shape
        slab = n_rows // n_steps
        assert slab * n_steps == n_rows and slab % 16 == 0
        first = layer * n_steps
        flats.append(w.reshape(-1, cols))
        in_specs.append(pl.BlockSpec((slab, cols), lambda *g, first=first: (first + step_of(*g), 0)))
        out_specs.append(pl.BlockSpec((slab, cols), lambda *g: (step_of(*g), 0)))
        out_shapes.append(jax.ShapeDtypeStruct((n_rows, cols), BF16))
    return flats, in_specs, out_specs, out_shapes


def _mod_block(c_ref, w_ref, b_ref, o_ref):
    c_hi, c_lo = _split_bf16(_silu(c_ref[...]))
    w_hi, w_lo = _split_bf16(w_ref[...])
    lhs = jnp.concatenate([c_hi, c_lo], axis=0)
    r = _dot(lhs, w_hi)
    o_ref[...] = r[:BATCH] + r[BATCH:] + _dot(c_hi, w_lo) + b_ref[...]


def _adaln_mod(c, ada_w, ada_b):
    n_out = N_MOD * D_MODEL
    return pl.pallas_call(
        _mod_block,
        grid=(DEPTH, n_out // TN_MOD),
        in_specs=[
            _full_spec((BATCH, D_MODEL)),
            pl.BlockSpec((None, D_MODEL, TN_MOD), lambda i, j: (i, 0, j)),
            pl.BlockSpec((None, 1, TN_MOD), lambda i, j: (i, 0, j)),
        ],
        out_specs=pl.BlockSpec((None, BATCH, TN_MOD), lambda i, j: (i, 0, j)),
        out_shape=jax.ShapeDtypeStruct((DEPTH, BATCH, n_out), F32),
        compiler_params=_params("arbitrary", "arbitrary"),
        name="adaln_mod",
    )(c, ada_w, ada_b.reshape(DEPTH, 1, n_out))


def _gla_layer_kernel(x_ref, shift_ref, scale_ref, g_ref, wqk_ref, wv_ref, wr_ref, wa_ref,
                      wa2_ref, ba_ref, br_ref, ng_ref, *rest, n_cast, tiles_per_seq):
    cast_in, rest = rest[:n_cast], rest[n_cast:]
    o_ref, cast_out = rest[0], rest[1:1 + n_cast]
    wbf_ref, st_ref, qd_ref, ki_ref, v_ref, gate_ref, dec_ref = rest[1 + n_cast:]
    for src, dst in zip(cast_in, cast_out):
        dst[...] = src[...].astype(BF16)

    @pl.when(pl.program_id(0) == 0)
    def _():
        for i, w_ref in enumerate((wqk_ref, wv_ref, wr_ref)):
            wbf_ref[i] = w_ref[...].T.astype(BF16)

    @pl.when(pl.program_id(0) % tiles_per_seq == 0)
    def _():
        st_ref[...] = jnp.zeros_like(st_ref)

    wqk_ref, wv_ref, wr_ref = wbf_ref.at[0], wbf_ref.at[1], wbf_ref.at[2]
    row = lax.broadcasted_iota(jnp.int32, (GLA_CHUNK, GLA_CHUNK), 0)
    col = lax.broadcasted_iota(jnp.int32, (GLA_CHUNK, GLA_CHUNK), 1)
    causal = row >= col
    tril = jnp.where(causal, 1.0, 0.0).astype(BF16)
    w_hi, w_lo = _split_bf16(wa2_ref[...])
    w_gate = jnp.concatenate([w_hi, w_hi, w_lo], axis=0)
    q_scale = GLA_DK ** -0.5
    group_rows = x_ref.shape[0] // GLA_ROW_GROUPS
    chunks_per_group = group_rows // GLA_CHUNK
    chunk = lambda n: slice(n * GLA_CHUNK, (n + 1) * GLA_CHUNK)
    kcols = lambda h: slice(h * GLA_DK, (h + 1) * GLA_DK)
    vcols = lambda h: slice(h * GLA_DV, (h + 1) * GLA_DV)

    def phase_a(grp):
        rows = slice(grp * group_rows, (grp + 1) * group_rows)
        hb = _modulated_norm(x_ref[rows, :], g_ref[...], scale_ref[...],
                             shift_ref[...]).astype(BF16)
        a_hi, a_lo = _split_bf16(lax.dot_general(
            hb, wa_ref[...].astype(BF16), NT_DIMS, preferred_element_type=F32))
        r = _dot(hb, wr_ref[...]) + br_ref[...]
        gate_ref[rows, :] = _silu(r).astype(BF16)
        la = _dot(jnp.concatenate([a_hi, a_lo, a_hi], axis=1), w_gate) + ba_ref[...]
        qk = _dot(hb, wqk_ref[...])
        log_a = (jnp.minimum(la, 0.0) - jnp.log(1.0 + jnp.exp(-jnp.abs(la)))) * (1.0 / GLA_TAU)
        l_hi = log_a.astype(BF16)
        cums = [_dot(tril, l_hi[chunk(i)]) for i in range(chunks_per_group)]
        v_ref[rows, :] = _dot(hb, wv_ref[...]).astype(BF16)
        for i in range(chunks_per_group):
            n = grp * chunks_per_group + i
            b = cums[i]
            b_last = b[GLA_CHUNK - 1:GLA_CHUNK, :]
            qd_ref[chunk(n), :] = (qk[chunk(i), :GLA_HK] * q_scale * jnp.exp(b)).astype(BF16)
            ki_ref[chunk(n), :] = (qk[chunk(i), GLA_HK:] * jnp.exp(-b)).astype(BF16)
            dec_ref[n] = jnp.exp(b_last)

    def phase_b(grp):
        chunks = range(grp * chunks_per_group, (grp + 1) * chunks_per_group)
        work = [(h, n) for h in range(GLA_HEADS) for n in chunks]
        scores = {}
        updates = {}
        for h, n in work:
            scores[h, n] = lax.dot_general(qd_ref[chunk(n), kcols(h)], ki_ref[chunk(n), kcols(h)],
                                           NT_DIMS, preferred_element_type=F32)
            updates[h, n] = lax.dot_general(ki_ref[chunk(n), kcols(h)], v_ref[chunk(n), vcols(h)],
                                            TN_DIMS, preferred_element_type=F32)
        states = {}
        for h in range(GLA_HEADS):
            st = st_ref[h]
            for n in chunks:
                states[h, n] = st.astype(BF16)
                dec_col = jnp.broadcast_to(dec_ref[n][:, kcols(h)], (8, GLA_DK)).T[:, 0:1]
                st = (st + updates[h, n]) * dec_col
            st_ref[h] = st
        for h, n in work:
            s = jnp.where(causal, scores[h, n], 0.0).astype(BF16)
            o = _dot(jnp.concatenate([qd_ref[chunk(n), kcols(h)], s], axis=1),
                     jnp.concatenate([states[h, n], v_ref[chunk(n), vcols(h)]], axis=0))
            y = o * _rms_scale(o) * ng_ref[...]
            o_ref[chunk(n), vcols(h)] = (y * gate_ref[chunk(n), vcols(h)].astype(F32)).astype(BF16)

    for grp in range(GLA_ROW_GROUPS):
        phase_a(grp)
        phase_b(grp)


def _gla_layer(x2, mod3, g, w_in_t, wa2, ba, br, ng, cast_jobs):
    tm = TM_GLA
    n_steps = N_TOK // tm
    row = lambda m: (m, 0)
    assert 2 * GLA_HK == D_MODEL
    row_block = lambda idx: _resident_spec((D_MODEL, D_MODEL), (idx, 0))
    flats, cast_in, cast_out, cast_shapes = _cast_slabs(cast_jobs, n_steps, lambda m: m)
    out = pl.pallas_call(
        functools.partial(_gla_layer_kernel, n_cast=len(flats), tiles_per_seq=SEQ // tm),
        grid=(n_steps,),
        in_specs=[
            pl.BlockSpec((tm, D_MODEL), row),
            _mod_spec(0, tm),
            _mod_spec(1, tm),
            _full_spec((1, D_MODEL)),
            row_block(0),
            row_block(1),
            row_block(2),
            _resident_spec((GLA_GATE_RANK, D_MODEL), (3 * D_MODEL // GLA_GATE_RANK, 0)),
            _full_spec((GLA_GATE_RANK, GLA_HK)),
            _full_spec((1, GLA_HK)),
            _full_spec((1, D_MODEL)),
            _full_spec((1, GLA_DV)),
        ] + cast_in,
        out_specs=[pl.BlockSpec((tm, D_MODEL), row)] + cast_out,
        out_shape=[jax.ShapeDtypeStruct((N_TOK, D_MODEL), BF16)] + cast_shapes,
        scratch_shapes=[
            pltpu.VMEM((3, D_MODEL, D_MODEL), BF16),
            pltpu.VMEM((GLA_HEADS, GLA_DK, GLA_DV), F32),
            pltpu.VMEM((tm, GLA_HK), BF16),
            pltpu.VMEM((tm, GLA_HK), BF16),
            pltpu.VMEM((tm, D_MODEL), BF16),
            pltpu.VMEM((tm, D_MODEL), BF16),
            pltpu.VMEM((tm // GLA_CHUNK, 1, GLA_HK), F32),
        ],
        compiler_params=_params("arbitrary"),
        name="gla_layer",
    )(x2, mod3, mod3, g, w_in_t, w_in_t, w_in_t, w_in_t, wa2, ba, br, ng, *flats)
    return out[0], out[1:]


def _rope_kernel(pos_ref, freq_ref, cs_ref):
    ang = freq_ref[...] * pos_ref[...]
    cs_ref[0:ROPE_HALF, :] = jnp.cos(ang)
    cs_ref[ROPE_HALF:, :] = jnp.sin(ang)


def _rope_tables(positions):
    inv_freq = ROPE_THETA ** (-jnp.arange(0, ROPE_DIM, 2, dtype=F32) / ROPE_DIM)
    return pl.pallas_call(
        _rope_kernel,
        grid=(1,),
        in_specs=[_full_spec((1, N_TOK)), _full_spec((ROPE_HALF, 1))],
        out_specs=_full_spec((ROPE_DIM, N_TOK)),
        out_shape=jax.ShapeDtypeStruct((ROPE_DIM, N_TOK), F32),
        compiler_params=_params("arbitrary"),
        name="rope_tables",
    )(positions.astype(F32).reshape(1, N_TOK), inv_freq.reshape(ROPE_HALF, 1))


def _rope_expander():
    e = np.zeros((3, ROPE_DIM, 2 * DIFF_DV), np.float32)
    for lane in range(DIFF_DV):
        d = lane % DIFF_DH
        if d < ROPE_DIM:
            e[:, d % ROPE_HALF, lane] = 1.0
            e[:, ROPE_HALF + d % ROPE_HALF, DIFF_DV + lane] = -1.0 if d < ROPE_HALF else 1.0
    return jnp.asarray(e.reshape(3 * ROPE_DIM, 2 * DIFF_DV), BF16)


def _inproj_diff_kernel(x_ref, shift_ref, scale_ref, g_ref, wq_ref, wk_ref, wv_ref,
                        cs_ref, e_ref, q_ref, k_ref, vt_ref, wvt_ref):
    @pl.when(pl.program_id(0) == 0)
    def _():
        wvt_ref[...] = wv_ref[...].T

    d = lax.broadcasted_iota(jnp.int32, (1, DIFF_DV), 1) & (DIFF_DH - 1)
    first_half = d < ROPE_HALF
    q_scale = DIFF_DH ** -0.5 * math.log2(math.e)
    group_rows = x_ref.shape[0] // INPROJ_ROW_GROUPS

    for grp in range(INPROJ_ROW_GROUPS):
        rows = slice(grp * group_rows, (grp + 1) * group_rows)
        hb = _modulated_norm(x_ref[rows, :], g_ref[...], scale_ref[...],
                             shift_ref[...]).astype(BF16)

        cs = cs_ref[:, rows]
        hi = cs.astype(BF16).astype(F32)
        mid = (cs - hi).astype(BF16).astype(F32)
        lo = cs - hi - mid
        parts = jnp.concatenate([hi, mid, lo], axis=0)
        tab = _dot(parts.T.astype(BF16), e_ref[...])
        ct = tab[:, :DIFF_DV] + jnp.where(d >= ROPE_DIM, 1.0, 0.0)
        s12 = tab[:, DIFF_DV:]

        for w_ref, out_ref, scl in ((wq_ref, q_ref, q_scale), (wk_ref, k_ref, 1.0)):
            z = _dot(hb, w_ref[...])
            for h in range(DIFF_HEADS):
                sl = slice(h * DIFF_DV, (h + 1) * DIFF_DV)
                xh = z[:, sl]
                partner = jnp.where(first_half, pltpu.roll(xh, DIFF_DV - ROPE_HALF, 1),
                                    pltpu.roll(xh, ROPE_HALF, 1))
                out_ref[rows, sl] = ((xh * ct + partner * s12) * scl).astype(BF16)
        vt = lax.dot_general(wvt_ref[...], hb, NT_DIMS, preferred_element_type=F32).astype(BF16)
        ones = jnp.ones((VT_ROWS - DIFF_DV, group_rows), BF16)
        for h in range(DIFF_HEADS):
            vt_ref[h * VT_ROWS:h * VT_ROWS + DIFF_DV, rows] = vt[h * DIFF_DV:(h + 1) * DIFF_DV]
            vt_ref[h * VT_ROWS + DIFF_DV:(h + 1) * VT_ROWS, rows] = ones


def _inproj_diff(x2, mod3, g, w_in, cs, expander):
    tm = TM_INPROJ
    row = lambda m: (m, 0)
    o_spec = pl.BlockSpec((tm, D_MODEL), row)
    return pl.pallas_call(
        _inproj_diff_kernel,
        grid=(N_TOK // tm,),
        in_specs=[
            pl.BlockSpec((tm, D_MODEL), row),
            _mod_spec(0, tm),
            _mod_spec(1, tm),
            _full_spec((1, D_MODEL)),
            _resident_spec((D_MODEL, D_MODEL), (0, 0)),
            _resident_spec((D_MODEL, D_MODEL), (0, 1)),
            _resident_spec((D_MODEL, D_MODEL), (0, 2)),
            pl.BlockSpec((ROPE_DIM, tm), lambda m: (0, m)),
            _full_spec((3 * ROPE_DIM, 2 * DIFF_DV)),
        ],
        out_specs=[o_spec, o_spec, pl.BlockSpec((DIFF_HEADS * VT_ROWS, tm), lambda m: (0, m))],
        out_shape=[jax.ShapeDtypeStruct((N_TOK, D_MODEL), BF16)] * 2
        + [jax.ShapeDtypeStruct((DIFF_HEADS * VT_ROWS, N_TOK), BF16)],
        scratch_shapes=[pltpu.VMEM((D_MODEL, D_MODEL), BF16)],
        compiler_params=_params("arbitrary"),
        name="inproj_diff",
    )(x2, mod3, mod3, g, w_in, w_in, w_in, cs, expander)


def _attn_kernel(q_ref, k_ref, vt_ref, lam_ref, g_ref, o_ref, qs_ref, s_ref, m_ref, acc_ref,
                 *, lambda_init):
    tq, tk = ATT_TQ, ATT_TK
    assert tq == tk
    lf = lam_ref[...]
    lam = (jnp.exp(jnp.sum(lf[0:1] * lf[1:2], axis=-1, keepdims=True))
           - jnp.exp(jnp.sum(lf[2:3] * lf[3:4], axis=-1, keepdims=True)) + lambda_init)
    feat = lax.broadcasted_iota(jnp.int32, (DIFF_DV, tq), 0)
    k_pos = lax.broadcasted_iota(jnp.int32, (tk, 2 * tq), 0)
    q_pos = lax.broadcasted_iota(jnp.int32, (tk, 2 * tq), 1) & (tq - 1)
    heads = [slice(h * DIFF_DV, (h + 1) * DIFF_DV) for h in range(ATT_HEADS)]

    def stage_queries(qi, slot):
        q0 = pl.multiple_of(qi * tq, tq)
        for h, hs in enumerate(heads):
            qt = q_ref[pl.ds(q0, tq), hs].astype(F32).T
            qs_ref[slot, h, :, 0:tq] = jnp.where(feat < DIFF_DH, qt, 0.0).astype(BF16)
            qs_ref[slot, h, :, tq:2 * tq] = jnp.where(feat >= DIFF_DH, qt, 0.0).astype(BF16)

    def scores_for(j, h, slot):
        start = pl.multiple_of(j * tk, tk)
        return _dot(k_ref[pl.ds(start, tk), heads[h]], qs_ref[slot, h])

    def reset_state():
        m_ref[...] = jnp.full_like(m_ref, MASKED)

    def kv_step(j, masked, next_j, next_slot):
        start = pl.multiple_of(j * tk, tk)
        for h in range(ATT_HEADS):
            vtb = vt_ref[h * VT_ROWS:(h + 1) * VT_ROWS, pl.ds(start, tk)]
            s = s_ref[h]
            next_s = scores_for(next_j, h, next_slot)
            if masked:
                s = jnp.where(k_pos <= q_pos, s, MASKED)
            m_prev = m_ref[h]
            m_new = jnp.maximum(m_prev, jnp.max(s, axis=0, keepdims=True))
            p = jnp.exp2(s - m_new)
            alpha = jnp.exp2(m_prev - m_new)
            acc_ref[h] = alpha * acc_ref[h] + _dot(vtb, p.astype(BF16))
            m_ref[h] = m_new
            s_ref[h] = next_s

    n_q = SEQ // tq
    stage_queries(0, 0)
    acc_ref[...] = jnp.zeros_like(acc_ref)
    reset_state()
    for h in range(ATT_HEADS):
        s_ref[h] = scores_for(0, h, 0)

    def q_block(qi, carry):
        q0 = pl.multiple_of(qi * tq, tq)
        slot = qi & 1

        def body(j, c):
            kv_step(j, False, j + 1, slot)
            return c

        lax.fori_loop(0, qi, body, 0)
        stage_queries(jnp.minimum(qi + 1, n_q - 1), 1 - slot)
        kv_step(qi, True, 0, 1 - slot)
        for h, hs in enumerate(heads):
            acc = acc_ref[h, 0:DIFF_DV, :]
            inv_l = 1.0 / acc_ref[h, DIFF_DV:DIFF_DV + 1, :]
            o = acc[:, :tq] * inv_l[:, :tq] - lam * (acc[:, tq:] * inv_l[:, tq:])
            rms = lax.rsqrt(jnp.mean(o * o, axis=0, keepdims=True) + NORM_EPS)
            y = o * rms * (g_ref[...] * (1.0 - lambda_init))
            o_ref[pl.ds(q0, tq), hs] = y.T.astype(BF16)
        reset_state()
        return carry

    lax.fori_loop(0, n_q, q_block, 0)


def _diff_attention(q, k, vt, lam_vecs, g, lambda_init):
    width = ATT_HEADS * DIFF_DV
    spec = pl.BlockSpec((None, SEQ, width), lambda b, h: (b, 0, h))
    return pl.pallas_call(
        functools.partial(_attn_kernel, lambda_init=lambda_init),
        grid=(BATCH, DIFF_HEADS // ATT_HEADS),
        in_specs=[spec, spec, pl.BlockSpec((ATT_HEADS * VT_ROWS, SEQ), lambda b, h: (h, b)),
                  _full_spec((4, DIFF_DH)), _full_spec((DIFF_DV, 1))],
        out_specs=spec,
        out_shape=jax.ShapeDtypeStruct((BATCH, SEQ, D_MODEL), BF16),
        scratch_shapes=[
            pltpu.VMEM((2, ATT_HEADS, DIFF_DV, 2 * ATT_TQ), BF16),
            pltpu.VMEM((ATT_HEADS, ATT_TK, 2 * ATT_TQ), F32),
            pltpu.VMEM((ATT_HEADS, 1, 2 * ATT_TQ), F32),
            pltpu.VMEM((ATT_HEADS, VT_ROWS, 2 * ATT_TQ), F32),
        ],
        compiler_params=_params("arbitrary", "arbitrary"),
        name="diff_attention",
    )(q.reshape(BATCH, SEQ, D_MODEL), k.reshape(BATCH, SEQ, D_MODEL), vt, lam_vecs, g)


def _mlp_kernel(x_ref, o_ref, wo_ref, gt_ref, sh_ref, sc_ref, gc_ref, g_ref, w1_ref, w2_ref,
                fg_ref, *rest, final, n_cast):
    cast_in, (out_ref, *cast_out) = rest[:n_cast], rest[n_cast:]
    for src, dst in zip(cast_in, cast_out):
        dst[...] = src[...].astype(BF16)

    x1 = x_ref[...] + gt_ref[...] * _dot(o_ref[...], wo_ref[...])
    hb = _modulated_norm(x1, g_ref[...], sc_ref[...], sh_ref[...]).astype(BF16)
    a = jnp.square(jnp.maximum(_dot(hb, w1_ref[...]), 0.0)).astype(BF16)
    out = x1 + gc_ref[...] * _dot(a, w2_ref[...])
    if final:
        out = out * _rms_scale(out) * fg_ref[...]
    out_ref[...] = out


def _outproj_mlp(x2, o2, mod3, wo, g, w1, w2, fg, final, cast_jobs=()):
    tm = TM_MLP
    n_steps = N_TOK // tm
    row = lambda m: (m, 0)
    flats, cast_in, cast_out, cast_shapes = _cast_slabs(cast_jobs, n_steps, lambda m: m)
    out = pl.pallas_call(
        functools.partial(_mlp_kernel, final=final, n_cast=len(flats)),
        grid=(n_steps,),
        in_specs=[
            pl.BlockSpec((tm, D_MODEL), row),
            pl.BlockSpec((tm, D_MODEL), row),
            _resident_spec((D_MODEL, D_MODEL), (0, 0)),
            _mod_spec(2, tm),
            _mod_spec(3, tm),
            _mod_spec(4, tm),
            _mod_spec(5, tm),
            _full_spec((1, D_MODEL)),
            _resident_spec((D_MODEL, D_FF), (0, 0)),
            _resident_spec((D_FF, D_MODEL), (0, 0)),
            _full_spec((1, D_MODEL)),
        ] + cast_in,
        out_specs=[pl.BlockSpec((tm, D_MODEL), row)] + cast_out,
        out_shape=[jax.ShapeDtypeStruct((N_TOK, D_MODEL), F32)] + cast_shapes,
        compiler_params=_params("arbitrary"),
        name="outproj_mlp_final" if final else "outproj_mlp",
    )(x2, o2, wo, mod3, mod3, mod3, mod3, g, w1, w2, fg, *flats)
    return out[0], out[1:]


def kernel(x, c, positions, ada_w, ada_b, norm_g, mlp_w1, mlp_w2, gla_w_in, gla_w_a2, gla_b_a,
           gla_b_r, gla_norm_g, gla_w_o, diff_w_in, diff_lambda, diff_subln_g, diff_w_o, final_g):
    x2 = x.reshape(N_TOK, D_MODEL)
    mod = _adaln_mod(c, ada_w, ada_b)
    mod_table = lambda layer: mod[layer].reshape(BATCH * N_MOD, 1, D_MODEL)
    mod3 = mod_table(0)
    fg = final_g.reshape(1, D_MODEL)

    o, (w1, w2, wo) = _gla_layer(
        x2, mod3, norm_g[0, 0].reshape(1, D_MODEL), gla_w_in[0].T, gla_w_a2[0],
        gla_b_a[0].reshape(1, GLA_HK), gla_b_r[0].reshape(1, D_MODEL),
        gla_norm_g[0].reshape(1, GLA_DV), [(mlp_w1, 0), (mlp_w2, 0), (gla_w_o, 0)])
    x2, (w1, w2, diff_win, wo) = _outproj_mlp(
        x2, o, mod3, wo, norm_g[0, 1].reshape(1, D_MODEL), w1, w2, fg,
        False, [(mlp_w1, 1), (mlp_w2, 1), (diff_w_in, 0), (diff_w_o, 0)])
    mod3 = mod_table(1)

    lambda_init = 0.8 - 0.6 * math.exp(-0.3 * 1)
    cs = _rope_tables(positions)
    q, k, vt = _inproj_diff(
        x2, mod3, norm_g[1, 0].reshape(1, D_MODEL), diff_win, cs, _rope_expander())
    o = _diff_attention(q, k, vt, diff_lambda[0], diff_subln_g[0].reshape(DIFF_DV, 1), lambda_init)
    x2, _ = _outproj_mlp(x2, o.reshape(N_TOK, D_MODEL), mod3, wo,
                         norm_g[1, 1].reshape(1, D_MODEL), w1, w2, fg, True)
    return x2.reshape(BATCH, SEQ, D_MODEL)
```

```python
import functools
import math

import jax
import jax.numpy as jnp
import numpy as np
from jax import lax
from jax.experimental import pallas as pl
from jax.experimental.pallas import tpu as pltpu

D_MODEL = 1024
BATCH = 8
SEQ = 2048
DEPTH = 2
D_FF = 4 * D_MODEL
NORM_EPS = 1e-6
GLA_HEADS = 4
GLA_DK = 128
GLA_DV = 256
GLA_HK = GLA_HEADS * GLA_DK
GLA_GATE_RANK = 16
GLA_TAU = 16.0
GLA_CHUNK = 64
DIFF_HEADS = 8
DIFF_DH = 64
DIFF_DV = 128
ROPE_THETA = 500000.0
ROPE_DIM = DIFF_DH // 4
ROPE_HALF = ROPE_DIM // 2
N_TOK = BATCH * SEQ
N_MOD = 6

TM_INPROJ = 1024
INPROJ_ROW_GROUPS = 2
TM_MLP = 1024
TM_MLP_HOST = 512
MLP_SUB_ROWS = 512
TM_GLA = 1024
GLA_ROW_GROUPS = 2
TN_MOD = 1536
ATT_TQ = 256
ATT_TK = 256
VT_ROWS = DIFF_DV + 16
ATT_HEADS = 8
VMEM_LIMIT = 56 * 1024 * 1024

F32 = jnp.float32
BF16 = jnp.bfloat16
MASKED = -1e30
NT_DIMS = (((1,), (1,)), ((), ()))
TN_DIMS = (((0,), (0,)), ((), ()))


def _dot(a, b):
    return jnp.dot(a, b, preferred_element_type=F32)


def _split_bf16(a):
    hi = a.astype(BF16)
    lo = (a - hi.astype(F32)).astype(BF16)
    return hi, lo


def _rms_scale(x):
    return lax.rsqrt(jnp.mean(x * x, axis=-1, keepdims=True) + NORM_EPS)


def _silu(x):
    return x * (1.0 / (1.0 + jnp.exp(-x)))


def _params(*sem):
    return pltpu.CompilerParams(dimension_semantics=sem, vmem_limit_bytes=VMEM_LIMIT)


def _full_spec(shape):
    zeros = (0,) * len(shape)
    return pl.BlockSpec(shape, lambda *_: zeros)


def _resident_spec(block_shape, index):
    return pl.BlockSpec(block_shape, lambda m: index, pipeline_mode=pl.Buffered(1))


def _mod_spec(k, tm):
    tiles_per_seq = SEQ // tm
    return pl.BlockSpec((None, 1, D_MODEL), lambda m, *_: ((m // tiles_per_seq) * N_MOD + k, 0, 0))


def _modulated_norm(x, g, scale, shift):
    return (x * _rms_scale(x) * g) * (1.0 + scale) + shift


def _cast_slabs(jobs, n_steps, step_of):
    flats, in_specs, out_specs, out_shapes = [], [], [], []
    for w, layer in jobs:
        _, n_rows, cols = w.shape
        slab = n_rows // n_steps
        assert slab * n_steps == n_rows and slab % 16 == 0
        first = layer * n_steps
        flats.append(w.reshape(-1, cols))
        in_specs.append(pl.BlockSpec((slab, cols), lambda *g, first=first: (first + step_of(*g), 0)))
        out_specs.append(pl.BlockSpec((slab, cols), lambda *g: (step_of(*g), 0)))
        out_shapes.append(jax.ShapeDtypeStruct((n_rows, cols), BF16))
    return flats, in_specs, out_specs, out_shapes


def _mod_block(c_ref, w_ref, b_ref, o_ref):
    c_hi, c_lo = _split_bf16(_silu(c_ref[...]))
    w_hi, w_lo = _split_bf16(w_ref[...])
    lhs = jnp.concatenate([c_hi, c_lo], axis=0)
    r = _dot(lhs, w_hi)
    o_ref[...] = r[:BATCH] + r[BATCH:] + _dot(c_hi, w_lo) + b_ref[...]


def _adaln_mod(c, ada_w, ada_b):
    n_out = N_MOD * D_MODEL
    return pl.pallas_call(
        _mod_block,
        grid=(DEPTH, n_out // TN_MOD),
        in_specs=[
            _full_spec((BATCH, D_MODEL)),
            pl.BlockSpec((None, D_MODEL, TN_MOD), lambda i, j: (i, 0, j)),
            pl.BlockSpec((None, 1, TN_MOD), lambda i, j: (i, 0, j)),
        ],
        out_specs=pl.BlockSpec((None, BATCH, TN_MOD), lambda i, j: (i, 0, j)),
        out_shape=jax.ShapeDtypeStruct((DEPTH, BATCH, n_out), F32),
        compiler_params=_params("arbitrary", "arbitrary"),
        name="adaln_mod",
    )(c, ada_w, ada_b.reshape(DEPTH, 1, n_out))


def _gla_layer_kernel(x_ref, shift_ref, scale_ref, g_ref, wqk_ref, wv_ref, wr_ref, wa_ref,
                      wa2_ref, ba_ref, br_ref, ng_ref, *rest, n_cast, tiles_per_seq):
    cast_in, rest = rest[:n_cast], rest[n_cast:]
    o_ref, cast_out = rest[0], rest[1:1 + n_cast]
    wbf_ref, st_ref, qd_ref, ki_ref, v_ref, gate_ref, dec_ref = rest[1 + n_cast:]
    for src, dst in zip(cast_in, cast_out):
        dst[...] = src[...].astype(BF16)

    @pl.when(pl.program_id(0) == 0)
    def _():
        for i, w_ref in enumerate((wqk_ref, wv_ref, wr_ref)):
            wbf_ref[i] = w_ref[...].T.astype(BF16)

    @pl.when(pl.program_id(0) % tiles_per_seq == 0)
    def _():
        st_ref[...] = jnp.zeros_like(st_ref)

    wqk_ref, wv_ref, wr_ref = wbf_ref.at[0], wbf_ref.at[1], wbf_ref.at[2]
    row = lax.broadcasted_iota(jnp.int32, (GLA_CHUNK, GLA_CHUNK), 0)
    col = lax.broadcasted_iota(jnp.int32, (GLA_CHUNK, GLA_CHUNK), 1)
    causal = row >= col
    tril = jnp.where(causal, 1.0, 0.0).astype(BF16)
    w_hi, w_lo = _split_bf16(wa2_ref[...])
    w_gate = jnp.concatenate([w_hi, w_hi, w_lo], axis=0)
    q_scale = GLA_DK ** -0.5
    group_rows = x_ref.shape[0] // GLA_ROW_GROUPS
    chunks_per_group = group_rows // GLA_CHUNK
    chunk = lambda n: slice(n * GLA_CHUNK, (n + 1) * GLA_CHUNK)
    kcols = lambda h: slice(h * GLA_DK, (h + 1) * GLA_DK)
    vcols = lambda h: slice(h * GLA_DV, (h + 1) * GLA_DV)

    def phase_a(grp):
        rows = slice(grp * group_rows, (grp + 1) * group_rows)
        hb = _modulated_norm(x_ref[rows, :], g_ref[...], scale_ref[...],
                             shift_ref[...]).astype(BF16)
        a_hi, a_lo = _split_bf16(lax.dot_general(
            hb, wa_ref[...].astype(BF16), NT_DIMS, preferred_element_type=F32))
        r = _dot(hb, wr_ref[...]) + br_ref[...]
        gate_ref[rows, :] = _silu(r).astype(BF16)
        la = _dot(jnp.concatenate([a_hi, a_lo, a_hi], axis=1), w_gate) + ba_ref[...]
        qk = _dot(hb, wqk_ref[...])
        log_a = (jnp.minimum(la, 0.0) - jnp.log(1.0 + jnp.exp(-jnp.abs(la)))) * (1.0 / GLA_TAU)
        l_hi = log_a.astype(BF16)
        cums = [_dot(tril, l_hi[chunk(i)]) for i in range(chunks_per_group)]
        v_ref[rows, :] = _dot(hb, wv_ref[...]).astype(BF16)
        for i in range(chunks_per_group):
            n = grp * chunks_per_group + i
            b = cums[i]
            b_last = b[GLA_CHUNK - 1:GLA_CHUNK, :]
            qd_ref[chunk(n), :] = (qk[chunk(i), :GLA_HK] * q_scale * jnp.exp(b)).astype(BF16)
            ki_ref[chunk(n), :] = (qk[chunk(i), GLA_HK:] * jnp.exp(-b)).astype(BF16)
            dec_ref[n] = jnp.exp(b_last)

    def phase_b(grp):
        chunks = range(grp * chunks_per_group, (grp + 1) * chunks_per_group)
        work = [(h, n) for h in range(GLA_HEADS) for n in chunks]
        scores = {}
        updates = {}
        for h, n in work:
            scores[h, n] = lax.dot_general(qd_ref[chunk(n), kcols(h)], ki_ref[chunk(n), kcols(h)],
                                           NT_DIMS, preferred_element_type=F32)
            updates[h, n] = lax.dot_general(ki_ref[chunk(n), kcols(h)], v_ref[chunk(n), vcols(h)],
                                            TN_DIMS, preferred_element_type=F32)
        states = {}
        for h in range(GLA_HEADS):
            st = st_ref[h]
            for n in chunks:
                states[h, n] = st.astype(BF16)
                dec_col = jnp.broadcast_to(dec_ref[n][:, kcols(h)], (8, GLA_DK)).T[:, 0:1]
                st = (st + updates[h, n]) * dec_col
            st_ref[h] = st
        for h, n in work:
            s = jnp.where(causal, scores[h, n], 0.0).astype(BF16)
            o = _dot(jnp.concatenate([qd_ref[chunk(n), kcols(h)], s], axis=1),
                     jnp.concatenate([states[h, n], v_ref[chunk(n), vcols(h)]], axis=0))
            y = o * _rms_scale(o) * ng_ref[...]
            o_ref[chunk(n), vcols(h)] = (y * gate_ref[chunk(n), vcols(h)].astype(F32)).astype(BF16)

    for grp in range(GLA_ROW_GROUPS):
        phase_a(grp)
        phase_b(grp)


def _gla_layer(x2, mod3, g, w_in_t, wa2, ba, br, ng, cast_jobs):
    tm = TM_GLA
    n_steps = N_TOK // tm
    row = lambda m: (m, 0)
    assert 2 * GLA_HK == D_MODEL
    row_block = lambda idx: _resident_spec((D_MODEL, D_MODEL), (idx, 0))
    flats, cast_in, cast_out, cast_shapes = _cast_slabs(cast_jobs, n_steps, lambda m: m)
    out = pl.pallas_call(
        functools.partial(_gla_layer_kernel, n_cast=len(flats), tiles_per_seq=SEQ // tm),
        grid=(n_steps,),
        in_specs=[
            pl.BlockSpec((tm, D_MODEL), row),
            _mod_spec(0, tm),
            _mod_spec(1, tm),
            _full_spec((1, D_MODEL)),
            row_block(0),
            row_block(1),
            row_block(2),
            _resident_spec((GLA_GATE_RANK, D_MODEL), (3 * D_MODEL // GLA_GATE_RANK, 0)),
            _full_spec((GLA_GATE_RANK, GLA_HK)),
            _full_spec((1, GLA_HK)),
            _full_spec((1, D_MODEL)),
            _full_spec((1, GLA_DV)),
        ] + cast_in,
        out_specs=[pl.BlockSpec((tm, D_MODEL), row)] + cast_out,
        out_shape=[jax.ShapeDtypeStruct((N_TOK, D_MODEL), BF16)] + cast_shapes,
        scratch_shapes=[
            pltpu.VMEM((3, D_MODEL, D_MODEL), BF16),
            pltpu.VMEM((GLA_HEADS, GLA_DK, GLA_DV), F32),
            pltpu.VMEM((tm, GLA_HK), BF16),
            pltpu.VMEM((tm, GLA_HK), BF16),
            pltpu.VMEM((tm, D_MODEL), BF16),
            pltpu.VMEM((tm, D_MODEL), BF16),
            pltpu.VMEM((tm // GLA_CHUNK, 1, GLA_HK), F32),
        ],
        compiler_params=_params("arbitrary"),
        name="gla_layer",
    )(x2, mod3, mod3, g, w_in_t, w_in_t, w_in_t, w_in_t, wa2, ba, br, ng, *flats)
    return out[0], out[1:]


def _rope_kernel(pos_ref, freq_ref, cs_ref):
    ang = freq_ref[...] * pos_ref[...]
    cs_ref[0:ROPE_HALF, :] = jnp.cos(ang)
    cs_ref[ROPE_HALF:, :] = jnp.sin(ang)


def _rope_tables(positions):
    inv_freq = ROPE_THETA ** (-jnp.arange(0, ROPE_DIM, 2, dtype=F32) / ROPE_DIM)
    return pl.pallas_call(
        _rope_kernel,
        grid=(1,),
        in_specs=[_full_spec((1, N_TOK)), _full_spec((ROPE_HALF, 1))],
        out_specs=_full_spec((ROPE_DIM, N_TOK)),
        out_shape=jax.ShapeDtypeStruct((ROPE_DIM, N_TOK), F32),
        compiler_params=_params("arbitrary"),
        name="rope_tables",
    )(positions.astype(F32).reshape(1, N_TOK), inv_freq.reshape(ROPE_HALF, 1))


def _rope_expander():
    e = np.zeros((3, ROPE_DIM, 2 * DIFF_DV), np.float32)
    for lane in range(DIFF_DV):
        d = lane % DIFF_DH
        if d < ROPE_DIM:
            e[:, d % ROPE_HALF, lane] = 1.0
            e[:, ROPE_HALF + d % ROPE_HALF, DIFF_DV + lane] = -1.0 if d < ROPE_HALF else 1.0
    return jnp.asarray(e.reshape(3 * ROPE_DIM, 2 * DIFF_DV), BF16)


def _inproj_diff_kernel(x_ref, shift_ref, scale_ref, g_ref, wq_ref, wk_ref, wv_ref,
                        cs_ref, e_ref, q_ref, k_ref, vt_ref, wvt_ref):
    @pl.when(pl.program_id(0) == 0)
    def _():
        wvt_ref[...] = wv_ref[...].T

    d = lax.broadcasted_iota(jnp.int32, (1, DIFF_DV), 1) & (DIFF_DH - 1)
    first_half = d < ROPE_HALF
    q_scale = DIFF_DH ** -0.5 * math.log2(math.e)
    group_rows = x_ref.shape[0] // INPROJ_ROW_GROUPS

    for grp in range(INPROJ_ROW_GROUPS):
        rows = slice(grp * group_rows, (grp + 1) * group_rows)
        hb = _modulated_norm(x_ref[rows, :], g_ref[...], scale_ref[...],
                             shift_ref[...]).astype(BF16)

        cs = cs_ref[:, rows]
        hi = cs.astype(BF16).astype(F32)
        mid = (cs - hi).astype(BF16).astype(F32)
        lo = cs - hi - mid
        parts = jnp.concatenate([hi, mid, lo], axis=0)
        tab = _dot(parts.T.astype(BF16), e_ref[...])
        ct = tab[:, :DIFF_DV] + jnp.where(d >= ROPE_DIM, 1.0, 0.0)
        s12 = tab[:, DIFF_DV:]

        for w_ref, out_ref, scl in ((wq_ref, q_ref, q_scale), (wk_ref, k_ref, 1.0)):
            z = _dot(hb, w_ref[...])
            for h in range(DIFF_HEADS):
                sl = slice(h * DIFF_DV, (h + 1) * DIFF_DV)
                xh = z[:, sl]
                partner = jnp.where(first_half, pltpu.roll(xh, DIFF_DV - ROPE_HALF, 1),
                                    pltpu.roll(xh, ROPE_HALF, 1))
                out_ref[rows, sl] = ((xh * ct + partner * s12) * scl).astype(BF16)
        vt = lax.dot_general(wvt_ref[...], hb, NT_DIMS, preferred_element_type=F32).astype(BF16)
        ones = jnp.ones((VT_ROWS - DIFF_DV, group_rows), BF16)
        for h in range(DIFF_HEADS):
            vt_ref[h * VT_ROWS:h * VT_ROWS + DIFF_DV, rows] = vt[h * DIFF_DV:(h + 1) * DIFF_DV]
            vt_ref[h * VT_ROWS + DIFF_DV:(h + 1) * VT_ROWS, rows] = ones


def _inproj_diff(x2, mod3, g, w_in, cs, expander):
    tm = TM_INPROJ
    row = lambda m: (m, 0)
    o_spec = pl.BlockSpec((tm, D_MODEL), row)
    return pl.pallas_call(
        _inproj_diff_kernel,
        grid=(N_TOK // tm,),
        in_specs=[
            pl.BlockSpec((tm, D_MODEL), row),
            _mod_spec(0, tm),
            _mod_spec(1, tm),
            _full_spec((1, D_MODEL)),
            _resident_spec((D_MODEL, D_MODEL), (0, 0)),
            _resident_spec((D_MODEL, D_MODEL), (0, 1)),
            _resident_spec((D_MODEL, D_MODEL), (0, 2)),
            pl.BlockSpec((ROPE_DIM, tm), lambda m: (0, m)),
            _full_spec((3 * ROPE_DIM, 2 * DIFF_DV)),
        ],
        out_specs=[o_spec, o_spec, pl.BlockSpec((DIFF_HEADS * VT_ROWS, tm), lambda m: (0, m))],
        out_shape=[jax.ShapeDtypeStruct((N_TOK, D_MODEL), BF16)] * 2
        + [jax.ShapeDtypeStruct((DIFF_HEADS * VT_ROWS, N_TOK), BF16)],
        scratch_shapes=[pltpu.VMEM((D_MODEL, D_MODEL), BF16)],
        compiler_params=_params("arbitrary"),
        name="inproj_diff",
    )(x2, mod3, mod3, g, w_in, w_in, w_in, cs, expander)


def _attn_kernel(q_ref, k_ref, vt_ref, lam_ref, g_ref, o_ref, qs_ref, s_ref, m_ref, acc_ref,
                 *, lambda_init):
    tq, tk = ATT_TQ, ATT_TK
    assert tq == tk
    lf = lam_ref[...]
    lam = (jnp.exp(jnp.sum(lf[0:1] * lf[1:2], axis=-1, keepdims=True))
           - jnp.exp(jnp.sum(lf[2:3] * lf[3:4], axis=-1, keepdims=True)) + lambda_init)
    feat = lax.broadcasted_iota(jnp.int32, (DIFF_DV, tq), 0)
    k_pos = lax.broadcasted_iota(jnp.int32, (tk, 2 * tq), 0)
    q_pos = lax.broadcasted_iota(jnp.int32, (tk, 2 * tq), 1) & (tq - 1)
    heads = [slice(h * DIFF_DV, (h + 1) * DIFF_DV) for h in range(ATT_HEADS)]

    def stage_queries(qi, slot):
        q0 = pl.multiple_of(qi * tq, tq)
        for h, hs in enumerate(heads):
            qt = q_ref[pl.ds(q0, tq), hs].astype(F32).T
            qs_ref[slot, h, :, 0:tq] = jnp.where(feat < DIFF_DH, qt, 0.0).astype(BF16)
            qs_ref[slot, h, :, tq:2 * tq] = jnp.where(feat >= DIFF_DH, qt, 0.0).astype(BF16)

    def scores_for(j, h, slot):
        start = pl.multiple_of(j * tk, tk)
        return _dot(k_ref[pl.ds(start, tk), heads[h]], qs_ref[slot, h])

    def reset_state():
        m_ref[...] = jnp.full_like(m_ref, MASKED)

    def kv_step(j, masked, next_j, next_slot):
        start = pl.multiple_of(j * tk, tk)
        for h in range(ATT_HEADS):
            vtb = vt_ref[h * VT_ROWS:(h + 1) * VT_ROWS, pl.ds(start, tk)]
            s = s_ref[h]
            next_s = scores_for(next_j, h, next_slot)
            if masked:
                s = jnp.where(k_pos <= q_pos, s, MASKED)
            m_prev = m_ref[h]
            m_new = jnp.maximum(m_prev, jnp.max(s, axis=0, keepdims=True))
            p = jnp.exp2(s - m_new)
            alpha = jnp.exp2(m_prev - m_new)
            acc_ref[h] = alpha * acc_ref[h] + _dot(vtb, p.astype(BF16))
            m_ref[h] = m_new
            s_ref[h] = next_s

    n_q = SEQ // tq
    stage_queries(0, 0)
    acc_ref[...] = jnp.zeros_like(acc_ref)
    reset_state()
    for h in range(ATT_HEADS):
        s_ref[h] = scores_for(0, h, 0)

    def q_block(qi, carry):
        q0 = pl.multiple_of(qi * tq, tq)
        slot = qi & 1

        def body(j, c):
            kv_step(j, False, j + 1, slot)
            return c

        lax.fori_loop(0, qi, body, 0)
        stage_queries(jnp.minimum(qi + 1, n_q - 1), 1 - slot)
        kv_step(qi, True, 0, 1 - slot)
        for h, hs in enumerate(heads):
            acc = acc_ref[h, 0:DIFF_DV, :]
            inv_l = 1.0 / acc_ref[h, DIFF_DV:DIFF_DV + 1, :]
            o = acc[:, :tq] * inv_l[:, :tq] - lam * (acc[:, tq:] * inv_l[:, tq:])
            rms = lax.rsqrt(jnp.mean(o * o, axis=0, keepdims=True) + NORM_EPS)
            y = o * rms * (g_ref[...] * (1.0 - lambda_init))
            o_ref[pl.ds(q0, tq), hs] = y.T.astype(BF16)
        reset_state()
        return carry

    lax.fori_loop(0, n_q, q_block, 0)


def _diff_attention(q, k, vt, lam_vecs, g, lambda_init):
    width = ATT_HEADS * DIFF_DV
    spec = pl.BlockSpec((None, SEQ, width), lambda b, h: (b, 0, h))
    return pl.pallas_call(
        functools.partial(_attn_kernel, lambda_init=lambda_init),
        grid=(BATCH, DIFF_HEADS // ATT_HEADS),
        in_specs=[spec, spec, pl.BlockSpec((ATT_HEADS * VT_ROWS, SEQ), lambda b, h: (h, b)),
                  _full_spec((4, DIFF_DH)), _full_spec((DIFF_DV, 1))],
        out_specs=spec,
        out_shape=jax.ShapeDtypeStruct((BATCH, SEQ, D_MODEL), BF16),
        scratch_shapes=[
            pltpu.VMEM((2, ATT_HEADS, DIFF_DV, 2 * ATT_TQ), BF16),
            pltpu.VMEM((ATT_HEADS, ATT_TK, 2 * ATT_TQ), F32),
            pltpu.VMEM((ATT_HEADS, 1, 2 * ATT_TQ), F32),
            pltpu.VMEM((ATT_HEADS, VT_ROWS, 2 * ATT_TQ), F32),
        ],
        compiler_params=_params("arbitrary", "arbitrary"),
        name="diff_attention",
    )(q.reshape(BATCH, SEQ, D_MODEL), k.reshape(BATCH, SEQ, D_MODEL), vt, lam_vecs, g)


def _mlp_kernel(x_ref, o_ref, wo_ref, gt_ref, sh_ref, sc_ref, gc_ref, g_ref, w1_ref, w2_ref,
                fg_ref, *rest, final, n_cast):
    cast_in, (out_ref, *cast_out) = rest[:n_cast], rest[n_cast:]
    for src, dst in zip(cast_in, cast_out):
        dst[...] = src[...].astype(BF16)

    def rows_block(rows):
        x1 = x_ref[rows, :] + gt_ref[...] * _dot(o_ref[rows, :], wo_ref[...])
        hb = _modulated_norm(x1, g_ref[...], sc_ref[...], sh_ref[...]).astype(BF16)
        a = jnp.square(jnp.maximum(_dot(hb, w1_ref[...]), 0.0)).astype(BF16)
        out = x1 + gc_ref[...] * _dot(a, w2_ref[...])
        if final:
            out = out * _rms_scale(out) * fg_ref[...]
        out_ref[rows, :] = out

    n_sub = x_ref.shape[0] // MLP_SUB_ROWS
    if n_sub == 1:
        rows_block(slice(None))
    else:
        def body(i, carry):
            rows_block(pl.ds(pl.multiple_of(i * MLP_SUB_ROWS, MLP_SUB_ROWS), MLP_SUB_ROWS))
            return carry
        lax.fori_loop(0, n_sub, body, 0)


def _outproj_mlp(x2, o2, mod3, wo, g, w1, w2, fg, final, cast_jobs=()):
    tm = TM_MLP_HOST if cast_jobs else TM_MLP
    n_steps = N_TOK // tm
    row = lambda m: (m, 0)
    flats, cast_in, cast_out, cast_shapes = _cast_slabs(cast_jobs, n_steps, lambda m: m)
    out = pl.pallas_call(
        functools.partial(_mlp_kernel, final=final, n_cast=len(flats)),
        grid=(n_steps,),
        in_specs=[
            pl.BlockSpec((tm, D_MODEL), row),
            pl.BlockSpec((tm, D_MODEL), row),
            _resident_spec((D_MODEL, D_MODEL), (0, 0)),
            _mod_spec(2, tm),
            _mod_spec(3, tm),
            _mod_spec(4, tm),
            _mod_spec(5, tm),
            _full_spec((1, D_MODEL)),
            _resident_spec((D_MODEL, D_FF), (0, 0)),
            _resident_spec((D_FF, D_MODEL), (0, 0)),
            _full_spec((1, D_MODEL)),
        ] + cast_in,
        out_specs=[pl.BlockSpec((tm, D_MODEL), row)] + cast_out,
        out_shape=[jax.ShapeDtypeStruct((N_TOK, D_MODEL), F32)] + cast_shapes,
        compiler_params=_params("arbitrary"),
        name="outproj_mlp_final" if final else "outproj_mlp",
    )(x2, o2, wo, mod3, mod3, mod3, mod3, g, w1, w2, fg, *flats)
    return out[0], out[1:]


def kernel(x, c, positions, ada_w, ada_b, norm_g, mlp_w1, mlp_w2, gla_w_in, gla_w_a2, gla_b_a,
           gla_b_r, gla_norm_g, gla_w_o, diff_w_in, diff_lambda, diff_subln_g, diff_w_o, final_g):
    x2 = x.reshape(N_TOK, D_MODEL)
    mod = _adaln_mod(c, ada_w, ada_b)
    mod_table = lambda layer: mod[layer].reshape(BATCH * N_MOD, 1, D_MODEL)
    mod3 = mod_table(0)
    fg = final_g.reshape(1, D_MODEL)

    o, (w1, w2, wo) = _gla_layer(
        x2, mod3, norm_g[0, 0].reshape(1, D_MODEL), gla_w_in[0].T, gla_w_a2[0],
        gla_b_a[0].reshape(1, GLA_HK), gla_b_r[0].reshape(1, D_MODEL),
        gla_norm_g[0].reshape(1, GLA_DV), [(mlp_w1, 0), (mlp_w2, 0), (gla_w_o, 0)])
    x2, (w1, w2, diff_win, wo) = _outproj_mlp(
        x2, o, mod3, wo, norm_g[0, 1].reshape(1, D_MODEL), w1, w2, fg,
        False, [(mlp_w1, 1), (mlp_w2, 1), (diff_w_in, 0), (diff_w_o, 0)])
    mod3 = mod_table(1)

    lambda_init = 0.8 - 0.6 * math.exp(-0.3 * 1)
    cs = _rope_tables(positions)
    q, k, vt = _inproj_diff(
        x2, mod3, norm_g[1, 0].reshape(1, D_MODEL), diff_win, cs, _rope_expander())
    o = _diff_attention(q, k, vt, diff_lambda[0], diff_subln_g[0].reshape(DIFF_DV, 1), lambda_init)
    x2, _ = _outproj_mlp(x2, o.reshape(N_TOK, D_MODEL), mod3, wo,
                         norm_g[1, 1].reshape(1, D_MODEL), w1, w2, fg, True)
    return x2.reshape(BATCH, SEQ, D_MODEL)
```

```python
import functools
import math

import jax
import jax.numpy as jnp
import numpy as np
from jax import lax
from jax.experimental import pallas as pl
from jax.experimental.pallas import tpu as pltpu

D_MODEL = 1024
BATCH = 8
SEQ = 2048
DEPTH = 2
D_FF = 4 * D_MODEL
NORM_EPS = 1e-6
GLA_HEADS = 4
GLA_DK = 128
GLA_DV = 256
GLA_HK = GLA_HEADS * GLA_DK
GLA_GATE_RANK = 16
GLA_TAU = 16.0
GLA_CHUNK = 64
DIFF_HEADS = 8
DIFF_DH = 64
DIFF_DV = 128
ROPE_THETA = 500000.0
ROPE_DIM = DIFF_DH // 4
ROPE_HALF = ROPE_DIM // 2
N_TOK = BATCH * SEQ
N_MOD = 6

TM_INPROJ = 1024
INPROJ_ROW_GROUPS = 2
TM_MLP = 1024
TM_MLP_HOST = 512
MLP_SUB_ROWS = 512
TM_GLA = 1024
GLA_ROW_GROUPS = 2
TN_MOD = 1536
ATT_TQ = 256
ATT_TK = 256
VT_ROWS = DIFF_DV + 16
ATT_HEADS = 8
VMEM_LIMIT = 56 * 1024 * 1024

F32 = jnp.float32
BF16 = jnp.bfloat16
MASKED = -1e30
NT_DIMS = (((1,), (1,)), ((), ()))
TN_DIMS = (((0,), (0,)), ((), ()))


def _dot(a, b):
    return jnp.dot(a, b, preferred_element_type=F32)


def _split_bf16(a):
    hi = a.astype(BF16)
    lo = (a - hi.astype(F32)).astype(BF16)
    return hi, lo


def _rms_scale(x):
    return lax.rsqrt(jnp.mean(x * x, axis=-1, keepdims=True) + NORM_EPS)


def _silu(x):
    return x * (1.0 / (1.0 + jnp.exp(-x)))


def _params(*sem):
    return pltpu.CompilerParams(dimension_semantics=sem, vmem_limit_bytes=VMEM_LIMIT)


def _full_spec(shape):
    zeros = (0,) * len(shape)
    return pl.BlockSpec(shape, lambda *_: zeros)


def _resident_spec(block_shape, index):
    return pl.BlockSpec(block_shape, lambda m: index, pipeline_mode=pl.Buffered(1))


def _mod_spec(k, tm):
    tiles_per_seq = SEQ // tm
    return pl.BlockSpec((None, 1, D_MODEL), lambda m, *_: ((m // tiles_per_seq) * N_MOD + k, 0, 0))


def _modulated_norm(x, g, scale, shift):
    return (x * _rms_scale(x) * g) * (1.0 + scale) + shift


def _cast_slabs(jobs, n_steps, step_of):
    flats, in_specs, out_specs, out_shapes = [], [], [], []
    for w, layer in jobs:
        _, n_rows, cols = w.shape
        slab = n_rows // n_steps
        assert slab * n_steps == n_rows and slab % 16 == 0
        first = layer * n_steps
        flats.append(w.reshape(-1, cols))
        in_specs.append(pl.BlockSpec((slab, cols), lambda *g, first=first: (first + step_of(*g), 0)))
        out_specs.append(pl.BlockSpec((slab, cols), lambda *g: (step_of(*g), 0)))
        out_shapes.append(jax.ShapeDtypeStruct((n_rows, cols), BF16))
    return flats, in_specs, out_specs, out_shapes


def _mod_block(c_ref, w_ref, b_ref, o_ref):
    c_hi, c_lo = _split_bf16(_silu(c_ref[...]))
    w_hi, w_lo = _split_bf16(w_ref[...])
    lhs = jnp.concatenate([c_hi, c_lo], axis=0)
    r = _dot(lhs, w_hi)
    o_ref[...] = r[:BATCH] + r[BATCH:] + _dot(c_hi, w_lo) + b_ref[...]


def _adaln_mod(c, ada_w, ada_b):
    n_out = N_MOD * D_MODEL
    return pl.pallas_call(
        _mod_block,
        grid=(DEPTH, n_out // TN_MOD),
        in_specs=[
            _full_spec((BATCH, D_MODEL)),
            pl.BlockSpec((None, D_MODEL, TN_MOD), lambda i, j: (i, 0, j)),
            pl.BlockSpec((None, 1, TN_MOD), lambda i, j: (i, 0, j)),
        ],
        out_specs=pl.BlockSpec((None, BATCH, TN_MOD), lambda i, j: (i, 0, j)),
        out_shape=jax.ShapeDtypeStruct((DEPTH, BATCH, n_out), F32),
        compiler_params=_params("arbitrary", "arbitrary"),
        name="adaln_mod",
    )(c, ada_w, ada_b.reshape(DEPTH, 1, n_out))


def _gla_layer_kernel(x_ref, shift_ref, scale_ref, g_ref, wqk_ref, wv_ref, wr_ref, wa_ref,
                      wa2_ref, ba_ref, br_ref, ng_ref, *rest, n_cast, tiles_per_seq):
    cast_in, rest = rest[:n_cast], rest[n_cast:]
    o_ref, cast_out = rest[0], rest[1:1 + n_cast]
    wbf_ref, st_ref, qd_ref, ki_ref, v_ref, gate_ref, dec_ref = rest[1 + n_cast:]
    for src, dst in zip(cast_in, cast_out):
        dst[...] = src[...].astype(BF16)

    @pl.when(pl.program_id(0) == 0)
    def _():
        for i, w_ref in enumerate((wqk_ref, wv_ref, wr_ref)):
            wbf_ref[i] = w_ref[...].T.astype(BF16)

    @pl.when(pl.program_id(0) % tiles_per_seq == 0)
    def _():
        st_ref[...] = jnp.zeros_like(st_ref)

    wqk_ref, wv_ref, wr_ref = wbf_ref.at[0], wbf_ref.at[1], wbf_ref.at[2]
    row = lax.broadcasted_iota(jnp.int32, (GLA_CHUNK, GLA_CHUNK), 0)
    col = lax.broadcasted_iota(jnp.int32, (GLA_CHUNK, GLA_CHUNK), 1)
    causal = row >= col
    tril = jnp.where(causal, 1.0, 0.0).astype(BF16)
    w_hi, w_lo = _split_bf16(wa2_ref[...])
    w_gate = jnp.concatenate([w_hi, w_hi, w_lo], axis=0)
    q_scale = GLA_DK ** -0.5
    group_rows = x_ref.shape[0] // GLA_ROW_GROUPS
    chunks_per_group = group_rows // GLA_CHUNK
    chunk = lambda n: slice(n * GLA_CHUNK, (n + 1) * GLA_CHUNK)
    kcols = lambda h: slice(h * GLA_DK, (h + 1) * GLA_DK)
    vcols = lambda h: slice(h * GLA_DV, (h + 1) * GLA_DV)

    def phase_a(grp):
        rows = slice(grp * group_rows, (grp + 1) * group_rows)
        hb = _modulated_norm(x_ref[rows, :], g_ref[...], scale_ref[...],
                             shift_ref[...]).astype(BF16)
        a_hi, a_lo = _split_bf16(lax.dot_general(
            hb, wa_ref[...].astype(BF16), NT_DIMS, preferred_element_type=F32))
        r = _dot(hb, wr_ref[...]) + br_ref[...]
        gate_ref[rows, :] = _silu(r).astype(BF16)
        la = _dot(jnp.concatenate([a_hi, a_lo, a_hi], axis=1), w_gate) + ba_ref[...]
        qk = _dot(hb, wqk_ref[...])
        log_a = (jnp.minimum(la, 0.0) - jnp.log(1.0 + jnp.exp(-jnp.abs(la)))) * (1.0 / GLA_TAU)
        l_hi = log_a.astype(BF16)
        cums = [_dot(tril, l_hi[chunk(i)]) for i in range(chunks_per_group)]
        v_ref[rows, :] = _dot(hb, wv_ref[...]).astype(BF16)
        for i in range(chunks_per_group):
            n = grp * chunks_per_group + i
            b = cums[i]
            b_last = b[GLA_CHUNK - 1:GLA_CHUNK, :]
            qd_ref[chunk(n), :] = (qk[chunk(i), :GLA_HK] * q_scale * jnp.exp(b)).astype(BF16)
            ki_ref[chunk(n), :] = (qk[chunk(i), GLA_HK:] * jnp.exp(-b)).astype(BF16)
            dec_ref[n] = jnp.exp(b_last)

    def phase_b(grp):
        chunks = range(grp * chunks_per_group, (grp + 1) * chunks_per_group)
        work = [(h, n) for h in range(GLA_HEADS) for n in chunks]
        scores = {}
        updates = {}
        for h, n in work:
            scores[h, n] = lax.dot_general(qd_ref[chunk(n), kcols(h)], ki_ref[chunk(n), kcols(h)],
                                           NT_DIMS, preferred_element_type=F32)
            updates[h, n] = lax.dot_general(ki_ref[chunk(n), kcols(h)], v_ref[chunk(n), vcols(h)],
                                            TN_DIMS, preferred_element_type=F32)
        states = {}
        for h in range(GLA_HEADS):
            st = st_ref[h]
            for n in chunks:
                states[h, n] = st.astype(BF16)
                dec_col = jnp.broadcast_to(dec_ref[n][:, kcols(h)], (8, GLA_DK)).T[:, 0:1]
                st = (st + updates[h, n]) * dec_col
            st_ref[h] = st
        for h, n in work:
            s = jnp.where(causal, scores[h, n], 0.0).astype(BF16)
            o = _dot(jnp.concatenate([qd_ref[chunk(n), kcols(h)], s], axis=1),
                     jnp.concatenate([states[h, n], v_ref[chunk(n), vcols(h)]], axis=0))
            y = o * _rms_scale(o) * ng_ref[...]
            o_ref[chunk(n), vcols(h)] = (y * gate_ref[chunk(n), vcols(h)].astype(F32)).astype(BF16)

    for grp in range(GLA_ROW_GROUPS):
        phase_a(grp)
        phase_b(grp)


def _gla_layer(x2, mod3, g, w_in_t, wa2, ba, br, ng, cast_jobs):
    tm = TM_GLA
    n_steps = N_TOK // tm
    row = lambda m: (m, 0)
    assert 2 * GLA_HK == D_MODEL
    row_block = lambda idx: _resident_spec((D_MODEL, D_MODEL), (idx, 0))
    flats, cast_in, cast_out, cast_shapes = _cast_slabs(cast_jobs, n_steps, lambda m: m)
    out = pl.pallas_call(
        functools.partial(_gla_layer_kernel, n_cast=len(flats), tiles_per_seq=SEQ // tm),
        grid=(n_steps,),
        in_specs=[
            pl.BlockSpec((tm, D_MODEL), row),
            _mod_spec(0, tm),
            _mod_spec(1, tm),
            _full_spec((1, D_MODEL)),
            row_block(0),
            row_block(1),
            row_block(2),
            _resident_spec((GLA_GATE_RANK, D_MODEL), (3 * D_MODEL // GLA_GATE_RANK, 0)),
            _full_spec((GLA_GATE_RANK, GLA_HK)),
            _full_spec((1, GLA_HK)),
            _full_spec((1, D_MODEL)),
            _full_spec((1, GLA_DV)),
        ] + cast_in,
        out_specs=[pl.BlockSpec((tm, D_MODEL), row)] + cast_out,
        out_shape=[jax.ShapeDtypeStruct((N_TOK, D_MODEL), BF16)] + cast_shapes,
        scratch_shapes=[
            pltpu.VMEM((3, D_MODEL, D_MODEL), BF16),
            pltpu.VMEM((GLA_HEADS, GLA_DK, GLA_DV), F32),
            pltpu.VMEM((tm, GLA_HK), BF16),
            pltpu.VMEM((tm, GLA_HK), BF16),
            pltpu.VMEM((tm, D_MODEL), BF16),
            pltpu.VMEM((tm, D_MODEL), BF16),
            pltpu.VMEM((tm // GLA_CHUNK, 1, GLA_HK), F32),
        ],
        compiler_params=_params("arbitrary"),
        name="gla_layer",
    )(x2, mod3, mod3, g, w_in_t, w_in_t, w_in_t, w_in_t, wa2, ba, br, ng, *flats)
    return out[0], out[1:]


def _rope_kernel(pos_ref, freq_ref, cs_ref):
    ang = freq_ref[...] * pos_ref[...]
    cs_ref[0:ROPE_HALF, :] = jnp.cos(ang)
    cs_ref[ROPE_HALF:, :] = jnp.sin(ang)


def _rope_tables(positions):
    inv_freq = ROPE_THETA ** (-jnp.arange(0, ROPE_DIM, 2, dtype=F32) / ROPE_DIM)
    return pl.pallas_call(
        _rope_kernel,
        grid=(1,),
        in_specs=[_full_spec((1, N_TOK)), _full_spec((ROPE_HALF, 1))],
        out_specs=_full_spec((ROPE_DIM, N_TOK)),
        out_shape=jax.ShapeDtypeStruct((ROPE_DIM, N_TOK), F32),
        compiler_params=_params("arbitrary"),
        name="rope_tables",
    )(positions.astype(F32).reshape(1, N_TOK), inv_freq.reshape(ROPE_HALF, 1))


def _rope_expander():
    e = np.zeros((3, ROPE_DIM, 2 * DIFF_DV), np.float32)
    for lane in range(DIFF_DV):
        d = lane % DIFF_DH
        if d < ROPE_DIM:
            e[:, d % ROPE_HALF, lane] = 1.0
            e[:, ROPE_HALF + d % ROPE_HALF, DIFF_DV + lane] = -1.0 if d < ROPE_HALF else 1.0
    return jnp.asarray(e.reshape(3 * ROPE_DIM, 2 * DIFF_DV), BF16)


def _inproj_diff_kernel(x_ref, shift_ref, scale_ref, g_ref, wq_ref, wk_ref, wv_ref,
                        cs_ref, e_ref, q_ref, k_ref, vt_ref, wvt_ref):
    @pl.when(pl.program_id(0) == 0)
    def _():
        wvt_ref[...] = wv_ref[...].T

    d = lax.broadcasted_iota(jnp.int32, (1, DIFF_DV), 1) & (DIFF_DH - 1)
    first_half = d < ROPE_HALF
    q_scale = DIFF_DH ** -0.5 * math.log2(math.e)
    group_rows = x_ref.shape[0] // INPROJ_ROW_GROUPS

    for grp in range(INPROJ_ROW_GROUPS):
        rows = slice(grp * group_rows, (grp + 1) * group_rows)
        hb = _modulated_norm(x_ref[rows, :], g_ref[...], scale_ref[...],
                             shift_ref[...]).astype(BF16)

        cs = cs_ref[:, rows]
        hi = cs.astype(BF16).astype(F32)
        mid = (cs - hi).astype(BF16).astype(F32)
        lo = cs - hi - mid
        parts = jnp.concatenate([hi, mid, lo], axis=0)
        tab = _dot(parts.T.astype(BF16), e_ref[...])
        ct = tab[:, :DIFF_DV] + jnp.where(d >= ROPE_DIM, 1.0, 0.0)
        s12 = tab[:, DIFF_DV:]

        for w_ref, scl, is_q in ((wq_ref, q_scale, True), (wk_ref, 1.0, False)):
            z = _dot(hb, w_ref[...])
            for h in range(DIFF_HEADS):
                sl = slice(h * DIFF_DV, (h + 1) * DIFF_DV)
                xh = z[:, sl]
                partner = jnp.where(first_half, pltpu.roll(xh, DIFF_DV - ROPE_HALF, 1),
                                    pltpu.roll(xh, ROPE_HALF, 1))
                roped = ((xh * ct + partner * s12) * scl).astype(BF16)
                if is_q:
                    q_ref[rows, sl] = roped
                else:
                    k_ref[h, rows, :] = roped
        vt = lax.dot_general(wvt_ref[...], hb, NT_DIMS, preferred_element_type=F32).astype(BF16)
        ones = jnp.ones((VT_ROWS - DIFF_DV, ATT_TK), BF16)
        for blk in range(group_rows // ATT_TK):
            dst = grp * (group_rows // ATT_TK) + blk
            cols = slice(blk * ATT_TK, (blk + 1) * ATT_TK)
            for h in range(DIFF_HEADS):
                vt_ref[dst, h * VT_ROWS:h * VT_ROWS + DIFF_DV, :] = vt[h * DIFF_DV:(h + 1) * DIFF_DV,
                                                                        cols]
                vt_ref[dst, h * VT_ROWS + DIFF_DV:(h + 1) * VT_ROWS, :] = ones


def _inproj_diff(x2, mod3, g, w_in, cs, expander):
    tm = TM_INPROJ
    row = lambda m: (m, 0)
    o_spec = pl.BlockSpec((tm, D_MODEL), row)
    return pl.pallas_call(
        _inproj_diff_kernel,
        grid=(N_TOK // tm,),
        in_specs=[
            pl.BlockSpec((tm, D_MODEL), row),
            _mod_spec(0, tm),
            _mod_spec(1, tm),
            _full_spec((1, D_MODEL)),
            _resident_spec((D_MODEL, D_MODEL), (0, 0)),
            _resident_spec((D_MODEL, D_MODEL), (0, 1)),
            _resident_spec((D_MODEL, D_MODEL), (0, 2)),
            pl.BlockSpec((ROPE_DIM, tm), lambda m: (0, m)),
            _full_spec((3 * ROPE_DIM, 2 * DIFF_DV)),
        ],
        out_specs=[o_spec,
                   pl.BlockSpec((DIFF_HEADS, tm, DIFF_DV), lambda m: (0, m, 0)),
                   pl.BlockSpec((tm // ATT_TK, DIFF_HEADS * VT_ROWS, ATT_TK), lambda m: (m, 0, 0))],
        out_shape=[jax.ShapeDtypeStruct((N_TOK, D_MODEL), BF16),
                   jax.ShapeDtypeStruct((DIFF_HEADS, N_TOK, DIFF_DV), BF16),
                   jax.ShapeDtypeStruct((N_TOK // ATT_TK, DIFF_HEADS * VT_ROWS, ATT_TK), BF16)],
        scratch_shapes=[pltpu.VMEM((D_MODEL, D_MODEL), BF16)],
        compiler_params=_params("arbitrary"),
        name="inproj_diff",
    )(x2, mod3, mod3, g, w_in, w_in, w_in, cs, expander)


def _attn_kernel(q_ref, k_ref, vt_ref, lam_ref, g_ref, o_ref, qs_ref, s_ref, m_ref, acc_ref,
                 *, lambda_init):
    tq, tk = ATT_TQ, ATT_TK
    assert tq == tk
    lf = lam_ref[...]
    lam = (jnp.exp(jnp.sum(lf[0:1] * lf[1:2], axis=-1, keepdims=True))
           - jnp.exp(jnp.sum(lf[2:3] * lf[3:4], axis=-1, keepdims=True)) + lambda_init)
    feat = lax.broadcasted_iota(jnp.int32, (DIFF_DV, tq), 0)
    k_pos = lax.broadcasted_iota(jnp.int32, (tk, 2 * tq), 0)
    q_pos = lax.broadcasted_iota(jnp.int32, (tk, 2 * tq), 1) & (tq - 1)
    heads = [slice(h * DIFF_DV, (h + 1) * DIFF_DV) for h in range(ATT_HEADS)]

    def stage_queries(qi, slot):
        q0 = pl.multiple_of(qi * tq, tq)
        for h, hs in enumerate(heads):
            qt = q_ref[pl.ds(q0, tq), hs].astype(F32).T
            qs_ref[slot, h, :, 0:tq] = jnp.where(feat < DIFF_DH, qt, 0.0).astype(BF16)
            qs_ref[slot, h, :, tq:2 * tq] = jnp.where(feat >= DIFF_DH, qt, 0.0).astype(BF16)

    def scores_for(j, h, slot):
        start = pl.multiple_of(j * tk, tk)
        return _dot(k_ref[h, pl.ds(start, tk), :], qs_ref[slot, h])

    def reset_state():
        m_ref[...] = jnp.full_like(m_ref, MASKED)

    def kv_step(j, masked, next_j, next_slot):
        for h in range(ATT_HEADS):
            vtb = vt_ref[j, h * VT_ROWS:(h + 1) * VT_ROWS, :]
            s = s_ref[h]
            next_s = scores_for(next_j, h, next_slot)
            if masked:
                s = jnp.where(k_pos <= q_pos, s, MASKED)
            m_prev = m_ref[h]
            m_new = jnp.maximum(m_prev, jnp.max(s, axis=0, keepdims=True))
            p = jnp.exp2(s - m_new)
            alpha = jnp.exp2(m_prev - m_new)
            acc_ref[h] = alpha * acc_ref[h] + _dot(vtb, p.astype(BF16))
            m_ref[h] = m_new
            s_ref[h] = next_s

    n_q = SEQ // tq
    stage_queries(0, 0)
    acc_ref[...] = jnp.zeros_like(acc_ref)
    reset_state()
    for h in range(ATT_HEADS):
        s_ref[h] = scores_for(0, h, 0)

    def q_block(qi, carry):
        q0 = pl.multiple_of(qi * tq, tq)
        slot = qi & 1

        def body(j, c):
            kv_step(j, False, j + 1, slot)
            return c

        lax.fori_loop(0, qi, body, 0)
        stage_queries(jnp.minimum(qi + 1, n_q - 1), 1 - slot)
        kv_step(qi, True, 0, 1 - slot)
        for h, hs in enumerate(heads):
            acc = acc_ref[h, 0:DIFF_DV, :]
            inv_l = 1.0 / acc_ref[h, DIFF_DV:DIFF_DV + 1, :]
            o = acc[:, :tq] * inv_l[:, :tq] - lam * (acc[:, tq:] * inv_l[:, tq:])
            rms = lax.rsqrt(jnp.mean(o * o, axis=0, keepdims=True) + NORM_EPS)
            y = o * rms * (g_ref[...] * (1.0 - lambda_init))
            o_ref[pl.ds(q0, tq), hs] = y.T.astype(BF16)
        reset_state()
        return carry

    lax.fori_loop(0, n_q, q_block, 0)


def _diff_attention(q, k, vt, lam_vecs, g, lambda_init):
    assert ATT_HEADS == DIFF_HEADS
    spec = pl.BlockSpec((None, SEQ, D_MODEL), lambda b, h: (b, 0, 0))
    return pl.pallas_call(
        functools.partial(_attn_kernel, lambda_init=lambda_init),
        grid=(BATCH, 1),
        in_specs=[spec,
                  pl.BlockSpec((DIFF_HEADS, SEQ, DIFF_DV), lambda b, h: (0, b, 0)),
                  pl.BlockSpec((SEQ // ATT_TK, DIFF_HEADS * VT_ROWS, ATT_TK), lambda b, h: (b, 0, 0)),
                  _full_spec((4, DIFF_DH)), _full_spec((DIFF_DV, 1))],
        out_specs=spec,
        out_shape=jax.ShapeDtypeStruct((BATCH, SEQ, D_MODEL), BF16),
        scratch_shapes=[
            pltpu.VMEM((2, ATT_HEADS, DIFF_DV, 2 * ATT_TQ), BF16),
            pltpu.VMEM((ATT_HEADS, ATT_TK, 2 * ATT_TQ), F32),
            pltpu.VMEM((ATT_HEADS, 1, 2 * ATT_TQ), F32),
            pltpu.VMEM((ATT_HEADS, VT_ROWS, 2 * ATT_TQ), F32),
        ],
        compiler_params=_params("arbitrary", "arbitrary"),
        name="diff_attention",
    )(q.reshape(BATCH, SEQ, D_MODEL), k, vt, lam_vecs, g)


def _mlp_kernel(x_ref, o_ref, wo_ref, gt_ref, sh_ref, sc_ref, gc_ref, g_ref, w1_ref, w2_ref,
                fg_ref, *rest, final, n_cast):
    cast_in, (out_ref, *cast_out) = rest[:n_cast], rest[n_cast:]
    for src, dst in zip(cast_in, cast_out):
        dst[...] = src[...].astype(BF16)

    def rows_block(rows):
        x1 = x_ref[rows, :] + gt_ref[...] * _dot(o_ref[rows, :], wo_ref[...])
        hb = _modulated_norm(x1, g_ref[...], sc_ref[...], sh_ref[...]).astype(BF16)
        a = jnp.square(jnp.maximum(_dot(hb, w1_ref[...]), 0.0)).astype(BF16)
        out = x1 + gc_ref[...] * _dot(a, w2_ref[...])
        if final:
            out = out * _rms_scale(out) * fg_ref[...]
        out_ref[rows, :] = out

    n_sub = x_ref.shape[0] // MLP_SUB_ROWS
    if n_sub == 1:
        rows_block(slice(None))
    else:
        def body(i, carry):
            rows_block(pl.ds(pl.multiple_of(i * MLP_SUB_ROWS, MLP_SUB_ROWS), MLP_SUB_ROWS))
            return carry
        lax.fori_loop(0, n_sub, body, 0)


def _outproj_mlp(x2, o2, mod3, wo, g, w1, w2, fg, final, cast_jobs=()):
    tm = TM_MLP_HOST if cast_jobs else TM_MLP
    n_steps = N_TOK // tm
    row = lambda m: (m, 0)
    flats, cast_in, cast_out, cast_shapes = _cast_slabs(cast_jobs, n_steps, lambda m: m)
    out = pl.pallas_call(
        functools.partial(_mlp_kernel, final=final, n_cast=len(flats)),
        grid=(n_steps,),
        in_specs=[
            pl.BlockSpec((tm, D_MODEL), row),
            pl.BlockSpec((tm, D_MODEL), row),
            _resident_spec((D_MODEL, D_MODEL), (0, 0)),
            _mod_spec(2, tm),
            _mod_spec(3, tm),
            _mod_spec(4, tm),
            _mod_spec(5, tm),
            _full_spec((1, D_MODEL)),
            _resident_spec((D_MODEL, D_FF), (0, 0)),
            _resident_spec((D_FF, D_MODEL), (0, 0)),
            _full_spec((1, D_MODEL)),
        ] + cast_in,
        out_specs=[pl.BlockSpec((tm, D_MODEL), row)] + cast_out,
        out_shape=[jax.ShapeDtypeStruct((N_TOK, D_MODEL), F32)] + cast_shapes,
        compiler_params=_params("arbitrary"),
        name="outproj_mlp_final" if final else "outproj_mlp",
    )(x2, o2, wo, mod3, mod3, mod3, mod3, g, w1, w2, fg, *flats)
    return out[0], out[1:]


def kernel(x, c, positions, ada_w, ada_b, norm_g, mlp_w1, mlp_w2, gla_w_in, gla_w_a2, gla_b_a,
           gla_b_r, gla_norm_g, gla_w_o, diff_w_in, diff_lambda, diff_subln_g, diff_w_o, final_g):
    x2 = x.reshape(N_TOK, D_MODEL)
    mod = _adaln_mod(c, ada_w, ada_b)
    mod_table = lambda layer: mod[layer].reshape(BATCH * N_MOD, 1, D_MODEL)
    mod3 = mod_table(0)
    fg = final_g.reshape(1, D_MODEL)

    o, (w1, w2, wo) = _gla_layer(
        x2, mod3, norm_g[0, 0].reshape(1, D_MODEL), gla_w_in[0].T, gla_w_a2[0],
        gla_b_a[0].reshape(1, GLA_HK), gla_b_r[0].reshape(1, D_MODEL),
        gla_norm_g[0].reshape(1, GLA_DV), [(mlp_w1, 0), (mlp_w2, 0), (gla_w_o, 0)])
    x2, (w1, w2, diff_win, wo) = _outproj_mlp(
        x2, o, mod3, wo, norm_g[0, 1].reshape(1, D_MODEL), w1, w2, fg,
        False, [(mlp_w1, 1), (mlp_w2, 1), (diff_w_in, 0), (diff_w_o, 0)])
    mod3 = mod_table(1)

    lambda_init = 0.8 - 0.6 * math.exp(-0.3 * 1)
    cs = _rope_tables(positions)
    q, k, vt = _inproj_diff(
        x2, mod3, norm_g[1, 0].reshape(1, D_MODEL), diff_win, cs, _rope_expander())
    o = _diff_attention(q, k, vt, diff_lambda[0], diff_subln_g[0].reshape(DIFF_DV, 1), lambda_init)
    x2, _ = _outproj_mlp(x2, o.reshape(N_TOK, D_MODEL), mod3, wo,
                         norm_g[1, 1].reshape(1, D_MODEL), w1, w2, fg, True)
    return x2.reshape(BATCH, SEQ, D_MODEL)
```

```python
import functools
import math

import jax
import jax.numpy as jnp
import numpy as np
from jax import lax
from jax.experimental import pallas as pl
from jax.experimental.pallas import tpu as pltpu

D_MODEL = 1024
BATCH = 8
SEQ = 2048
DEPTH = 2
D_FF = 4 * D_MODEL
NORM_EPS = 1e-6
GLA_HEADS = 4
GLA_DK = 128
GLA_DV = 256
GLA_HK = GLA_HEADS * GLA_DK
GLA_GATE_RANK = 16
GLA_TAU = 16.0
GLA_CHUNK = 64
DIFF_HEADS = 8
DIFF_DH = 64
DIFF_DV = 128
ROPE_THETA = 500000.0
ROPE_DIM = DIFF_DH // 4
ROPE_HALF = ROPE_DIM // 2
N_TOK = BATCH * SEQ
N_MOD = 6

TM_INPROJ = 1024
INPROJ_ROW_GROUPS = 2
TM_MLP = 1024
TM_MLP_HOST = 512
MLP_SUB_ROWS = 512
TM_GLA = 1024
GLA_ROW_GROUPS = 2
TN_MOD = 1536
ATT_TQ = 256
ATT_TK = 256
VT_ROWS = DIFF_DV + 16
ATT_HEADS = 8
VMEM_LIMIT = 56 * 1024 * 1024

F32 = jnp.float32
BF16 = jnp.bfloat16
MASKED = -1e30
NT_DIMS = (((1,), (1,)), ((), ()))
TN_DIMS = (((0,), (0,)), ((), ()))


def _dot(a, b):
    return jnp.dot(a, b, preferred_element_type=F32)


def _split_bf16(a):
    hi = a.astype(BF16)
    lo = (a - hi.astype(F32)).astype(BF16)
    return hi, lo


def _rms_scale(x):
    return lax.rsqrt(jnp.mean(x * x, axis=-1, keepdims=True) + NORM_EPS)


def _silu(x):
    return x * (1.0 / (1.0 + jnp.exp(-x)))


def _params(*sem):
    return pltpu.CompilerParams(dimension_semantics=sem, vmem_limit_bytes=VMEM_LIMIT)


def _full_spec(shape):
    zeros = (0,) * len(shape)
    return pl.BlockSpec(shape, lambda *_: zeros)


def _resident_spec(block_shape, index):
    return pl.BlockSpec(block_shape, lambda m: index, pipeline_mode=pl.Buffered(1))


def _mod_spec(k, tm):
    tiles_per_seq = SEQ // tm
    return pl.BlockSpec((None, 1, D_MODEL), lambda m, *_: ((m // tiles_per_seq) * N_MOD + k, 0, 0))


def _modulated_norm(x, g, scale, shift):
    return (x * _rms_scale(x) * g) * (1.0 + scale) + shift


def _cast_slabs(jobs, n_steps, step_of):
    flats, in_specs, out_specs, out_shapes = [], [], [], []
    for w, layer in jobs:
        _, n_rows, cols = w.shape
        slab = n_rows // n_steps
        assert slab * n_steps == n_rows and slab % 16 == 0
        first = layer * n_steps
        flats.append(w.reshape(-1, cols))
        in_specs.append(pl.BlockSpec((slab, cols), lambda *g, first=first: (first + step_of(*g), 0)))
        out_specs.append(pl.BlockSpec((slab, cols), lambda *g: (step_of(*g), 0)))
        out_shapes.append(jax.ShapeDtypeStruct((n_rows, cols), BF16))
    return flats, in_specs, out_specs, out_shapes


def _mod_block(c_ref, w_ref, b_ref, o_ref):
    c_hi, c_lo = _split_bf16(_silu(c_ref[...]))
    w_hi, w_lo = _split_bf16(w_ref[...])
    lhs = jnp.concatenate([c_hi, c_lo], axis=0)
    r = _dot(lhs, w_hi)
    o_ref[...] = r[:BATCH] + r[BATCH:] + _dot(c_hi, w_lo) + b_ref[...]


def _adaln_mod(c, ada_w, ada_b):
    n_out = N_MOD * D_MODEL
    return pl.pallas_call(
        _mod_block,
        grid=(DEPTH, n_out // TN_MOD),
        in_specs=[
            _full_spec((BATCH, D_MODEL)),
            pl.BlockSpec((None, D_MODEL, TN_MOD), lambda i, j: (i, 0, j)),
            pl.BlockSpec((None, 1, TN_MOD), lambda i, j: (i, 0, j)),
        ],
        out_specs=pl.BlockSpec((None, BATCH, TN_MOD), lambda i, j: (i, 0, j)),
        out_shape=jax.ShapeDtypeStruct((DEPTH, BATCH, n_out), F32),
        compiler_params=_params("arbitrary", "arbitrary"),
        name="adaln_mod",
    )(c, ada_w, ada_b.reshape(DEPTH, 1, n_out))


def _gla_layer_kernel(x_ref, shift_ref, scale_ref, g_ref, wqk_ref, wv_ref, wr_ref, wa_ref,
                      wa2_ref, ba_ref, br_ref, ng_ref, *rest, n_cast, tiles_per_seq):
    cast_in, rest = rest[:n_cast], rest[n_cast:]
    o_ref, cast_out = rest[0], rest[1:1 + n_cast]
    wbf_ref, st_ref, qd_ref, ki_ref, v_ref, gate_ref, dec_ref = rest[1 + n_cast:]
    for src, dst in zip(cast_in, cast_out):
        dst[...] = src[...].astype(BF16)

    @pl.when(pl.program_id(0) == 0)
    def _():
        for i, w_ref in enumerate((wqk_ref, wv_ref, wr_ref)):
            wbf_ref[i] = w_ref[...].T.astype(BF16)

    @pl.when(pl.program_id(0) % tiles_per_seq == 0)
    def _():
        st_ref[...] = jnp.zeros_like(st_ref)

    wqk_ref, wv_ref, wr_ref = wbf_ref.at[0], wbf_ref.at[1], wbf_ref.at[2]
    row = lax.broadcasted_iota(jnp.int32, (GLA_CHUNK, GLA_CHUNK), 0)
    col = lax.broadcasted_iota(jnp.int32, (GLA_CHUNK, GLA_CHUNK), 1)
    causal = row >= col
    tril = jnp.where(causal, 1.0, 0.0).astype(BF16)
    w_hi, w_lo = _split_bf16(wa2_ref[...])
    w_gate = jnp.concatenate([w_hi, w_hi, w_lo], axis=0)
    q_scale = GLA_DK ** -0.5
    group_rows = x_ref.shape[0] // GLA_ROW_GROUPS
    chunks_per_group = group_rows // GLA_CHUNK
    chunk = lambda n: slice(n * GLA_CHUNK, (n + 1) * GLA_CHUNK)
    kcols = lambda h: slice(h * GLA_DK, (h + 1) * GLA_DK)
    vcols = lambda h: slice(h * GLA_DV, (h + 1) * GLA_DV)

    def phase_a(grp):
        rows = slice(grp * group_rows, (grp + 1) * group_rows)
        hb = _modulated_norm(x_ref[rows, :], g_ref[...], scale_ref[...],
                             shift_ref[...]).astype(BF16)
        a_hi, a_lo = _split_bf16(lax.dot_general(
            hb, wa_ref[...].astype(BF16), NT_DIMS, preferred_element_type=F32))
        gate = _silu(_dot(hb, wr_ref[...]) + br_ref[...]).astype(BF16)
        for h in range(GLA_HEADS):
            gate_ref[h, rows, :] = gate[:, vcols(h)]
        la = _dot(jnp.concatenate([a_hi, a_lo, a_hi], axis=1), w_gate) + ba_ref[...]
        qk = _dot(hb, wqk_ref[...])
        log_a = (jnp.minimum(la, 0.0) - jnp.log(1.0 + jnp.exp(-jnp.abs(la)))) * (1.0 / GLA_TAU)
        l_hi = log_a.astype(BF16)
        cums = [_dot(tril, l_hi[chunk(i)]) for i in range(chunks_per_group)]
        v = _dot(hb, wv_ref[...]).astype(BF16)
        for h in range(GLA_HEADS):
            v_ref[h, rows, :] = v[:, vcols(h)]
        for i in range(chunks_per_group):
            n = grp * chunks_per_group + i
            b = cums[i]
            b_last = b[GLA_CHUNK - 1:GLA_CHUNK, :]
            qd = (qk[chunk(i), :GLA_HK] * q_scale * jnp.exp(b)).astype(BF16)
            ki = (qk[chunk(i), GLA_HK:] * jnp.exp(-b)).astype(BF16)
            for h in range(GLA_HEADS):
                qd_ref[h, chunk(n), :] = qd[:, kcols(h)]
                ki_ref[h, chunk(n), :] = ki[:, kcols(h)]
            dec_ref[n] = jnp.exp(b_last)

    def phase_b(grp):
        chunks = range(grp * chunks_per_group, (grp + 1) * chunks_per_group)
        work = [(h, n) for h in range(GLA_HEADS) for n in chunks]
        scores = {}
        updates = {}
        for h, n in work:
            scores[h, n] = lax.dot_general(qd_ref[h, chunk(n), :], ki_ref[h, chunk(n), :],
                                           NT_DIMS, preferred_element_type=F32)
            updates[h, n] = lax.dot_general(ki_ref[h, chunk(n), :], v_ref[h, chunk(n), :],
                                            TN_DIMS, preferred_element_type=F32)
        states = {}
        for h in range(GLA_HEADS):
            st = st_ref[h]
            for n in chunks:
                states[h, n] = st.astype(BF16)
                dec_col = jnp.broadcast_to(dec_ref[n][:, kcols(h)], (8, GLA_DK)).T[:, 0:1]
                st = (st + updates[h, n]) * dec_col
            st_ref[h] = st
        for h, n in work:
            s = jnp.where(causal, scores[h, n], 0.0).astype(BF16)
            o = _dot(jnp.concatenate([qd_ref[h, chunk(n), :], s], axis=1),
                     jnp.concatenate([states[h, n], v_ref[h, chunk(n), :]], axis=0))
            y = o * _rms_scale(o) * ng_ref[...]
            o_ref[chunk(n), vcols(h)] = (y * gate_ref[h, chunk(n), :].astype(F32)).astype(BF16)

    for grp in range(GLA_ROW_GROUPS):
        phase_a(grp)
        phase_b(grp)


def _gla_layer(x2, mod3, g, w_in_t, wa2, ba, br, ng, cast_jobs):
    tm = TM_GLA
    n_steps = N_TOK // tm
    row = lambda m: (m, 0)
    assert 2 * GLA_HK == D_MODEL
    row_block = lambda idx: _resident_spec((D_MODEL, D_MODEL), (idx, 0))
    flats, cast_in, cast_out, cast_shapes = _cast_slabs(cast_jobs, n_steps, lambda m: m)
    out = pl.pallas_call(
        functools.partial(_gla_layer_kernel, n_cast=len(flats), tiles_per_seq=SEQ // tm),
        grid=(n_steps,),
        in_specs=[
            pl.BlockSpec((tm, D_MODEL), row),
            _mod_spec(0, tm),
            _mod_spec(1, tm),
            _full_spec((1, D_MODEL)),
            row_block(0),
            row_block(1),
            row_block(2),
            _resident_spec((GLA_GATE_RANK, D_MODEL), (3 * D_MODEL // GLA_GATE_RANK, 0)),
            _full_spec((GLA_GATE_RANK, GLA_HK)),
            _full_spec((1, GLA_HK)),
            _full_spec((1, D_MODEL)),
            _full_spec((1, GLA_DV)),
        ] + cast_in,
        out_specs=[pl.BlockSpec((tm, D_MODEL), row)] + cast_out,
        out_shape=[jax.ShapeDtypeStruct((N_TOK, D_MODEL), BF16)] + cast_shapes,
        scratch_shapes=[
            pltpu.VMEM((3, D_MODEL, D_MODEL), BF16),
            pltpu.VMEM((GLA_HEADS, GLA_DK, GLA_DV), F32),
            pltpu.VMEM((GLA_HEADS, tm, GLA_DK), BF16),
            pltpu.VMEM((GLA_HEADS, tm, GLA_DK), BF16),
            pltpu.VMEM((GLA_HEADS, tm, GLA_DV), BF16),
            pltpu.VMEM((GLA_HEADS, tm, GLA_DV), BF16),
            pltpu.VMEM((tm // GLA_CHUNK, 1, GLA_HK), F32),
        ],
        compiler_params=_params("arbitrary"),
        name="gla_layer",
    )(x2, mod3, mod3, g, w_in_t, w_in_t, w_in_t, w_in_t, wa2, ba, br, ng, *flats)
    return out[0], out[1:]


def _rope_kernel(pos_ref, freq_ref, cs_ref):
    ang = freq_ref[...] * pos_ref[...]
    cs_ref[0:ROPE_HALF, :] = jnp.cos(ang)
    cs_ref[ROPE_HALF:, :] = jnp.sin(ang)


def _rope_tables(positions):
    inv_freq = ROPE_THETA ** (-jnp.arange(0, ROPE_DIM, 2, dtype=F32) / ROPE_DIM)
    return pl.pallas_call(
        _rope_kernel,
        grid=(1,),
        in_specs=[_full_spec((1, N_TOK)), _full_spec((ROPE_HALF, 1))],
        out_specs=_full_spec((ROPE_DIM, N_TOK)),
        out_shape=jax.ShapeDtypeStruct((ROPE_DIM, N_TOK), F32),
        compiler_params=_params("arbitrary"),
        name="rope_tables",
    )(positions.astype(F32).reshape(1, N_TOK), inv_freq.reshape(ROPE_HALF, 1))


def _rope_expander():
    e = np.zeros((3, ROPE_DIM, 2 * DIFF_DV), np.float32)
    for lane in range(DIFF_DV):
        d = lane % DIFF_DH
        if d < ROPE_DIM:
            e[:, d % ROPE_HALF, lane] = 1.0
            e[:, ROPE_HALF + d % ROPE_HALF, DIFF_DV + lane] = -1.0 if d < ROPE_HALF else 1.0
    return jnp.asarray(e.reshape(3 * ROPE_DIM, 2 * DIFF_DV), BF16)


def _inproj_diff_kernel(x_ref, shift_ref, scale_ref, g_ref, wq_ref, wk_ref, wv_ref,
                        cs_ref, e_ref, q_ref, k_ref, vt_ref, wvt_ref):
    @pl.when(pl.program_id(0) == 0)
    def _():
        wvt_ref[...] = wv_ref[...].T

    d = lax.broadcasted_iota(jnp.int32, (1, DIFF_DV), 1) & (DIFF_DH - 1)
    first_half = d < ROPE_HALF
    q_scale = DIFF_DH ** -0.5 * math.log2(math.e)
    group_rows = x_ref.shape[0] // INPROJ_ROW_GROUPS

    for grp in range(INPROJ_ROW_GROUPS):
        rows = slice(grp * group_rows, (grp + 1) * group_rows)
        hb = _modulated_norm(x_ref[rows, :], g_ref[...], scale_ref[...],
                             shift_ref[...]).astype(BF16)

        cs = cs_ref[:, rows]
        hi = cs.astype(BF16).astype(F32)
        mid = (cs - hi).astype(BF16).astype(F32)
        lo = cs - hi - mid
        parts = jnp.concatenate([hi, mid, lo], axis=0)
        tab = _dot(parts.T.astype(BF16), e_ref[...])
        ct = tab[:, :DIFF_DV] + jnp.where(d >= ROPE_DIM, 1.0, 0.0)
        s12 = tab[:, DIFF_DV:]

        for w_ref, out_ref, scl in ((wq_ref, q_ref, q_scale), (wk_ref, k_ref, 1.0)):
            z = _dot(hb, w_ref[...])
            for h in range(DIFF_HEADS):
                xh = z[:, h * DIFF_DV:(h + 1) * DIFF_DV]
                partner = jnp.where(first_half, pltpu.roll(xh, DIFF_DV - ROPE_HALF, 1),
                                    pltpu.roll(xh, ROPE_HALF, 1))
                out_ref[h, rows, :] = ((xh * ct + partner * s12) * scl).astype(BF16)
        vt = lax.dot_general(wvt_ref[...], hb, NT_DIMS, preferred_element_type=F32).astype(BF16)
        ones = jnp.ones((VT_ROWS - DIFF_DV, ATT_TK), BF16)
        for blk in range(group_rows // ATT_TK):
            dst = grp * (group_rows // ATT_TK) + blk
            cols = slice(blk * ATT_TK, (blk + 1) * ATT_TK)
            for h in range(DIFF_HEADS):
                vt_ref[dst, h * VT_ROWS:h * VT_ROWS + DIFF_DV, :] = vt[h * DIFF_DV:(h + 1) * DIFF_DV,
                                                                        cols]
                vt_ref[dst, h * VT_ROWS + DIFF_DV:(h + 1) * VT_ROWS, :] = ones


def _inproj_diff(x2, mod3, g, w_in, cs, expander):
    tm = TM_INPROJ
    row = lambda m: (m, 0)
    h_spec = pl.BlockSpec((DIFF_HEADS, tm, DIFF_DV), lambda m: (0, m, 0))
    return pl.pallas_call(
        _inproj_diff_kernel,
        grid=(N_TOK // tm,),
        in_specs=[
            pl.BlockSpec((tm, D_MODEL), row),
            _mod_spec(0, tm),
            _mod_spec(1, tm),
            _full_spec((1, D_MODEL)),
            _resident_spec((D_MODEL, D_MODEL), (0, 0)),
            _resident_spec((D_MODEL, D_MODEL), (0, 1)),
            _resident_spec((D_MODEL, D_MODEL), (0, 2)),
            pl.BlockSpec((ROPE_DIM, tm), lambda m: (0, m)),
            _full_spec((3 * ROPE_DIM, 2 * DIFF_DV)),
        ],
        out_specs=[h_spec, h_spec,
                   pl.BlockSpec((tm // ATT_TK, DIFF_HEADS * VT_ROWS, ATT_TK), lambda m: (m, 0, 0))],
        out_shape=[jax.ShapeDtypeStruct((DIFF_HEADS, N_TOK, DIFF_DV), BF16),
                   jax.ShapeDtypeStruct((DIFF_HEADS, N_TOK, DIFF_DV), BF16),
                   jax.ShapeDtypeStruct((N_TOK // ATT_TK, DIFF_HEADS * VT_ROWS, ATT_TK), BF16)],
        scratch_shapes=[pltpu.VMEM((D_MODEL, D_MODEL), BF16)],
        compiler_params=_params("arbitrary"),
        name="inproj_diff",
    )(x2, mod3, mod3, g, w_in, w_in, w_in, cs, expander)


def _attn_kernel(q_ref, k_ref, vt_ref, lam_ref, g_ref, o_ref, qs_ref, s_ref, m_ref, acc_ref,
                 *, lambda_init):
    tq, tk = ATT_TQ, ATT_TK
    assert tq == tk
    lf = lam_ref[...]
    lam = (jnp.exp(jnp.sum(lf[0:1] * lf[1:2], axis=-1, keepdims=True))
           - jnp.exp(jnp.sum(lf[2:3] * lf[3:4], axis=-1, keepdims=True)) + lambda_init)
    feat = lax.broadcasted_iota(jnp.int32, (DIFF_DV, tq), 0)
    k_pos = lax.broadcasted_iota(jnp.int32, (tk, 2 * tq), 0)
    q_pos = lax.broadcasted_iota(jnp.int32, (tk, 2 * tq), 1) & (tq - 1)
    heads = [slice(h * DIFF_DV, (h + 1) * DIFF_DV) for h in range(ATT_HEADS)]

    def stage_queries(qi, slot):
        q0 = pl.multiple_of(qi * tq, tq)
        for h in range(ATT_HEADS):
            qt = q_ref[h, pl.ds(q0, tq), :].astype(F32).T
            qs_ref[slot, h, :, 0:tq] = jnp.where(feat < DIFF_DH, qt, 0.0).astype(BF16)
            qs_ref[slot, h, :, tq:2 * tq] = jnp.where(feat >= DIFF_DH, qt, 0.0).astype(BF16)

    def scores_for(j, h, slot):
        start = pl.multiple_of(j * tk, tk)
        return _dot(k_ref[h, pl.ds(start, tk), :], qs_ref[slot, h])

    def reset_state():
        m_ref[...] = jnp.full_like(m_ref, MASKED)

    def kv_step(j, masked, next_j, next_slot):
        for h in range(ATT_HEADS):
            vtb = vt_ref[j, h * VT_ROWS:(h + 1) * VT_ROWS, :]
            s = s_ref[h]
            next_s = scores_for(next_j, h, next_slot)
            if masked:
                s = jnp.where(k_pos <= q_pos, s, MASKED)
            m_prev = m_ref[h]
            m_new = jnp.maximum(m_prev, jnp.max(s, axis=0, keepdims=True))
            p = jnp.exp2(s - m_new)
            alpha = jnp.exp2(m_prev - m_new)
            acc_ref[h] = alpha * acc_ref[h] + _dot(vtb, p.astype(BF16))
            m_ref[h] = m_new
            s_ref[h] = next_s

    n_q = SEQ // tq
    stage_queries(0, 0)
    acc_ref[...] = jnp.zeros_like(acc_ref)
    reset_state()
    for h in range(ATT_HEADS):
        s_ref[h] = scores_for(0, h, 0)

    def q_block(qi, carry):
        q0 = pl.multiple_of(qi * tq, tq)
        slot = qi & 1

        def body(j, c):
            kv_step(j, False, j + 1, slot)
            return c

        lax.fori_loop(0, qi, body, 0)
        stage_queries(jnp.minimum(qi + 1, n_q - 1), 1 - slot)
        kv_step(qi, True, 0, 1 - slot)
        for h, hs in enumerate(heads):
            acc = acc_ref[h, 0:DIFF_DV, :]
            inv_l = 1.0 / acc_ref[h, DIFF_DV:DIFF_DV + 1, :]
            o = acc[:, :tq] * inv_l[:, :tq] - lam * (acc[:, tq:] * inv_l[:, tq:])
            rms = lax.rsqrt(jnp.mean(o * o, axis=0, keepdims=True) + NORM_EPS)
            y = o * rms * (g_ref[...] * (1.0 - lambda_init))
            o_ref[pl.ds(q0, tq), hs] = y.T.astype(BF16)
        reset_state()
        return carry

    lax.fori_loop(0, n_q, q_block, 0)


def _diff_attention(q, k, vt, lam_vecs, g, lambda_init):
    assert ATT_HEADS == DIFF_HEADS
    h_spec = pl.BlockSpec((DIFF_HEADS, SEQ, DIFF_DV), lambda b, h: (0, b, 0))
    return pl.pallas_call(
        functools.partial(_attn_kernel, lambda_init=lambda_init),
        grid=(BATCH, 1),
        in_specs=[h_spec, h_spec,
                  pl.BlockSpec((SEQ // ATT_TK, DIFF_HEADS * VT_ROWS, ATT_TK), lambda b, h: (b, 0, 0)),
                  _full_spec((4, DIFF_DH)), _full_spec((DIFF_DV, 1))],
        out_specs=pl.BlockSpec((None, SEQ, D_MODEL), lambda b, h: (b, 0, 0)),
        out_shape=jax.ShapeDtypeStruct((BATCH, SEQ, D_MODEL), BF16),
        scratch_shapes=[
            pltpu.VMEM((2, ATT_HEADS, DIFF_DV, 2 * ATT_TQ), BF16),
            pltpu.VMEM((ATT_HEADS, ATT_TK, 2 * ATT_TQ), F32),
            pltpu.VMEM((ATT_HEADS, 1, 2 * ATT_TQ), F32),
            pltpu.VMEM((ATT_HEADS, VT_ROWS, 2 * ATT_TQ), F32),
        ],
        compiler_params=_params("arbitrary", "arbitrary"),
        name="diff_attention",
    )(q, k, vt, lam_vecs, g)


def _mlp_kernel(x_ref, o_ref, wo_ref, gt_ref, sh_ref, sc_ref, gc_ref, g_ref, w1_ref, w2_ref,
                fg_ref, *rest, final, n_cast):
    cast_in, (out_ref, *cast_out) = rest[:n_cast], rest[n_cast:]
    for src, dst in zip(cast_in, cast_out):
        dst[...] = src[...].astype(BF16)

    def rows_block(rows):
        x1 = x_ref[rows, :] + gt_ref[...] * _dot(o_ref[rows, :], wo_ref[...])
        hb = _modulated_norm(x1, g_ref[...], sc_ref[...], sh_ref[...]).astype(BF16)
        a = jnp.square(jnp.maximum(_dot(hb, w1_ref[...]), 0.0)).astype(BF16)
        out = x1 + gc_ref[...] * _dot(a, w2_ref[...])
        if final:
            out = out * _rms_scale(out) * fg_ref[...]
        out_ref[rows, :] = out

    n_sub = x_ref.shape[0] // MLP_SUB_ROWS
    if n_sub == 1:
        rows_block(slice(None))
    else:
        def body(i, carry):
            rows_block(pl.ds(pl.multiple_of(i * MLP_SUB_ROWS, MLP_SUB_ROWS), MLP_SUB_ROWS))
            return carry
        lax.fori_loop(0, n_sub, body, 0)


def _outproj_mlp(x2, o2, mod3, wo, g, w1, w2, fg, final, cast_jobs=()):
    tm = TM_MLP_HOST if cast_jobs else TM_MLP
    n_steps = N_TOK // tm
    row = lambda m: (m, 0)
    flats, cast_in, cast_out, cast_shapes = _cast_slabs(cast_jobs, n_steps, lambda m: m)
    out = pl.pallas_call(
        functools.partial(_mlp_kernel, final=final, n_cast=len(flats)),
        grid=(n_steps,),
        in_specs=[
            pl.BlockSpec((tm, D_MODEL), row),
            pl.BlockSpec((tm, D_MODEL), row),
            _resident_spec((D_MODEL, D_MODEL), (0, 0)),
            _mod_spec(2, tm),
            _mod_spec(3, tm),
            _mod_spec(4, tm),
            _mod_spec(5, tm),
            _full_spec((1, D_MODEL)),
            _resident_spec((D_MODEL, D_FF), (0, 0)),
            _resident_spec((D_FF, D_MODEL), (0, 0)),
            _full_spec((1, D_MODEL)),
        ] + cast_in,
        out_specs=[pl.BlockSpec((tm, D_MODEL), row)] + cast_out,
        out_shape=[jax.ShapeDtypeStruct((N_TOK, D_MODEL), F32)] + cast_shapes,
        compiler_params=_params("arbitrary"),
        name="outproj_mlp_final" if final else "outproj_mlp",
    )(x2, o2, wo, mod3, mod3, mod3, mod3, g, w1, w2, fg, *flats)
    return out[0], out[1:]


def kernel(x, c, positions, ada_w, ada_b, norm_g, mlp_w1, mlp_w2, gla_w_in, gla_w_a2, gla_b_a,
           gla_b_r, gla_norm_g, gla_w_o, diff_w_in, diff_lambda, diff_subln_g, diff_w_o, final_g):
    x2 = x.reshape(N_TOK, D_MODEL)
    mod = _adaln_mod(c, ada_w, ada_b)
    mod_table = lambda layer: mod[layer].reshape(BATCH * N_MOD, 1, D_MODEL)
    mod3 = mod_table(0)
    fg = final_g.reshape(1, D_MODEL)

    o, (w1, w2, wo) = _gla_layer(
        x2, mod3, norm_g[0, 0].reshape(1, D_MODEL), gla_w_in[0].T, gla_w_a2[0],
        gla_b_a[0].reshape(1, GLA_HK), gla_b_r[0].reshape(1, D_MODEL),
        gla_norm_g[0].reshape(1, GLA_DV), [(mlp_w1, 0), (mlp_w2, 0), (gla_w_o, 0)])
    x2, (w1, w2, diff_win, wo) = _outproj_mlp(
        x2, o, mod3, wo, norm_g[0, 1].reshape(1, D_MODEL), w1, w2, fg,
        False, [(mlp_w1, 1), (mlp_w2, 1), (diff_w_in, 0), (diff_w_o, 0)])
    mod3 = mod_table(1)

    lambda_init = 0.8 - 0.6 * math.exp(-0.3 * 1)
    cs = _rope_tables(positions)
    q, k, vt = _inproj_diff(
        x2, mod3, norm_g[1, 0].reshape(1, D_MODEL), diff_win, cs, _rope_expander())
    o = _diff_attention(q, k, vt, diff_lambda[0], diff_subln_g[0].reshape(DIFF_DV, 1), lambda_init)
    x2, _ = _outproj_mlp(x2, o.reshape(N_TOK, D_MODEL), mod3, wo,
                         norm_g[1, 1].reshape(1, D_MODEL), w1, w2, fg, True)
    return x2.reshape(BATCH, SEQ, D_MODEL)
```

```python
import functools
import math

import jax
import jax.numpy as jnp
import numpy as np
from jax import lax
from jax.experimental import pallas as pl
from jax.experimental.pallas import tpu as pltpu

D_MODEL = 1024
BATCH = 8
SEQ = 2048
DEPTH = 2
D_FF = 4 * D_MODEL
NORM_EPS = 1e-6
GLA_HEADS = 4
GLA_DK = 128
GLA_DV = 256
GLA_HK = GLA_HEADS * GLA_DK
GLA_GATE_RANK = 16
GLA_TAU = 16.0
GLA_CHUNK = 64
DIFF_HEADS = 8
DIFF_DH = 64
DIFF_DV = 128
ROPE_THETA = 500000.0
ROPE_DIM = DIFF_DH // 4
ROPE_HALF = ROPE_DIM // 2
N_TOK = BATCH * SEQ
N_MOD = 6

TM_INPROJ = 1024
INPROJ_ROW_GROUPS = 2
TM_MLP = 1024
TM_MLP_HOST = 512
MLP_SUB_ROWS = 512
TM_GLA = 1024
GLA_ROW_GROUPS = 2
TK_MOD = 256
ATT_TQ = 256
ATT_TK = 256
VT_ROWS = DIFF_DV + 16
ATT_HEADS = 8
VMEM_LIMIT = 56 * 1024 * 1024

F32 = jnp.float32
BF16 = jnp.bfloat16
MASKED = -1e30
NT_DIMS = (((1,), (1,)), ((), ()))
TN_DIMS = (((0,), (0,)), ((), ()))


def _dot(a, b):
    return jnp.dot(a, b, preferred_element_type=F32)


def _split_bf16(a):
    hi = a.astype(BF16)
    lo = (a - hi.astype(F32)).astype(BF16)
    return hi, lo


def _rms_scale(x):
    return lax.rsqrt(jnp.mean(x * x, axis=-1, keepdims=True) + NORM_EPS)


def _silu(x):
    return x * (1.0 / (1.0 + jnp.exp(-x)))


def _params(*sem):
    return pltpu.CompilerParams(dimension_semantics=sem, vmem_limit_bytes=VMEM_LIMIT)


def _full_spec(shape):
    zeros = (0,) * len(shape)
    return pl.BlockSpec(shape, lambda *_: zeros)


def _resident_spec(block_shape, index):
    return pl.BlockSpec(block_shape, lambda m: index, pipeline_mode=pl.Buffered(1))


def _mod_spec(k, tm):
    tiles_per_seq = SEQ // tm
    return pl.BlockSpec((None, 1, D_MODEL), lambda m, *_: ((m // tiles_per_seq) * N_MOD + k, 0, 0))


def _modulated_norm(x, g, scale, shift):
    return (x * _rms_scale(x) * g) * (1.0 + scale) + shift


def _cast_slabs(jobs, n_steps, step_of):
    flats, in_specs, out_specs, out_shapes = [], [], [], []
    for w, layer in jobs:
        _, n_rows, cols = w.shape
        slab = n_rows // n_steps
        assert slab * n_steps == n_rows and slab % 16 == 0
        first = layer * n_steps
        flats.append(w.reshape(-1, cols))
        in_specs.append(pl.BlockSpec((slab, cols), lambda *g, first=first: (first + step_of(*g), 0)))
        out_specs.append(pl.BlockSpec((slab, cols), lambda *g: (step_of(*g), 0)))
        out_shapes.append(jax.ShapeDtypeStruct((n_rows, cols), BF16))
    return flats, in_specs, out_specs, out_shapes


def _mod_block(c_ref, w_ref, b_ref, o_ref):
    c_hi, c_lo = _split_bf16(_silu(c_ref[...]))
    w_hi, w_lo = _split_bf16(w_ref[...])
    lhs = jnp.concatenate([c_hi, c_lo], axis=0)
    r = _dot(lhs, w_hi)
    part = r[:BATCH] + r[BATCH:] + _dot(c_hi, w_lo)

    @pl.when(pl.program_id(1) == 0)
    def _():
        o_ref[...] = part + b_ref[...]

    @pl.when(pl.program_id(1) > 0)
    def _():
        o_ref[...] += part


def _adaln_mod(c, ada_w, ada_b):
    n_out = N_MOD * D_MODEL
    return pl.pallas_call(
        _mod_block,
        grid=(DEPTH, D_MODEL // TK_MOD),
        in_specs=[
            pl.BlockSpec((BATCH, TK_MOD), lambda i, k: (0, k)),
            pl.BlockSpec((None, TK_MOD, n_out), lambda i, k: (i, k, 0)),
            pl.BlockSpec((None, 1, n_out), lambda i, k: (i, 0, 0)),
        ],
        out_specs=pl.BlockSpec((None, BATCH, n_out), lambda i, k: (i, 0, 0)),
        out_shape=jax.ShapeDtypeStruct((DEPTH, BATCH, n_out), F32),
        compiler_params=_params("arbitrary", "arbitrary"),
        name="adaln_mod",
    )(c, ada_w, ada_b.reshape(DEPTH, 1, n_out))


def _gla_layer_kernel(x_ref, shift_ref, scale_ref, g_ref, wqk_ref, wv_ref, wr_ref, wa_ref,
                      wa2_ref, ba_ref, br_ref, ng_ref, *rest, n_cast, tiles_per_seq):
    cast_in, rest = rest[:n_cast], rest[n_cast:]
    o_ref, cast_out = rest[0], rest[1:1 + n_cast]
    wbf_ref, st_ref, qd_ref, ki_ref, v_ref, gate_ref, dec_ref = rest[1 + n_cast:]
    for src, dst in zip(cast_in, cast_out):
        dst[...] = src[...].astype(BF16)

    @pl.when(pl.program_id(0) == 0)
    def _():
        for i, w_ref in enumerate((wqk_ref, wv_ref, wr_ref)):
            wbf_ref[i] = w_ref[...].T.astype(BF16)

    @pl.when(pl.program_id(0) % tiles_per_seq == 0)
    def _():
        st_ref[...] = jnp.zeros_like(st_ref)

    wqk_ref, wv_ref, wr_ref = wbf_ref.at[0], wbf_ref.at[1], wbf_ref.at[2]
    row = lax.broadcasted_iota(jnp.int32, (GLA_CHUNK, GLA_CHUNK), 0)
    col = lax.broadcasted_iota(jnp.int32, (GLA_CHUNK, GLA_CHUNK), 1)
    causal = row >= col
    tril = jnp.where(causal, 1.0, 0.0).astype(BF16)
    w_hi, w_lo = _split_bf16(wa2_ref[...])
    w_gate = jnp.concatenate([w_hi, w_hi, w_lo], axis=0)
    q_scale = GLA_DK ** -0.5
    group_rows = x_ref.shape[0] // GLA_ROW_GROUPS
    chunks_per_group = group_rows // GLA_CHUNK
    chunk = lambda n: slice(n * GLA_CHUNK, (n + 1) * GLA_CHUNK)
    kcols = lambda h: slice(h * GLA_DK, (h + 1) * GLA_DK)
    vcols = lambda h: slice(h * GLA_DV, (h + 1) * GLA_DV)

    def phase_a(grp):
        rows = slice(grp * group_rows, (grp + 1) * group_rows)
        hb = _modulated_norm(x_ref[rows, :], g_ref[...], scale_ref[...],
                             shift_ref[...]).astype(BF16)
        a_hi, a_lo = _split_bf16(lax.dot_general(
            hb, wa_ref[...].astype(BF16), NT_DIMS, preferred_element_type=F32))
        gate = _silu(_dot(hb, wr_ref[...]) + br_ref[...]).astype(BF16)
        for h in range(GLA_HEADS):
            gate_ref[h, rows, :] = gate[:, vcols(h)]
        la = _dot(jnp.concatenate([a_hi, a_lo, a_hi], axis=1), w_gate) + ba_ref[...]
        qk = _dot(hb, wqk_ref[...])
        log_a = (jnp.minimum(la, 0.0) - jnp.log(1.0 + jnp.exp(-jnp.abs(la)))) * (1.0 / GLA_TAU)
        l_hi = log_a.astype(BF16)
        cums = [_dot(tril, l_hi[chunk(i)]) for i in range(chunks_per_group)]
        v = _dot(hb, wv_ref[...]).astype(BF16)
        for h in range(GLA_HEADS):
            v_ref[h, rows, :] = v[:, vcols(h)]
        for i in range(chunks_per_group):
            n = grp * chunks_per_group + i
            b = cums[i]
            b_last = b[GLA_CHUNK - 1:GLA_CHUNK, :]
            qd = (qk[chunk(i), :GLA_HK] * q_scale * jnp.exp(b)).astype(BF16)
            ki = (qk[chunk(i), GLA_HK:] * jnp.exp(-b)).astype(BF16)
            for h in range(GLA_HEADS):
                qd_ref[h, chunk(n), :] = qd[:, kcols(h)]
                ki_ref[h, chunk(n), :] = ki[:, kcols(h)]
            dec_ref[n] = jnp.exp(b_last)

    def phase_b(grp):
        chunks = range(grp * chunks_per_group, (grp + 1) * chunks_per_group)
        work = [(h, n) for h in range(GLA_HEADS) for n in chunks]
        scores = {}
        updates = {}
        for h, n in work:
            scores[h, n] = lax.dot_general(qd_ref[h, chunk(n), :], ki_ref[h, chunk(n), :],
                                           NT_DIMS, preferred_element_type=F32)
            updates[h, n] = lax.dot_general(ki_ref[h, chunk(n), :], v_ref[h, chunk(n), :],
                                            TN_DIMS, preferred_element_type=F32)
        states = {}
        for h in range(GLA_HEADS):
            st = st_ref[h]
            for n in chunks:
                states[h, n] = st.astype(BF16)
                dec_col = jnp.broadcast_to(dec_ref[n][:, kcols(h)], (8, GLA_DK)).T[:, 0:1]
                st = (st + updates[h, n]) * dec_col
            st_ref[h] = st
        for h, n in work:
            s = jnp.where(causal, scores[h, n], 0.0).astype(BF16)
            o = _dot(jnp.concatenate([qd_ref[h, chunk(n), :], s], axis=1),
                     jnp.concatenate([states[h, n], v_ref[h, chunk(n), :]], axis=0))
            y = o * _rms_scale(o) * ng_ref[...]
            o_ref[chunk(n), vcols(h)] = (y * gate_ref[h, chunk(n), :].astype(F32)).astype(BF16)

    for grp in range(GLA_ROW_GROUPS):
        phase_a(grp)
        phase_b(grp)


def _gla_layer(x2, mod3, g, w_in_t, wa2, ba, br, ng, cast_jobs):
    tm = TM_GLA
    n_steps = N_TOK // tm
    row = lambda m: (m, 0)
    assert 2 * GLA_HK == D_MODEL
    row_block = lambda idx: _resident_spec((D_MODEL, D_MODEL), (idx, 0))
    flats, cast_in, cast_out, cast_shapes = _cast_slabs(cast_jobs, n_steps, lambda m: m)
    out = pl.pallas_call(
        functools.partial(_gla_layer_kernel, n_cast=len(flats), tiles_per_seq=SEQ // tm),
        grid=(n_steps,),
        in_specs=[
            pl.BlockSpec((tm, D_MODEL), row),
            _mod_spec(0, tm),
            _mod_spec(1, tm),
            _full_spec((1, D_MODEL)),
            row_block(0),
            row_block(1),
            row_block(2),
            _resident_spec((GLA_GATE_RANK, D_MODEL), (3 * D_MODEL // GLA_GATE_RANK, 0)),
            _full_spec((GLA_GATE_RANK, GLA_HK)),
            _full_spec((1, GLA_HK)),
            _full_spec((1, D_MODEL)),
            _full_spec((1, GLA_DV)),
        ] + cast_in,
        out_specs=[pl.BlockSpec((tm, D_MODEL), row)] + cast_out,
        out_shape=[jax.ShapeDtypeStruct((N_TOK, D_MODEL), BF16)] + cast_shapes,
        scratch_shapes=[
            pltpu.VMEM((3, D_MODEL, D_MODEL), BF16),
            pltpu.VMEM((GLA_HEADS, GLA_DK, GLA_DV), F32),
            pltpu.VMEM((GLA_HEADS, tm, GLA_DK), BF16),
            pltpu.VMEM((GLA_HEADS, tm, GLA_DK), BF16),
            pltpu.VMEM((GLA_HEADS, tm, GLA_DV), BF16),
            pltpu.VMEM((GLA_HEADS, tm, GLA_DV), BF16),
            pltpu.VMEM((tm // GLA_CHUNK, 1, GLA_HK), F32),
        ],
        compiler_params=_params("arbitrary"),
        name="gla_layer",
    )(x2, mod3, mod3, g, w_in_t, w_in_t, w_in_t, w_in_t, wa2, ba, br, ng, *flats)
    return out[0], out[1:]


def _rope_kernel(pos_ref, freq_ref, cs_ref):
    ang = freq_ref[...] * pos_ref[...]
    cs_ref[0:ROPE_HALF, :] = jnp.cos(ang)
    cs_ref[ROPE_HALF:, :] = jnp.sin(ang)


def _rope_tables(positions):
    inv_freq = ROPE_THETA ** (-jnp.arange(0, ROPE_DIM, 2, dtype=F32) / ROPE_DIM)
    return pl.pallas_call(
        _rope_kernel,
        grid=(1,),
        in_specs=[_full_spec((1, N_TOK)), _full_spec((ROPE_HALF, 1))],
        out_specs=_full_spec((ROPE_DIM, N_TOK)),
        out_shape=jax.ShapeDtypeStruct((ROPE_DIM, N_TOK), F32),
        compiler_params=_params("arbitrary"),
        name="rope_tables",
    )(positions.astype(F32).reshape(1, N_TOK), inv_freq.reshape(ROPE_HALF, 1))


def _rope_expander():
    e = np.zeros((3, ROPE_DIM, 2 * DIFF_DV), np.float32)
    for lane in range(DIFF_DV):
        d = lane % DIFF_DH
        if d < ROPE_DIM:
            e[:, d % ROPE_HALF, lane] = 1.0
            e[:, ROPE_HALF + d % ROPE_HALF, DIFF_DV + lane] = -1.0 if d < ROPE_HALF else 1.0
    return jnp.asarray(e.reshape(3 * ROPE_DIM, 2 * DIFF_DV), BF16)


def _inproj_diff_kernel(x_ref, shift_ref, scale_ref, g_ref, wq_ref, wk_ref, wv_ref,
                        cs_ref, e_ref, q_ref, k_ref, vt_ref, wvt_ref):
    @pl.when(pl.program_id(0) == 0)
    def _():
        wvt_ref[...] = wv_ref[...].T

    d = lax.broadcasted_iota(jnp.int32, (1, DIFF_DV), 1) & (DIFF_DH - 1)
    first_half = d < ROPE_HALF
    q_scale = DIFF_DH ** -0.5 * math.log2(math.e)
    group_rows = x_ref.shape[0] // INPROJ_ROW_GROUPS

    for grp in range(INPROJ_ROW_GROUPS):
        rows = slice(grp * group_rows, (grp + 1) * group_rows)
        hb = _modulated_norm(x_ref[rows, :], g_ref[...], scale_ref[...],
                             shift_ref[...]).astype(BF16)

        cs = cs_ref[:, rows]
        hi = cs.astype(BF16).astype(F32)
        mid = (cs - hi).astype(BF16).astype(F32)
        lo = cs - hi - mid
        parts = jnp.concatenate([hi, mid, lo], axis=0)
        tab = _dot(parts.T.astype(BF16), e_ref[...])
        ct = tab[:, :DIFF_DV] + jnp.where(d >= ROPE_DIM, 1.0, 0.0)
        s12 = tab[:, DIFF_DV:]

        for w_ref, out_ref, scl in ((wq_ref, q_ref, q_scale), (wk_ref, k_ref, 1.0)):
            z = _dot(hb, w_ref[...])
            for h in range(DIFF_HEADS):
                xh = z[:, h * DIFF_DV:(h + 1) * DIFF_DV]
                partner = jnp.where(first_half, pltpu.roll(xh, DIFF_DV - ROPE_HALF, 1),
                                    pltpu.roll(xh, ROPE_HALF, 1))
                out_ref[h, rows, :] = ((xh * ct + partner * s12) * scl).astype(BF16)
        vt = lax.dot_general(wvt_ref[...], hb, NT_DIMS, preferred_element_type=F32).astype(BF16)
        ones = jnp.ones((VT_ROWS - DIFF_DV, ATT_TK), BF16)
        for blk in range(group_rows // ATT_TK):
            dst = grp * (group_rows // ATT_TK) + blk
            cols = slice(blk * ATT_TK, (blk + 1) * ATT_TK)
            for h in range(DIFF_HEADS):
                vt_ref[dst, h * VT_ROWS:h * VT_ROWS + DIFF_DV, :] = vt[h * DIFF_DV:(h + 1) * DIFF_DV,
                                                                        cols]
                vt_ref[dst, h * VT_ROWS + DIFF_DV:(h + 1) * VT_ROWS, :] = ones


def _inproj_diff(x2, mod3, g, w_in, cs, expander):
    tm = TM_INPROJ
    row = lambda m: (m, 0)
    h_spec = pl.BlockSpec((DIFF_HEADS, tm, DIFF_DV), lambda m: (0, m, 0))
    return pl.pallas_call(
        _inproj_diff_kernel,
        grid=(N_TOK // tm,),
        in_specs=[
            pl.BlockSpec((tm, D_MODEL), row),
            _mod_spec(0, tm),
            _mod_spec(1, tm),
            _full_spec((1, D_MODEL)),
            _resident_spec((D_MODEL, D_MODEL), (0, 0)),
            _resident_spec((D_MODEL, D_MODEL), (0, 1)),
            _resident_spec((D_MODEL, D_MODEL), (0, 2)),
            pl.BlockSpec((ROPE_DIM, tm), lambda m: (0, m)),
            _full_spec((3 * ROPE_DIM, 2 * DIFF_DV)),
        ],
        out_specs=[h_spec, h_spec,
                   pl.BlockSpec((tm // ATT_TK, DIFF_HEADS * VT_ROWS, ATT_TK), lambda m: (m, 0, 0))],
        out_shape=[jax.ShapeDtypeStruct((DIFF_HEADS, N_TOK, DIFF_DV), BF16),
                   jax.ShapeDtypeStruct((DIFF_HEADS, N_TOK, DIFF_DV), BF16),
                   jax.ShapeDtypeStruct((N_TOK // ATT_TK, DIFF_HEADS * VT_ROWS, ATT_TK), BF16)],
        scratch_shapes=[pltpu.VMEM((D_MODEL, D_MODEL), BF16)],
        compiler_params=_params("arbitrary"),
        name="inproj_diff",
    )(x2, mod3, mod3, g, w_in, w_in, w_in, cs, expander)


def _attn_kernel(q_ref, k_ref, vt_ref, lam_ref, g_ref, o_ref, qs_ref, s_ref, m_ref, acc_ref,
                 *, lambda_init):
    tq, tk = ATT_TQ, ATT_TK
    assert tq == tk
    lf = lam_ref[...]
    lam = (jnp.exp(jnp.sum(lf[0:1] * lf[1:2], axis=-1, keepdims=True))
           - jnp.exp(jnp.sum(lf[2:3] * lf[3:4], axis=-1, keepdims=True)) + lambda_init)
    feat = lax.broadcasted_iota(jnp.int32, (DIFF_DV, tq), 0)
    k_pos = lax.broadcasted_iota(jnp.int32, (tk, 2 * tq), 0)
    q_pos = lax.broadcasted_iota(jnp.int32, (tk, 2 * tq), 1) & (tq - 1)
    heads = [slice(h * DIFF_DV, (h + 1) * DIFF_DV) for h in range(ATT_HEADS)]

    def stage_queries(qi, slot):
        q0 = pl.multiple_of(qi * tq, tq)
        for h in range(ATT_HEADS):
            qt = q_ref[h, pl.ds(q0, tq), :].astype(F32).T
            qs_ref[slot, h, :, 0:tq] = jnp.where(feat < DIFF_DH, qt, 0.0).astype(BF16)
            qs_ref[slot, h, :, tq:2 * tq] = jnp.where(feat >= DIFF_DH, qt, 0.0).astype(BF16)

    def scores_for(j, h, slot):
        start = pl.multiple_of(j * tk, tk)
        return _dot(k_ref[h, pl.ds(start, tk), :], qs_ref[slot, h])

    def reset_state():
        m_ref[...] = jnp.full_like(m_ref, MASKED)

    def kv_step(j, masked, next_j, next_slot):
        for h in range(ATT_HEADS):
            vtb = vt_ref[j, h * VT_ROWS:(h + 1) * VT_ROWS, :]
            s = s_ref[h]
            next_s = scores_for(next_j, h, next_slot)
            if masked:
                s = jnp.where(k_pos <= q_pos, s, MASKED)
            m_prev = m_ref[h]
            m_new = jnp.maximum(m_prev, jnp.max(s, axis=0, keepdims=True))
            p = jnp.exp2(s - m_new)
            alpha = jnp.exp2(m_prev - m_new)
            acc_ref[h] = alpha * acc_ref[h] + _dot(vtb, p.astype(BF16))
            m_ref[h] = m_new
            s_ref[h] = next_s

    n_q = SEQ // tq
    stage_queries(0, 0)
    acc_ref[...] = jnp.zeros_like(acc_ref)
    reset_state()
    for h in range(ATT_HEADS):
        s_ref[h] = scores_for(0, h, 0)

    def q_block(qi, carry):
        q0 = pl.multiple_of(qi * tq, tq)
        slot = qi & 1

        def body(j, c):
            kv_step(j, False, j + 1, slot)
            return c

        lax.fori_loop(0, qi, body, 0)
        stage_queries(jnp.minimum(qi + 1, n_q - 1), 1 - slot)
        kv_step(qi, True, 0, 1 - slot)
        for h, hs in enumerate(heads):
            acc = acc_ref[h, 0:DIFF_DV, :]
            inv_l = 1.0 / acc_ref[h, DIFF_DV:DIFF_DV + 1, :]
            o = acc[:, :tq] * inv_l[:, :tq] - lam * (acc[:, tq:] * inv_l[:, tq:])
            rms = lax.rsqrt(jnp.mean(o * o, axis=0, keepdims=True) + NORM_EPS)
            y = o * rms * (g_ref[...] * (1.0 - lambda_init))
            o_ref[pl.ds(q0, tq), hs] = y.T.astype(BF16)
        reset_state()
        return carry

    lax.fori_loop(0, n_q, q_block, 0)


def _diff_attention(q, k, vt, lam_vecs, g, lambda_init):
    assert ATT_HEADS == DIFF_HEADS
    h_spec = pl.BlockSpec((DIFF_HEADS, SEQ, DIFF_DV), lambda b, h: (0, b, 0))
    return pl.pallas_call(
        functools.partial(_attn_kernel, lambda_init=lambda_init),
        grid=(BATCH, 1),
        in_specs=[h_spec, h_spec,
                  pl.BlockSpec((SEQ // ATT_TK, DIFF_HEADS * VT_ROWS, ATT_TK), lambda b, h: (b, 0, 0)),
                  _full_spec((4, DIFF_DH)), _full_spec((DIFF_DV, 1))],
        out_specs=pl.BlockSpec((None, SEQ, D_MODEL), lambda b, h: (b, 0, 0)),
        out_shape=jax.ShapeDtypeStruct((BATCH, SEQ, D_MODEL), BF16),
        scratch_shapes=[
            pltpu.VMEM((2, ATT_HEADS, DIFF_DV, 2 * ATT_TQ), BF16),
            pltpu.VMEM((ATT_HEADS, ATT_TK, 2 * ATT_TQ), F32),
            pltpu.VMEM((ATT_HEADS, 1, 2 * ATT_TQ), F32),
            pltpu.VMEM((ATT_HEADS, VT_ROWS, 2 * ATT_TQ), F32),
        ],
        compiler_params=_params("arbitrary", "arbitrary"),
        name="diff_attention",
    )(q, k, vt, lam_vecs, g)


def _mlp_kernel(x_ref, o_ref, wo_ref, gt_ref, sh_ref, sc_ref, gc_ref, g_ref, w1_ref, w2_ref,
                fg_ref, *rest, final, n_cast):
    cast_in, (out_ref, *cast_out) = rest[:n_cast], rest[n_cast:]
    for src, dst in zip(cast_in, cast_out):
        dst[...] = src[...].astype(BF16)

    def rows_block(rows):
        x1 = x_ref[rows, :] + gt_ref[...] * _dot(o_ref[rows, :], wo_ref[...])
        hb = _modulated_norm(x1, g_ref[...], sc_ref[...], sh_ref[...]).astype(BF16)
        a = jnp.square(jnp.maximum(_dot(hb, w1_ref[...]), 0.0)).astype(BF16)
        out = x1 + gc_ref[...] * _dot(a, w2_ref[...])
        if final:
            out = out * _rms_scale(out) * fg_ref[...]
        out_ref[rows, :] = out

    n_sub = x_ref.shape[0] // MLP_SUB_ROWS
    if n_sub == 1:
        rows_block(slice(None))
    else:
        def body(i, carry):
            rows_block(pl.ds(pl.multiple_of(i * MLP_SUB_ROWS, MLP_SUB_ROWS), MLP_SUB_ROWS))
            return carry
        lax.fori_loop(0, n_sub, body, 0)


def _outproj_mlp(x2, o2, mod3, wo, g, w1, w2, fg, final, cast_jobs=()):
    tm = TM_MLP_HOST if cast_jobs else TM_MLP
    n_steps = N_TOK // tm
    row = lambda m: (m, 0)
    flats, cast_in, cast_out, cast_shapes = _cast_slabs(cast_jobs, n_steps, lambda m: m)
    out = pl.pallas_call(
        functools.partial(_mlp_kernel, final=final, n_cast=len(flats)),
        grid=(n_steps,),
        in_specs=[
            pl.BlockSpec((tm, D_MODEL), row),
            pl.BlockSpec((tm, D_MODEL), row),
            _resident_spec((D_MODEL, D_MODEL), (0, 0)),
            _mod_spec(2, tm),
            _mod_spec(3, tm),
            _mod_spec(4, tm),
            _mod_spec(5, tm),
            _full_spec((1, D_MODEL)),
            _resident_spec((D_MODEL, D_FF), (0, 0)),
            _resident_spec((D_FF, D_MODEL), (0, 0)),
            _full_spec((1, D_MODEL)),
        ] + cast_in,
        out_specs=[pl.BlockSpec((tm, D_MODEL), row)] + cast_out,
        out_shape=[jax.ShapeDtypeStruct((N_TOK, D_MODEL), F32)] + cast_shapes,
        compiler_params=_params("arbitrary"),
        name="outproj_mlp_final" if final else "outproj_mlp",
    )(x2, o2, wo, mod3, mod3, mod3, mod3, g, w1, w2, fg, *flats)
    return out[0], out[1:]


def kernel(x, c, positions, ada_w, ada_b, norm_g, mlp_w1, mlp_w2, gla_w_in, gla_w_a2, gla_b_a,
           gla_b_r, gla_norm_g, gla_w_o, diff_w_in, diff_lambda, diff_subln_g, diff_w_o, final_g):
    x2 = x.reshape(N_TOK, D_MODEL)
    mod = _adaln_mod(c, ada_w, ada_b)
    mod_table = lambda layer: mod[layer].reshape(BATCH * N_MOD, 1, D_MODEL)
    mod3 = mod_table(0)
    fg = final_g.reshape(1, D_MODEL)

    o, (w1, w2, wo) = _gla_layer(
        x2, mod3, norm_g[0, 0].reshape(1, D_MODEL), gla_w_in[0].T, gla_w_a2[0],
        gla_b_a[0].reshape(1, GLA_HK), gla_b_r[0].reshape(1, D_MODEL),
        gla_norm_g[0].reshape(1, GLA_DV), [(mlp_w1, 0), (mlp_w2, 0), (gla_w_o, 0)])
    x2, (w1, w2, diff_win, wo) = _outproj_mlp(
        x2, o, mod3, wo, norm_g[0, 1].reshape(1, D_MODEL), w1, w2, fg,
        False, [(mlp_w1, 1), (mlp_w2, 1), (diff_w_in, 0), (diff_w_o, 0)])
    mod3 = mod_table(1)

    lambda_init = 0.8 - 0.6 * math.exp(-0.3 * 1)
    cs = _rope_tables(positions)
    q, k, vt = _inproj_diff(
        x2, mod3, norm_g[1, 0].reshape(1, D_MODEL), diff_win, cs, _rope_expander())
    o = _diff_attention(q, k, vt, diff_lambda[0], diff_subln_g[0].reshape(DIFF_DV, 1), lambda_init)
    x2, _ = _outproj_mlp(x2, o.reshape(N_TOK, D_MODEL), mod3, wo,
                         norm_g[1, 1].reshape(1, D_MODEL), w1, w2, fg, True)
    return x2.reshape(BATCH, SEQ, D_MODEL)
```

```python
import functools
import math

import jax
import jax.numpy as jnp
import numpy as np
from jax import lax
from jax.experimental import pallas as pl
from jax.experimental.pallas import tpu as pltpu

D_MODEL = 1024
BATCH = 8
SEQ = 2048
DEPTH = 2
D_FF = 4 * D_MODEL
NORM_EPS = 1e-6
GLA_HEADS = 4
GLA_DK = 128
GLA_DV = 256
GLA_HK = GLA_HEADS * GLA_DK
GLA_GATE_RANK = 16
GLA_TAU = 16.0
GLA_CHUNK = 64
DIFF_HEADS = 8
DIFF_DH = 64
DIFF_DV = 128
ROPE_THETA = 500000.0
ROPE_DIM = DIFF_DH // 4
ROPE_HALF = ROPE_DIM // 2
N_TOK = BATCH * SEQ
N_MOD = 6

TM_INPROJ = 1024
INPROJ_ROW_GROUPS = 2
TM_MLP = 1024
TM_MLP_HOST = 512
MLP_SUB_ROWS = 512
TM_GLA = 1024
GLA_ROW_GROUPS = 2
TK_MOD = 256
ATT_TQ = 256
ATT_TK = 256
VT_ROWS = DIFF_DV + 16
ATT_HEADS = 8
VMEM_LIMIT = 56 * 1024 * 1024

F32 = jnp.float32
BF16 = jnp.bfloat16
MASKED = -1e30
NT_DIMS = (((1,), (1,)), ((), ()))
TN_DIMS = (((0,), (0,)), ((), ()))


def _dot(a, b):
    return jnp.dot(a, b, preferred_element_type=F32)


def _split_bf16(a):
    hi = a.astype(BF16)
    lo = (a - hi.astype(F32)).astype(BF16)
    return hi, lo


def _rms_scale(x):
    return lax.rsqrt(jnp.mean(x * x, axis=-1, keepdims=True) + NORM_EPS)


def _silu(x):
    return x * (1.0 / (1.0 + jnp.exp(-x)))


def _params(*sem):
    return pltpu.CompilerParams(dimension_semantics=sem, vmem_limit_bytes=VMEM_LIMIT)


def _full_spec(shape):
    zeros = (0,) * len(shape)
    return pl.BlockSpec(shape, lambda *_: zeros)


def _resident_spec(block_shape, index):
    return pl.BlockSpec(block_shape, lambda m: index, pipeline_mode=pl.Buffered(1))


def _mod_spec(k, tm):
    tiles_per_seq = SEQ // tm
    return pl.BlockSpec((None, 1, D_MODEL), lambda m, *_: ((m // tiles_per_seq) * N_MOD + k, 0, 0))


def _modulated_norm(x, g, scale, shift):
    return (x * _rms_scale(x) * g) * (1.0 + scale) + shift


def _cast_slabs(jobs, n_steps, step_of):
    flats, in_specs, out_specs, out_shapes = [], [], [], []
    for w, layer in jobs:
        _, n_rows, cols = w.shape
        slab = n_rows // n_steps
        assert slab * n_steps == n_rows and slab % 16 == 0
        first = layer * n_steps
        flats.append(w.reshape(-1, cols))
        in_specs.append(pl.BlockSpec((slab, cols), lambda *g, first=first: (first + step_of(*g), 0)))
        out_specs.append(pl.BlockSpec((slab, cols), lambda *g: (step_of(*g), 0)))
        out_shapes.append(jax.ShapeDtypeStruct((n_rows, cols), BF16))
    return flats, in_specs, out_specs, out_shapes


def _mod_block(c_ref, w_ref, b_ref, o_ref):
    c_hi, c_lo = _split_bf16(_silu(c_ref[...]))
    w_hi, w_lo = _split_bf16(w_ref[...])
    lhs = jnp.concatenate([c_hi, c_lo], axis=0)
    r = _dot(lhs, w_hi)
    part = r[:BATCH] + r[BATCH:] + _dot(c_hi, w_lo)

    @pl.when(pl.program_id(1) == 0)
    def _():
        o_ref[...] = part + b_ref[...]

    @pl.when(pl.program_id(1) > 0)
    def _():
        o_ref[...] += part


def _adaln_mod(c, ada_w, ada_b):
    n_out = N_MOD * D_MODEL
    return pl.pallas_call(
        _mod_block,
        grid=(DEPTH, D_MODEL // TK_MOD),
        in_specs=[
            pl.BlockSpec((BATCH, TK_MOD), lambda i, k: (0, k)),
            pl.BlockSpec((None, TK_MOD, n_out), lambda i, k: (i, k, 0)),
            pl.BlockSpec((None, 1, n_out), lambda i, k: (i, 0, 0)),
        ],
        out_specs=pl.BlockSpec((None, BATCH, n_out), lambda i, k: (i, 0, 0)),
        out_shape=jax.ShapeDtypeStruct((DEPTH, BATCH, n_out), F32),
        compiler_params=_params("arbitrary", "arbitrary"),
        name="adaln_mod",
    )(c, ada_w, ada_b.reshape(DEPTH, 1, n_out))


def _gla_layer_kernel(x_ref, shift_ref, scale_ref, g_ref, wqk_ref, wv_ref, wr_ref, wa_ref,
                      wa2_ref, ba_ref, br_ref, ng_ref, *rest, n_cast, tiles_per_seq):
    cast_in, rest = rest[:n_cast], rest[n_cast:]
    o_ref, cast_out = rest[0], rest[1:1 + n_cast]
    wbf_ref, st_ref, qd_ref, ki_ref, v_ref, gate_ref, dec_ref = rest[1 + n_cast:]
    for src, dst in zip(cast_in, cast_out):
        dst[...] = src[...].astype(BF16)

    @pl.when(pl.program_id(0) == 0)
    def _():
        for i, w_ref in enumerate((wqk_ref, wv_ref, wr_ref)):
            wbf_ref[i] = w_ref[...].T.astype(BF16)

    @pl.when(pl.program_id(0) % tiles_per_seq == 0)
    def _():
        st_ref[...] = jnp.zeros_like(st_ref)

    wqk_ref, wv_ref, wr_ref = wbf_ref.at[0], wbf_ref.at[1], wbf_ref.at[2]
    row = lax.broadcasted_iota(jnp.int32, (GLA_CHUNK, GLA_CHUNK), 0)
    col = lax.broadcasted_iota(jnp.int32, (GLA_CHUNK, GLA_CHUNK), 1)
    causal = row >= col
    tril = jnp.where(causal, 1.0, 0.0).astype(BF16)
    w_hi, w_lo = _split_bf16(wa2_ref[...])
    w_gate = jnp.concatenate([w_hi, w_hi, w_lo], axis=0)
    q_scale = GLA_DK ** -0.5
    group_rows = x_ref.shape[0] // GLA_ROW_GROUPS
    chunks_per_group = group_rows // GLA_CHUNK
    chunk = lambda n: slice(n * GLA_CHUNK, (n + 1) * GLA_CHUNK)
    kcols = lambda h: slice(h * GLA_DK, (h + 1) * GLA_DK)
    vcols = lambda h: slice(h * GLA_DV, (h + 1) * GLA_DV)

    def phase_a(grp):
        rows = slice(grp * group_rows, (grp + 1) * group_rows)
        hb = _modulated_norm(x_ref[rows, :], g_ref[...], scale_ref[...],
                             shift_ref[...]).astype(BF16)
        a_hi, a_lo = _split_bf16(lax.dot_general(
            hb, wa_ref[...].astype(BF16), NT_DIMS, preferred_element_type=F32))
        gate = _silu(_dot(hb, wr_ref[...]) + br_ref[...]).astype(BF16)
        for h in range(GLA_HEADS):
            gate_ref[h, rows, :] = gate[:, vcols(h)]
        la = _dot(jnp.concatenate([a_hi, a_lo, a_hi], axis=1), w_gate) + ba_ref[...]
        qk = _dot(hb, wqk_ref[...])
        log_a = (jnp.minimum(la, 0.0) - jnp.log(1.0 + jnp.exp(-jnp.abs(la)))) * (1.0 / GLA_TAU)
        l_hi = log_a.astype(BF16)
        cums = [_dot(tril, l_hi[chunk(i)]) for i in range(chunks_per_group)]
        v = _dot(hb, wv_ref[...]).astype(BF16)
        for h in range(GLA_HEADS):
            v_ref[h, rows, :] = v[:, vcols(h)]
        for i in range(chunks_per_group):
            n = grp * chunks_per_group + i
            b = cums[i]
            b_last = b[GLA_CHUNK - 1:GLA_CHUNK, :]
            qd = (qk[chunk(i), :GLA_HK] * q_scale * jnp.exp(b)).astype(BF16)
            ki = (qk[chunk(i), GLA_HK:] * jnp.exp(-b)).astype(BF16)
            for h in range(GLA_HEADS):
                qd_ref[h, chunk(n), :] = qd[:, kcols(h)]
                ki_ref[h, chunk(n), :] = ki[:, kcols(h)]
            dec_ref[n] = jnp.exp(b_last)

    def phase_b(grp):
        chunks = range(grp * chunks_per_group, (grp + 1) * chunks_per_group)
        work = [(h, n) for h in range(GLA_HEADS) for n in chunks]
        scores = {}
        updates = {}
        for h, n in work:
            scores[h, n] = lax.dot_general(qd_ref[h, chunk(n), :], ki_ref[h, chunk(n), :],
                                           NT_DIMS, preferred_element_type=F32)
            updates[h, n] = lax.dot_general(ki_ref[h, chunk(n), :], v_ref[h, chunk(n), :],
                                            TN_DIMS, preferred_element_type=F32)
        states = {}
        for h in range(GLA_HEADS):
            st = st_ref[h]
            for n in chunks:
                states[h, n] = st.astype(BF16)
                dec_col = jnp.broadcast_to(dec_ref[n][:, kcols(h)], (8, GLA_DK)).T[:, 0:1]
                st = (st + updates[h, n]) * dec_col
            st_ref[h] = st
        for h, n in work:
            s = jnp.where(causal, scores[h, n], 0.0).astype(BF16)
            o = _dot(jnp.concatenate([qd_ref[h, chunk(n), :], s], axis=1),
                     jnp.concatenate([states[h, n], v_ref[h, chunk(n), :]], axis=0))
            y = o * _rms_scale(o) * ng_ref[...]
            o_ref[h, chunk(n), :] = (y * gate_ref[h, chunk(n), :].astype(F32)).astype(BF16)

    for grp in range(GLA_ROW_GROUPS):
        phase_a(grp)
        phase_b(grp)


def _gla_layer(x2, mod3, g, w_in_t, wa2, ba, br, ng, cast_jobs):
    tm = TM_GLA
    n_steps = N_TOK // tm
    row = lambda m: (m, 0)
    assert 2 * GLA_HK == D_MODEL
    row_block = lambda idx: _resident_spec((D_MODEL, D_MODEL), (idx, 0))
    flats, cast_in, cast_out, cast_shapes = _cast_slabs(cast_jobs, n_steps, lambda m: m)
    out = pl.pallas_call(
        functools.partial(_gla_layer_kernel, n_cast=len(flats), tiles_per_seq=SEQ // tm),
        grid=(n_steps,),
        in_specs=[
            pl.BlockSpec((tm, D_MODEL), row),
            _mod_spec(0, tm),
            _mod_spec(1, tm),
            _full_spec((1, D_MODEL)),
            row_block(0),
            row_block(1),
            row_block(2),
            _resident_spec((GLA_GATE_RANK, D_MODEL), (3 * D_MODEL // GLA_GATE_RANK, 0)),
            _full_spec((GLA_GATE_RANK, GLA_HK)),
            _full_spec((1, GLA_HK)),
            _full_spec((1, D_MODEL)),
            _full_spec((1, GLA_DV)),
        ] + cast_in,
        out_specs=[pl.BlockSpec((GLA_HEADS, tm, GLA_DV), lambda m: (0, m, 0))] + cast_out,
        out_shape=[jax.ShapeDtypeStruct((GLA_HEADS, N_TOK, GLA_DV), BF16)] + cast_shapes,
        scratch_shapes=[
            pltpu.VMEM((3, D_MODEL, D_MODEL), BF16),
            pltpu.VMEM((GLA_HEADS, GLA_DK, GLA_DV), F32),
            pltpu.VMEM((GLA_HEADS, tm, GLA_DK), BF16),
            pltpu.VMEM((GLA_HEADS, tm, GLA_DK), BF16),
            pltpu.VMEM((GLA_HEADS, tm, GLA_DV), BF16),
            pltpu.VMEM((GLA_HEADS, tm, GLA_DV), BF16),
            pltpu.VMEM((tm // GLA_CHUNK, 1, GLA_HK), F32),
        ],
        compiler_params=_params("arbitrary"),
        name="gla_layer",
    )(x2, mod3, mod3, g, w_in_t, w_in_t, w_in_t, w_in_t, wa2, ba, br, ng, *flats)
    return out[0], out[1:]


def _rope_kernel(pos_ref, freq_ref, cs_ref):
    ang = freq_ref[...] * pos_ref[...]
    cs_ref[0:ROPE_HALF, :] = jnp.cos(ang)
    cs_ref[ROPE_HALF:, :] = jnp.sin(ang)


def _rope_tables(positions):
    inv_freq = ROPE_THETA ** (-jnp.arange(0, ROPE_DIM, 2, dtype=F32) / ROPE_DIM)
    return pl.pallas_call(
        _rope_kernel,
        grid=(1,),
        in_specs=[_full_spec((1, N_TOK)), _full_spec((ROPE_HALF, 1))],
        out_specs=_full_spec((ROPE_DIM, N_TOK)),
        out_shape=jax.ShapeDtypeStruct((ROPE_DIM, N_TOK), F32),
        compiler_params=_params("arbitrary"),
        name="rope_tables",
    )(positions.astype(F32).reshape(1, N_TOK), inv_freq.reshape(ROPE_HALF, 1))


def _rope_expander():
    e = np.zeros((3, ROPE_DIM, 2 * DIFF_DV), np.float32)
    for lane in range(DIFF_DV):
        d = lane % DIFF_DH
        if d < ROPE_DIM:
            e[:, d % ROPE_HALF, lane] = 1.0
            e[:, ROPE_HALF + d % ROPE_HALF, DIFF_DV + lane] = -1.0 if d < ROPE_HALF else 1.0
    return jnp.asarray(e.reshape(3 * ROPE_DIM, 2 * DIFF_DV), BF16)


def _inproj_diff_kernel(x_ref, shift_ref, scale_ref, g_ref, wq_ref, wk_ref, wv_ref,
                        cs_ref, e_ref, q_ref, k_ref, vt_ref, wvt_ref):
    @pl.when(pl.program_id(0) == 0)
    def _():
        wvt_ref[...] = wv_ref[...].T

    d = lax.broadcasted_iota(jnp.int32, (1, DIFF_DV), 1) & (DIFF_DH - 1)
    first_half = d < ROPE_HALF
    q_scale = DIFF_DH ** -0.5 * math.log2(math.e)
    group_rows = x_ref.shape[0] // INPROJ_ROW_GROUPS

    for grp in range(INPROJ_ROW_GROUPS):
        rows = slice(grp * group_rows, (grp + 1) * group_rows)
        hb = _modulated_norm(x_ref[rows, :], g_ref[...], scale_ref[...],
                             shift_ref[...]).astype(BF16)

        cs = cs_ref[:, rows]
        hi = cs.astype(BF16).astype(F32)
        mid = (cs - hi).astype(BF16).astype(F32)
        lo = cs - hi - mid
        parts = jnp.concatenate([hi, mid, lo], axis=0)
        tab = _dot(parts.T.astype(BF16), e_ref[...])
        ct = tab[:, :DIFF_DV] + jnp.where(d >= ROPE_DIM, 1.0, 0.0)
        s12 = tab[:, DIFF_DV:]

        for w_ref, out_ref, scl in ((wq_ref, q_ref, q_scale), (wk_ref, k_ref, 1.0)):
            z = _dot(hb, w_ref[...])
            for h in range(DIFF_HEADS):
                xh = z[:, h * DIFF_DV:(h + 1) * DIFF_DV]
                partner = jnp.where(first_half, pltpu.roll(xh, DIFF_DV - ROPE_HALF, 1),
                                    pltpu.roll(xh, ROPE_HALF, 1))
                out_ref[h, rows, :] = ((xh * ct + partner * s12) * scl).astype(BF16)
        vt = lax.dot_general(wvt_ref[...], hb, NT_DIMS, preferred_element_type=F32).astype(BF16)
        ones = jnp.ones((VT_ROWS - DIFF_DV, ATT_TK), BF16)
        for blk in range(group_rows // ATT_TK):
            dst = grp * (group_rows // ATT_TK) + blk
            cols = slice(blk * ATT_TK, (blk + 1) * ATT_TK)
            for h in range(DIFF_HEADS):
                vt_ref[dst, h * VT_ROWS:h * VT_ROWS + DIFF_DV, :] = vt[h * DIFF_DV:(h + 1) * DIFF_DV,
                                                                        cols]
                vt_ref[dst, h * VT_ROWS + DIFF_DV:(h + 1) * VT_ROWS, :] = ones


def _inproj_diff(x2, mod3, g, w_in, cs, expander):
    tm = TM_INPROJ
    row = lambda m: (m, 0)
    h_spec = pl.BlockSpec((DIFF_HEADS, tm, DIFF_DV), lambda m: (0, m, 0))
    return pl.pallas_call(
        _inproj_diff_kernel,
        grid=(N_TOK // tm,),
        in_specs=[
            pl.BlockSpec((tm, D_MODEL), row),
            _mod_spec(0, tm),
            _mod_spec(1, tm),
            _full_spec((1, D_MODEL)),
            _resident_spec((D_MODEL, D_MODEL), (0, 0)),
            _resident_spec((D_MODEL, D_MODEL), (0, 1)),
            _resident_spec((D_MODEL, D_MODEL), (0, 2)),
            pl.BlockSpec((ROPE_DIM, tm), lambda m: (0, m)),
            _full_spec((3 * ROPE_DIM, 2 * DIFF_DV)),
        ],
        out_specs=[h_spec, h_spec,
                   pl.BlockSpec((tm // ATT_TK, DIFF_HEADS * VT_ROWS, ATT_TK), lambda m: (m, 0, 0))],
        out_shape=[jax.ShapeDtypeStruct((DIFF_HEADS, N_TOK, DIFF_DV), BF16),
                   jax.ShapeDtypeStruct((DIFF_HEADS, N_TOK, DIFF_DV), BF16),
                   jax.ShapeDtypeStruct((N_TOK // ATT_TK, DIFF_HEADS * VT_ROWS, ATT_TK), BF16)],
        scratch_shapes=[pltpu.VMEM((D_MODEL, D_MODEL), BF16)],
        compiler_params=_params("arbitrary"),
        name="inproj_diff",
    )(x2, mod3, mod3, g, w_in, w_in, w_in, cs, expander)


def _attn_kernel(q_ref, k_ref, vt_ref, lam_ref, g_ref, o_ref, qs_ref, s_ref, m_ref, acc_ref,
                 *, lambda_init):
    tq, tk = ATT_TQ, ATT_TK
    assert tq == tk
    lf = lam_ref[...]
    lam = (jnp.exp(jnp.sum(lf[0:1] * lf[1:2], axis=-1, keepdims=True))
           - jnp.exp(jnp.sum(lf[2:3] * lf[3:4], axis=-1, keepdims=True)) + lambda_init)
    feat = lax.broadcasted_iota(jnp.int32, (DIFF_DV, tq), 0)
    k_pos = lax.broadcasted_iota(jnp.int32, (tk, 2 * tq), 0)
    q_pos = lax.broadcasted_iota(jnp.int32, (tk, 2 * tq), 1) & (tq - 1)

    def stage_queries(qi, slot):
        q0 = pl.multiple_of(qi * tq, tq)
        for h in range(ATT_HEADS):
            qt = q_ref[h, pl.ds(q0, tq), :].astype(F32).T
            qs_ref[slot, h, :, 0:tq] = jnp.where(feat < DIFF_DH, qt, 0.0).astype(BF16)
            qs_ref[slot, h, :, tq:2 * tq] = jnp.where(feat >= DIFF_DH, qt, 0.0).astype(BF16)

    def scores_for(j, h, slot):
        start = pl.multiple_of(j * tk, tk)
        return _dot(k_ref[h, pl.ds(start, tk), :], qs_ref[slot, h])

    def reset_state():
        m_ref[...] = jnp.full_like(m_ref, MASKED)

    def kv_step(j, masked, next_j, next_slot):
        for h in range(ATT_HEADS):
            vtb = vt_ref[j, h * VT_ROWS:(h + 1) * VT_ROWS, :]
            s = s_ref[h]
            next_s = scores_for(next_j, h, next_slot)
            if masked:
                s = jnp.where(k_pos <= q_pos, s, MASKED)
            m_prev = m_ref[h]
            m_new = jnp.maximum(m_prev, jnp.max(s, axis=0, keepdims=True))
            p = jnp.exp2(s - m_new)
            alpha = jnp.exp2(m_prev - m_new)
            acc_ref[h] = alpha * acc_ref[h] + _dot(vtb, p.astype(BF16))
            m_ref[h] = m_new
            s_ref[h] = next_s

    n_q = SEQ // tq
    stage_queries(0, 0)
    acc_ref[...] = jnp.zeros_like(acc_ref)
    reset_state()
    for h in range(ATT_HEADS):
        s_ref[h] = scores_for(0, h, 0)

    def q_block(qi, carry):
        q0 = pl.multiple_of(qi * tq, tq)
        slot = qi & 1

        def body(j, c):
            kv_step(j, False, j + 1, slot)
            return c

        lax.fori_loop(0, qi, body, 0)
        stage_queries(jnp.minimum(qi + 1, n_q - 1), 1 - slot)
        kv_step(qi, True, 0, 1 - slot)
        for h in range(ATT_HEADS):
            acc = acc_ref[h, 0:DIFF_DV, :]
            inv_l = 1.0 / acc_ref[h, DIFF_DV:DIFF_DV + 1, :]
            o = acc[:, :tq] * inv_l[:, :tq] - lam * (acc[:, tq:] * inv_l[:, tq:])
            rms = lax.rsqrt(jnp.mean(o * o, axis=0, keepdims=True) + NORM_EPS)
            y = o * rms * (g_ref[...] * (1.0 - lambda_init))
            o_ref[h, pl.ds(q0, tq), :] = y.T.astype(BF16)
        reset_state()
        return carry

    lax.fori_loop(0, n_q, q_block, 0)


def _diff_attention(q, k, vt, lam_vecs, g, lambda_init):
    assert ATT_HEADS == DIFF_HEADS
    h_spec = pl.BlockSpec((DIFF_HEADS, SEQ, DIFF_DV), lambda b, h: (0, b, 0))
    return pl.pallas_call(
        functools.partial(_attn_kernel, lambda_init=lambda_init),
        grid=(BATCH, 1),
        in_specs=[h_spec, h_spec,
                  pl.BlockSpec((SEQ // ATT_TK, DIFF_HEADS * VT_ROWS, ATT_TK), lambda b, h: (b, 0, 0)),
                  _full_spec((4, DIFF_DH)), _full_spec((DIFF_DV, 1))],
        out_specs=h_spec,
        out_shape=jax.ShapeDtypeStruct((DIFF_HEADS, N_TOK, DIFF_DV), BF16),
        scratch_shapes=[
            pltpu.VMEM((2, ATT_HEADS, DIFF_DV, 2 * ATT_TQ), BF16),
            pltpu.VMEM((ATT_HEADS, ATT_TK, 2 * ATT_TQ), F32),
            pltpu.VMEM((ATT_HEADS, 1, 2 * ATT_TQ), F32),
            pltpu.VMEM((ATT_HEADS, VT_ROWS, 2 * ATT_TQ), F32),
        ],
        compiler_params=_params("arbitrary", "arbitrary"),
        name="diff_attention",
    )(q, k, vt, lam_vecs, g)


def _mlp_kernel(x_ref, o_ref, wo_ref, gt_ref, sh_ref, sc_ref, gc_ref, g_ref, w1_ref, w2_ref,
                fg_ref, *rest, final, n_cast):
    cast_in, (out_ref, *cast_out) = rest[:n_cast], rest[n_cast:]
    for src, dst in zip(cast_in, cast_out):
        dst[...] = src[...].astype(BF16)

    def rows_block(rows):
        o = jnp.concatenate([o_ref[h, rows, :] for h in range(o_ref.shape[0])], axis=1)
        x1 = x_ref[rows, :] + gt_ref[...] * _dot(o, wo_ref[...])
        hb = _modulated_norm(x1, g_ref[...], sc_ref[...], sh_ref[...]).astype(BF16)
        a = jnp.square(jnp.maximum(_dot(hb, w1_ref[...]), 0.0)).astype(BF16)
        out = x1 + gc_ref[...] * _dot(a, w2_ref[...])
        if final:
            out = out * _rms_scale(out) * fg_ref[...]
        out_ref[rows, :] = out

    n_sub = x_ref.shape[0] // MLP_SUB_ROWS
    if n_sub == 1:
        rows_block(slice(None))
    else:
        def body(i, carry):
            rows_block(pl.ds(pl.multiple_of(i * MLP_SUB_ROWS, MLP_SUB_ROWS), MLP_SUB_ROWS))
            return carry
        lax.fori_loop(0, n_sub, body, 0)


def _outproj_mlp(x2, o2, mod3, wo, g, w1, w2, fg, final, cast_jobs=()):
    tm = TM_MLP_HOST if cast_jobs else TM_MLP
    n_steps = N_TOK // tm
    row = lambda m: (m, 0)
    flats, cast_in, cast_out, cast_shapes = _cast_slabs(cast_jobs, n_steps, lambda m: m)
    out = pl.pallas_call(
        functools.partial(_mlp_kernel, final=final, n_cast=len(flats)),
        grid=(n_steps,),
        in_specs=[
            pl.BlockSpec((tm, D_MODEL), row),
            pl.BlockSpec((o2.shape[0], tm, o2.shape[2]), lambda m: (0, m, 0)),
            _resident_spec((D_MODEL, D_MODEL), (0, 0)),
            _mod_spec(2, tm),
            _mod_spec(3, tm),
            _mod_spec(4, tm),
            _mod_spec(5, tm),
            _full_spec((1, D_MODEL)),
            _resident_spec((D_MODEL, D_FF), (0, 0)),
            _resident_spec((D_FF, D_MODEL), (0, 0)),
            _full_spec((1, D_MODEL)),
        ] + cast_in,
        out_specs=[pl.BlockSpec((tm, D_MODEL), row)] + cast_out,
        out_shape=[jax.ShapeDtypeStruct((N_TOK, D_MODEL), F32)] + cast_shapes,
        compiler_params=_params("arbitrary"),
        name="outproj_mlp_final" if final else "outproj_mlp",
    )(x2, o2, wo, mod3, mod3, mod3, mod3, g, w1, w2, fg, *flats)
    return out[0], out[1:]


def kernel(x, c, positions, ada_w, ada_b, norm_g, mlp_w1, mlp_w2, gla_w_in, gla_w_a2, gla_b_a,
           gla_b_r, gla_norm_g, gla_w_o, diff_w_in, diff_lambda, diff_subln_g, diff_w_o, final_g):
    x2 = x.reshape(N_TOK, D_MODEL)
    mod = _adaln_mod(c, ada_w, ada_b)
    mod_table = lambda layer: mod[layer].reshape(BATCH * N_MOD, 1, D_MODEL)
    mod3 = mod_table(0)
    fg = final_g.reshape(1, D_MODEL)

    o, (w1, w2, wo) = _gla_layer(
        x2, mod3, norm_g[0, 0].reshape(1, D_MODEL), gla_w_in[0].T, gla_w_a2[0],
        gla_b_a[0].reshape(1, GLA_HK), gla_b_r[0].reshape(1, D_MODEL),
        gla_norm_g[0].reshape(1, GLA_DV), [(mlp_w1, 0), (mlp_w2, 0), (gla_w_o, 0)])
    x2, (w1, w2, diff_win, wo) = _outproj_mlp(
        x2, o, mod3, wo, norm_g[0, 1].reshape(1, D_MODEL), w1, w2, fg,
        False, [(mlp_w1, 1), (mlp_w2, 1), (diff_w_in, 0), (diff_w_o, 0)])
    mod3 = mod_table(1)

    lambda_init = 0.8 - 0.6 * math.exp(-0.3 * 1)
    cs = _rope_tables(positions)
    q, k, vt = _inproj_diff(
        x2, mod3, norm_g[1, 0].reshape(1, D_MODEL), diff_win, cs, _rope_expander())
    o = _diff_attention(q, k, vt, diff_lambda[0], diff_subln_g[0].reshape(DIFF_DV, 1), lambda_init)
    x2, _ = _outproj_mlp(x2, o, mod3, wo,
                         norm_g[1, 1].reshape(1, D_MODEL), w1, w2, fg, True)
    return x2.reshape(BATCH, SEQ, D_MODEL)
```

```python
import functools
import math

import jax
import jax.numpy as jnp
import numpy as np
from jax import lax
from jax.experimental import pallas as pl
from jax.experimental.pallas import tpu as pltpu

D_MODEL = 1024
BATCH = 8
SEQ = 2048
DEPTH = 2
D_FF = 4 * D_MODEL
NORM_EPS = 1e-6
GLA_HEADS = 4
GLA_DK = 128
GLA_DV = 256
GLA_HK = GLA_HEADS * GLA_DK
GLA_GATE_RANK = 16
GLA_TAU = 16.0
GLA_CHUNK = 64
DIFF_HEADS = 8
DIFF_DH = 64
DIFF_DV = 128
ROPE_THETA = 500000.0
ROPE_DIM = DIFF_DH // 4
ROPE_HALF = ROPE_DIM // 2
N_TOK = BATCH * SEQ
N_MOD = 6

TM_INPROJ = 1024
INPROJ_ROW_GROUPS = 2
TM_MLP = 1024
TM_MLP_HOST = 512
MLP_SUB_ROWS = 512
TM_GLA = 1024
GLA_ROW_GROUPS = 2
TK_MOD = 512
ATT_TQ = 256
ATT_TK = 256
VT_ROWS = DIFF_DV + 16
ATT_HEADS = 8
VMEM_LIMIT = 56 * 1024 * 1024

F32 = jnp.float32
BF16 = jnp.bfloat16
MASKED = -1e30
NT_DIMS = (((1,), (1,)), ((), ()))
TN_DIMS = (((0,), (0,)), ((), ()))


def _dot(a, b):
    return jnp.dot(a, b, preferred_element_type=F32)


def _split_bf16(a):
    hi = a.astype(BF16)
    lo = (a - hi.astype(F32)).astype(BF16)
    return hi, lo


def _rms_scale(x):
    return lax.rsqrt(jnp.mean(x * x, axis=-1, keepdims=True) + NORM_EPS)


def _silu(x):
    return x * (1.0 / (1.0 + jnp.exp(-x)))


def _params(*sem):
    return pltpu.CompilerParams(dimension_semantics=sem, vmem_limit_bytes=VMEM_LIMIT)


def _full_spec(shape):
    zeros = (0,) * len(shape)
    return pl.BlockSpec(shape, lambda *_: zeros)


def _resident_spec(block_shape, index):
    return pl.BlockSpec(block_shape, lambda m: index, pipeline_mode=pl.Buffered(1))


def _mod_spec(k, tm):
    tiles_per_seq = SEQ // tm
    return pl.BlockSpec((None, 1, D_MODEL), lambda m, *_: ((m // tiles_per_seq) * N_MOD + k, 0, 0))


def _modulated_norm(x, g, scale, shift):
    return (x * _rms_scale(x) * g) * (1.0 + scale) + shift


def _cast_slabs(jobs, n_steps, step_of):
    flats, in_specs, out_specs, out_shapes = [], [], [], []
    for w, layer in jobs:
        _, n_rows, cols = w.shape
        slab = n_rows // n_steps
        assert slab * n_steps == n_rows and slab % 16 == 0
        first = layer * n_steps
        flats.append(w.reshape(-1, cols))
        in_specs.append(pl.BlockSpec((slab, cols), lambda *g, first=first: (first + step_of(*g), 0)))
        out_specs.append(pl.BlockSpec((slab, cols), lambda *g: (step_of(*g), 0)))
        out_shapes.append(jax.ShapeDtypeStruct((n_rows, cols), BF16))
    return flats, in_specs, out_specs, out_shapes


def _mod_block(c_ref, w_ref, b_ref, o_ref):
    c_hi, c_lo = _split_bf16(_silu(c_ref[...]))
    w_hi, w_lo = _split_bf16(w_ref[...])
    lhs = jnp.concatenate([c_hi, c_lo], axis=0)
    r = _dot(lhs, w_hi)
    part = r[:BATCH] + r[BATCH:] + _dot(c_hi, w_lo)

    @pl.when(pl.program_id(1) == 0)
    def _():
        o_ref[...] = part + b_ref[...]

    @pl.when(pl.program_id(1) > 0)
    def _():
        o_ref[...] += part


def _adaln_mod(c, ada_w, ada_b):
    n_out = N_MOD * D_MODEL
    return pl.pallas_call(
        _mod_block,
        grid=(DEPTH, D_MODEL // TK_MOD),
        in_specs=[
            pl.BlockSpec((BATCH, TK_MOD), lambda i, k: (0, k)),
            pl.BlockSpec((None, TK_MOD, n_out), lambda i, k: (i, k, 0)),
            pl.BlockSpec((None, 1, n_out), lambda i, k: (i, 0, 0)),
        ],
        out_specs=pl.BlockSpec((None, BATCH, n_out), lambda i, k: (i, 0, 0)),
        out_shape=jax.ShapeDtypeStruct((DEPTH, BATCH, n_out), F32),
        compiler_params=_params("arbitrary", "arbitrary"),
        name="adaln_mod",
    )(c, ada_w, ada_b.reshape(DEPTH, 1, n_out))


def _gla_layer_kernel(x_ref, shift_ref, scale_ref, g_ref, wqk_ref, wv_ref, wr_ref, wa_ref,
                      wa2_ref, ba_ref, br_ref, ng_ref, *rest, n_cast, tiles_per_seq):
    cast_in, rest = rest[:n_cast], rest[n_cast:]
    o_ref, cast_out = rest[0], rest[1:1 + n_cast]
    wbf_ref, st_ref, qd_ref, ki_ref, v_ref, gate_ref, dec_ref = rest[1 + n_cast:]
    for src, dst in zip(cast_in, cast_out):
        dst[...] = src[...].astype(BF16)

    @pl.when(pl.program_id(0) == 0)
    def _():
        for i, w_ref in enumerate((wqk_ref, wv_ref, wr_ref)):
            wbf_ref[i] = w_ref[...].T.astype(BF16)

    @pl.when(pl.program_id(0) % tiles_per_seq == 0)
    def _():
        st_ref[...] = jnp.zeros_like(st_ref)

    wqk_ref, wv_ref, wr_ref = wbf_ref.at[0], wbf_ref.at[1], wbf_ref.at[2]
    row = lax.broadcasted_iota(jnp.int32, (GLA_CHUNK, GLA_CHUNK), 0)
    col = lax.broadcasted_iota(jnp.int32, (GLA_CHUNK, GLA_CHUNK), 1)
    causal = row >= col
    tril = jnp.where(causal, 1.0, 0.0).astype(BF16)
    w_hi, w_lo = _split_bf16(wa2_ref[...])
    w_gate = jnp.concatenate([w_hi, w_hi, w_lo], axis=0)
    q_scale = GLA_DK ** -0.5
    group_rows = x_ref.shape[0] // GLA_ROW_GROUPS
    chunks_per_group = group_rows // GLA_CHUNK
    chunk = lambda n: slice(n * GLA_CHUNK, (n + 1) * GLA_CHUNK)
    kcols = lambda h: slice(h * GLA_DK, (h + 1) * GLA_DK)
    vcols = lambda h: slice(h * GLA_DV, (h + 1) * GLA_DV)

    def phase_a(grp):
        rows = slice(grp * group_rows, (grp + 1) * group_rows)
        hb = _modulated_norm(x_ref[rows, :], g_ref[...], scale_ref[...],
                             shift_ref[...]).astype(BF16)
        a_hi, a_lo = _split_bf16(lax.dot_general(
            hb, wa_ref[...].astype(BF16), NT_DIMS, preferred_element_type=F32))
        gate = _silu(_dot(hb, wr_ref[...]) + br_ref[...]).astype(BF16)
        for h in range(GLA_HEADS):
            gate_ref[h, rows, :] = gate[:, vcols(h)]
        la = _dot(jnp.concatenate([a_hi, a_lo, a_hi], axis=1), w_gate) + ba_ref[...]
        qk = _dot(hb, wqk_ref[...])
        log_a = (jnp.minimum(la, 0.0) - jnp.log(1.0 + jnp.exp(-jnp.abs(la)))) * (1.0 / GLA_TAU)
        l_hi = log_a.astype(BF16)
        cums = [_dot(tril, l_hi[chunk(i)]) for i in range(chunks_per_group)]
        v = _dot(hb, wv_ref[...]).astype(BF16)
        for h in range(GLA_HEADS):
            v_ref[h, rows, :] = v[:, vcols(h)]
        for i in range(chunks_per_group):
            n = grp * chunks_per_group + i
            b = cums[i]
            b_last = b[GLA_CHUNK - 1:GLA_CHUNK, :]
            qd = (qk[chunk(i), :GLA_HK] * q_scale * jnp.exp(b)).astype(BF16)
            ki = (qk[chunk(i), GLA_HK:] * jnp.exp(-b)).astype(BF16)
            for h in range(GLA_HEADS):
                qd_ref[h, chunk(n), :] = qd[:, kcols(h)]
                ki_ref[h, chunk(n), :] = ki[:, kcols(h)]
            dec_ref[n] = jnp.exp(b_last)

    def phase_b(grp):
        chunks = range(grp * chunks_per_group, (grp + 1) * chunks_per_group)
        work = [(h, n) for h in range(GLA_HEADS) for n in chunks]
        scores = {}
        updates = {}
        for h, n in work:
            scores[h, n] = lax.dot_general(qd_ref[h, chunk(n), :], ki_ref[h, chunk(n), :],
                                           NT_DIMS, preferred_element_type=F32)
            updates[h, n] = lax.dot_general(ki_ref[h, chunk(n), :], v_ref[h, chunk(n), :],
                                            TN_DIMS, preferred_element_type=F32)
        states = {}
        for h in range(GLA_HEADS):
            st = st_ref[h]
            for n in chunks:
                states[h, n] = st.astype(BF16)
                dec_col = jnp.broadcast_to(dec_ref[n][:, kcols(h)], (8, GLA_DK)).T[:, 0:1]
                st = (st + updates[h, n]) * dec_col
            st_ref[h] = st
        for h, n in work:
            s = jnp.where(causal, scores[h, n], 0.0).astype(BF16)
            o = _dot(jnp.concatenate([qd_ref[h, chunk(n), :], s], axis=1),
                     jnp.concatenate([states[h, n], v_ref[h, chunk(n), :]], axis=0))
            y = o * _rms_scale(o) * ng_ref[...]
            o_ref[chunk(n), vcols(h)] = (y * gate_ref[h, chunk(n), :].astype(F32)).astype(BF16)

    for grp in range(GLA_ROW_GROUPS):
        phase_a(grp)
        phase_b(grp)


def _gla_layer(x2, mod3, g, w_in_t, wa2, ba, br, ng, cast_jobs):
    tm = TM_GLA
    n_steps = N_TOK // tm
    row = lambda m: (m, 0)
    assert 2 * GLA_HK == D_MODEL
    row_block = lambda idx: _resident_spec((D_MODEL, D_MODEL), (idx, 0))
    flats, cast_in, cast_out, cast_shapes = _cast_slabs(cast_jobs, n_steps, lambda m: m)
    out = pl.pallas_call(
        functools.partial(_gla_layer_kernel, n_cast=len(flats), tiles_per_seq=SEQ // tm),
        grid=(n_steps,),
        in_specs=[
            pl.BlockSpec((tm, D_MODEL), row),
            _mod_spec(0, tm),
            _mod_spec(1, tm),
            _full_spec((1, D_MODEL)),
            row_block(0),
            row_block(1),
            row_block(2),
            _resident_spec((GLA_GATE_RANK, D_MODEL), (3 * D_MODEL // GLA_GATE_RANK, 0)),
            _full_spec((GLA_GATE_RANK, GLA_HK)),
            _full_spec((1, GLA_HK)),
            _full_spec((1, D_MODEL)),
            _full_spec((1, GLA_DV)),
        ] + cast_in,
        out_specs=[pl.BlockSpec((tm, D_MODEL), row)] + cast_out,
        out_shape=[jax.ShapeDtypeStruct((N_TOK, D_MODEL), BF16)] + cast_shapes,
        scratch_shapes=[
            pltpu.VMEM((3, D_MODEL, D_MODEL), BF16),
            pltpu.VMEM((GLA_HEADS, GLA_DK, GLA_DV), F32),
            pltpu.VMEM((GLA_HEADS, tm, GLA_DK), BF16),
            pltpu.VMEM((GLA_HEADS, tm, GLA_DK), BF16),
            pltpu.VMEM((GLA_HEADS, tm, GLA_DV), BF16),
            pltpu.VMEM((GLA_HEADS, tm, GLA_DV), BF16),
            pltpu.VMEM((tm // GLA_CHUNK, 1, GLA_HK), F32),
        ],
        compiler_params=_params("arbitrary"),
        name="gla_layer",
    )(x2, mod3, mod3, g, w_in_t, w_in_t, w_in_t, w_in_t, wa2, ba, br, ng, *flats)
    return out[0], out[1:]


def _rope_kernel(pos_ref, freq_ref, cs_ref):
    ang = freq_ref[...] * pos_ref[...]
    cs_ref[0:ROPE_HALF, :] = jnp.cos(ang)
    cs_ref[ROPE_HALF:, :] = jnp.sin(ang)


def _rope_tables(positions):
    inv_freq = ROPE_THETA ** (-jnp.arange(0, ROPE_DIM, 2, dtype=F32) / ROPE_DIM)
    return pl.pallas_call(
        _rope_kernel,
        grid=(1,),
        in_specs=[_full_spec((1, N_TOK)), _full_spec((ROPE_HALF, 1))],
        out_specs=_full_spec((ROPE_DIM, N_TOK)),
        out_shape=jax.ShapeDtypeStruct((ROPE_DIM, N_TOK), F32),
        compiler_params=_params("arbitrary"),
        name="rope_tables",
    )(positions.astype(F32).reshape(1, N_TOK), inv_freq.reshape(ROPE_HALF, 1))


def _rope_expander():
    e = np.zeros((3, ROPE_DIM, 2 * DIFF_DV), np.float32)
    for lane in range(DIFF_DV):
        d = lane % DIFF_DH
        if d < ROPE_DIM:
            e[:, d % ROPE_HALF, lane] = 1.0
            e[:, ROPE_HALF + d % ROPE_HALF, DIFF_DV + lane] = -1.0 if d < ROPE_HALF else 1.0
    return jnp.asarray(e.reshape(3 * ROPE_DIM, 2 * DIFF_DV), BF16)


def _inproj_diff_kernel(x_ref, shift_ref, scale_ref, g_ref, wq_ref, wk_ref, wv_ref,
                        cs_ref, e_ref, q_ref, k_ref, vt_ref, wvt_ref):
    @pl.when(pl.program_id(0) == 0)
    def _():
        wvt_ref[...] = wv_ref[...].T

    d = lax.broadcasted_iota(jnp.int32, (1, DIFF_DV), 1) & (DIFF_DH - 1)
    first_half = d < ROPE_HALF
    q_scale = DIFF_DH ** -0.5 * math.log2(math.e)
    group_rows = x_ref.shape[0] // INPROJ_ROW_GROUPS

    for grp in range(INPROJ_ROW_GROUPS):
        rows = slice(grp * group_rows, (grp + 1) * group_rows)
        hb = _modulated_norm(x_ref[rows, :], g_ref[...], scale_ref[...],
                             shift_ref[...]).astype(BF16)

        cs = cs_ref[:, rows]
        hi = cs.astype(BF16).astype(F32)
        mid = (cs - hi).astype(BF16).astype(F32)
        lo = cs - hi - mid
        parts = jnp.concatenate([hi, mid, lo], axis=0)
        tab = _dot(parts.T.astype(BF16), e_ref[...])
        ct = tab[:, :DIFF_DV] + jnp.where(d >= ROPE_DIM, 1.0, 0.0)
        s12 = tab[:, DIFF_DV:]

        for w_ref, out_ref, scl in ((wq_ref, q_ref, q_scale), (wk_ref, k_ref, 1.0)):
            z = _dot(hb, w_ref[...])
            for h in range(DIFF_HEADS):
                xh = z[:, h * DIFF_DV:(h + 1) * DIFF_DV]
                partner = jnp.where(first_half, pltpu.roll(xh, DIFF_DV - ROPE_HALF, 1),
                                    pltpu.roll(xh, ROPE_HALF, 1))
                out_ref[h, rows, :] = ((xh * ct + partner * s12) * scl).astype(BF16)
        vt = lax.dot_general(wvt_ref[...], hb, NT_DIMS, preferred_element_type=F32).astype(BF16)
        ones = jnp.ones((VT_ROWS - DIFF_DV, ATT_TK), BF16)
        for blk in range(group_rows // ATT_TK):
            dst = grp * (group_rows // ATT_TK) + blk
            cols = slice(blk * ATT_TK, (blk + 1) * ATT_TK)
            for h in range(DIFF_HEADS):
                vt_ref[dst, h * VT_ROWS:h * VT_ROWS + DIFF_DV, :] = vt[h * DIFF_DV:(h + 1) * DIFF_DV,
                                                                        cols]
                vt_ref[dst, h * VT_ROWS + DIFF_DV:(h + 1) * VT_ROWS, :] = ones


def _inproj_diff(x2, mod3, g, w_in, cs, expander):
    tm = TM_INPROJ
    row = lambda m: (m, 0)
    h_spec = pl.BlockSpec((DIFF_HEADS, tm, DIFF_DV), lambda m: (0, m, 0))
    return pl.pallas_call(
        _inproj_diff_kernel,
        grid=(N_TOK // tm,),
        in_specs=[
            pl.BlockSpec((tm, D_MODEL), row),
            _mod_spec(0, tm),
            _mod_spec(1, tm),
            _full_spec((1, D_MODEL)),
            _resident_spec((D_MODEL, D_MODEL), (0, 0)),
            _resident_spec((D_MODEL, D_MODEL), (0, 1)),
            _resident_spec((D_MODEL, D_MODEL), (0, 2)),
            pl.BlockSpec((ROPE_DIM, tm), lambda m: (0, m)),
            _full_spec((3 * ROPE_DIM, 2 * DIFF_DV)),
        ],
        out_specs=[h_spec, h_spec,
                   pl.BlockSpec((tm // ATT_TK, DIFF_HEADS * VT_ROWS, ATT_TK), lambda m: (m, 0, 0))],
        out_shape=[jax.ShapeDtypeStruct((DIFF_HEADS, N_TOK, DIFF_DV), BF16),
                   jax.ShapeDtypeStruct((DIFF_HEADS, N_TOK, DIFF_DV), BF16),
                   jax.ShapeDtypeStruct((N_TOK // ATT_TK, DIFF_HEADS * VT_ROWS, ATT_TK), BF16)],
        scratch_shapes=[pltpu.VMEM((D_MODEL, D_MODEL), BF16)],
        compiler_params=_params("arbitrary"),
        name="inproj_diff",
    )(x2, mod3, mod3, g, w_in, w_in, w_in, cs, expander)


def _attn_kernel(q_ref, k_ref, vt_ref, lam_ref, g_ref, o_ref, qs_ref, s_ref, m_ref, acc_ref,
                 *, lambda_init):
    tq, tk = ATT_TQ, ATT_TK
    assert tq == tk
    lf = lam_ref[...]
    lam = (jnp.exp(jnp.sum(lf[0:1] * lf[1:2], axis=-1, keepdims=True))
           - jnp.exp(jnp.sum(lf[2:3] * lf[3:4], axis=-1, keepdims=True)) + lambda_init)
    feat = lax.broadcasted_iota(jnp.int32, (DIFF_DV, tq), 0)
    k_pos = lax.broadcasted_iota(jnp.int32, (tk, 2 * tq), 0)
    q_pos = lax.broadcasted_iota(jnp.int32, (tk, 2 * tq), 1) & (tq - 1)
    heads = [slice(h * DIFF_DV, (h + 1) * DIFF_DV) for h in range(ATT_HEADS)]

    def stage_queries(qi, slot):
        q0 = pl.multiple_of(qi * tq, tq)
        for h in range(ATT_HEADS):
            qt = q_ref[h, pl.ds(q0, tq), :].astype(F32).T
            qs_ref[slot, h, :, 0:tq] = jnp.where(feat < DIFF_DH, qt, 0.0).astype(BF16)
            qs_ref[slot, h, :, tq:2 * tq] = jnp.where(feat >= DIFF_DH, qt, 0.0).astype(BF16)

    def scores_for(j, h, slot):
        start = pl.multiple_of(j * tk, tk)
        return _dot(k_ref[h, pl.ds(start, tk), :], qs_ref[slot, h])

    def reset_state():
        m_ref[...] = jnp.full_like(m_ref, MASKED)

    def kv_step(j, masked, next_j, next_slot):
        for h in range(ATT_HEADS):
            vtb = vt_ref[j, h * VT_ROWS:(h + 1) * VT_ROWS, :]
            s = s_ref[h]
            next_s = scores_for(next_j, h, next_slot)
            if masked:
                s = jnp.where(k_pos <= q_pos, s, MASKED)
            m_prev = m_ref[h]
            m_new = jnp.maximum(m_prev, jnp.max(s, axis=0, keepdims=True))
            p = jnp.exp2(s - m_new)
            alpha = jnp.exp2(m_prev - m_new)
            acc_ref[h] = alpha * acc_ref[h] + _dot(vtb, p.astype(BF16))
            m_ref[h] = m_new
            s_ref[h] = next_s

    n_q = SEQ // tq
    stage_queries(0, 0)
    acc_ref[...] = jnp.zeros_like(acc_ref)
    reset_state()
    for h in range(ATT_HEADS):
        s_ref[h] = scores_for(0, h, 0)

    def q_block(qi, carry):
        q0 = pl.multiple_of(qi * tq, tq)
        slot = qi & 1

        def body(j, c):
            kv_step(j, False, j + 1, slot)
            return c

        lax.fori_loop(0, qi, body, 0)
        stage_queries(jnp.minimum(qi + 1, n_q - 1), 1 - slot)
        kv_step(qi, True, 0, 1 - slot)
        for h, hs in enumerate(heads):
            acc = acc_ref[h, 0:DIFF_DV, :]
            inv_l = 1.0 / acc_ref[h, DIFF_DV:DIFF_DV + 1, :]
            o = acc[:, :tq] * inv_l[:, :tq] - lam * (acc[:, tq:] * inv_l[:, tq:])
            rms = lax.rsqrt(jnp.mean(o * o, axis=0, keepdims=True) + NORM_EPS)
            y = o * rms * (g_ref[...] * (1.0 - lambda_init))
            o_ref[pl.ds(q0, tq), hs] = y.T.astype(BF16)
        reset_state()
        return carry

    lax.fori_loop(0, n_q, q_block, 0)


def _diff_attention(q, k, vt, lam_vecs, g, lambda_init):
    assert ATT_HEADS == DIFF_HEADS
    h_spec = pl.BlockSpec((DIFF_HEADS, SEQ, DIFF_DV), lambda b, h: (0, b, 0))
    return pl.pallas_call(
        functools.partial(_attn_kernel, lambda_init=lambda_init),
        grid=(BATCH, 1),
        in_specs=[h_spec, h_spec,
                  pl.BlockSpec((SEQ // ATT_TK, DIFF_HEADS * VT_ROWS, ATT_TK), lambda b, h: (b, 0, 0)),
                  _full_spec((4, DIFF_DH)), _full_spec((DIFF_DV, 1))],
        out_specs=pl.BlockSpec((None, SEQ, D_MODEL), lambda b, h: (b, 0, 0)),
        out_shape=jax.ShapeDtypeStruct((BATCH, SEQ, D_MODEL), BF16),
        scratch_shapes=[
            pltpu.VMEM((2, ATT_HEADS, DIFF_DV, 2 * ATT_TQ), BF16),
            pltpu.VMEM((ATT_HEADS, ATT_TK, 2 * ATT_TQ), F32),
            pltpu.VMEM((ATT_HEADS, 1, 2 * ATT_TQ), F32),
            pltpu.VMEM((ATT_HEADS, VT_ROWS, 2 * ATT_TQ), F32),
        ],
        compiler_params=_params("arbitrary", "arbitrary"),
        name="diff_attention",
    )(q, k, vt, lam_vecs, g)


def _mlp_kernel(x_ref, o_ref, wo_ref, gt_ref, sh_ref, sc_ref, gc_ref, g_ref, w1_ref, w2_ref,
                fg_ref, *rest, final, n_cast):
    cast_in, (out_ref, *cast_out) = rest[:n_cast], rest[n_cast:]
    for src, dst in zip(cast_in, cast_out):
        dst[...] = src[...].astype(BF16)

    def rows_block(rows):
        x1 = x_ref[rows, :] + gt_ref[...] * _dot(o_ref[rows, :], wo_ref[...])
        hb = _modulated_norm(x1, g_ref[...], sc_ref[...], sh_ref[...]).astype(BF16)
        a = jnp.square(jnp.maximum(_dot(hb, w1_ref[...]), 0.0)).astype(BF16)
        out = x1 + gc_ref[...] * _dot(a, w2_ref[...])
        if final:
            out = out * _rms_scale(out) * fg_ref[...]
        out_ref[rows, :] = out

    n_sub = x_ref.shape[0] // MLP_SUB_ROWS
    if n_sub == 1:
        rows_block(slice(None))
    else:
        def body(i, carry):
            rows_block(pl.ds(pl.multiple_of(i * MLP_SUB_ROWS, MLP_SUB_ROWS), MLP_SUB_ROWS))
            return carry
        lax.fori_loop(0, n_sub, body, 0)


def _outproj_mlp(x2, o2, mod3, wo, g, w1, w2, fg, final, cast_jobs=()):
    tm = TM_MLP_HOST if cast_jobs else TM_MLP
    n_steps = N_TOK // tm
    row = lambda m: (m, 0)
    flats, cast_in, cast_out, cast_shapes = _cast_slabs(cast_jobs, n_steps, lambda m: m)
    out = pl.pallas_call(
        functools.partial(_mlp_kernel, final=final, n_cast=len(flats)),
        grid=(n_steps,),
        in_specs=[
            pl.BlockSpec((tm, D_MODEL), row),
            pl.BlockSpec((tm, D_MODEL), row),
            _resident_spec((D_MODEL, D_MODEL), (0, 0)),
            _mod_spec(2, tm),
            _mod_spec(3, tm),
            _mod_spec(4, tm),
            _mod_spec(5, tm),
            _full_spec((1, D_MODEL)),
            _resident_spec((D_MODEL, D_FF), (0, 0)),
            _resident_spec((D_FF, D_MODEL), (0, 0)),
            _full_spec((1, D_MODEL)),
        ] + cast_in,
        out_specs=[pl.BlockSpec((tm, D_MODEL), row)] + cast_out,
        out_shape=[jax.ShapeDtypeStruct((N_TOK, D_MODEL), F32)] + cast_shapes,
        compiler_params=_params("arbitrary"),
        name="outproj_mlp_final" if final else "outproj_mlp",
    )(x2, o2, wo, mod3, mod3, mod3, mod3, g, w1, w2, fg, *flats)
    return out[0], out[1:]


def kernel(x, c, positions, ada_w, ada_b, norm_g, mlp_w1, mlp_w2, gla_w_in, gla_w_a2, gla_b_a,
           gla_b_r, gla_norm_g, gla_w_o, diff_w_in, diff_lambda, diff_subln_g, diff_w_o, final_g):
    x2 = x.reshape(N_TOK, D_MODEL)
    mod = _adaln_mod(c, ada_w, ada_b)
    mod_table = lambda layer: mod[layer].reshape(BATCH * N_MOD, 1, D_MODEL)
    mod3 = mod_table(0)
    fg = final_g.reshape(1, D_MODEL)

    o, (w1, w2, wo) = _gla_layer(
        x2, mod3, norm_g[0, 0].reshape(1, D_MODEL), gla_w_in[0].T, gla_w_a2[0],
        gla_b_a[0].reshape(1, GLA_HK), gla_b_r[0].reshape(1, D_MODEL),
        gla_norm_g[0].reshape(1, GLA_DV), [(mlp_w1, 0), (mlp_w2, 0), (gla_w_o, 0)])
    x2, (w1, w2, diff_win, wo) = _outproj_mlp(
        x2, o, mod3, wo, norm_g[0, 1].reshape(1, D_MODEL), w1, w2, fg,
        False, [(mlp_w1, 1), (mlp_w2, 1), (diff_w_in, 0), (diff_w_o, 0)])
    mod3 = mod_table(1)

    lambda_init = 0.8 - 0.6 * math.exp(-0.3 * 1)
    cs = _rope_tables(positions)
    q, k, vt = _inproj_diff(
        x2, mod3, norm_g[1, 0].reshape(1, D_MODEL), diff_win, cs, _rope_expander())
    o = _diff_attention(q, k, vt, diff_lambda[0], diff_subln_g[0].reshape(DIFF_DV, 1), lambda_init)
    x2, _ = _outproj_mlp(x2, o.reshape(N_TOK, D_MODEL), mod3, wo,
                         norm_g[1, 1].reshape(1, D_MODEL), w1, w2, fg, True)
    return x2.reshape(BATCH, SEQ, D_MODEL)
```

```python
import functools
import math

import jax
import jax.numpy as jnp
import numpy as np
from jax import lax
from jax.experimental import pallas as pl
from jax.experimental.pallas import tpu as pltpu

D_MODEL = 1024
BATCH = 8
SEQ = 2048
DEPTH = 2
D_FF = 4 * D_MODEL
NORM_EPS = 1e-6
GLA_HEADS = 4
GLA_DK = 128
GLA_DV = 256
GLA_HK = GLA_HEADS * GLA_DK
GLA_GATE_RANK = 16
GLA_TAU = 16.0
GLA_CHUNK = 64
DIFF_HEADS = 8
DIFF_DH = 64
DIFF_DV = 128
ROPE_THETA = 500000.0
ROPE_DIM = DIFF_DH // 4
ROPE_HALF = ROPE_DIM // 2
N_TOK = BATCH * SEQ
N_MOD = 6

TM_INPROJ = 1024
INPROJ_ROW_GROUPS = 2
TM_MLP = 1024
TM_MLP_HOST = 512
MLP_SUB_ROWS = 512
TM_GLA = 1024
GLA_ROW_GROUPS = 2
TK_MOD = 256
ATT_TQ = 256
ATT_TK = 256
VT_ROWS = DIFF_DV + 16
ATT_HEADS = 8
VMEM_LIMIT = 56 * 1024 * 1024

F32 = jnp.float32
BF16 = jnp.bfloat16
MASKED = -1e30
NT_DIMS = (((1,), (1,)), ((), ()))
TN_DIMS = (((0,), (0,)), ((), ()))


def _dot(a, b):
    return jnp.dot(a, b, preferred_element_type=F32)


def _split_bf16(a):
    hi = a.astype(BF16)
    lo = (a - hi.astype(F32)).astype(BF16)
    return hi, lo


def _rms_scale(x):
    return lax.rsqrt(jnp.mean(x * x, axis=-1, keepdims=True) + NORM_EPS)


def _silu(x):
    return x * (1.0 / (1.0 + jnp.exp(-x)))


def _params(*sem):
    return pltpu.CompilerParams(dimension_semantics=sem, vmem_limit_bytes=VMEM_LIMIT)


def _full_spec(shape):
    zeros = (0,) * len(shape)
    return pl.BlockSpec(shape, lambda *_: zeros)


def _resident_spec(block_shape, index):
    return pl.BlockSpec(block_shape, lambda m: index, pipeline_mode=pl.Buffered(1))


def _mod_spec(k, tm):
    tiles_per_seq = SEQ // tm
    return pl.BlockSpec((None, 1, D_MODEL), lambda m, *_: ((m // tiles_per_seq) * N_MOD + k, 0, 0))


def _modulated_norm(x, g, scale, shift):
    return (x * _rms_scale(x) * g) * (1.0 + scale) + shift


def _cast_slabs(jobs, n_steps, step_of):
    flats, in_specs, out_specs, out_shapes = [], [], [], []
    for w, layer in jobs:
        _, n_rows, cols = w.shape
        slab = n_rows // n_steps
        assert slab * n_steps == n_rows and slab % 16 == 0
        first = layer * n_steps
        flats.append(w.reshape(-1, cols))
        in_specs.append(pl.BlockSpec((slab, cols), lambda *g, first=first: (first + step_of(*g), 0)))
        out_specs.append(pl.BlockSpec((slab, cols), lambda *g: (step_of(*g), 0)))
        out_shapes.append(jax.ShapeDtypeStruct((n_rows, cols), BF16))
    return flats, in_specs, out_specs, out_shapes


def _mod_block(c_ref, w_ref, b_ref, o_ref):
    c_hi, c_lo = _split_bf16(_silu(c_ref[...]))
    w_hi, w_lo = _split_bf16(w_ref[...])
    lhs = jnp.concatenate([c_hi, c_lo], axis=0)
    r = _dot(lhs, w_hi)
    part = r[:BATCH] + r[BATCH:] + _dot(c_hi, w_lo)

    @pl.when(pl.program_id(1) == 0)
    def _():
        o_ref[...] = part + b_ref[...]

    @pl.when(pl.program_id(1) > 0)
    def _():
        o_ref[...] += part


def _adaln_mod(c, ada_w, ada_b):
    n_out = N_MOD * D_MODEL
    return pl.pallas_call(
        _mod_block,
        grid=(DEPTH, D_MODEL // TK_MOD),
        in_specs=[
            pl.BlockSpec((BATCH, TK_MOD), lambda i, k: (0, k)),
            pl.BlockSpec((None, TK_MOD, n_out), lambda i, k: (i, k, 0)),
            pl.BlockSpec((None, 1, n_out), lambda i, k: (i, 0, 0)),
        ],
        out_specs=pl.BlockSpec((None, BATCH, n_out), lambda i, k: (i, 0, 0)),
        out_shape=jax.ShapeDtypeStruct((DEPTH, BATCH, n_out), F32),
        compiler_params=_params("arbitrary", "arbitrary"),
        name="adaln_mod",
    )(c, ada_w, ada_b.reshape(DEPTH, 1, n_out))


def _gla_layer_kernel(x_ref, shift_ref, scale_ref, g_ref, wqk_ref, wv_ref, wr_ref, wa_ref,
                      wa2_ref, ba_ref, br_ref, ng_ref, *rest, n_cast, tiles_per_seq):
    cast_in, rest = rest[:n_cast], rest[n_cast:]
    o_ref, cast_out = rest[0], rest[1:1 + n_cast]
    wbf_ref, st_ref, qd_ref, ki_ref, v_ref, gate_ref, dec_ref = rest[1 + n_cast:]
    for src, dst in zip(cast_in, cast_out):
        dst[...] = src[...].astype(BF16)

    @pl.when(pl.program_id(0) == 0)
    def _():
        for i, w_ref in enumerate((wqk_ref, wv_ref, wr_ref)):
            wbf_ref[i] = w_ref[...].T.astype(BF16)

    @pl.when(pl.program_id(0) % tiles_per_seq == 0)
    def _():
        st_ref[...] = jnp.zeros_like(st_ref)

    wqk_ref, wv_ref, wr_ref = wbf_ref.at[0], wbf_ref.at[1], wbf_ref.at[2]
    row = lax.broadcasted_iota(jnp.int32, (GLA_CHUNK, GLA_CHUNK), 0)
    col = lax.broadcasted_iota(jnp.int32, (GLA_CHUNK, GLA_CHUNK), 1)
    causal = row >= col
    tril = jnp.where(causal, 1.0, 0.0).astype(BF16)
    w_hi, w_lo = _split_bf16(wa2_ref[...])
    w_gate = jnp.concatenate([w_hi, w_hi, w_lo], axis=0)
    q_scale = GLA_DK ** -0.5
    group_rows = x_ref.shape[0] // GLA_ROW_GROUPS
    chunks_per_group = group_rows // GLA_CHUNK
    chunk = lambda n: slice(n * GLA_CHUNK, (n + 1) * GLA_CHUNK)
    kcols = lambda h: slice(h * GLA_DK, (h + 1) * GLA_DK)
    vcols = lambda h: slice(h * GLA_DV, (h + 1) * GLA_DV)

    def phase_a(grp):
        rows = slice(grp * group_rows, (grp + 1) * group_rows)
        hb = _modulated_norm(x_ref[rows, :], g_ref[...], scale_ref[...],
                             shift_ref[...]).astype(BF16)
        a_hi, a_lo = _split_bf16(lax.dot_general(
            hb, wa_ref[...].astype(BF16), NT_DIMS, preferred_element_type=F32))
        gate = _silu(_dot(hb, wr_ref[...]) + br_ref[...]).astype(BF16)
        for h in range(GLA_HEADS):
            gate_ref[h, rows, :] = gate[:, vcols(h)]
        la = _dot(jnp.concatenate([a_hi, a_lo, a_hi], axis=1), w_gate) + ba_ref[...]
        qk = _dot(hb, wqk_ref[...])
        log_a = (jnp.minimum(la, 0.0) - jnp.log(1.0 + jnp.exp(-jnp.abs(la)))) * (1.0 / GLA_TAU)
        l_hi = log_a.astype(BF16)
        cums = [_dot(tril, l_hi[chunk(i)]) for i in range(chunks_per_group)]
        v = _dot(hb, wv_ref[...]).astype(BF16)
        for h in range(GLA_HEADS):
            v_ref[h, rows, :] = v[:, vcols(h)]
        for i in range(chunks_per_group):
            n = grp * chunks_per_group + i
            b = cums[i]
            b_last = b[GLA_CHUNK - 1:GLA_CHUNK, :]
            qd = (qk[chunk(i), :GLA_HK] * q_scale * jnp.exp(b)).astype(BF16)
            ki = (qk[chunk(i), GLA_HK:] * jnp.exp(-b)).astype(BF16)
            for h in range(GLA_HEADS):
                qd_ref[h, chunk(n), :] = qd[:, kcols(h)]
                ki_ref[h, chunk(n), :] = ki[:, kcols(h)]
            dec_ref[n] = jnp.exp(b_last)

    def phase_b(grp):
        chunks = range(grp * chunks_per_group, (grp + 1) * chunks_per_group)
        work = [(h, n) for h in range(GLA_HEADS) for n in chunks]
        scores = {}
        updates = {}
        for h, n in work:
            scores[h, n] = lax.dot_general(qd_ref[h, chunk(n), :], ki_ref[h, chunk(n), :],
                                           NT_DIMS, preferred_element_type=F32)
            updates[h, n] = lax.dot_general(ki_ref[h, chunk(n), :], v_ref[h, chunk(n), :],
                                            TN_DIMS, preferred_element_type=F32)
        states = {}
        for h in range(GLA_HEADS):
            st = st_ref[h]
            for n in chunks:
                states[h, n] = st.astype(BF16)
                dec_col = jnp.broadcast_to(dec_ref[n][:, kcols(h)], (8, GLA_DK)).T[:, 0:1]
                st = (st + updates[h, n]) * dec_col
            st_ref[h] = st
        for h, n in work:
            s = jnp.where(causal, scores[h, n], 0.0).astype(BF16)
            o = _dot(jnp.concatenate([qd_ref[h, chunk(n), :], s], axis=1),
                     jnp.concatenate([states[h, n], v_ref[h, chunk(n), :]], axis=0))
            y = o * _rms_scale(o) * ng_ref[...]
            o_ref[chunk(n), vcols(h)] = (y * gate_ref[h, chunk(n), :].astype(F32)).astype(BF16)

    for grp in range(GLA_ROW_GROUPS):
        phase_a(grp)
        phase_b(grp)


def _gla_layer(x2, mod3, g, w_in_t, wa2, ba, br, ng, cast_jobs):
    tm = TM_GLA
    n_steps = N_TOK // tm
    row = lambda m: (m, 0)
    assert 2 * GLA_HK == D_MODEL
    row_block = lambda idx: _resident_spec((D_MODEL, D_MODEL), (idx, 0))
    flats, cast_in, cast_out, cast_shapes = _cast_slabs(cast_jobs, n_steps, lambda m: m)
    out = pl.pallas_call(
        functools.partial(_gla_layer_kernel, n_cast=len(flats), tiles_per_seq=SEQ // tm),
        grid=(n_steps,),
        in_specs=[
            pl.BlockSpec((tm, D_MODEL), row),
            _mod_spec(0, tm),
            _mod_spec(1, tm),
            _full_spec((1, D_MODEL)),
            row_block(0),
            row_block(1),
            row_block(2),
            _resident_spec((GLA_GATE_RANK, D_MODEL), (3 * D_MODEL // GLA_GATE_RANK, 0)),
            _full_spec((GLA_GATE_RANK, GLA_HK)),
            _full_spec((1, GLA_HK)),
            _full_spec((1, D_MODEL)),
            _full_spec((1, GLA_DV)),
        ] + cast_in,
        out_specs=[pl.BlockSpec((tm, D_MODEL), row)] + cast_out,
        out_shape=[jax.ShapeDtypeStruct((N_TOK, D_MODEL), BF16)] + cast_shapes,
        scratch_shapes=[
            pltpu.VMEM((3, D_MODEL, D_MODEL), BF16),
            pltpu.VMEM((GLA_HEADS, GLA_DK, GLA_DV), F32),
            pltpu.VMEM((GLA_HEADS, tm, GLA_DK), BF16),
            pltpu.VMEM((GLA_HEADS, tm, GLA_DK), BF16),
            pltpu.VMEM((GLA_HEADS, tm, GLA_DV), BF16),
            pltpu.VMEM((GLA_HEADS, tm, GLA_DV), BF16),
            pltpu.VMEM((tm // GLA_CHUNK, 1, GLA_HK), F32),
        ],
        compiler_params=_params("arbitrary"),
        name="gla_layer",
    )(x2, mod3, mod3, g, w_in_t, w_in_t, w_in_t, w_in_t, wa2, ba, br, ng, *flats)
    return out[0], out[1:]


def _rope_kernel(pos_ref, freq_ref, cs_ref):
    ang = freq_ref[...] * pos_ref[...]
    cs_ref[0:ROPE_HALF, :] = jnp.cos(ang)
    cs_ref[ROPE_HALF:, :] = jnp.sin(ang)


def _rope_tables(positions):
    inv_freq = ROPE_THETA ** (-jnp.arange(0, ROPE_DIM, 2, dtype=F32) / ROPE_DIM)
    return pl.pallas_call(
        _rope_kernel,
        grid=(1,),
        in_specs=[_full_spec((1, N_TOK)), _full_spec((ROPE_HALF, 1))],
        out_specs=_full_spec((ROPE_DIM, N_TOK)),
        out_shape=jax.ShapeDtypeStruct((ROPE_DIM, N_TOK), F32),
        compiler_params=_params("arbitrary"),
        name="rope_tables",
    )(positions.astype(F32).reshape(1, N_TOK), inv_freq.reshape(ROPE_HALF, 1))


def _rope_expander():
    e = np.zeros((3, ROPE_DIM, 2 * DIFF_DV), np.float32)
    for lane in range(DIFF_DV):
        d = lane % DIFF_DH
        if d < ROPE_DIM:
            e[:, d % ROPE_HALF, lane] = 1.0
            e[:, ROPE_HALF + d % ROPE_HALF, DIFF_DV + lane] = -1.0 if d < ROPE_HALF else 1.0
    return jnp.asarray(e.reshape(3 * ROPE_DIM, 2 * DIFF_DV), BF16)


def _inproj_diff_kernel(x_ref, shift_ref, scale_ref, g_ref, wq_ref, wk_ref, wv_ref,
                        cs_ref, e_ref, q_ref, k_ref, vt_ref, wvt_ref):
    @pl.when(pl.program_id(0) == 0)
    def _():
        wvt_ref[...] = wv_ref[...].T

    d = lax.broadcasted_iota(jnp.int32, (1, DIFF_DV), 1) & (DIFF_DH - 1)
    first_half = d < ROPE_HALF
    q_scale = DIFF_DH ** -0.5 * math.log2(math.e)
    group_rows = x_ref.shape[0] // INPROJ_ROW_GROUPS

    for grp in range(INPROJ_ROW_GROUPS):
        rows = slice(grp * group_rows, (grp + 1) * group_rows)
        hb = _modulated_norm(x_ref[rows, :], g_ref[...], scale_ref[...],
                             shift_ref[...]).astype(BF16)

        cs = cs_ref[:, rows]
        hi = cs.astype(BF16).astype(F32)
        mid = (cs - hi).astype(BF16).astype(F32)
        lo = cs - hi - mid
        parts = jnp.concatenate([hi, mid, lo], axis=0)
        tab = _dot(parts.T.astype(BF16), e_ref[...])
        ct = tab[:, :DIFF_DV] + jnp.where(d >= ROPE_DIM, 1.0, 0.0)
        s12 = tab[:, DIFF_DV:]

        for w_ref, out_ref, scl in ((wq_ref, q_ref, q_scale), (wk_ref, k_ref, 1.0)):
            z = _dot(hb, w_ref[...])
            for h in range(DIFF_HEADS):
                xh = z[:, h * DIFF_DV:(h + 1) * DIFF_DV]
                partner = jnp.where(first_half, pltpu.roll(xh, DIFF_DV - ROPE_HALF, 1),
                                    pltpu.roll(xh, ROPE_HALF, 1))
                out_ref[h, rows, :] = ((xh * ct + partner * s12) * scl).astype(BF16)
        vt = lax.dot_general(wvt_ref[...], hb, NT_DIMS, preferred_element_type=F32).astype(BF16)
        ones = jnp.ones((VT_ROWS - DIFF_DV, ATT_TK), BF16)
        for blk in range(group_rows // ATT_TK):
            dst = grp * (group_rows // ATT_TK) + blk
            cols = slice(blk * ATT_TK, (blk + 1) * ATT_TK)
            for h in range(DIFF_HEADS):
                vt_ref[dst, h * VT_ROWS:h * VT_ROWS + DIFF_DV, :] = vt[h * DIFF_DV:(h + 1) * DIFF_DV,
                                                                        cols]
                vt_ref[dst, h * VT_ROWS + DIFF_DV:(h + 1) * VT_ROWS, :] = ones


def _inproj_diff(x2, mod3, g, w_in, cs, expander):
    tm = TM_INPROJ
    row = lambda m: (m, 0)
    h_spec = pl.BlockSpec((DIFF_HEADS, tm, DIFF_DV), lambda m: (0, m, 0))
    return pl.pallas_call(
        _inproj_diff_kernel,
        grid=(N_TOK // tm,),
        in_specs=[
            pl.BlockSpec((tm, D_MODEL), row),
            _mod_spec(0, tm),
            _mod_spec(1, tm),
            _full_spec((1, D_MODEL)),
            _resident_spec((D_MODEL, D_MODEL), (0, 0)),
            _resident_spec((D_MODEL, D_MODEL), (0, 1)),
            _resident_spec((D_MODEL, D_MODEL), (0, 2)),
            pl.BlockSpec((ROPE_DIM, tm), lambda m: (0, m)),
            _full_spec((3 * ROPE_DIM, 2 * DIFF_DV)),
        ],
        out_specs=[h_spec, h_spec,
                   pl.BlockSpec((tm // ATT_TK, DIFF_HEADS * VT_ROWS, ATT_TK), lambda m: (m, 0, 0))],
        out_shape=[jax.ShapeDtypeStruct((DIFF_HEADS, N_TOK, DIFF_DV), BF16),
                   jax.ShapeDtypeStruct((DIFF_HEADS, N_TOK, DIFF_DV), BF16),
                   jax.ShapeDtypeStruct((N_TOK // ATT_TK, DIFF_HEADS * VT_ROWS, ATT_TK), BF16)],
        scratch_shapes=[pltpu.VMEM((D_MODEL, D_MODEL), BF16)],
        compiler_params=_params("arbitrary"),
        name="inproj_diff",
    )(x2, mod3, mod3, g, w_in, w_in, w_in, cs, expander)


def _attn_kernel(q_ref, k_ref, vt_ref, lam_ref, g_ref, o_ref, qs_ref, s_ref, m_ref, acc_ref,
                 *, lambda_init):
    tq, tk = ATT_TQ, ATT_TK
    assert tq == tk
    lf = lam_ref[...]
    lam = (jnp.exp(jnp.sum(lf[0:1] * lf[1:2], axis=-1, keepdims=True))
           - jnp.exp(jnp.sum(lf[2:3] * lf[3:4], axis=-1, keepdims=True)) + lambda_init)
    feat = lax.broadcasted_iota(jnp.int32, (DIFF_DV, tq), 0)
    k_pos = lax.broadcasted_iota(jnp.int32, (tk, tq), 0)
    q_pos = lax.broadcasted_iota(jnp.int32, (tk, tq), 1)
    heads = [slice(h * DIFF_DV, (h + 1) * DIFF_DV) for h in range(ATT_HEADS)]

    def stage_queries(qi, slot):
        q0 = pl.multiple_of(qi * tq, tq)
        for h in range(ATT_HEADS):
            qt = q_ref[h, pl.ds(q0, tq), :].astype(F32).T
            qs_ref[slot, h, :, 0:tq] = jnp.where(feat < DIFF_DH, qt, 0.0).astype(BF16)
            qs_ref[slot, h, :, tq:2 * tq] = jnp.where(feat >= DIFF_DH, qt, 0.0).astype(BF16)

    def scores_for(j, h, slot):
        start = pl.multiple_of(j * tk, tk)
        return _dot(k_ref[h, pl.ds(start, tk), :], qs_ref[slot, h])

    def reset_state():
        m_ref[...] = jnp.full_like(m_ref, MASKED)

    def kv_step(j, masked, next_j, next_slot):
        for h in range(ATT_HEADS):
            vtb = vt_ref[j, h * VT_ROWS:(h + 1) * VT_ROWS, :]
            next_s = scores_for(next_j, h, next_slot)
            for sub in range(2):
                s = s_ref[h, sub]
                if masked:
                    s = jnp.where(k_pos <= q_pos, s, MASKED)
                m_prev = m_ref[h, sub]
                m_new = jnp.maximum(m_prev, jnp.max(s, axis=0, keepdims=True))
                p = jnp.exp2(s - m_new)
                alpha = jnp.exp2(m_prev - m_new)
                acc_ref[h, sub] = alpha * acc_ref[h, sub] + _dot(vtb, p.astype(BF16))
                m_ref[h, sub] = m_new
                s_ref[h, sub] = next_s[:, sub * tq:(sub + 1) * tq]

    n_q = SEQ // tq
    stage_queries(0, 0)
    acc_ref[...] = jnp.zeros_like(acc_ref)
    reset_state()
    for h in range(ATT_HEADS):
        first = scores_for(0, h, 0)
        for sub in range(2):
            s_ref[h, sub] = first[:, sub * tq:(sub + 1) * tq]

    def q_block(qi, carry):
        q0 = pl.multiple_of(qi * tq, tq)
        slot = qi & 1

        def body(j, c):
            kv_step(j, False, j + 1, slot)
            return c

        lax.fori_loop(0, qi, body, 0)
        stage_queries(jnp.minimum(qi + 1, n_q - 1), 1 - slot)
        kv_step(qi, True, 0, 1 - slot)
        for h, hs in enumerate(heads):
            o1, o2 = [acc_ref[h, sub, 0:DIFF_DV, :] * (1.0 / acc_ref[h, sub, DIFF_DV:DIFF_DV + 1, :])
                      for sub in range(2)]
            o = o1 - lam * o2
            rms = lax.rsqrt(jnp.mean(o * o, axis=0, keepdims=True) + NORM_EPS)
            y = o * rms * (g_ref[...] * (1.0 - lambda_init))
            o_ref[pl.ds(q0, tq), hs] = y.T.astype(BF16)
        reset_state()
        return carry

    lax.fori_loop(0, n_q, q_block, 0)


def _diff_attention(q, k, vt, lam_vecs, g, lambda_init):
    assert ATT_HEADS == DIFF_HEADS
    h_spec = pl.BlockSpec((DIFF_HEADS, SEQ, DIFF_DV), lambda b, h: (0, b, 0))
    return pl.pallas_call(
        functools.partial(_attn_kernel, lambda_init=lambda_init),
        grid=(BATCH, 1),
        in_specs=[h_spec, h_spec,
                  pl.BlockSpec((SEQ // ATT_TK, DIFF_HEADS * VT_ROWS, ATT_TK), lambda b, h: (b, 0, 0)),
                  _full_spec((4, DIFF_DH)), _full_spec((DIFF_DV, 1))],
        out_specs=pl.BlockSpec((None, SEQ, D_MODEL), lambda b, h: (b, 0, 0)),
        out_shape=jax.ShapeDtypeStruct((BATCH, SEQ, D_MODEL), BF16),
        scratch_shapes=[
            pltpu.VMEM((2, ATT_HEADS, DIFF_DV, 2 * ATT_TQ), BF16),
            pltpu.VMEM((ATT_HEADS, 2, ATT_TK, ATT_TQ), F32),
            pltpu.VMEM((ATT_HEADS, 2, 1, ATT_TQ), F32),
            pltpu.VMEM((ATT_HEADS, 2, VT_ROWS, ATT_TQ), F32),
        ],
        compiler_params=_params("arbitrary", "arbitrary"),
        name="diff_attention",
    )(q, k, vt, lam_vecs, g)


def _mlp_kernel(x_ref, o_ref, wo_ref, gt_ref, sh_ref, sc_ref, gc_ref, g_ref, w1_ref, w2_ref,
                fg_ref, *rest, final, n_cast):
    cast_in, (out_ref, *cast_out) = rest[:n_cast], rest[n_cast:]
    for src, dst in zip(cast_in, cast_out):
        dst[...] = src[...].astype(BF16)

    def rows_block(rows):
        x1 = x_ref[rows, :] + gt_ref[...] * _dot(o_ref[rows, :], wo_ref[...])
        hb = _modulated_norm(x1, g_ref[...], sc_ref[...], sh_ref[...]).astype(BF16)
        a = jnp.square(jnp.maximum(_dot(hb, w1_ref[...]), 0.0)).astype(BF16)
        out = x1 + gc_ref[...] * _dot(a, w2_ref[...])
        if final:
            out = out * _rms_scale(out) * fg_ref[...]
        out_ref[rows, :] = out

    n_sub = x_ref.shape[0] // MLP_SUB_ROWS
    if n_sub == 1:
        rows_block(slice(None))
    else:
        def body(i, carry):
            rows_block(pl.ds(pl.multiple_of(i * MLP_SUB_ROWS, MLP_SUB_ROWS), MLP_SUB_ROWS))
            return carry
        lax.fori_loop(0, n_sub, body, 0)


def _outproj_mlp(x2, o2, mod3, wo, g, w1, w2, fg, final, cast_jobs=()):
    tm = TM_MLP_HOST if cast_jobs else TM_MLP
    n_steps = N_TOK // tm
    row = lambda m: (m, 0)
    flats, cast_in, cast_out, cast_shapes = _cast_slabs(cast_jobs, n_steps, lambda m: m)
    out = pl.pallas_call(
        functools.partial(_mlp_kernel, final=final, n_cast=len(flats)),
        grid=(n_steps,),
        in_specs=[
            pl.BlockSpec((tm, D_MODEL), row),
            pl.BlockSpec((tm, D_MODEL), row),
            _resident_spec((D_MODEL, D_MODEL), (0, 0)),
            _mod_spec(2, tm),
            _mod_spec(3, tm),
            _mod_spec(4, tm),
            _mod_spec(5, tm),
            _full_spec((1, D_MODEL)),
            _resident_spec((D_MODEL, D_FF), (0, 0)),
            _resident_spec((D_FF, D_MODEL), (0, 0)),
            _full_spec((1, D_MODEL)),
        ] + cast_in,
        out_specs=[pl.BlockSpec((tm, D_MODEL), row)] + cast_out,
        out_shape=[jax.ShapeDtypeStruct((N_TOK, D_MODEL), F32)] + cast_shapes,
        compiler_params=_params("arbitrary"),
        name="outproj_mlp_final" if final else "outproj_mlp",
    )(x2, o2, wo, mod3, mod3, mod3, mod3, g, w1, w2, fg, *flats)
    return out[0], out[1:]


def kernel(x, c, positions, ada_w, ada_b, norm_g, mlp_w1, mlp_w2, gla_w_in, gla_w_a2, gla_b_a,
           gla_b_r, gla_norm_g, gla_w_o, diff_w_in, diff_lambda, diff_subln_g, diff_w_o, final_g):
    x2 = x.reshape(N_TOK, D_MODEL)
    mod = _adaln_mod(c, ada_w, ada_b)
    mod_table = lambda layer: mod[layer].reshape(BATCH * N_MOD, 1, D_MODEL)
    mod3 = mod_table(0)
    fg = final_g.reshape(1, D_MODEL)

    o, (w1, w2, wo) = _gla_layer(
        x2, mod3, norm_g[0, 0].reshape(1, D_MODEL), gla_w_in[0].T, gla_w_a2[0],
        gla_b_a[0].reshape(1, GLA_HK), gla_b_r[0].reshape(1, D_MODEL),
        gla_norm_g[0].reshape(1, GLA_DV), [(mlp_w1, 0), (mlp_w2, 0), (gla_w_o, 0)])
    x2, (w1, w2, diff_win, wo) = _outproj_mlp(
        x2, o, mod3, wo, norm_g[0, 1].reshape(1, D_MODEL), w1, w2, fg,
        False, [(mlp_w1, 1), (mlp_w2, 1), (diff_w_in, 0), (diff_w_o, 0)])
    mod3 = mod_table(1)

    lambda_init = 0.8 - 0.6 * math.exp(-0.3 * 1)
    cs = _rope_tables(positions)
    q, k, vt = _inproj_diff(
        x2, mod3, norm_g[1, 0].reshape(1, D_MODEL), diff_win, cs, _rope_expander())
    o = _diff_attention(q, k, vt, diff_lambda[0], diff_subln_g[0].reshape(DIFF_DV, 1), lambda_init)
    x2, _ = _outproj_mlp(x2, o.reshape(N_TOK, D_MODEL), mod3, wo,
                         norm_g[1, 1].reshape(1, D_MODEL), w1, w2, fg, True)
    return x2.reshape(BATCH, SEQ, D_MODEL)
```
